```python
import jax, jax.numpy as jnp
from jax import lax
import numpy as np

D_MODEL = 1024
BATCH = 8
SEQ = 2048
DEPTH = 4

N_A = DEPTH // 2
N_B = DEPTH - N_A
HEAD_DIM = 64
MEM_LEN = 256
N_MEM_HEADS = 4
MEM_WIDTH = N_MEM_HEADS * HEAD_DIM
MIX_WIDTH = D_MODEL - MEM_WIDTH
CONV_CH = MIX_WIDTH
CONV_WIDTH = 31
N_FOX_HEADS = MIX_WIDTH // HEAD_DIM
D_FF = 2816
FFN_CONV_WIDTH = 3
BLOCK_Q = 128
RMS_EPS = 1e-6
LN_EPS = 1e-5

kernel_name = "yoco_conformer_fox_hybrid"


def rmsnorm(x, g):
    xf = x.astype(jnp.float32)
    y = xf * lax.rsqrt(jnp.mean(xf * xf, axis=-1, keepdims=True) + RMS_EPS)
    return (y * g.astype(jnp.float32)).astype(x.dtype)


def layernorm(x, g, b):
    xf = x.astype(jnp.float32)
    mu = jnp.mean(xf, axis=-1, keepdims=True)
    var = jnp.mean(jnp.square(xf - mu), axis=-1, keepdims=True)
    y = (xf - mu) * lax.rsqrt(var + LN_EPS)
    return (y * g.astype(jnp.float32) + b.astype(jnp.float32)).astype(x.dtype)


def causal_dwconv(x, w, b):
    width, ch = w.shape
    y = lax.conv_general_dilated(
        x, w[:, None, :].astype(x.dtype), window_strides=(1,),
        padding=((width - 1, 0),), dimension_numbers=("NWC", "WIO", "NWC"),
        feature_group_count=ch)
    return y + b


def conformer_conv(u, b_glu, w_dw, b_dw, ln_g, ln_b):
    u = u + b_glu
    a, gate = jnp.split(u, 2, axis=-1)
    v = a * jax.nn.sigmoid(gate)
    v = causal_dwconv(v, w_dw, b_dw)
    v = layernorm(v, ln_g, ln_b)
    return jax.nn.silu(v)


def memory_attention(q, mem_k, mem_v):
    b, s, _ = q.shape
    qh = q.reshape(b, s, N_MEM_HEADS, HEAD_DIM)
    kh = mem_k.reshape(b, -1, N_MEM_HEADS, HEAD_DIM)
    vh = mem_v.reshape(b, -1, N_MEM_HEADS, HEAD_DIM)
    logits = jnp.einsum("bshd,bmhd->bhsm", qh, kh).astype(jnp.float32) * (HEAD_DIM ** -0.5)
    p = jax.nn.softmax(logits, axis=-1).astype(vh.dtype)
    o = jnp.einsum("bhsm,bmhd->bshd", p, vh)
    return o.reshape(b, s, MEM_WIDTH)


def forgetting_attention(q, k, v, cum_logf):
    b, s, h, dh = q.shape
    scale = dh ** -0.5
    outs = []
    for i in range(s // BLOCK_Q):
        q0 = i * BLOCK_Q
        kend = q0 + BLOCK_Q
        qb = q[:, q0:kend]
        kb = k[:, :kend]
        vb = v[:, :kend]
        logits = jnp.einsum("bqhd,bkhd->bhqk", qb, kb).astype(jnp.float32) * scale
        logits = logits + cum_logf[:, :, q0:kend, None] - cum_logf[:, :, None, :kend]
        qpos = q0 + jnp.arange(BLOCK_Q)
        kpos = jnp.arange(kend)
        logits = jnp.where(kpos[None, :] <= qpos[:, None], logits, -jnp.inf)
        p = jax.nn.softmax(logits, axis=-1).astype(vb.dtype)
        outs.append(jnp.einsum("bhqk,bkhd->bqhd", p, vb))
    return jnp.concatenate(outs, axis=1)


def conv_ffn(h, w_up, w_dw, b_dw, w_down):
    u = h @ w_up
    u = causal_dwconv(u, w_dw, b_dw)
    gate, val = jnp.split(u, 2, axis=-1)
    return (jax.nn.silu(gate) * val) @ w_down


def _fwd_setup_inputs(seed: int = 0) -> dict:
    key = jax.random.key(seed)
    ks = jax.random.split(key, 24)
    f32 = jnp.float32
    nrm = lambda k, shape, scale: jax.random.normal(k, shape, f32) * scale
    gain = lambda k, shape: 1.0 + 0.05 * jax.random.normal(k, shape, f32)
    d = D_MODEL
    return {
        "x": jax.random.normal(ks[0], (BATCH, SEQ, d), f32),
        "mem": jax.random.normal(ks[1], (BATCH, MEM_LEN, d), f32),
        "g_mix": gain(ks[2], (DEPTH, d)),
        "w_in_a": nrm(ks[3], (N_A, d, 2 * CONV_CH + MEM_WIDTH), d ** -0.5),
        "b_glu": nrm(ks[4], (N_A, 2 * CONV_CH), 0.02),
        "w_dw_a": nrm(ks[5], (N_A, CONV_WIDTH, CONV_CH), CONV_WIDTH ** -0.5),
        "b_dw_a": nrm(ks[6], (N_A, CONV_CH), 0.02),
        "ln_g": gain(ks[7], (N_A, CONV_CH)),
        "ln_b": nrm(ks[8], (N_A, CONV_CH), 0.02),
        "g_kv": gain(ks[9], (d,)),
        "w_kvf": nrm(ks[10], (d, 2 * MIX_WIDTH + N_FOX_HEADS), d ** -0.5),
        "b_f": 2.0 + 0.5 * jax.random.normal(ks[11], (N_FOX_HEADS,), f32),
        "w_in_b": nrm(ks[12], (N_B, d, MIX_WIDTH + MEM_WIDTH), d ** -0.5),
        "g_mem": gain(ks[13], (d,)),
        "w_mem_kv": nrm(ks[14], (DEPTH, d, 2 * MEM_WIDTH), d ** -0.5),
        "w_out": nrm(ks[15], (DEPTH, d, d), d ** -0.5),
        "g_ffn": gain(ks[16], (DEPTH, d)),
        "w_up": nrm(ks[17], (DEPTH, d, 2 * D_FF), d ** -0.5),
        "w_dw_f": nrm(ks[18], (DEPTH, FFN_CONV_WIDTH, 2 * D_FF), FFN_CONV_WIDTH ** -0.5),
        "b_dw_f": nrm(ks[19], (DEPTH, 2 * D_FF), 0.02),
        "w_down": nrm(ks[20], (DEPTH, D_FF, d), D_FF ** -0.5),
        "g_final": gain(ks[21], (d,)),
    }


def _fwd_reference(x, mem, g_mix, w_in_a, b_glu, w_dw_a, b_dw_a, ln_g, ln_b, g_kv, w_kvf, b_f,
              w_in_b, g_mem, w_mem_kv, w_out, g_ffn, w_up, w_dw_f, b_dw_f, w_down, g_final):
    bsz, seq, _ = x.shape
    mem_n = rmsnorm(mem, g_mem)
    k_sh = v_sh = cum_logf = None
    for l in range(DEPTH):
        h = rmsnorm(x, g_mix[l])
        mem_k, mem_v = jnp.split(mem_n @ w_mem_kv[l], 2, axis=-1)
        if l < N_A:
            p = h @ w_in_a[l]
            u, q_mem = p[..., :2 * CONV_CH], p[..., 2 * CONV_CH:]
            mix = conformer_conv(u, b_glu[l], w_dw_a[l], b_dw_a[l], ln_g[l], ln_b[l])
        else:
            if l == N_A:
                hk = rmsnorm(x, g_kv)
                kvf = hk @ w_kvf
                k_sh = kvf[..., :MIX_WIDTH].reshape(bsz, seq, N_FOX_HEADS, HEAD_DIM)
                v_sh = kvf[..., MIX_WIDTH:2 * MIX_WIDTH].reshape(bsz, seq, N_FOX_HEADS, HEAD_DIM)
                f_logit = (kvf[..., 2 * MIX_WIDTH:] + b_f).astype(jnp.float32)
                cum_logf = jnp.cumsum(jax.nn.log_sigmoid(f_logit), axis=1).transpose(0, 2, 1)
            p = h @ w_in_b[l - N_A]
            q, q_mem = p[..., :MIX_WIDTH], p[..., MIX_WIDTH:]
            q = q.reshape(bsz, seq, N_FOX_HEADS, HEAD_DIM)
            mix = forgetting_attention(q, k_sh, v_sh, cum_logf).reshape(bsz, seq, MIX_WIDTH)
        mem_o = memory_attention(q_mem, mem_k, mem_v)
        x = x + jnp.concatenate([mix, mem_o], axis=-1) @ w_out[l]
        x = x + conv_ffn(rmsnorm(x, g_ffn[l]), w_up[l], w_dw_f[l], b_dw_f[l], w_down[l])
    return rmsnorm(x, g_final)


import jax as _jax
import jax.numpy as _jnp

TWIN_FORMAT = 'train_step'
FWD_PARAMS = ['x', 'mem', 'g_mix', 'w_in_a', 'b_glu', 'w_dw_a', 'b_dw_a', 'ln_g', 'ln_b', 'g_kv', 'w_kvf', 'b_f', 'w_in_b', 'g_mem', 'w_mem_kv', 'w_out', 'g_ffn', 'w_up', 'w_dw_f', 'b_dw_f', 'w_down', 'g_final']
TWIN_WEIGHTS = ['g_mix', 'w_in_a', 'b_glu', 'w_dw_a', 'b_dw_a', 'ln_g', 'ln_b', 'g_kv', 'w_kvf', 'b_f', 'w_in_b', 'g_mem', 'w_mem_kv', 'w_out', 'g_ffn', 'w_up', 'w_dw_f', 'b_dw_f', 'w_down', 'g_final']
TWIN_DIFF_INPUT = 'x'
TWIN_INPUTS = ['x', 'mem', 'g_mix', 'w_in_a', 'b_glu', 'w_dw_a', 'b_dw_a', 'ln_g', 'ln_b', 'g_kv', 'w_kvf', 'b_f', 'w_in_b', 'g_mem', 'w_mem_kv', 'w_out', 'g_ffn', 'w_up', 'w_dw_f', 'b_dw_f', 'w_down', 'g_final', 'loss_target', 'm_g_mix', 'm_w_in_a', 'm_b_glu', 'm_w_dw_a', 'm_b_dw_a', 'm_ln_g', 'm_ln_b', 'm_g_kv', 'm_w_kvf', 'm_b_f', 'm_w_in_b', 'm_g_mem', 'm_w_mem_kv', 'm_w_out', 'm_g_ffn', 'm_w_up', 'm_w_dw_f', 'm_b_dw_f', 'm_w_down', 'm_g_final', 'v_g_mix', 'v_w_in_a', 'v_b_glu', 'v_w_dw_a', 'v_b_dw_a', 'v_ln_g', 'v_ln_b', 'v_g_kv', 'v_w_kvf', 'v_b_f', 'v_w_in_b', 'v_g_mem', 'v_w_mem_kv', 'v_w_out', 'v_g_ffn', 'v_w_up', 'v_w_dw_f', 'v_b_dw_f', 'v_w_down', 'v_g_final']
TWIN_OUTPUTS = ['loss', 'grad_x', 'grad_g_mix', 'grad_w_in_a', 'grad_b_glu', 'grad_w_dw_a', 'grad_b_dw_a', 'grad_ln_g', 'grad_ln_b', 'grad_g_kv', 'grad_w_kvf', 'grad_b_f', 'grad_w_in_b', 'grad_g_mem', 'grad_w_mem_kv', 'grad_w_out', 'grad_g_ffn', 'grad_w_up', 'grad_w_dw_f', 'grad_b_dw_f', 'grad_w_down', 'grad_g_final', 'delta_g_mix', 'delta_w_in_a', 'delta_b_glu', 'delta_w_dw_a', 'delta_b_dw_a', 'delta_ln_g', 'delta_ln_b', 'delta_g_kv', 'delta_w_kvf', 'delta_b_f', 'delta_w_in_b', 'delta_g_mem', 'delta_w_mem_kv', 'delta_w_out', 'delta_g_ffn', 'delta_w_up', 'delta_w_dw_f', 'delta_b_dw_f', 'delta_w_down', 'delta_g_final', 'new_m_g_mix', 'new_m_w_in_a', 'new_m_b_glu', 'new_m_w_dw_a', 'new_m_b_dw_a', 'new_m_ln_g', 'new_m_ln_b', 'new_m_g_kv', 'new_m_w_kvf', 'new_m_b_f', 'new_m_w_in_b', 'new_m_g_mem', 'new_m_w_mem_kv', 'new_m_w_out', 'new_m_g_ffn', 'new_m_w_up', 'new_m_w_dw_f', 'new_m_b_dw_f', 'new_m_w_down', 'new_m_g_final', 'new_v_g_mix', 'new_v_w_in_a', 'new_v_b_glu', 'new_v_w_dw_a', 'new_v_b_dw_a', 'new_v_ln_g', 'new_v_ln_b', 'new_v_g_kv', 'new_v_w_kvf', 'new_v_b_f', 'new_v_w_in_b', 'new_v_g_mem', 'new_v_w_mem_kv', 'new_v_w_out', 'new_v_g_ffn', 'new_v_w_up', 'new_v_w_dw_f', 'new_v_b_dw_f', 'new_v_w_down', 'new_v_g_final']
TWIN_LEAF_KINDS = {'loss': 'loss', 'grad_x': 'grad_x', 'grad_g_mix': 'grad_w', 'grad_w_in_a': 'grad_w', 'grad_b_glu': 'grad_w', 'grad_w_dw_a': 'grad_w', 'grad_b_dw_a': 'grad_w', 'grad_ln_g': 'grad_w', 'grad_ln_b': 'grad_w', 'grad_g_kv': 'grad_w', 'grad_w_kvf': 'grad_w', 'grad_b_f': 'grad_w', 'grad_w_in_b': 'grad_w', 'grad_g_mem': 'grad_w', 'grad_w_mem_kv': 'grad_w', 'grad_w_out': 'grad_w', 'grad_g_ffn': 'grad_w', 'grad_w_up': 'grad_w', 'grad_w_dw_f': 'grad_w', 'grad_b_dw_f': 'grad_w', 'grad_w_down': 'grad_w', 'grad_g_final': 'grad_w', 'delta_g_mix': 'delta_w', 'delta_w_in_a': 'delta_w', 'delta_b_glu': 'delta_w', 'delta_w_dw_a': 'delta_w', 'delta_b_dw_a': 'delta_w', 'delta_ln_g': 'delta_w', 'delta_ln_b': 'delta_w', 'delta_g_kv': 'delta_w', 'delta_w_kvf': 'delta_w', 'delta_b_f': 'delta_w', 'delta_w_in_b': 'delta_w', 'delta_g_mem': 'delta_w', 'delta_w_mem_kv': 'delta_w', 'delta_w_out': 'delta_w', 'delta_g_ffn': 'delta_w', 'delta_w_up': 'delta_w', 'delta_w_dw_f': 'delta_w', 'delta_b_dw_f': 'delta_w', 'delta_w_down': 'delta_w', 'delta_g_final': 'delta_w', 'new_m_g_mix': 'new_m', 'new_m_w_in_a': 'new_m', 'new_m_b_glu': 'new_m', 'new_m_w_dw_a': 'new_m', 'new_m_b_dw_a': 'new_m', 'new_m_ln_g': 'new_m', 'new_m_ln_b': 'new_m', 'new_m_g_kv': 'new_m', 'new_m_w_kvf': 'new_m', 'new_m_b_f': 'new_m', 'new_m_w_in_b': 'new_m', 'new_m_g_mem': 'new_m', 'new_m_w_mem_kv': 'new_m', 'new_m_w_out': 'new_m', 'new_m_g_ffn': 'new_m', 'new_m_w_up': 'new_m', 'new_m_w_dw_f': 'new_m', 'new_m_b_dw_f': 'new_m', 'new_m_w_down': 'new_m', 'new_m_g_final': 'new_m', 'new_v_g_mix': 'new_v', 'new_v_w_in_a': 'new_v', 'new_v_b_glu': 'new_v', 'new_v_w_dw_a': 'new_v', 'new_v_b_dw_a': 'new_v', 'new_v_ln_g': 'new_v', 'new_v_ln_b': 'new_v', 'new_v_g_kv': 'new_v', 'new_v_w_kvf': 'new_v', 'new_v_b_f': 'new_v', 'new_v_w_in_b': 'new_v', 'new_v_g_mem': 'new_v', 'new_v_w_mem_kv': 'new_v', 'new_v_w_out': 'new_v', 'new_v_g_ffn': 'new_v', 'new_v_w_up': 'new_v', 'new_v_w_dw_f': 'new_v', 'new_v_b_dw_f': 'new_v', 'new_v_w_down': 'new_v', 'new_v_g_final': 'new_v'}


def _forward(args):
    return _fwd_reference(*[args[k] for k in FWD_PARAMS])


def _output_shape():
    out = _jax.eval_shape(lambda: _forward(_fwd_setup_inputs(0)))
    return out.shape, out.dtype

N_MICROBATCH = 1
ADAM_LR = 0.001
ADAM_B1 = 0.9
ADAM_B2 = 0.999
ADAM_EPS = 1e-08
ADAM_WD = 0.01
ADAM_STEP = 10
PER_EXAMPLE_BATCH_AXIS = {'x': 0, 'mem': 0, 'loss_target': 0}
SHARED_INPUTS = []
_WEIGHT_DTYPES = {'g_mix': _jnp.float32, 'w_in_a': _jnp.float32, 'b_glu': _jnp.float32, 'w_dw_a': _jnp.float32, 'b_dw_a': _jnp.float32, 'ln_g': _jnp.float32, 'ln_b': _jnp.float32, 'g_kv': _jnp.float32, 'w_kvf': _jnp.float32, 'b_f': _jnp.float32, 'w_in_b': _jnp.float32, 'g_mem': _jnp.float32, 'w_mem_kv': _jnp.float32, 'w_out': _jnp.float32, 'g_ffn': _jnp.float32, 'w_up': _jnp.float32, 'w_dw_f': _jnp.float32, 'b_dw_f': _jnp.float32, 'w_down': _jnp.float32, 'g_final': _jnp.float32}
MOMENT_SCALE = {'g_mix': 6.023884e-02, 'w_in_a': 6.200078e-02, 'b_glu': 8.854534e-02, 'w_dw_a': 8.901528e-02, 'b_dw_a': 2.238489e-01, 'ln_g': 1.203466e-01, 'ln_b': 1.216923e-01, 'g_kv': 5.922682e-02, 'w_kvf': 4.727138e-02, 'b_f': 2.097092e-01, 'w_in_b': 2.194228e-02, 'g_mem': 1.885281e-02, 'w_mem_kv': 1.307737e-02, 'w_out': 6.253993e-02, 'g_ffn': 8.444473e-02, 'w_up': 3.615887e-02, 'w_dw_f': 3.657273e-02, 'b_dw_f': 3.978242e-02, 'w_down': 5.935237e-02, 'g_final': 1.609418e+01}


def _to_microbatches(a, axis):
    t = _jnp.moveaxis(a, axis, 0)
    t = t.reshape((N_MICROBATCH, t.shape[0] // N_MICROBATCH) + t.shape[1:])
    return _jnp.moveaxis(t, 1, axis + 1)


def setup_inputs(seed: int = 0) -> dict:
    inp = _fwd_setup_inputs(seed)
    key = _jax.random.fold_in(_jax.random.key(seed), 7919)
    shape, _ = _output_shape()
    out = dict(inp)
    out["loss_target"] = _jax.random.normal(_jax.random.fold_in(key, 0), shape, _jnp.float32)
    for i, name in enumerate(TWIN_WEIGHTS):
        w = inp[name].astype(_jnp.float32)
        if MOMENT_SCALE is None:
            s = _jnp.sqrt(_jnp.mean(_jnp.square(w)) + 1e-30)
        else:
            s = MOMENT_SCALE[name]
        km, kv = _jax.random.split(_jax.random.fold_in(key, i + 1))
        out[name] = w
        out["m_" + name] = s * _jax.random.normal(km, w.shape, _jnp.float32)
        out["v_" + name] = (s * s) * _jax.random.uniform(kv, w.shape, _jnp.float32, 0.5, 1.5)
    if N_MICROBATCH > 1:
        for name, axis in PER_EXAMPLE_BATCH_AXIS.items():
            out[name] = _to_microbatches(out[name], axis)
    return {'x': out['x'], 'mem': out['mem'], 'g_mix': out['g_mix'], 'w_in_a': out['w_in_a'], 'b_glu': out['b_glu'], 'w_dw_a': out['w_dw_a'], 'b_dw_a': out['b_dw_a'], 'ln_g': out['ln_g'], 'ln_b': out['ln_b'], 'g_kv': out['g_kv'], 'w_kvf': out['w_kvf'], 'b_f': out['b_f'], 'w_in_b': out['w_in_b'], 'g_mem': out['g_mem'], 'w_mem_kv': out['w_mem_kv'], 'w_out': out['w_out'], 'g_ffn': out['g_ffn'], 'w_up': out['w_up'], 'w_dw_f': out['w_dw_f'], 'b_dw_f': out['b_dw_f'], 'w_down': out['w_down'], 'g_final': out['g_final'], 'loss_target': out['loss_target'], 'm_g_mix': out['m_g_mix'], 'm_w_in_a': out['m_w_in_a'], 'm_b_glu': out['m_b_glu'], 'm_w_dw_a': out['m_w_dw_a'], 'm_b_dw_a': out['m_b_dw_a'], 'm_ln_g': out['m_ln_g'], 'm_ln_b': out['m_ln_b'], 'm_g_kv': out['m_g_kv'], 'm_w_kvf': out['m_w_kvf'], 'm_b_f': out['m_b_f'], 'm_w_in_b': out['m_w_in_b'], 'm_g_mem': out['m_g_mem'], 'm_w_mem_kv': out['m_w_mem_kv'], 'm_w_out': out['m_w_out'], 'm_g_ffn': out['m_g_ffn'], 'm_w_up': out['m_w_up'], 'm_w_dw_f': out['m_w_dw_f'], 'm_b_dw_f': out['m_b_dw_f'], 'm_w_down': out['m_w_down'], 'm_g_final': out['m_g_final'], 'v_g_mix': out['v_g_mix'], 'v_w_in_a': out['v_w_in_a'], 'v_b_glu': out['v_b_glu'], 'v_w_dw_a': out['v_w_dw_a'], 'v_b_dw_a': out['v_b_dw_a'], 'v_ln_g': out['v_ln_g'], 'v_ln_b': out['v_ln_b'], 'v_g_kv': out['v_g_kv'], 'v_w_kvf': out['v_w_kvf'], 'v_b_f': out['v_b_f'], 'v_w_in_b': out['v_w_in_b'], 'v_g_mem': out['v_g_mem'], 'v_w_mem_kv': out['v_w_mem_kv'], 'v_w_out': out['v_w_out'], 'v_g_ffn': out['v_g_ffn'], 'v_w_up': out['v_w_up'], 'v_w_dw_f': out['v_w_dw_f'], 'v_b_dw_f': out['v_b_dw_f'], 'v_w_down': out['v_w_down'], 'v_g_final': out['v_g_final']}


def _loss(weights, diff, rest, loss_target):
    with _jax.named_scope("forward"):
        args = {**rest, TWIN_DIFF_INPUT: diff, **{k: w.astype(_WEIGHT_DTYPES[k]) for k, w in weights.items()}}
        y = _forward(args)
    with _jax.named_scope("loss_head"):
        err = _jnp.square(y.astype(_jnp.float32) - loss_target)
        return 0.5 * _jnp.sum(_jnp.mean(err, axis=-1)) if err.ndim else 0.5 * err


def _adamw(w, g, m, v):
    m = ADAM_B1 * m + (1.0 - ADAM_B1) * g
    v = ADAM_B2 * v + (1.0 - ADAM_B2) * _jnp.square(g)
    m_hat = m / (1.0 - ADAM_B1 ** ADAM_STEP)
    v_hat = v / (1.0 - ADAM_B2 ** ADAM_STEP)
    delta = -ADAM_LR * (m_hat / (_jnp.sqrt(v_hat) + ADAM_EPS) + ADAM_WD * w)
    return delta, m, v


def reference(x, mem, g_mix, w_in_a, b_glu, w_dw_a, b_dw_a, ln_g, ln_b, g_kv, w_kvf, b_f, w_in_b, g_mem, w_mem_kv, w_out, g_ffn, w_up, w_dw_f, b_dw_f, w_down, g_final, loss_target, m_g_mix, m_w_in_a, m_b_glu, m_w_dw_a, m_b_dw_a, m_ln_g, m_ln_b, m_g_kv, m_w_kvf, m_b_f, m_w_in_b, m_g_mem, m_w_mem_kv, m_w_out, m_g_ffn, m_w_up, m_w_dw_f, m_b_dw_f, m_w_down, m_g_final, v_g_mix, v_w_in_a, v_b_glu, v_w_dw_a, v_b_dw_a, v_ln_g, v_ln_b, v_g_kv, v_w_kvf, v_b_f, v_w_in_b, v_g_mem, v_w_mem_kv, v_w_out, v_g_ffn, v_w_up, v_w_dw_f, v_b_dw_f, v_w_down, v_g_final):
    given = dict(x=x, mem=mem, g_mix=g_mix, w_in_a=w_in_a, b_glu=b_glu, w_dw_a=w_dw_a, b_dw_a=b_dw_a, ln_g=ln_g, ln_b=ln_b, g_kv=g_kv, w_kvf=w_kvf, b_f=b_f, w_in_b=w_in_b, g_mem=g_mem, w_mem_kv=w_mem_kv, w_out=w_out, g_ffn=g_ffn, w_up=w_up, w_dw_f=w_dw_f, b_dw_f=b_dw_f, w_down=w_down, g_final=g_final, loss_target=loss_target, m_g_mix=m_g_mix, m_w_in_a=m_w_in_a, m_b_glu=m_b_glu, m_w_dw_a=m_w_dw_a, m_b_dw_a=m_b_dw_a, m_ln_g=m_ln_g, m_ln_b=m_ln_b, m_g_kv=m_g_kv, m_w_kvf=m_w_kvf, m_b_f=m_b_f, m_w_in_b=m_w_in_b, m_g_mem=m_g_mem, m_w_mem_kv=m_w_mem_kv, m_w_out=m_w_out, m_g_ffn=m_g_ffn, m_w_up=m_w_up, m_w_dw_f=m_w_dw_f, m_b_dw_f=m_b_dw_f, m_w_down=m_w_down, m_g_final=m_g_final, v_g_mix=v_g_mix, v_w_in_a=v_w_in_a, v_b_glu=v_b_glu, v_w_dw_a=v_w_dw_a, v_b_dw_a=v_b_dw_a, v_ln_g=v_ln_g, v_ln_b=v_ln_b, v_g_kv=v_g_kv, v_w_kvf=v_w_kvf, v_b_f=v_b_f, v_w_in_b=v_w_in_b, v_g_mem=v_g_mem, v_w_mem_kv=v_w_mem_kv, v_w_out=v_w_out, v_g_ffn=v_g_ffn, v_w_up=v_w_up, v_w_dw_f=v_w_dw_f, v_b_dw_f=v_b_dw_f, v_w_down=v_w_down, v_g_final=v_g_final)
    weights = {n: given[n] for n in TWIN_WEIGHTS}
    shared = {n: given[n] for n in SHARED_INPUTS}
    per_example = {n: given[n] for n in ['x', 'mem']}
    grad_fn = _jax.value_and_grad(_loss, argnums=(0, 1))

    def one_microbatch(ex, loss_target):
        ex = dict(ex)
        diff = ex.pop(TWIN_DIFF_INPUT)
        return grad_fn(weights, diff, {**shared, **ex}, loss_target)

    if N_MICROBATCH == 1:
        loss, (grad_w, grad_x) = one_microbatch(per_example, given["loss_target"])
    else:
        def body(carry, xs):
            loss_sum, grad_sum = carry
            l_k, (gw_k, gx_k) = one_microbatch(xs[0], xs[1])
            with _jax.named_scope("update"):
                return (loss_sum + l_k, _jax.tree.map(_jnp.add, grad_sum, gw_k)), gx_k

        init = (_jnp.zeros((), _jnp.float32), _jax.tree.map(_jnp.zeros_like, weights))
        (loss, grad_w), grad_x = _jax.lax.scan(body, init, (per_example, given["loss_target"]))
    with _jax.named_scope("update"):
        delta_w, new_m, new_v = {}, {}, {}
        for n in TWIN_WEIGHTS:
            delta_w[n], new_m[n], new_v[n] = _adamw(weights[n], grad_w[n], given["m_" + n], given["v_" + n])
    return (loss, grad_x, *[grad_w[n] for n in TWIN_WEIGHTS], *[delta_w[n] for n in TWIN_WEIGHTS],
            *[new_m[n] for n in TWIN_WEIGHTS], *[new_v[n] for n in TWIN_WEIGHTS])
```

```python
import jax
import jax.numpy as jnp
from jax import lax
from jax.experimental import pallas as pl
from jax.experimental.pallas import tpu as pltpu

F32 = jnp.float32
MXU_DT = jnp.bfloat16
COMM_DT = jnp.bfloat16

N_DEV = 8
HEAD_DIM = 64
RMS_EPS = 1e-6
LN_EPS = 1e-5
ADAM_LR = 0.001
ADAM_B1 = 0.9
ADAM_B2 = 0.999
ADAM_EPS = 1e-08
ADAM_WD = 0.01
ADAM_STEP = 10

PACK_LANES = 1024
GATE_LANES = 128
VMEM_LIMIT_V7X = 56 << 20
NEG = -1e30
MESH = pl.DeviceIdType.MESH
ANY = pl.BlockSpec(memory_space=pl.ANY)
NT = (((1,), (1,)), ((), ()))
NN = (((1,), (0,)), ((), ()))

BIG = (("w_in_a", 2), ("w_kvf", 0), ("w_in_b", 1), ("w_mem_kv", 1), ("w_out", 1), ("w_up", 2), ("w_down", 1))
SMALL = (("b_glu", 1), ("w_dw_a", 2), ("b_dw_a", 1), ("ln_g", 1), ("ln_b", 1), ("w_dw_f", 2))
REP = ("g_mix", "g_kv", "b_f", "g_mem", "g_ffn", "b_dw_f", "g_final")
WEIGHTS = ("g_mix", "w_in_a", "b_glu", "w_dw_a", "b_dw_a", "ln_g", "ln_b", "g_kv", "w_kvf", "b_f", "w_in_b",
           "g_mem", "w_mem_kv", "w_out", "g_ffn", "w_up", "w_dw_f", "b_dw_f", "w_down", "g_final")


def _sds(shape, dtype):
    return jax.ShapeDtypeStruct(tuple(shape), dtype)


def _call(body, *, name, out_shape, grid=(), in_specs=None, out_specs=None, scratch_shapes=()):
    params = dict(vmem_limit_bytes=VMEM_LIMIT_V7X)
    if grid:
        params["dimension_semantics"] = ("arbitrary",) * len(grid)
    kw = {}
    if in_specs is not None:
        kw["in_specs"] = in_specs
    if out_specs is not None:
        kw["out_specs"] = out_specs
    return pl.pallas_call(body, name=name, grid=grid, out_shape=out_shape, scratch_shapes=list(scratch_shapes),
                          compiler_params=pltpu.CompilerParams(**params), **kw)


def _res(shape):
    nd = len(shape)
    return pl.BlockSpec(tuple(shape), lambda *_: (0,) * nd)


def _colblk(rows, tc, off=0):
    return pl.BlockSpec((rows, tc), lambda j: (0, j + off))


def _rowblk(tm, cols, off=0):
    return pl.BlockSpec((tm, cols), lambda i: (i, off))


def _pick(n, opts=(512, 256, 128)):
    for t in opts:
        if n % t == 0:
            return t
    return n


def _sig(z):
    return 1.0 / (1.0 + jnp.exp(-z))


def _dot(a, b, dims=NN):
    return lax.dot_general(a, b, dims, preferred_element_type=F32)


def _dot_tn(a, b):
    return _dot(a.T.astype(b.dtype), b)


def _mm(a, w, *, name, out_dtype, nt=False, add=None, mode="n"):
    M, K = a.shape
    N = w.shape[0] if nt else w.shape[1]
    assert (w.shape[1] if nt else w.shape[0]) == K
    dims = NT if nt else NN
    has_add = add is not None
    if mode == "n":
        tn, rc = _pick(N), _pick(M)

        def body(*refs):
            a_ref, w_ref, o_ref = refs[0], refs[1], refs[-1]
            wv = w_ref[...]
            for r0 in range(0, M, rc):
                acc = _dot(a_ref[r0:r0 + rc, :], wv, dims)
                if has_add:
                    acc = acc + refs[2][r0:r0 + rc, :]
                o_ref[r0:r0 + rc, :] = acc.astype(out_dtype)

        w_spec = pl.BlockSpec((tn, K), lambda j: (j, 0)) if nt else pl.BlockSpec((K, tn), lambda j: (0, j))
        in_specs = [_res((M, K)), w_spec] + ([_colblk(M, tn)] if has_add else [])
        out_specs, grid = _colblk(M, tn), (N // tn,)
    else:
        tm, nc = _pick(M), _pick(N)

        def body(*refs):
            a_ref, w_ref, o_ref = refs[0], refs[1], refs[-1]
            av = a_ref[...]
            for n0 in range(0, N, nc):
                wv = w_ref[n0:n0 + nc, :] if nt else w_ref[:, n0:n0 + nc]
                acc = _dot(av, wv, dims)
                if has_add:
                    acc = acc + refs[2][:, n0:n0 + nc]
                o_ref[:, n0:n0 + nc] = acc.astype(out_dtype)

        in_specs = [_rowblk(tm, K), _res(w.shape)] + ([_rowblk(tm, N)] if has_add else [])
        out_specs, grid = _rowblk(tm, N), (M // tm,)
    args = (a, w) + ((add,) if has_add else ())
    return _call(body, name=name, grid=grid, in_specs=in_specs, out_specs=out_specs,
                 out_shape=_sds((M, N), out_dtype))(*args)


def _rms_fwd(x, g, *, name):
    M, D = x.shape
    tm = _pick(M, (256, 128))

    def body(x_ref, g_ref, h_ref):
        xf = x_ref[...]
        r = lax.rsqrt(jnp.mean(xf * xf, axis=-1, keepdims=True) + RMS_EPS)
        h_ref[...] = ((xf * r) * g_ref[...]).astype(MXU_DT)

    return _call(body, name=name, grid=(M // tm,), in_specs=[_rowblk(tm, D), _res((1, D))],
                 out_specs=_rowblk(tm, D), out_shape=_sds((M, D), MXU_DT))(x, g.reshape(1, D))


def _rms_bwd(x, g, dh, dx_in, *, name):
    M, D = x.shape
    tm = _pick(M, (256, 128))
    with_dx = dx_in is not None

    def body(*refs):
        if with_dx:
            x_ref, g_ref, dh_ref, dxin_ref, dx_ref, dg_ref = refs
        else:
            x_ref, g_ref, dh_ref, dg_ref = refs
        i = pl.program_id(0)
        xf = x_ref[...]
        r = lax.rsqrt(jnp.mean(xf * xf, axis=-1, keepdims=True) + RMS_EPS)
        y = xf * r
        dh_v = dh_ref[...]
        if with_dx:
            dy = dh_v * g_ref[...]
            dx_ref[...] = dxin_ref[...] + r * (dy - y * jnp.mean(dy * y, axis=-1, keepdims=True))
        part = jnp.sum(dh_v * y, axis=0, keepdims=True)

        @pl.when(i == 0)
        def _():
            dg_ref[...] = part

        @pl.when(i > 0)
        def _():
            dg_ref[...] += part

    ins = [x, g.reshape(1, D), dh] + ([dx_in] if with_dx else [])
    in_specs = [_rowblk(tm, D), _res((1, D)), _rowblk(tm, D)] + ([_rowblk(tm, D)] if with_dx else [])
    if with_dx:
        out_specs, out_shape = [_rowblk(tm, D), _res((1, D))], [_sds((M, D), F32), _sds((1, D), F32)]
    else:
        out_specs, out_shape = [_res((1, D))], [_sds((1, D), F32)]
    out = _call(body, name=name, grid=(M // tm,), in_specs=in_specs, out_specs=out_specs, out_shape=out_shape)(*ins)
    return out if with_dx else (None, out[0])


def _loss_bwd(x, g, t, *, name):
    M, D = x.shape
    tm = _pick(M, (256, 128))

    def body(x_ref, g_ref, t_ref, dx_ref, dg_ref, ls_ref):
        i = pl.program_id(0)
        xf = x_ref[...]
        r = lax.rsqrt(jnp.mean(xf * xf, axis=-1, keepdims=True) + RMS_EPS)
        xr = xf * r
        e = xr * g_ref[...] - t_ref[...]
        dout = e * (1.0 / D)
        dy = dout * g_ref[...]
        dx_ref[...] = r * (dy - xr * jnp.mean(dy * xr, axis=-1, keepdims=True))
        part = jnp.sum(dout * xr, axis=0, keepdims=True)
        lpart = jnp.zeros(ls_ref.shape, F32) + (0.5 / D) * jnp.sum(e * e, keepdims=True)

        @pl.when(i == 0)
        def _():
            dg_ref[...] = part
            ls_ref[...] = lpart

        @pl.when(i > 0)
        def _():
            dg_ref[...] += part
            ls_ref[...] += lpart

    dx, dg, ls = _call(body, name=name, grid=(M // tm,),
                       in_specs=[_rowblk(tm, D), _res((1, D)), _rowblk(tm, D)],
                       out_specs=[_rowblk(tm, D), _res((1, D)), _res((8, 128))],
                       out_shape=[_sds((M, D), F32), _sds((1, D), F32), _sds((8, 128), F32)])(x, g.reshape(1, D), t)
    return ls[0, 0], dx, dg


def _shift_rows(ext, off, rows):
    if off % 8 == 0:
        return ext[off:off + rows, :]
    return pltpu.roll(ext, ext.shape[0] - off, 0)[0:rows, :]


def _ext(pad_ref, c, rows, halo):
    return pad_ref[pl.ds(pl.multiple_of(c * rows, rows), rows + halo), :]


def _conv_chunk(pad_ref, c, rows, halo, w_ref, taps):
    ext = _ext(pad_ref, c, rows, halo)
    acc = None
    for k in range(taps):
        term = w_ref[k:k + 1, :] * _shift_rows(ext, halo - (taps - 1) + k, rows)
        acc = term if acc is None else acc + term
    return acc


def _conv_t_chunk(pad_ref, c, rows, halo, w_ref, taps):
    ext = _ext(pad_ref, c, rows, halo)
    acc = None
    for k in range(taps):
        term = w_ref[k:k + 1, :] * _shift_rows(ext, taps - 1 - k, rows)
        acc = term if acc is None else acc + term
    return acc


def _conv_wgrad_chunk(pad_ref, c, rows, halo, dy, dw_ref, taps):
    ext = _ext(pad_ref, c, rows, halo)
    for k in range(taps):
        dw_ref[k:k + 1, :] += jnp.sum(dy * _shift_rows(ext, halo - (taps - 1) + k, rows), axis=0, keepdims=True)


A_HALO, F_HALO = 32, 8


def _glu_conv_fwd(p, b_glu, w_dw, b_dw, *, name):
    S = p.shape[0]
    taps, C = w_dw.shape
    tc, rows = 128, _pick(S, (256, 128))
    nb, nch = C // tc, S // rows

    def body(a_ref, g_ref, ba_ref, bg_ref, w_ref, bd_ref, o_ref, pad_ref):
        pad_ref[0:A_HALO, :] = jnp.zeros((A_HALO, tc), F32)

        def fill(c, _):
            r = pl.ds(pl.multiple_of(c * rows, rows), rows)
            v1 = (a_ref[r, :] + ba_ref[...]) * _sig(g_ref[r, :] + bg_ref[...])
            pad_ref[pl.ds(pl.multiple_of(A_HALO + c * rows, 8), rows), :] = v1
            return 0

        lax.fori_loop(0, nch, fill, 0)

        def conv(c, _):
            o_ref[pl.ds(pl.multiple_of(c * rows, rows), rows), :] = (
                _conv_chunk(pad_ref, c, rows, A_HALO, w_ref, taps) + bd_ref[...])
            return 0

        lax.fori_loop(0, nch, conv, 0)

    b2 = b_glu.reshape(1, 2 * C)
    return _call(body, name=name, grid=(nb,),
                 in_specs=[_colblk(S, tc), _colblk(S, tc, nb), _colblk(1, tc), _colblk(1, tc, nb),
                           _colblk(taps, tc), _colblk(1, tc)],
                 out_specs=_colblk(S, tc), out_shape=_sds((S, C), F32),
                 scratch_shapes=[pltpu.VMEM((S + A_HALO, tc), F32)])(p, p, b2, b2, w_dw, b_dw.reshape(1, C))


def _glu_conv_bwd(p, b_glu, w_dw, dv2, *, name):
    S = p.shape[0]
    taps, C = w_dw.shape
    tc, rows = 128, _pick(S, (256, 128))
    nb, nch = C // tc, S // rows

    def body(a_ref, g_ref, ba_ref, bg_ref, w_ref, dy_ref, da_ref, dgt_ref, dw_ref, dbd_ref, dba_ref, dbg_ref,
             padx_ref, pady_ref):
        padx_ref[0:A_HALO, :] = jnp.zeros((A_HALO, tc), F32)
        pady_ref[S:S + A_HALO, :] = jnp.zeros((A_HALO, tc), F32)
        dw_ref[...] = jnp.zeros((taps, tc), F32)

        def fill(c, _):
            r = pl.ds(pl.multiple_of(c * rows, rows), rows)
            v1 = (a_ref[r, :] + ba_ref[...]) * _sig(g_ref[r, :] + bg_ref[...])
            padx_ref[pl.ds(pl.multiple_of(A_HALO + c * rows, 8), rows), :] = v1
            pady_ref[r, :] = dy_ref[r, :]
            return 0

        lax.fori_loop(0, nch, fill, 0)

        def back(c, carry):
            sd, sa, sg = carry
            r = pl.ds(pl.multiple_of(c * rows, rows), rows)
            dy = dy_ref[r, :]
            _conv_wgrad_chunk(padx_ref, c, rows, A_HALO, dy, dw_ref, taps)
            dv1 = _conv_t_chunk(pady_ref, c, rows, A_HALO, w_ref, taps)
            a = a_ref[r, :] + ba_ref[...]
            s = _sig(g_ref[r, :] + bg_ref[...])
            da = dv1 * s
            dgt = dv1 * a * s * (1.0 - s)
            da_ref[r, :] = da
            dgt_ref[r, :] = dgt
            return (sd + jnp.sum(dy, axis=0, keepdims=True), sa + jnp.sum(da, axis=0, keepdims=True),
                    sg + jnp.sum(dgt, axis=0, keepdims=True))

        z = jnp.zeros((1, tc), F32)
        sd, sa, sg = lax.fori_loop(0, nch, back, (z, z, z))
        dbd_ref[...] = sd
        dba_ref[...] = sa
        dbg_ref[...] = sg

    b2 = b_glu.reshape(1, 2 * C)
    return _call(body, name=name, grid=(nb,),
                 in_specs=[_colblk(S, tc), _colblk(S, tc, nb), _colblk(1, tc), _colblk(1, tc, nb),
                           _colblk(taps, tc), _colblk(S, tc)],
                 out_specs=[_colblk(S, tc), _colblk(S, tc), _colblk(taps, tc), _colblk(1, tc), _colblk(1, tc),
                            _colblk(1, tc)],
                 out_shape=[_sds((S, C), F32), _sds((S, C), F32), _sds((taps, C), F32), _sds((1, C), F32),
                            _sds((1, C), F32), _sds((1, C), F32)],
                 scratch_shapes=[pltpu.VMEM((S + A_HALO, tc), F32), pltpu.VMEM((S + A_HALO, tc), F32)])(
                     p, p, b2, b2, w_dw, dv2)


def _ln_silu_cat_fwd(v2, ln_g, ln_b, memo, *, name):
    S, C = v2.shape
    Mw = memo.shape[1]
    tm = _pick(S, (256, 128))

    def body(v_ref, g_ref, b_ref, m_ref, o_ref):
        v = v_ref[...]
        mu = jnp.mean(v, axis=-1, keepdims=True)
        d = v - mu
        y = d * lax.rsqrt(jnp.mean(d * d, axis=-1, keepdims=True) + LN_EPS) * g_ref[...] + b_ref[...]
        o_ref[:, 0:C] = (y * _sig(y)).astype(MXU_DT)
        o_ref[:, C:C + Mw] = m_ref[...].astype(MXU_DT)

    return _call(body, name=name, grid=(S // tm,),
                 in_specs=[_rowblk(tm, C), _res((1, C)), _res((1, C)), _rowblk(tm, Mw)],
                 out_specs=_rowblk(tm, C + Mw), out_shape=_sds((S, C + Mw), MXU_DT))(
                     v2, ln_g.reshape(1, C), ln_b.reshape(1, C), memo)


def _ln_silu_bwd(v2, ln_g, ln_b, dcat, *, name):
    S, C = v2.shape
    tm = _pick(S, (256, 128))

    def body(v_ref, g_ref, b_ref, dm_ref, dv_ref, dg_ref, db_ref):
        i = pl.program_id(0)
        v = v_ref[...]
        mu = jnp.mean(v, axis=-1, keepdims=True)
        d = v - mu
        rstd = lax.rsqrt(jnp.mean(d * d, axis=-1, keepdims=True) + LN_EPS)
        xh = d * rstd
        y = xh * g_ref[...] + b_ref[...]
        s = _sig(y)
        dyv = dm_ref[...] * (s * (1.0 + y * (1.0 - s)))
        dxh = dyv * g_ref[...]
        dv_ref[...] = rstd * (dxh - jnp.mean(dxh, axis=-1, keepdims=True)
                              - xh * jnp.mean(dxh * xh, axis=-1, keepdims=True))
        pg = jnp.sum(dyv * xh, axis=0, keepdims=True)
        pb = jnp.sum(dyv, axis=0, keepdims=True)

        @pl.when(i == 0)
        def _():
            dg_ref[...] = pg
            db_ref[...] = pb

        @pl.when(i > 0)
        def _():
            dg_ref[...] += pg
            db_ref[...] += pb

    return _call(body, name=name, grid=(S // tm,),
                 in_specs=[_rowblk(tm, C), _res((1, C)), _res((1, C)), _rowblk(tm, C)],
                 out_specs=[_rowblk(tm, C), _res((1, C)), _res((1, C))],
                 out_shape=[_sds((S, C), F32), _sds((1, C), F32), _sds((1, C), F32)])(
                     v2, ln_g.reshape(1, C), ln_b.reshape(1, C), dcat)


def _ffn_act_fwd(ug, uv, w_dw, b_dw, *, name):
    S, Fw = ug.shape
    taps = w_dw.shape[0]
    tc, rows = _pick(Fw, (256, 128)), _pick(S, (256, 128))
    nb, nch = Fw // tc, S // rows

    def body(ug_ref, uv_ref, wg_ref, wv_ref, bg_ref, bv_ref, o_ref, pg_ref, pv_ref):
        pg_ref[0:F_HALO, :] = jnp.zeros((F_HALO, tc), F32)
        pv_ref[0:F_HALO, :] = jnp.zeros((F_HALO, tc), F32)
        pg_ref[F_HALO:F_HALO + S, :] = ug_ref[...]
        pv_ref[F_HALO:F_HALO + S, :] = uv_ref[...]

        def act(c, _):
            gc = _conv_chunk(pg_ref, c, rows, F_HALO, wg_ref, taps) + bg_ref[...]
            vc = _conv_chunk(pv_ref, c, rows, F_HALO, wv_ref, taps) + bv_ref[...]
            o_ref[pl.ds(pl.multiple_of(c * rows, rows), rows), :] = (gc * _sig(gc) * vc).astype(MXU_DT)
            return 0

        lax.fori_loop(0, nch, act, 0)

    b2 = b_dw.reshape(1, 2 * Fw)
    return _call(body, name=name, grid=(nb,),
                 in_specs=[_colblk(S, tc), _colblk(S, tc), _colblk(taps, tc), _colblk(taps, tc, nb),
                           _colblk(1, tc), _colblk(1, tc, nb)],
                 out_specs=_colblk(S, tc), out_shape=_sds((S, Fw), MXU_DT),
                 scratch_shapes=[pltpu.VMEM((S + F_HALO, tc), F32), pltpu.VMEM((S + F_HALO, tc), F32)])(
                     ug, uv, w_dw, w_dw, b2, b2)


def _ffn_act_bwd(ug, uv, dact, w_dw, b_dw, *, name):
    S, Fw = ug.shape
    taps = w_dw.shape[0]
    tc, rows = _pick(Fw, (256, 128)), _pick(S, (256, 128))
    nb, nch = Fw // tc, S // rows

    def body(ug_ref, uv_ref, da_ref, wg_ref, wv_ref, bg_ref, bv_ref, dug_ref, duv_ref, dwg_ref, dwv_ref,
             dbg_ref, dbv_ref, pg_ref, pv_ref, qg_ref, qv_ref):
        pg_ref[0:F_HALO, :] = jnp.zeros((F_HALO, tc), F32)
        pv_ref[0:F_HALO, :] = jnp.zeros((F_HALO, tc), F32)
        qg_ref[S:S + F_HALO, :] = jnp.zeros((F_HALO, tc), F32)
        qv_ref[S:S + F_HALO, :] = jnp.zeros((F_HALO, tc), F32)
        pg_ref[F_HALO:F_HALO + S, :] = ug_ref[...]
        pv_ref[F_HALO:F_HALO + S, :] = uv_ref[...]
        dwg_ref[...] = jnp.zeros((taps, tc), F32)
        dwv_ref[...] = jnp.zeros((taps, tc), F32)

        def grads(c, carry):
            sg, sv = carry
            r = pl.ds(pl.multiple_of(c * rows, rows), rows)
            gc = _conv_chunk(pg_ref, c, rows, F_HALO, wg_ref, taps) + bg_ref[...]
            vc = _conv_chunk(pv_ref, c, rows, F_HALO, wv_ref, taps) + bv_ref[...]
            s = _sig(gc)
            da = da_ref[r, :]
            dgc = da * vc * (s * (1.0 + gc * (1.0 - s)))
            dvc = da * (gc * s)
            qg_ref[r, :] = dgc
            qv_ref[r, :] = dvc
            _conv_wgrad_chunk(pg_ref, c, rows, F_HALO, dgc, dwg_ref, taps)
            _conv_wgrad_chunk(pv_ref, c, rows, F_HALO, dvc, dwv_ref, taps)
            return sg + jnp.sum(dgc, axis=0, keepdims=True), sv + jnp.sum(dvc, axis=0, keepdims=True)

        z = jnp.zeros((1, tc), F32)
        sg, sv = lax.fori_loop(0, nch, grads, (z, z))
        dbg_ref[...] = sg
        dbv_ref[...] = sv

        def back(c, _):
            r = pl.ds(pl.multiple_of(c * rows, rows), rows)
            dug_ref[r, :] = _conv_t_chunk(qg_ref, c, rows, F_HALO, wg_ref, taps).astype(MXU_DT)
            duv_ref[r, :] = _conv_t_chunk(qv_ref, c, rows, F_HALO, wv_ref, taps).astype(MXU_DT)
            return 0

        lax.fori_loop(0, nch, back, 0)

    b2 = b_dw.reshape(1, 2 * Fw)
    pad = pltpu.VMEM((S + F_HALO, tc), F32)
    dug, duv, dwg, dwv, dbg, dbv = _call(
        body, name=name, grid=(nb,),
        in_specs=[_colblk(S, tc), _colblk(S, tc), _colblk(S, tc), _colblk(taps, tc), _colblk(taps, tc, nb),
                  _colblk(1, tc), _colblk(1, tc, nb)],
        out_specs=[_colblk(S, tc), _colblk(S, tc), _colblk(taps, tc), _colblk(taps, tc), _colblk(1, tc),
                   _colblk(1, tc)],
        out_shape=[_sds((S, Fw), MXU_DT), _sds((S, Fw), MXU_DT), _sds((taps, Fw), F32), _sds((taps, Fw), F32),
                   _sds((1, Fw), F32), _sds((1, Fw), F32)],
        scratch_shapes=[pad, pad, pad, pad])(ug, uv, dact, w_dw, w_dw, b2, b2)
    return dug, duv, jnp.concatenate([dwg, dwv], axis=1), jnp.concatenate([dbg, dbv], axis=1)


def _head_mask(h, width):
    lane = lax.broadcasted_iota(jnp.int32, (1, width), 1)
    return (lane >= h * HEAD_DIM) & (lane < (h + 1) * HEAD_DIM)


def _mem_attn_fwd(p, qblk, mkv, l, *, Mw, name):
    S, ML = p.shape[0], mkv.shape[0]
    tm, nh, scale = _pick(S, (256, 128)), Mw // HEAD_DIM, HEAD_DIM ** -0.5

    def body(q_ref, k_ref, v_ref, o_ref):
        q, kv, vv = q_ref[...], k_ref[...], v_ref[...]
        out = jnp.zeros((tm, Mw), F32)
        for h in range(nh):
            mk = _head_mask(h, Mw)
            s = _dot(jnp.where(mk, q, 0.0).astype(MXU_DT), kv, NT) * scale
            e = jnp.exp(s - jnp.max(s, axis=-1, keepdims=True))
            pr = e / jnp.sum(e, axis=-1, keepdims=True)
            out = out + _dot(pr.astype(MXU_DT), jnp.where(mk, vv, jnp.zeros_like(vv)))
        o_ref[...] = out

    return _call(body, name=name, grid=(S // tm,),
                 in_specs=[_rowblk(tm, Mw, qblk), pl.BlockSpec((ML, Mw), lambda i: (0, 2 * l)),
                           pl.BlockSpec((ML, Mw), lambda i: (0, 2 * l + 1))],
                 out_specs=_rowblk(tm, Mw), out_shape=_sds((S, Mw), F32))(p, mkv, mkv)


def _mem_attn_bwd(p, qblk, mkv, l, dcat, doblk, *, Mw, name):
    S, ML = p.shape[0], mkv.shape[0]
    tm, nh, scale = _pick(S, (256, 128)), Mw // HEAD_DIM, HEAD_DIM ** -0.5

    def body(q_ref, k_ref, v_ref, do_ref, dq_ref, dk_ref, dv_ref):
        i = pl.program_id(0)

        @pl.when(i == 0)
        def _():
            dk_ref[...] = jnp.zeros((ML, Mw), F32)
            dv_ref[...] = jnp.zeros((ML, Mw), F32)

        q, kv, vv, do = q_ref[...], k_ref[...], v_ref[...], do_ref[...]
        dq = jnp.zeros((tm, Mw), F32)
        for h in range(nh):
            mk = _head_mask(h, Mw)
            qh = jnp.where(mk, q, 0.0).astype(MXU_DT)
            s = _dot(qh, kv, NT) * scale
            e = jnp.exp(s - jnp.max(s, axis=-1, keepdims=True))
            pr = e / jnp.sum(e, axis=-1, keepdims=True)
            doh = jnp.where(mk, do, 0.0).astype(MXU_DT)
            dv_ref[...] += _dot_tn(pr, doh)
            dp = _dot(doh, vv, NT)
            ds = pr * (dp - jnp.sum(dp * pr, axis=-1, keepdims=True))
            dq = dq + _dot(ds.astype(MXU_DT), jnp.where(mk, kv, jnp.zeros_like(kv))) * scale
            dk_ref[...] += _dot_tn(ds, qh) * scale
        dq_ref[...] = dq

    return _call(body, name=name, grid=(S // tm,),
                 in_specs=[_rowblk(tm, Mw, qblk), pl.BlockSpec((ML, Mw), lambda i: (0, 2 * l)),
                           pl.BlockSpec((ML, Mw), lambda i: (0, 2 * l + 1)), _rowblk(tm, Mw, doblk)],
                 out_specs=[_rowblk(tm, Mw), _res((ML, Mw)), _res((ML, Mw))],
                 out_shape=[_sds((S, Mw), F32), _sds((ML, Mw), F32), _sds((ML, Mw), F32)])(p, mkv, mkv, dcat)


def _fox_specs(S, dh, tq):
    nb = S // tq
    qs = pl.BlockSpec((1, tq, dh), lambda h, i: (h, i, 0))
    ks = pl.BlockSpec((1, S, dh), lambda h, i: (h, 0, 0))
    cqs = pl.BlockSpec((1, tq, 1), lambda h, i: (h, i, 0))
    cks = pl.BlockSpec((1, nb, 1, tq), lambda h, i: (h, 0, 0, 0))
    return qs, ks, cqs, cks


def _fox_logits(qv, kv, cqv, ckv, i, j, tq, scale):
    s = _dot(qv, kv, NT) * scale + cqv - ckv
    rows = i * tq + lax.broadcasted_iota(jnp.int32, (tq, tq), 0)
    cols = j * tq + lax.broadcasted_iota(jnp.int32, (tq, tq), 1)
    return jnp.where(cols <= rows, s, NEG)


def _fox_fwd(q, k, v, cq, ck, *, tq, name):
    H, S, dh = q.shape
    scale = dh ** -0.5
    qs, ks, cqs, cks = _fox_specs(S, dh, tq)

    def body(q_ref, k_ref, v_ref, cq_ref, ck_ref, o_ref, lse_ref):
        i = pl.program_id(1)
        qv, cqv = q_ref[0], cq_ref[0]

        def kblock(j, carry):
            m, l, acc = carry
            r = pl.ds(pl.multiple_of(j * tq, tq), tq)
            s = _fox_logits(qv, k_ref[0, r, :], cqv, ck_ref[0, j], i, j, tq, scale)
            m2 = jnp.maximum(m, jnp.max(s, axis=-1, keepdims=True))
            pr = jnp.exp(s - m2)
            al = jnp.exp(m - m2)
            return (m2, al * l + jnp.sum(pr, axis=-1, keepdims=True),
                    al * acc + _dot(pr.astype(MXU_DT), v_ref[0, r, :]))

        init = (jnp.full((tq, 1), NEG, F32), jnp.zeros((tq, 1), F32), jnp.zeros((tq, dh), F32))
        m, l, acc = lax.fori_loop(0, i + 1, kblock, init)
        o_ref[0] = acc / l
        lse_ref[0] = m + jnp.log(l)

    return _call(body, name=name, grid=(H, S // tq), in_specs=[qs, ks, ks, cqs, cks], out_specs=[qs, cqs],
                 out_shape=[_sds((H, S, dh), F32), _sds((H, S, 1), F32)])(q, k, v, cq, ck)


def _fox_bwd(q, k, v, cq, ck, o, lse, do, *, tq, name):
    H, S, dh = q.shape
    nb, scale = S // tq, dh ** -0.5
    qs, ks, cqs, cks = _fox_specs(S, dh, tq)

    def body(q_ref, k_ref, v_ref, cq_ref, ck_ref, o_ref, lse_ref, do_ref, dq_ref, dk_ref, dv_ref, dcq_ref,
             dck_ref):
        i = pl.program_id(1)

        @pl.when(i == 0)
        def _():
            dk_ref[...] = jnp.zeros((1, S, dh), F32)
            dv_ref[...] = jnp.zeros((1, S, dh), F32)
            dck_ref[...] = jnp.zeros((1, nb, 1, tq), F32)

        qv, cqv, lsev = q_ref[0], cq_ref[0], lse_ref[0]
        dob = do_ref[0].astype(MXU_DT)
        delta = jnp.sum(dob.astype(F32) * o_ref[0], axis=-1, keepdims=True)

        def kblock(j, carry):
            dq, rs = carry
            r = pl.ds(pl.multiple_of(j * tq, tq), tq)
            kv, vv = k_ref[0, r, :], v_ref[0, r, :]
            pr = jnp.exp(_fox_logits(qv, kv, cqv, ck_ref[0, j], i, j, tq, scale) - lsev)
            ds = pr * (_dot(dob, vv, NT) - delta)
            dk_ref[0, r, :] += _dot_tn(ds, qv) * scale
            dv_ref[0, r, :] += _dot_tn(pr, dob)
            dck_ref[0, j] += -jnp.sum(ds, axis=0, keepdims=True)
            return dq + _dot(ds.astype(MXU_DT), kv), rs + jnp.sum(ds, axis=-1, keepdims=True)

        dq, rs = lax.fori_loop(0, i + 1, kblock, (jnp.zeros((tq, dh), F32), jnp.zeros((tq, 1), F32)))
        dq_ref[0] = dq * scale
        dcq_ref[0] = rs

    return _call(body, name=name, grid=(H, nb), in_specs=[qs, ks, ks, cqs, cks, qs, cqs, qs],
                 out_specs=[qs, ks, ks, cqs, cks],
                 out_shape=[_sds((H, S, dh), F32), _sds((H, S, dh), F32), _sds((H, S, dh), F32),
                            _sds((H, S, 1), F32), _sds((H, nb, 1, tq), F32)])(q, k, v, cq, ck, o, lse, do)


def _tri(n, lower):
    r = lax.broadcasted_iota(jnp.int32, (n, n), 0)
    c = lax.broadcasted_iota(jnp.int32, (n, n), 1)
    return ((c <= r) if lower else (c >= r)).astype(F32)


def _fgate_fwd(fr, bf, *, name):
    S, W = fr.shape
    B = _pick(S, (256, 128))

    def body(f_ref, b_ref, cum_ref):
        L = _tri(B, True)
        carry = jnp.zeros((1, W), F32)
        for blk in range(S // B):
            z = f_ref[blk * B:(blk + 1) * B, :] + b_ref[...]
            ls = jnp.minimum(z, 0.0) - jnp.log(1.0 + jnp.exp(-jnp.abs(z)))
            cum_ref[blk * B:(blk + 1) * B, :] = jnp.dot(L, ls, precision=lax.Precision.HIGHEST,
                                                        preferred_element_type=F32) + carry
            carry = carry + jnp.sum(ls, axis=0, keepdims=True)

    return _call(body, name=name, out_shape=_sds((S, W), F32))(fr, bf)


def _fgate_bwd(fr, bf, dcum, *, name):
    S, W = fr.shape
    B = _pick(S, (256, 128))

    def body(f_ref, b_ref, dc_ref, df_ref, db_ref):
        U = _tri(B, False)
        carry = jnp.zeros((1, W), F32)
        dbs = jnp.zeros((1, W), F32)
        for blk in reversed(range(S // B)):
            dc = dc_ref[blk * B:(blk + 1) * B, :]
            dls = jnp.dot(U, dc, precision=lax.Precision.HIGHEST, preferred_element_type=F32) + carry
            carry = carry + jnp.sum(dc, axis=0, keepdims=True)
            z = f_ref[blk * B:(blk + 1) * B, :] + b_ref[...]
            df = dls * (1.0 / (1.0 + jnp.exp(z)))
            df_ref[blk * B:(blk + 1) * B, :] = df
            dbs = dbs + jnp.sum(df, axis=0, keepdims=True)
        db_ref[...] = dbs

    return _call(body, name=name, out_shape=[_sds((S, W), F32), _sds((1, W), F32)])(fr, bf, dcum)


def _flip(v, bit):
    return 1 - v if bit else v


def _w_gather(srcs, *, name):
    n = len(srcs)

    def body(*refs):
        src, out = refs[:n], refs[n:2 * n]
        send, recv, loc = refs[2 * n:]
        x, y, c = lax.axis_index("x"), lax.axis_index("y"), lax.axis_index("c")
        me, sib = (x, y, c), (x, y, 1 - c)
        chips = [(1 - x, y), (x, 1 - y), (1 - x, 1 - y)]

        def slot(i, px, py, pc):
            return out[i].at[4 * px + 2 * py + pc]

        def copy(i, k, block, to, s=None):
            return pltpu.make_async_remote_copy(
                src_ref=slot(i, *block) if s is None else s, dst_ref=slot(i, *block),
                send_sem=send.at[i, k], recv_sem=recv.at[i, k], device_id=to, device_id_type=MESH)

        mine = [pltpu.make_async_copy(src[i], slot(i, *me), loc.at[i]) for i in range(n)]
        for cp in mine:
            cp.start()
        first = []
        for i in range(n):
            first.append(copy(i, 0, me, sib, src[i]))
            first += [copy(i, 1 + j, me, (*chip, c), src[i]) for j, chip in enumerate(chips)]
        for cp in first:
            cp.start()
        passed = []
        for j, chip in enumerate(chips):
            for i in range(n):
                copy(i, 1 + j, (*chip, c), me).wait_recv()
                fwd = copy(i, 4 + j, (*chip, c), sib)
                fwd.start()
                passed.append(fwd)
        for i in range(n):
            copy(i, 0, sib, me).wait_recv()
            for j, chip in enumerate(chips):
                copy(i, 4 + j, (*chip, 1 - c), me).wait_recv()
        for cp in first + passed:
            cp.wait_send()
        for cp in mine:
            cp.wait()

    return _call(body, name=name, in_specs=[ANY] * n, out_specs=[ANY] * n,
                 out_shape=[_sds((N_DEV,) + s.shape, s.dtype) for s in srcs],
                 scratch_shapes=[pltpu.SemaphoreType.DMA((n, 7)), pltpu.SemaphoreType.DMA((n, 7)),
                                 pltpu.SemaphoreType.DMA((n,))])(*srcs)


def _g_exchange(srcs, scatter, *, name):
    n = len(srcs)

    def body(*refs):
        src, out = refs[:n], refs[n:2 * n]
        send, recv, loc = refs[2 * n:]
        x, y, c = lax.axis_index("x"), lax.axis_index("y"), lax.axis_index("c")
        me = 4 * x + 2 * y + c

        def peer(m):
            return _flip(x, m & 4), _flip(y, m & 2), _flip(c, m & 1)

        def copy(i, m):
            px, py, pc = peer(m)
            return pltpu.make_async_remote_copy(
                src_ref=src[i].at[4 * px + 2 * py + pc] if scatter[i] else src[i], dst_ref=out[i].at[me],
                send_sem=send.at[i, m - 1], recv_sem=recv.at[i, m - 1], device_id=(px, py, pc),
                device_id_type=MESH)

        def arrival(i, m):
            px, py, pc = peer(m)
            land = out[i].at[4 * px + 2 * py + pc]
            return pltpu.make_async_remote_copy(src_ref=land, dst_ref=land, send_sem=send.at[i, m - 1],
                                                recv_sem=recv.at[i, m - 1], device_id=(px, py, pc),
                                                device_id_type=MESH)

        mine = [pltpu.make_async_copy(src[i].at[me] if scatter[i] else src[i], out[i].at[me], loc.at[i])
                for i in range(n)]
        for cp in mine:
            cp.start()
        sent = [copy(i, m) for i in range(n) for m in range(1, N_DEV)]
        for cp in sent:
            cp.start()
        for i in range(n):
            for m in range(1, N_DEV):
                arrival(i, m).wait_recv()
        for cp in sent:
            cp.wait_send()
        for cp in mine:
            cp.wait()

    return _call(body, name=name, in_specs=[ANY] * n, out_specs=[ANY] * n,
                 out_shape=[_sds((N_DEV,) + s.shape[-2:], s.dtype) for s in srcs],
                 scratch_shapes=[pltpu.SemaphoreType.DMA((n, 7)), pltpu.SemaphoreType.DMA((n, 7)),
                                 pltpu.SemaphoreType.DMA((n,))])(*srcs)


def _reduce_adam(recv, w, m, v, *, name):
    R, L = w.shape
    tr = _pick(R, (256, 128, 64, 32, 16, 8))

    def body(r_ref, w_ref, m_ref, v_ref, g_ref, d_ref, m2_ref, v2_ref):
        g = r_ref[0].astype(F32)
        for s in range(1, N_DEV):
            g = g + r_ref[s].astype(F32)
        mm = ADAM_B1 * m_ref[...] + (1.0 - ADAM_B1) * g
        vv = ADAM_B2 * v_ref[...] + (1.0 - ADAM_B2) * (g * g)
        m_hat = mm / (1.0 - ADAM_B1 ** ADAM_STEP)
        v_hat = vv / (1.0 - ADAM_B2 ** ADAM_STEP)
        g_ref[...] = g
        d_ref[...] = -ADAM_LR * (m_hat / (jnp.sqrt(v_hat) + ADAM_EPS) + ADAM_WD * w_ref[...])
        m2_ref[...] = mm
        v2_ref[...] = vv

    blk = _rowblk(tr, L)
    return _call(body, name=name, grid=(R // tr,),
                 in_specs=[pl.BlockSpec((N_DEV, tr, L), lambda i: (0, i, 0)), blk, blk, blk],
                 out_specs=[blk, blk, blk, blk], out_shape=[_sds((R, L), F32)] * 4)(recv, w, m, v)


class _Pack:
    def __init__(self, shapes, row_mult):
        self.shapes, self.offs, rows = dict(shapes), {}, 0
        for name, shp in shapes:
            size = 1
            for d in shp:
                size *= d
            nr = -(-size // (16 * PACK_LANES)) * 16
            self.offs[name] = (rows, size, nr)
            rows += nr
        self.used = rows
        self.rows = -(-rows // row_mult) * row_mult

    def pack(self, arrays, dtype, lead=()):
        parts = []
        for name, (r0, size, nr) in self.offs.items():
            flat = arrays[name].astype(dtype).reshape(lead + (size,))
            flat = jnp.pad(flat, [(0, 0)] * len(lead) + [(0, nr * PACK_LANES - size)])
            parts.append(flat.reshape(lead + (nr, PACK_LANES)))
        if self.rows > self.used:
            parts.append(jnp.zeros(lead + (self.rows - self.used, PACK_LANES), dtype))
        return jnp.concatenate(parts, axis=len(lead))

    def unpack(self, buf, lead=()):
        out = {}
        for name, (r0, size, nr) in self.offs.items():
            flat = buf[..., r0:r0 + nr, :].reshape(lead + (nr * PACK_LANES,))
            out[name] = flat[..., :size].reshape(lead + tuple(self.shapes[name]))
        return out


def _to_full(g8, ax):
    t = jnp.moveaxis(g8, 0, ax)
    return t.reshape(t.shape[:ax] + (t.shape[ax] * t.shape[ax + 1],) + t.shape[ax + 2:])


def _to_shards(full, ax):
    shp = full.shape
    return jnp.moveaxis(full.reshape(shp[:ax] + (N_DEV, shp[ax] // N_DEV) + shp[ax + 1:]), ax, 0)


def _to_heads(a, H):
    S = a.shape[0]
    return a.reshape(S, H, HEAD_DIM).transpose(1, 0, 2)


def _from_heads(a):
    H, S, dh = a.shape
    return a.transpose(1, 0, 2).reshape(S, H * dh)


def kernel(x, mem, g_mix, w_in_a, b_glu, w_dw_a, b_dw_a, ln_g, ln_b, g_kv, w_kvf, b_f, w_in_b, g_mem, w_mem_kv, w_out, g_ffn, w_up, w_dw_f, b_dw_f, w_down, g_final, loss_target, m_g_mix, m_w_in_a, m_b_glu, m_w_dw_a, m_b_dw_a, m_ln_g, m_ln_b, m_g_kv, m_w_kvf, m_b_f, m_w_in_b, m_g_mem, m_w_mem_kv, m_w_out, m_g_ffn, m_w_up, m_w_dw_f, m_b_dw_f, m_w_down, m_g_final, v_g_mix, v_w_in_a, v_b_glu, v_w_dw_a, v_b_dw_a, v_ln_g, v_ln_b, v_g_kv, v_w_kvf, v_b_f, v_w_in_b, v_g_mem, v_w_mem_kv, v_w_out, v_g_ffn, v_w_up, v_w_dw_f, v_b_dw_f, v_w_down, v_g_final):
    given = dict(locals())
    W = {n: given[n] for n in WEIGHTS}
    x0, mem0, tgt = x[0], mem[0], loss_target[0]
    S, D = x0.shape
    depth, n_a = g_mix.shape[0], w_in_a.shape[0]
    C = w_dw_a.shape[2] * N_DEV
    Mw = D - C
    Fw = w_down.shape[1] * N_DEV
    H = b_f.shape[0]
    assert C == H * HEAD_DIM and (2 * C) % Mw == 0 and C % Mw == 0 and H <= GATE_LANES
    tq = _pick(S, (256, 128))
    nkv = 2 * C + GATE_LANES

    pk_big = _Pack([(n, W[n].shape) for n, _ in BIG], 256)
    pk_small = _Pack([(n, W[n].shape) for n, _ in SMALL], 8)
    pk_rep = _Pack([(n, W[n].shape) for n in REP], 8)
    wb32 = pk_big.pack(W, F32)
    ws32 = pk_small.pack(W, F32)
    gb, gs = _w_gather([wb32.astype(COMM_DT), ws32], name="w_gather")
    full = {n: _to_full(a, ax) for (n, ax), a in zip(BIG, pk_big.unpack(gb, (N_DEV,)).values())}
    full.update({n: _to_full(a, ax) for (n, ax), a in zip(SMALL, pk_small.unpack(gs, (N_DEV,)).values())})
    full = {n: (a.astype(MXU_DT) if a.dtype == COMM_DT else a) for n, a in full.items()}
    wkvf = jnp.pad(full["w_kvf"], ((0, 0), (0, nkv - full["w_kvf"].shape[1])))
    wmem = full["w_mem_kv"].transpose(1, 0, 2).reshape(D, depth * 2 * Mw)
    bf_pad = jnp.pad(b_f, (0, GATE_LANES - H)).reshape(1, GATE_LANES)

    mem_n = _rms_fwd(mem0, g_mem, name="mem_norm")
    mkv = _mm(mem_n, wmem, name="mem_kv", out_dtype=MXU_DT)
    sv = []
    xs = x0
    for l in range(depth):
        t = dict(x_in=xs)
        t["h"] = _rms_fwd(xs, g_mix[l], name=f"mix_norm{l}")
        if l < n_a:
            t["p"] = _mm(t["h"], full["w_in_a"][l], name=f"in_proj{l}", out_dtype=F32)
            t["v2"] = _glu_conv_fwd(t["p"], full["b_glu"][l], full["w_dw_a"][l], full["b_dw_a"][l],
                                    name=f"glu_conv{l}")
            memo = _mem_attn_fwd(t["p"], 2 * C // Mw, mkv, l, Mw=Mw, name=f"mem_attn{l}")
            t["cat"] = _ln_silu_cat_fwd(t["v2"], full["ln_g"][l], full["ln_b"][l], memo, name=f"ln_silu{l}")
        else:
            if l == n_a:
                hk = _rms_fwd(xs, g_kv, name="kv_norm")
                kvf = _mm(hk, wkvf, name="kv_proj", out_dtype=F32)
                k_h = _to_heads(kvf[:, :C], H).astype(MXU_DT)
                v_h = _to_heads(kvf[:, C:2 * C], H).astype(MXU_DT)
                fr = kvf[:, 2 * C:]
                cum = _fgate_fwd(fr, bf_pad, name="fgate")
                cum_t = cum[:, :H].T
                cq, ck = cum_t.reshape(H, S, 1), cum_t.reshape(H, S // tq, 1, tq)
            t["p"] = _mm(t["h"], full["w_in_b"][l - n_a], name=f"in_proj{l}", out_dtype=F32)
            t["q_h"] = _to_heads(t["p"][:, :C], H).astype(MXU_DT)
            t["o_h"], t["lse"] = _fox_fwd(t["q_h"], k_h, v_h, cq, ck, tq=tq, name=f"fox{l}")
            memo = _mem_attn_fwd(t["p"], C // Mw, mkv, l, Mw=Mw, name=f"mem_attn{l}")
            t["cat"] = jnp.concatenate([_from_heads(t["o_h"]), memo], axis=1).astype(MXU_DT)
        t["x_mid"] = _mm(t["cat"], full["w_out"][l], name=f"out_proj{l}", out_dtype=F32, add=xs)
        t["h2"] = _rms_fwd(t["x_mid"], g_ffn[l], name=f"ffn_norm{l}")
        t["ug"] = _mm(t["h2"], full["w_up"][l][:, :Fw], name=f"up_gate{l}", out_dtype=F32)
        t["uv"] = _mm(t["h2"], full["w_up"][l][:, Fw:], name=f"up_val{l}", out_dtype=F32)
        t["act"] = _ffn_act_fwd(t["ug"], t["uv"], full["w_dw_f"][l], b_dw_f[l], name=f"ffn_act{l}")
        xs = _mm(t["act"], full["w_down"][l], name=f"down_proj{l}", out_dtype=F32, add=t["x_mid"])
        sv.append(t)
    loss_dev, dx, dg_final = _loss_bwd(xs, g_final, tgt, name="loss_head")

    G = {n: [None] * W[n].shape[0] for n in ("g_mix", "w_in_a", "b_glu", "w_dw_a", "b_dw_a", "ln_g", "ln_b",
                                              "w_in_b", "w_out", "g_ffn", "w_up", "w_dw_f", "b_dw_f", "w_down")}
    dmkv = [None] * depth
    dk_sum = dv_sum = dck_sum = None
    for l in reversed(range(depth)):
        t = sv[l]
        dxb = dx.astype(MXU_DT)
        dact = _mm(dxb, full["w_down"][l], name=f"d_act{l}", out_dtype=F32, nt=True)
        G["w_down"][l] = _mm(t["act"].T, dxb, name=f"dw_down{l}", out_dtype=COMM_DT)
        dug, duv, G["w_dw_f"][l], db = _ffn_act_bwd(t["ug"], t["uv"], dact, full["w_dw_f"][l], b_dw_f[l],
                                                    name=f"d_ffn_act{l}")
        G["b_dw_f"][l] = db[0]
        dh2 = _mm(dug, full["w_up"][l][:, :Fw], name=f"d_up_gate{l}", out_dtype=F32, nt=True, mode="m")
        dh2 = _mm(duv, full["w_up"][l][:, Fw:], name=f"d_up_val{l}", out_dtype=F32, nt=True, mode="m", add=dh2)
        h2t = t["h2"].T
        G["w_up"][l] = jnp.concatenate([_mm(h2t, dug, name=f"dw_up_gate{l}", out_dtype=COMM_DT),
                                        _mm(h2t, duv, name=f"dw_up_val{l}", out_dtype=COMM_DT)], axis=1)
        dx, dg = _rms_bwd(t["x_mid"], g_ffn[l], dh2, dx, name=f"d_ffn_norm{l}")
        G["g_ffn"][l] = dg[0]

        dxb = dx.astype(MXU_DT)
        dcat = _mm(dxb, full["w_out"][l], name=f"d_cat{l}", out_dtype=F32, nt=True)
        G["w_out"][l] = _mm(t["cat"].T, dxb, name=f"dw_out{l}", out_dtype=COMM_DT)
        if l >= n_a:
            do_h = _to_heads(dcat[:, :C], H)
            dq_h, dk_h, dv_h, dcq, dck = _fox_bwd(t["q_h"], k_h, v_h, cq, ck, t["o_h"], t["lse"], do_h, tq=tq,
                                                  name=f"d_fox{l}")
            dck = dck.reshape(H, S) + dcq.reshape(H, S)
            dk_sum = dk_h if dk_sum is None else dk_sum + dk_h
            dv_sum = dv_h if dv_sum is None else dv_sum + dv_h
            dck_sum = dck if dck_sum is None else dck_sum + dck
            dqm, dmk, dmv = _mem_attn_bwd(t["p"], C // Mw, mkv, l, dcat, C // Mw, Mw=Mw, name=f"d_mem_attn{l}")
            dp = jnp.concatenate([_from_heads(dq_h), dqm], axis=1).astype(MXU_DT)
            w_in, key, li = full["w_in_b"][l - n_a], "w_in_b", l - n_a
        else:
            dv2, dlg, dlb = _ln_silu_bwd(t["v2"], full["ln_g"][l], full["ln_b"][l], dcat, name=f"d_ln_silu{l}")
            G["ln_g"][l], G["ln_b"][l] = dlg[0], dlb[0]
            da, dgt, G["w_dw_a"][l], dbd, dba, dbg = _glu_conv_bwd(t["p"], full["b_glu"][l], full["w_dw_a"][l],
                                                                   dv2, name=f"d_glu_conv{l}")
            G["b_dw_a"][l] = dbd[0]
            G["b_glu"][l] = jnp.concatenate([dba[0], dbg[0]])
            dqm, dmk, dmv = _mem_attn_bwd(t["p"], 2 * C // Mw, mkv, l, dcat, C // Mw, Mw=Mw,
                                          name=f"d_mem_attn{l}")
            dp = jnp.concatenate([da, dgt, dqm], axis=1).astype(MXU_DT)
            w_in, key, li = full["w_in_a"][l], "w_in_a", l
        dmkv[l] = jnp.concatenate([dmk, dmv], axis=1)
        dh = _mm(dp, w_in, name=f"d_in_proj{l}", out_dtype=F32, nt=True)
        G[key][li] = _mm(t["h"].T, dp, name=f"dw_in_proj{l}", out_dtype=COMM_DT)
        dx, dg = _rms_bwd(t["x_in"], g_mix[l], dh, dx, name=f"d_mix_norm{l}")
        G["g_mix"][l] = dg[0]
        if l == n_a:
            dcum = jnp.pad(dck_sum.T, ((0, 0), (0, GATE_LANES - H)))
            df, dbf = _fgate_bwd(fr, bf_pad, dcum, name="d_fgate")
            dkvf = jnp.concatenate([_from_heads(dk_sum), _from_heads(dv_sum), df], axis=1).astype(MXU_DT)
            dhk = _mm(dkvf, wkvf, name="d_kv_proj", out_dtype=F32, nt=True)
            g_w_kvf = _mm(hk.T, dkvf, name="dw_kv_proj", out_dtype=COMM_DT)[:, :w_kvf.shape[1]]
            dx, dg_kv = _rms_bwd(t["x_in"], g_kv, dhk, dx, name="d_kv_norm")
    dmkv_all = jnp.concatenate(dmkv, axis=1).astype(MXU_DT)
    g_wmem = _mm(mem_n.T, dmkv_all, name="dw_mem_kv", out_dtype=COMM_DT)
    dmem_n = _mm(dmkv_all, wmem, name="d_mem_kv", out_dtype=F32, nt=True)
    _, dg_mem = _rms_bwd(mem0, g_mem, dmem_n, None, name="d_mem_norm")

    grads = {n: jnp.stack(v) for n, v in G.items()}
    grads.update(g_kv=dg_kv[0], w_kvf=g_w_kvf, b_f=dbf[0, :H], g_mem=dg_mem[0], g_final=dg_final[0],
                 w_mem_kv=g_wmem.reshape(D, depth, 2 * Mw).transpose(1, 0, 2))

    gb8 = pk_big.pack({n: _to_shards(grads[n], ax) for n, ax in BIG}, COMM_DT, (N_DEV,))
    gs8 = pk_small.pack({n: _to_shards(grads[n], ax) for n, ax in SMALL}, F32, (N_DEV,))
    gr = pk_rep.pack(grads, F32)
    rb, rs, rr = _g_exchange([gb8, gs8, gr], [True, True, False], name="g_exchange")
    M1 = {n: given["m_" + n] for n in WEIGHTS}
    V1 = {n: given["v_" + n] for n in WEIGHTS}
    res = {}
    for pk, recv, tag in ((pk_big, rb, "big"), (pk_small, rs, "small"), (pk_rep, rr, "rep")):
        w32 = wb32 if tag == "big" else (ws32 if tag == "small" else pk.pack(W, F32))
        outs = _reduce_adam(recv, w32, pk.pack(M1, F32), pk.pack(V1, F32), name=f"adam_{tag}")
        for kind, buf in zip(("grad", "delta", "new_m", "new_v"), outs):
            for n, a in pk.unpack(buf).items():
                res[kind, n] = a

    loss = lax.psum(loss_dev, ("x", "y", "c"))
    return (loss, dx[None], *[res[kind, n] for kind in ("grad", "delta", "new_m", "new_v") for n in WEIGHTS])
```

```python
import jax
import jax.numpy as jnp
from jax import lax
from jax.experimental import pallas as pl
from jax.experimental.pallas import tpu as pltpu

F32 = jnp.float32
MXU_DT = jnp.bfloat16
COMM_DT = jnp.bfloat16

N_DEV = 8
HEAD_DIM = 64
RMS_EPS = 1e-6
LN_EPS = 1e-5
ADAM_LR = 0.001
ADAM_B1 = 0.9
ADAM_B2 = 0.999
ADAM_EPS = 1e-08
ADAM_WD = 0.01
ADAM_STEP = 10

PACK_LANES = 1024
GATE_LANES = 128
VMEM_LIMIT_V7X = 56 << 20
NEG = -1e30
MESH = pl.DeviceIdType.MESH
ANY = pl.BlockSpec(memory_space=pl.ANY)
NT = (((1,), (1,)), ((), ()))
NN = (((1,), (0,)), ((), ()))

BIG = (("w_in_a", 2), ("w_kvf", 0), ("w_in_b", 1), ("w_mem_kv", 1), ("w_out", 1), ("w_up", 2), ("w_down", 1))
SMALL = (("b_glu", 1), ("w_dw_a", 2), ("b_dw_a", 1), ("ln_g", 1), ("ln_b", 1), ("w_dw_f", 2))
REP = ("g_mix", "g_kv", "b_f", "g_mem", "g_ffn", "b_dw_f", "g_final")
WEIGHTS = ("g_mix", "w_in_a", "b_glu", "w_dw_a", "b_dw_a", "ln_g", "ln_b", "g_kv", "w_kvf", "b_f", "w_in_b",
           "g_mem", "w_mem_kv", "w_out", "g_ffn", "w_up", "w_dw_f", "b_dw_f", "w_down", "g_final")


def _sds(shape, dtype):
    return jax.ShapeDtypeStruct(tuple(shape), dtype)


def _call(body, *, name, out_shape, grid=(), in_specs=None, out_specs=None, scratch_shapes=()):
    params = dict(vmem_limit_bytes=VMEM_LIMIT_V7X)
    if grid:
        params["dimension_semantics"] = ("arbitrary",) * len(grid)
    kw = {}
    if in_specs is not None:
        kw["in_specs"] = in_specs
    if out_specs is not None:
        kw["out_specs"] = out_specs
    return pl.pallas_call(body, name=name, grid=grid, out_shape=out_shape, scratch_shapes=list(scratch_shapes),
                          compiler_params=pltpu.CompilerParams(**params), **kw)


def _res(shape):
    nd = len(shape)
    return pl.BlockSpec(tuple(shape), lambda *_: (0,) * nd)


def _colblk(rows, tc, off=0):
    return pl.BlockSpec((rows, tc), lambda j: (0, j + off))


def _rowblk(tm, cols, off=0):
    return pl.BlockSpec((tm, cols), lambda i: (i, off))


def _pick(n, opts=(512, 256, 128)):
    for t in opts:
        if n % t == 0:
            return t
    return n


def _sig(z):
    return 1.0 / (1.0 + jnp.exp(-z))


def _dot(a, b, dims=NN):
    return lax.dot_general(a, b, dims, preferred_element_type=F32)


def _dot_tn(a, b):
    return _dot(a.T.astype(b.dtype), b)


def _mm(a, w, *, name, out_dtype, nt=False, add=None, mode="n"):
    M, K = a.shape
    N = w.shape[0] if nt else w.shape[1]
    assert (w.shape[1] if nt else w.shape[0]) == K
    dims = NT if nt else NN
    has_add = add is not None
    if mode == "n":
        tn, rc = _pick(N), _pick(M)

        def body(*refs):
            a_ref, w_ref, o_ref = refs[0], refs[1], refs[-1]
            wv = w_ref[...]
            for r0 in range(0, M, rc):
                acc = _dot(a_ref[r0:r0 + rc, :], wv, dims)
                if has_add:
                    acc = acc + refs[2][r0:r0 + rc, :]
                o_ref[r0:r0 + rc, :] = acc.astype(out_dtype)

        w_spec = pl.BlockSpec((tn, K), lambda j: (j, 0)) if nt else pl.BlockSpec((K, tn), lambda j: (0, j))
        in_specs = [_res((M, K)), w_spec] + ([_colblk(M, tn)] if has_add else [])
        out_specs, grid = _colblk(M, tn), (N // tn,)
    else:
        tm, nc = _pick(M), _pick(N)

        def body(*refs):
            a_ref, w_ref, o_ref = refs[0], refs[1], refs[-1]
            av = a_ref[...]
            for n0 in range(0, N, nc):
                wv = w_ref[n0:n0 + nc, :] if nt else w_ref[:, n0:n0 + nc]
                acc = _dot(av, wv, dims)
                if has_add:
                    acc = acc + refs[2][:, n0:n0 + nc]
                o_ref[:, n0:n0 + nc] = acc.astype(out_dtype)

        in_specs = [_rowblk(tm, K), _res(w.shape)] + ([_rowblk(tm, N)] if has_add else [])
        out_specs, grid = _rowblk(tm, N), (M // tm,)
    args = (a, w) + ((add,) if has_add else ())
    return _call(body, name=name, grid=grid, in_specs=in_specs, out_specs=out_specs,
                 out_shape=_sds((M, N), out_dtype))(*args)


def _rms_fwd(x, g, *, name):
    M, D = x.shape
    tm = _pick(M, (256, 128))

    def body(x_ref, g_ref, h_ref):
        xf = x_ref[...]
        r = lax.rsqrt(jnp.mean(xf * xf, axis=-1, keepdims=True) + RMS_EPS)
        h_ref[...] = ((xf * r) * g_ref[...]).astype(MXU_DT)

    return _call(body, name=name, grid=(M // tm,), in_specs=[_rowblk(tm, D), _res((1, D))],
                 out_specs=_rowblk(tm, D), out_shape=_sds((M, D), MXU_DT))(x, g.reshape(1, D))


def _rms_bwd(x, g, dh, dx_in, *, name):
    M, D = x.shape
    tm = _pick(M, (256, 128))
    with_dx = dx_in is not None

    def body(*refs):
        if with_dx:
            x_ref, g_ref, dh_ref, dxin_ref, dx_ref, dg_ref = refs
        else:
            x_ref, g_ref, dh_ref, dg_ref = refs
        i = pl.program_id(0)
        xf = x_ref[...]
        r = lax.rsqrt(jnp.mean(xf * xf, axis=-1, keepdims=True) + RMS_EPS)
        y = xf * r
        dh_v = dh_ref[...]
        if with_dx:
            dy = dh_v * g_ref[...]
            dx_ref[...] = dxin_ref[...] + r * (dy - y * jnp.mean(dy * y, axis=-1, keepdims=True))
        part = jnp.sum(dh_v * y, axis=0, keepdims=True)

        @pl.when(i == 0)
        def _():
            dg_ref[...] = part

        @pl.when(i > 0)
        def _():
            dg_ref[...] += part

    ins = [x, g.reshape(1, D), dh] + ([dx_in] if with_dx else [])
    in_specs = [_rowblk(tm, D), _res((1, D)), _rowblk(tm, D)] + ([_rowblk(tm, D)] if with_dx else [])
    if with_dx:
        out_specs, out_shape = [_rowblk(tm, D), _res((1, D))], [_sds((M, D), F32), _sds((1, D), F32)]
    else:
        out_specs, out_shape = [_res((1, D))], [_sds((1, D), F32)]
    out = _call(body, name=name, grid=(M // tm,), in_specs=in_specs, out_specs=out_specs, out_shape=out_shape)(*ins)
    return out if with_dx else (None, out[0])


def _loss_bwd(x, g, t, *, name):
    M, D = x.shape
    tm = _pick(M, (256, 128))

    def body(x_ref, g_ref, t_ref, dx_ref, dg_ref, ls_ref):
        i = pl.program_id(0)
        xf = x_ref[...]
        r = lax.rsqrt(jnp.mean(xf * xf, axis=-1, keepdims=True) + RMS_EPS)
        xr = xf * r
        e = xr * g_ref[...] - t_ref[...]
        dout = e * (1.0 / D)
        dy = dout * g_ref[...]
        dx_ref[...] = r * (dy - xr * jnp.mean(dy * xr, axis=-1, keepdims=True))
        part = jnp.sum(dout * xr, axis=0, keepdims=True)
        lpart = jnp.zeros(ls_ref.shape, F32) + (0.5 / D) * jnp.sum(e * e, keepdims=True)

        @pl.when(i == 0)
        def _():
            dg_ref[...] = part
            ls_ref[...] = lpart

        @pl.when(i > 0)
        def _():
            dg_ref[...] += part
            ls_ref[...] += lpart

    dx, dg, ls = _call(body, name=name, grid=(M // tm,),
                       in_specs=[_rowblk(tm, D), _res((1, D)), _rowblk(tm, D)],
                       out_specs=[_rowblk(tm, D), _res((1, D)), _res((8, 128))],
                       out_shape=[_sds((M, D), F32), _sds((1, D), F32), _sds((8, 128), F32)])(x, g.reshape(1, D), t)
    return ls[0, 0], dx, dg


def _shift_rows(ext, off, rows):
    if off % 8 == 0:
        return ext[off:off + rows, :]
    return pltpu.roll(ext, ext.shape[0] - off, 0)[0:rows, :]


def _ext(pad_ref, c, rows, halo):
    return pad_ref[pl.ds(pl.multiple_of(c * rows, rows), rows + halo), :]


def _conv_chunk(pad_ref, c, rows, halo, w_ref, taps):
    ext = _ext(pad_ref, c, rows, halo)
    acc = None
    for k in range(taps):
        term = w_ref[k:k + 1, :] * _shift_rows(ext, halo - (taps - 1) + k, rows)
        acc = term if acc is None else acc + term
    return acc


def _conv_t_chunk(pad_ref, c, rows, halo, w_ref, taps):
    ext = _ext(pad_ref, c, rows, halo)
    acc = None
    for k in range(taps):
        term = w_ref[k:k + 1, :] * _shift_rows(ext, taps - 1 - k, rows)
        acc = term if acc is None else acc + term
    return acc


def _conv_wgrad_chunk(pad_ref, c, rows, halo, dy, dw_ref, taps):
    ext = _ext(pad_ref, c, rows, halo)
    for k in range(taps):
        dw_ref[k:k + 1, :] += jnp.sum(dy * _shift_rows(ext, halo - (taps - 1) + k, rows), axis=0, keepdims=True)


A_HALO, F_HALO = 32, 8


def _glu_conv_fwd(p, b_glu, w_dw, b_dw, *, name):
    S = p.shape[0]
    taps, C = w_dw.shape
    tc, rows = 128, _pick(S, (256, 128))
    nb, nch = C // tc, S // rows

    def body(a_ref, g_ref, ba_ref, bg_ref, w_ref, bd_ref, o_ref, pad_ref):
        pad_ref[0:A_HALO, :] = jnp.zeros((A_HALO, tc), F32)

        def fill(c, _):
            r = pl.ds(pl.multiple_of(c * rows, rows), rows)
            v1 = (a_ref[r, :] + ba_ref[...]) * _sig(g_ref[r, :] + bg_ref[...])
            pad_ref[pl.ds(pl.multiple_of(A_HALO + c * rows, 8), rows), :] = v1
            return 0

        lax.fori_loop(0, nch, fill, 0)

        def conv(c, _):
            o_ref[pl.ds(pl.multiple_of(c * rows, rows), rows), :] = (
                _conv_chunk(pad_ref, c, rows, A_HALO, w_ref, taps) + bd_ref[...])
            return 0

        lax.fori_loop(0, nch, conv, 0)

    b2 = b_glu.reshape(1, 2 * C)
    return _call(body, name=name, grid=(nb,),
                 in_specs=[_colblk(S, tc), _colblk(S, tc, nb), _colblk(1, tc), _colblk(1, tc, nb),
                           _colblk(taps, tc), _colblk(1, tc)],
                 out_specs=_colblk(S, tc), out_shape=_sds((S, C), F32),
                 scratch_shapes=[pltpu.VMEM((S + A_HALO, tc), F32)])(p, p, b2, b2, w_dw, b_dw.reshape(1, C))


def _glu_conv_bwd(p, b_glu, w_dw, dv2, *, name):
    S = p.shape[0]
    taps, C = w_dw.shape
    tc, rows = 128, _pick(S, (256, 128))
    nb, nch = C // tc, S // rows

    def body(a_ref, g_ref, ba_ref, bg_ref, w_ref, dy_ref, da_ref, dgt_ref, dw_ref, dbd_ref, dba_ref, dbg_ref,
             padx_ref, pady_ref):
        padx_ref[0:A_HALO, :] = jnp.zeros((A_HALO, tc), F32)
        pady_ref[S:S + A_HALO, :] = jnp.zeros((A_HALO, tc), F32)
        dw_ref[...] = jnp.zeros((taps, tc), F32)

        def fill(c, _):
            r = pl.ds(pl.multiple_of(c * rows, rows), rows)
            v1 = (a_ref[r, :] + ba_ref[...]) * _sig(g_ref[r, :] + bg_ref[...])
            padx_ref[pl.ds(pl.multiple_of(A_HALO + c * rows, 8), rows), :] = v1
            pady_ref[r, :] = dy_ref[r, :]
            return 0

        lax.fori_loop(0, nch, fill, 0)

        def back(c, carry):
            sd, sa, sg = carry
            r = pl.ds(pl.multiple_of(c * rows, rows), rows)
            dy = dy_ref[r, :]
            _conv_wgrad_chunk(padx_ref, c, rows, A_HALO, dy, dw_ref, taps)
            dv1 = _conv_t_chunk(pady_ref, c, rows, A_HALO, w_ref, taps)
            a = a_ref[r, :] + ba_ref[...]
            s = _sig(g_ref[r, :] + bg_ref[...])
            da = dv1 * s
            dgt = dv1 * a * s * (1.0 - s)
            da_ref[r, :] = da
            dgt_ref[r, :] = dgt
            return (sd + jnp.sum(dy, axis=0, keepdims=True), sa + jnp.sum(da, axis=0, keepdims=True),
                    sg + jnp.sum(dgt, axis=0, keepdims=True))

        z = jnp.zeros((1, tc), F32)
        sd, sa, sg = lax.fori_loop(0, nch, back, (z, z, z))
        dbd_ref[...] = sd
        dba_ref[...] = sa
        dbg_ref[...] = sg

    b2 = b_glu.reshape(1, 2 * C)
    return _call(body, name=name, grid=(nb,),
                 in_specs=[_colblk(S, tc), _colblk(S, tc, nb), _colblk(1, tc), _colblk(1, tc, nb),
                           _colblk(taps, tc), _colblk(S, tc)],
                 out_specs=[_colblk(S, tc), _colblk(S, tc), _colblk(taps, tc), _colblk(1, tc), _colblk(1, tc),
                            _colblk(1, tc)],
                 out_shape=[_sds((S, C), F32), _sds((S, C), F32), _sds((taps, C), F32), _sds((1, C), F32),
                            _sds((1, C), F32), _sds((1, C), F32)],
                 scratch_shapes=[pltpu.VMEM((S + A_HALO, tc), F32), pltpu.VMEM((S + A_HALO, tc), F32)])(
                     p, p, b2, b2, w_dw, dv2)


def _ln_silu_cat_fwd(v2, ln_g, ln_b, memo, *, name):
    S, C = v2.shape
    Mw = memo.shape[1]
    tm = _pick(S, (256, 128))

    def body(v_ref, g_ref, b_ref, m_ref, o_ref):
        v = v_ref[...]
        mu = jnp.mean(v, axis=-1, keepdims=True)
        d = v - mu
        y = d * lax.rsqrt(jnp.mean(d * d, axis=-1, keepdims=True) + LN_EPS) * g_ref[...] + b_ref[...]
        o_ref[:, 0:C] = (y * _sig(y)).astype(MXU_DT)
        o_ref[:, C:C + Mw] = m_ref[...].astype(MXU_DT)

    return _call(body, name=name, grid=(S // tm,),
                 in_specs=[_rowblk(tm, C), _res((1, C)), _res((1, C)), _rowblk(tm, Mw)],
                 out_specs=_rowblk(tm, C + Mw), out_shape=_sds((S, C + Mw), MXU_DT))(
                     v2, ln_g.reshape(1, C), ln_b.reshape(1, C), memo)


def _ln_silu_bwd(v2, ln_g, ln_b, dcat, *, name):
    S, C = v2.shape
    tm = _pick(S, (256, 128))

    def body(v_ref, g_ref, b_ref, dm_ref, dv_ref, dg_ref, db_ref):
        i = pl.program_id(0)
        v = v_ref[...]
        mu = jnp.mean(v, axis=-1, keepdims=True)
        d = v - mu
        rstd = lax.rsqrt(jnp.mean(d * d, axis=-1, keepdims=True) + LN_EPS)
        xh = d * rstd
        y = xh * g_ref[...] + b_ref[...]
        s = _sig(y)
        dyv = dm_ref[...] * (s * (1.0 + y * (1.0 - s)))
        dxh = dyv * g_ref[...]
        dv_ref[...] = rstd * (dxh - jnp.mean(dxh, axis=-1, keepdims=True)
                              - xh * jnp.mean(dxh * xh, axis=-1, keepdims=True))
        pg = jnp.sum(dyv * xh, axis=0, keepdims=True)
        pb = jnp.sum(dyv, axis=0, keepdims=True)

        @pl.when(i == 0)
        def _():
            dg_ref[...] = pg
            db_ref[...] = pb

        @pl.when(i > 0)
        def _():
            dg_ref[...] += pg
            db_ref[...] += pb

    return _call(body, name=name, grid=(S // tm,),
                 in_specs=[_rowblk(tm, C), _res((1, C)), _res((1, C)), _rowblk(tm, C)],
                 out_specs=[_rowblk(tm, C), _res((1, C)), _res((1, C))],
                 out_shape=[_sds((S, C), F32), _sds((1, C), F32), _sds((1, C), F32)])(
                     v2, ln_g.reshape(1, C), ln_b.reshape(1, C), dcat)


def _ffn_act_fwd(ug, uv, w_dw, b_dw, *, name):
    S, Fw = ug.shape
    taps = w_dw.shape[0]
    tc, rows = _pick(Fw, (256, 128)), _pick(S, (256, 128))
    nb, nch = Fw // tc, S // rows

    def body(ug_ref, uv_ref, wg_ref, wv_ref, bg_ref, bv_ref, o_ref, pg_ref, pv_ref):
        pg_ref[0:F_HALO, :] = jnp.zeros((F_HALO, tc), F32)
        pv_ref[0:F_HALO, :] = jnp.zeros((F_HALO, tc), F32)
        pg_ref[F_HALO:F_HALO + S, :] = ug_ref[...]
        pv_ref[F_HALO:F_HALO + S, :] = uv_ref[...]

        def act(c, _):
            gc = _conv_chunk(pg_ref, c, rows, F_HALO, wg_ref, taps) + bg_ref[...]
            vc = _conv_chunk(pv_ref, c, rows, F_HALO, wv_ref, taps) + bv_ref[...]
            o_ref[pl.ds(pl.multiple_of(c * rows, rows), rows), :] = (gc * _sig(gc) * vc).astype(MXU_DT)
            return 0

        lax.fori_loop(0, nch, act, 0)

    b2 = b_dw.reshape(1, 2 * Fw)
    return _call(body, name=name, grid=(nb,),
                 in_specs=[_colblk(S, tc), _colblk(S, tc), _colblk(taps, tc), _colblk(taps, tc, nb),
                           _colblk(1, tc), _colblk(1, tc, nb)],
                 out_specs=_colblk(S, tc), out_shape=_sds((S, Fw), MXU_DT),
                 scratch_shapes=[pltpu.VMEM((S + F_HALO, tc), F32), pltpu.VMEM((S + F_HALO, tc), F32)])(
                     ug, uv, w_dw, w_dw, b2, b2)


def _ffn_act_bwd(ug, uv, dact, w_dw, b_dw, *, name):
    S, Fw = ug.shape
    taps = w_dw.shape[0]
    tc, rows = _pick(Fw, (256, 128)), _pick(S, (256, 128))
    nb, nch = Fw // tc, S // rows

    def body(ug_ref, uv_ref, da_ref, wg_ref, wv_ref, bg_ref, bv_ref, dug_ref, duv_ref, dwg_ref, dwv_ref,
             dbg_ref, dbv_ref, pg_ref, pv_ref, qg_ref, qv_ref):
        pg_ref[0:F_HALO, :] = jnp.zeros((F_HALO, tc), F32)
        pv_ref[0:F_HALO, :] = jnp.zeros((F_HALO, tc), F32)
        qg_ref[S:S + F_HALO, :] = jnp.zeros((F_HALO, tc), F32)
        qv_ref[S:S + F_HALO, :] = jnp.zeros((F_HALO, tc), F32)
        pg_ref[F_HALO:F_HALO + S, :] = ug_ref[...]
        pv_ref[F_HALO:F_HALO + S, :] = uv_ref[...]
        dwg_ref[...] = jnp.zeros((taps, tc), F32)
        dwv_ref[...] = jnp.zeros((taps, tc), F32)

        def grads(c, carry):
            sg, sv = carry
            r = pl.ds(pl.multiple_of(c * rows, rows), rows)
            gc = _conv_chunk(pg_ref, c, rows, F_HALO, wg_ref, taps) + bg_ref[...]
            vc = _conv_chunk(pv_ref, c, rows, F_HALO, wv_ref, taps) + bv_ref[...]
            s = _sig(gc)
            da = da_ref[r, :]
            dgc = da * vc * (s * (1.0 + gc * (1.0 - s)))
            dvc = da * (gc * s)
            qg_ref[r, :] = dgc
            qv_ref[r, :] = dvc
            _conv_wgrad_chunk(pg_ref, c, rows, F_HALO, dgc, dwg_ref, taps)
            _conv_wgrad_chunk(pv_ref, c, rows, F_HALO, dvc, dwv_ref, taps)
            return sg + jnp.sum(dgc, axis=0, keepdims=True), sv + jnp.sum(dvc, axis=0, keepdims=True)

        z = jnp.zeros((1, tc), F32)
        sg, sv = lax.fori_loop(0, nch, grads, (z, z))
        dbg_ref[...] = sg
        dbv_ref[...] = sv

        def back(c, _):
            r = pl.ds(pl.multiple_of(c * rows, rows), rows)
            dug_ref[r, :] = _conv_t_chunk(qg_ref, c, rows, F_HALO, wg_ref, taps).astype(MXU_DT)
            duv_ref[r, :] = _conv_t_chunk(qv_ref, c, rows, F_HALO, wv_ref, taps).astype(MXU_DT)
            return 0

        lax.fori_loop(0, nch, back, 0)

    b2 = b_dw.reshape(1, 2 * Fw)
    pad = pltpu.VMEM((S + F_HALO, tc), F32)
    dug, duv, dwg, dwv, dbg, dbv = _call(
        body, name=name, grid=(nb,),
        in_specs=[_colblk(S, tc), _colblk(S, tc), _colblk(S, tc), _colblk(taps, tc), _colblk(taps, tc, nb),
                  _colblk(1, tc), _colblk(1, tc, nb)],
        out_specs=[_colblk(S, tc), _colblk(S, tc), _colblk(taps, tc), _colblk(taps, tc), _colblk(1, tc),
                   _colblk(1, tc)],
        out_shape=[_sds((S, Fw), MXU_DT), _sds((S, Fw), MXU_DT), _sds((taps, Fw), F32), _sds((taps, Fw), F32),
                   _sds((1, Fw), F32), _sds((1, Fw), F32)],
        scratch_shapes=[pad, pad, pad, pad])(ug, uv, dact, w_dw, w_dw, b2, b2)
    return dug, duv, jnp.concatenate([dwg, dwv], axis=1), jnp.concatenate([dbg, dbv], axis=1)


def _head_mask(h, width):
    lane = lax.broadcasted_iota(jnp.int32, (1, width), 1)
    return (lane >= h * HEAD_DIM) & (lane < (h + 1) * HEAD_DIM)


def _mem_attn_fwd(p, qblk, mkv, l, *, Mw, name):
    S, ML = p.shape[0], mkv.shape[0]
    tm, nh, scale = _pick(S, (256, 128)), Mw // HEAD_DIM, HEAD_DIM ** -0.5

    def body(q_ref, k_ref, v_ref, o_ref):
        q, kv, vv = q_ref[...], k_ref[...], v_ref[...]
        out = jnp.zeros((tm, Mw), F32)
        for h in range(nh):
            mk = _head_mask(h, Mw)
            s = _dot(jnp.where(mk, q, 0.0).astype(MXU_DT), kv, NT) * scale
            e = jnp.exp(s - jnp.max(s, axis=-1, keepdims=True))
            pr = e / jnp.sum(e, axis=-1, keepdims=True)
            out = out + _dot(pr.astype(MXU_DT), jnp.where(mk, vv, jnp.zeros_like(vv)))
        o_ref[...] = out

    return _call(body, name=name, grid=(S // tm,),
                 in_specs=[_rowblk(tm, Mw, qblk), pl.BlockSpec((ML, Mw), lambda i: (0, 2 * l)),
                           pl.BlockSpec((ML, Mw), lambda i: (0, 2 * l + 1))],
                 out_specs=_rowblk(tm, Mw), out_shape=_sds((S, Mw), F32))(p, mkv, mkv)


def _mem_attn_bwd(p, qblk, mkv, l, dcat, doblk, *, Mw, name):
    S, ML = p.shape[0], mkv.shape[0]
    tm, nh, scale = _pick(S, (256, 128)), Mw // HEAD_DIM, HEAD_DIM ** -0.5

    def body(q_ref, k_ref, v_ref, do_ref, dq_ref, dk_ref, dv_ref):
        i = pl.program_id(0)

        @pl.when(i == 0)
        def _():
            dk_ref[...] = jnp.zeros((ML, Mw), F32)
            dv_ref[...] = jnp.zeros((ML, Mw), F32)

        q, kv, vv, do = q_ref[...], k_ref[...], v_ref[...], do_ref[...]
        dq = jnp.zeros((tm, Mw), F32)
        for h in range(nh):
            mk = _head_mask(h, Mw)
            qh = jnp.where(mk, q, 0.0).astype(MXU_DT)
            s = _dot(qh, kv, NT) * scale
            e = jnp.exp(s - jnp.max(s, axis=-1, keepdims=True))
            pr = e / jnp.sum(e, axis=-1, keepdims=True)
            doh = jnp.where(mk, do, 0.0).astype(MXU_DT)
            dv_ref[...] += _dot_tn(pr, doh)
            dp = _dot(doh, vv, NT)
            ds = pr * (dp - jnp.sum(dp * pr, axis=-1, keepdims=True))
            dq = dq + _dot(ds.astype(MXU_DT), jnp.where(mk, kv, jnp.zeros_like(kv))) * scale
            dk_ref[...] += _dot_tn(ds, qh) * scale
        dq_ref[...] = dq

    return _call(body, name=name, grid=(S // tm,),
                 in_specs=[_rowblk(tm, Mw, qblk), pl.BlockSpec((ML, Mw), lambda i: (0, 2 * l)),
                           pl.BlockSpec((ML, Mw), lambda i: (0, 2 * l + 1)), _rowblk(tm, Mw, doblk)],
                 out_specs=[_rowblk(tm, Mw), _res((ML, Mw)), _res((ML, Mw))],
                 out_shape=[_sds((S, Mw), F32), _sds((ML, Mw), F32), _sds((ML, Mw), F32)])(p, mkv, mkv, dcat)


def _fox_specs(S, dh, tq):
    nb = S // tq
    qs = pl.BlockSpec((1, tq, dh), lambda h, i: (h, i, 0))
    ks = pl.BlockSpec((1, S, dh), lambda h, i: (h, 0, 0))
    cqs = pl.BlockSpec((1, tq, 1), lambda h, i: (h, i, 0))
    cks = pl.BlockSpec((1, nb, 1, tq), lambda h, i: (h, 0, 0, 0))
    return qs, ks, cqs, cks


def _fox_logits(qv, kv, cqv, ckv, i, j, tq, scale):
    s = _dot(qv, kv, NT) * scale + cqv - ckv
    rows = i * tq + lax.broadcasted_iota(jnp.int32, (tq, tq), 0)
    cols = j * tq + lax.broadcasted_iota(jnp.int32, (tq, tq), 1)
    return jnp.where(cols <= rows, s, NEG)


def _fox_fwd(q, k, v, cq, ck, *, tq, name):
    H, S, dh = q.shape
    scale = dh ** -0.5
    qs, ks, cqs, cks = _fox_specs(S, dh, tq)

    def body(q_ref, k_ref, v_ref, cq_ref, ck_ref, o_ref, lse_ref):
        i = pl.program_id(1)
        qv, cqv = q_ref[0], cq_ref[0]

        def kblock(j, carry):
            m, l, acc = carry
            r = pl.ds(pl.multiple_of(j * tq, tq), tq)
            s = _fox_logits(qv, k_ref[0, r, :], cqv, ck_ref[0, j], i, j, tq, scale)
            m2 = jnp.maximum(m, jnp.max(s, axis=-1, keepdims=True))
            pr = jnp.exp(s - m2)
            al = jnp.exp(m - m2)
            return (m2, al * l + jnp.sum(pr, axis=-1, keepdims=True),
                    al * acc + _dot(pr.astype(MXU_DT), v_ref[0, r, :]))

        init = (jnp.full((tq, 1), NEG, F32), jnp.zeros((tq, 1), F32), jnp.zeros((tq, dh), F32))
        m, l, acc = lax.fori_loop(0, i + 1, kblock, init)
        o_ref[0] = acc / l
        lse_ref[0] = m + jnp.log(l)

    return _call(body, name=name, grid=(H, S // tq), in_specs=[qs, ks, ks, cqs, cks], out_specs=[qs, cqs],
                 out_shape=[_sds((H, S, dh), F32), _sds((H, S, 1), F32)])(q, k, v, cq, ck)


def _fox_bwd(q, k, v, cq, ck, o, lse, do, *, tq, name):
    H, S, dh = q.shape
    nb, scale = S // tq, dh ** -0.5
    qs, ks, cqs, cks = _fox_specs(S, dh, tq)

    def body(q_ref, k_ref, v_ref, cq_ref, ck_ref, o_ref, lse_ref, do_ref, dq_ref, dk_ref, dv_ref, dcq_ref,
             dck_ref):
        i = pl.program_id(1)

        @pl.when(i == 0)
        def _():
            dk_ref[...] = jnp.zeros((1, S, dh), F32)
            dv_ref[...] = jnp.zeros((1, S, dh), F32)
            dck_ref[...] = jnp.zeros((1, nb, 1, tq), F32)

        qv, cqv, lsev = q_ref[0], cq_ref[0], lse_ref[0]
        dob = do_ref[0].astype(MXU_DT)
        delta = jnp.sum(dob.astype(F32) * o_ref[0], axis=-1, keepdims=True)

        def kblock(j, carry):
            dq, rs = carry
            r = pl.ds(pl.multiple_of(j * tq, tq), tq)
            kv, vv = k_ref[0, r, :], v_ref[0, r, :]
            pr = jnp.exp(_fox_logits(qv, kv, cqv, ck_ref[0, j], i, j, tq, scale) - lsev)
            ds = pr * (_dot(dob, vv, NT) - delta)
            dk_ref[0, r, :] += _dot_tn(ds, qv) * scale
            dv_ref[0, r, :] += _dot_tn(pr, dob)
            dck_ref[0, j] += -jnp.sum(ds, axis=0, keepdims=True)
            return dq + _dot(ds.astype(MXU_DT), kv), rs + jnp.sum(ds, axis=-1, keepdims=True)

        dq, rs = lax.fori_loop(0, i + 1, kblock, (jnp.zeros((tq, dh), F32), jnp.zeros((tq, 1), F32)))
        dq_ref[0] = dq * scale
        dcq_ref[0] = rs

    return _call(body, name=name, grid=(H, nb), in_specs=[qs, ks, ks, cqs, cks, qs, cqs, qs],
                 out_specs=[qs, ks, ks, cqs, cks],
                 out_shape=[_sds((H, S, dh), F32), _sds((H, S, dh), F32), _sds((H, S, dh), F32),
                            _sds((H, S, 1), F32), _sds((H, nb, 1, tq), F32)])(q, k, v, cq, ck, o, lse, do)


def _tri(n, lower):
    r = lax.broadcasted_iota(jnp.int32, (n, n), 0)
    c = lax.broadcasted_iota(jnp.int32, (n, n), 1)
    return ((c <= r) if lower else (c >= r)).astype(F32)


def _fgate_fwd(fr, bf, *, name):
    S, W = fr.shape
    B = _pick(S, (256, 128))

    def body(f_ref, b_ref, cum_ref):
        L = _tri(B, True)
        carry = jnp.zeros((1, W), F32)
        for blk in range(S // B):
            z = f_ref[blk * B:(blk + 1) * B, :] + b_ref[...]
            ls = jnp.minimum(z, 0.0) - jnp.log(1.0 + jnp.exp(-jnp.abs(z)))
            cum_ref[blk * B:(blk + 1) * B, :] = jnp.dot(L, ls, precision=lax.Precision.HIGHEST,
                                                        preferred_element_type=F32) + carry
            carry = carry + jnp.sum(ls, axis=0, keepdims=True)

    return _call(body, name=name, out_shape=_sds((S, W), F32))(fr, bf)


def _fgate_bwd(fr, bf, dcum, *, name):
    S, W = fr.shape
    B = _pick(S, (256, 128))

    def body(f_ref, b_ref, dc_ref, df_ref, db_ref):
        U = _tri(B, False)
        carry = jnp.zeros((1, W), F32)
        dbs = jnp.zeros((1, W), F32)
        for blk in reversed(range(S // B)):
            dc = dc_ref[blk * B:(blk + 1) * B, :]
            dls = jnp.dot(U, dc, precision=lax.Precision.HIGHEST, preferred_element_type=F32) + carry
            carry = carry + jnp.sum(dc, axis=0, keepdims=True)
            z = f_ref[blk * B:(blk + 1) * B, :] + b_ref[...]
            df = dls * (1.0 / (1.0 + jnp.exp(z)))
            df_ref[blk * B:(blk + 1) * B, :] = df
            dbs = dbs + jnp.sum(df, axis=0, keepdims=True)
        db_ref[...] = dbs

    return _call(body, name=name, out_shape=[_sds((S, W), F32), _sds((1, W), F32)])(fr, bf, dcum)


def _flip(v, bit):
    return 1 - v if bit else v


HBM_SPEC = pl.BlockSpec(memory_space=pltpu.HBM)
SEM_SPEC = pl.BlockSpec(memory_space=pltpu.SEMAPHORE)


def _xchg_copies(src, land, send, recv, scatter):
    x, y, c = lax.axis_index("x"), lax.axis_index("y"), lax.axis_index("c")
    me = 4 * x + 2 * y + c

    def peer(m):
        return _flip(x, m & 4), _flip(y, m & 2), _flip(c, m & 1)

    def copy(i, m):
        px, py, pc = peer(m)
        return pltpu.make_async_remote_copy(
            src_ref=src[i].at[4 * px + 2 * py + pc] if scatter[i] else src[i], dst_ref=land[i].at[me],
            send_sem=send[7 * i + m - 1], recv_sem=recv[7 * i + m - 1], device_id=(px, py, pc), device_id_type=MESH)

    def arrival(i, m):
        px, py, pc = peer(m)
        slot = land[i].at[4 * px + 2 * py + pc]
        return pltpu.make_async_remote_copy(src_ref=slot, dst_ref=slot, send_sem=send[7 * i + m - 1],
                                            recv_sem=recv[7 * i + m - 1], device_id=(px, py, pc), device_id_type=MESH)

    return copy, arrival


def _xchg_start(srcs, scatter, *, name):
    n = len(srcs)
    lands = [_sds((N_DEV,) + s.shape[-2:], s.dtype) for s in srcs]

    ns = 7 * n

    def body(*refs):
        src, land = refs[:n], refs[n:2 * n]
        send, recv, token = refs[2 * n:2 * n + ns], refs[2 * n + ns:2 * n + 2 * ns], refs[-1]
        copy, _ = _xchg_copies(src, land, send, recv, scatter)
        for i in range(n):
            for m in range(1, N_DEV):
                copy(i, m).start()
        token[...] = jnp.zeros(token.shape, F32)

    thru = [pltpu.HBM(s.shape, s.dtype) for s in srcs] + [pltpu.HBM(s.shape, s.dtype) for s in lands]
    out = pl.pallas_call(
        body, name=name,
        out_shape=(*[pltpu.SemaphoreType.DMA(())] * (2 * ns), *thru, _sds((8, 128), F32)),
        in_specs=[HBM_SPEC] * (2 * n),
        out_specs=(*[SEM_SPEC] * (2 * ns), *[HBM_SPEC] * (2 * n), pl.BlockSpec(memory_space=pltpu.VMEM)),
        input_output_aliases={i: 2 * ns + i for i in range(2 * n)},
        compiler_params=pltpu.CompilerParams(has_side_effects=pltpu.SideEffectType.DATAFLOW_SIDE_EFFECTING),
    )(*[pltpu.with_memory_space_constraint(s, pltpu.HBM) for s in srcs],
      *[pltpu.with_memory_space_constraint(lax.empty(s.shape, s.dtype), pltpu.HBM) for s in lands])
    sems, bufs = list(out[:2 * ns]), list(out[2 * ns:2 * ns + 2 * n])
    return (sems[:ns], sems[ns:], bufs[:n], bufs[n:]), out[-1]


def _xchg_wait(handle, after, scatter, *, name):
    send, recv, srcs, lands = handle
    n = len(srcs)
    ns = 7 * n

    def body(*refs):
        src, land = refs[:n], refs[n:2 * n]
        copy, arrival = _xchg_copies(src, land, refs[2 * n:2 * n + ns], refs[2 * n + ns:2 * n + 2 * ns], scatter)
        for i in range(n):
            for m in range(1, N_DEV):
                copy(i, m).wait_send()
                arrival(i, m).wait_recv()

    out = pl.pallas_call(
        body, name=name,
        out_shape=tuple(pltpu.HBM(s.shape, s.dtype) for s in srcs + lands),
        in_specs=[HBM_SPEC] * (2 * n) + [SEM_SPEC] * (2 * ns) + [ANY],
        out_specs=tuple([HBM_SPEC] * (2 * n)),
        input_output_aliases={i: i for i in range(2 * n)},
        compiler_params=pltpu.CompilerParams(has_side_effects=pltpu.SideEffectType.DATAFLOW_SIDE_EFFECTING),
    )(*srcs, *lands, *send, *recv, after)
    return list(out[:n]), list(out[n:])


def _fill_own(land, own, me):
    return lax.dynamic_update_slice(land, own[None], (me, 0, 0))


def _reduce_adam(recv, w, m, v, *, name):
    R, L = w.shape
    tr = _pick(R, (256, 128, 64, 32, 16, 8))

    def body(r_ref, w_ref, m_ref, v_ref, g_ref, d_ref, m2_ref, v2_ref):
        g = r_ref[0].astype(F32)
        for s in range(1, N_DEV):
            g = g + r_ref[s].astype(F32)
        mm = ADAM_B1 * m_ref[...] + (1.0 - ADAM_B1) * g
        vv = ADAM_B2 * v_ref[...] + (1.0 - ADAM_B2) * (g * g)
        m_hat = mm / (1.0 - ADAM_B1 ** ADAM_STEP)
        v_hat = vv / (1.0 - ADAM_B2 ** ADAM_STEP)
        g_ref[...] = g
        d_ref[...] = -ADAM_LR * (m_hat / (jnp.sqrt(v_hat) + ADAM_EPS) + ADAM_WD * w_ref[...])
        m2_ref[...] = mm
        v2_ref[...] = vv

    blk = _rowblk(tr, L)
    return _call(body, name=name, grid=(R // tr,),
                 in_specs=[pl.BlockSpec((N_DEV, tr, L), lambda i: (0, i, 0)), blk, blk, blk],
                 out_specs=[blk, blk, blk, blk], out_shape=[_sds((R, L), F32)] * 4)(recv, w, m, v)


class _Pack:
    def __init__(self, shapes, row_mult):
        self.shapes, self.offs, rows = dict(shapes), {}, 0
        for name, shp in shapes:
            size = 1
            for d in shp:
                size *= d
            nr = -(-size // (16 * PACK_LANES)) * 16
            self.offs[name] = (rows, size, nr)
            rows += nr
        self.used = rows
        self.rows = -(-rows // row_mult) * row_mult

    def pack(self, arrays, dtype, lead=()):
        parts = []
        for name, (r0, size, nr) in self.offs.items():
            flat = arrays[name].astype(dtype).reshape(lead + (size,))
            flat = jnp.pad(flat, [(0, 0)] * len(lead) + [(0, nr * PACK_LANES - size)])
            parts.append(flat.reshape(lead + (nr, PACK_LANES)))
        if self.rows > self.used:
            parts.append(jnp.zeros(lead + (self.rows - self.used, PACK_LANES), dtype))
        return jnp.concatenate(parts, axis=len(lead))

    def unpack(self, buf, lead=()):
        out = {}
        for name, (r0, size, nr) in self.offs.items():
            flat = buf[..., r0:r0 + nr, :].reshape(lead + (nr * PACK_LANES,))
            out[name] = flat[..., :size].reshape(lead + tuple(self.shapes[name]))
        return out


def _to_full(g8, ax):
    t = jnp.moveaxis(g8, 0, ax)
    return t.reshape(t.shape[:ax] + (t.shape[ax] * t.shape[ax + 1],) + t.shape[ax + 2:])


def _to_shards(full, ax):
    shp = full.shape
    return jnp.moveaxis(full.reshape(shp[:ax] + (N_DEV, shp[ax] // N_DEV) + shp[ax + 1:]), ax, 0)


def _to_heads(a, H):
    S = a.shape[0]
    return a.reshape(S, H, HEAD_DIM).transpose(1, 0, 2)


def _from_heads(a):
    H, S, dh = a.shape
    return a.transpose(1, 0, 2).reshape(S, H * dh)


def kernel(x, mem, g_mix, w_in_a, b_glu, w_dw_a, b_dw_a, ln_g, ln_b, g_kv, w_kvf, b_f, w_in_b, g_mem, w_mem_kv, w_out, g_ffn, w_up, w_dw_f, b_dw_f, w_down, g_final, loss_target, m_g_mix, m_w_in_a, m_b_glu, m_w_dw_a, m_b_dw_a, m_ln_g, m_ln_b, m_g_kv, m_w_kvf, m_b_f, m_w_in_b, m_g_mem, m_w_mem_kv, m_w_out, m_g_ffn, m_w_up, m_w_dw_f, m_b_dw_f, m_w_down, m_g_final, v_g_mix, v_w_in_a, v_b_glu, v_w_dw_a, v_b_dw_a, v_ln_g, v_ln_b, v_g_kv, v_w_kvf, v_b_f, v_w_in_b, v_g_mem, v_w_mem_kv, v_w_out, v_g_ffn, v_w_up, v_w_dw_f, v_b_dw_f, v_w_down, v_g_final):
    given = dict(locals())
    W = {n: given[n] for n in WEIGHTS}
    x0, mem0, tgt = x[0], mem[0], loss_target[0]
    S, D = x0.shape
    depth, n_a = g_mix.shape[0], w_in_a.shape[0]
    C = w_dw_a.shape[2] * N_DEV
    Mw = D - C
    Fw = w_down.shape[1] * N_DEV
    H = b_f.shape[0]
    assert C == H * HEAD_DIM and (2 * C) % Mw == 0 and C % Mw == 0 and H <= GATE_LANES
    tq = _pick(S, (256, 128))
    nkv = 2 * C + GATE_LANES

    big_ax = dict(BIG)

    def layer_keys(l):
        keys = [("w_in_a", l) if l < n_a else ("w_in_b", l - n_a), ("w_mem_kv", l), ("w_out", l), ("w_up", l),
                ("w_down", l)]
        return keys + ([("w_kvf", None)] if l == n_a else [])

    def part(d, key, pre=""):
        return d[pre + key[0]] if key[1] is None else d[pre + key[0]][key[1]]

    def key_ax(key):
        return big_ax[key[0]] - (0 if key[1] is None else 1)

    def pack_layer(l, d, pre=""):
        return pk_layer[l].pack({k: part(d, k, pre) for k in layer_keys(l)}, F32)

    pk_layer = [_Pack([(k, part(W, k).shape) for k in layer_keys(l)], 256) for l in range(depth)]
    pk_small = _Pack([(n, W[n].shape) for n, _ in SMALL], 8)
    pk_rep = _Pack([(n, W[n].shape) for n in REP], 8)
    me = 4 * lax.axis_index("x") + 2 * lax.axis_index("y") + lax.axis_index("c")

    wl32 = [pack_layer(l, W) for l in range(depth)]
    ws32 = pk_small.pack(W, F32)
    gathers, tok = [], 0.0
    for l in range(depth):
        srcs = [wl32[l].astype(COMM_DT)] + ([ws32] if l == 0 else [])
        handle, t = _xchg_start(srcs, [False] * len(srcs), name=f"w_gather_start{l}")
        gathers.append(handle)
        tok = tok + t[0, 0]
    g_mix, g_mem = g_mix + tok, g_mem + tok
    bf_pad = jnp.pad(b_f, (0, GATE_LANES - H)).reshape(1, GATE_LANES)

    def layer_weights(l, after):
        n = len(gathers[l][2])
        srcs, lands = _xchg_wait(gathers[l], after, [False] * n, name=f"w_gather_wait{l}")
        wl = {k: _to_full(a, key_ax(k)).astype(MXU_DT)
              for k, a in pk_layer[l].unpack(_fill_own(lands[0], srcs[0], me), (N_DEV,)).items()}
        if l > 0:
            return wl, None
        gs = pk_small.unpack(_fill_own(lands[1], srcs[1], me), (N_DEV,))
        return wl, {n: _to_full(gs[n], ax) for n, ax in SMALL}

    mem_n = _rms_fwd(mem0, g_mem, name="mem_norm")
    sv = []
    xs = x0
    for l in range(depth):
        wl, sm = layer_weights(l, xs)
        if sm is not None:
            small = sm
        t = dict(x_in=xs, w=wl)
        t["mkv"] = _mm(mem_n, wl["w_mem_kv", l], name=f"mem_kv{l}", out_dtype=MXU_DT)
        t["h"] = _rms_fwd(xs, g_mix[l], name=f"mix_norm{l}")
        if l < n_a:
            t["p"] = _mm(t["h"], wl["w_in_a", l], name=f"in_proj{l}", out_dtype=F32)
            t["v2"] = _glu_conv_fwd(t["p"], small["b_glu"][l], small["w_dw_a"][l], small["b_dw_a"][l],
                                    name=f"glu_conv{l}")
            memo = _mem_attn_fwd(t["p"], 2 * C // Mw, t["mkv"], 0, Mw=Mw, name=f"mem_attn{l}")
            t["cat"] = _ln_silu_cat_fwd(t["v2"], small["ln_g"][l], small["ln_b"][l], memo, name=f"ln_silu{l}")
        else:
            if l == n_a:
                wkvf = jnp.pad(wl["w_kvf", None], ((0, 0), (0, nkv - w_kvf.shape[1])))
                hk = _rms_fwd(xs, g_kv, name="kv_norm")
                kvf = _mm(hk, wkvf, name="kv_proj", out_dtype=F32)
                k_h = _to_heads(kvf[:, :C], H).astype(MXU_DT)
                v_h = _to_heads(kvf[:, C:2 * C], H).astype(MXU_DT)
                fr = kvf[:, 2 * C:]
                cum = _fgate_fwd(fr, bf_pad, name="fgate")
                cum_t = cum[:, :H].T
                cq, ck = cum_t.reshape(H, S, 1), cum_t.reshape(H, S // tq, 1, tq)
            t["p"] = _mm(t["h"], wl["w_in_b", l - n_a], name=f"in_proj{l}", out_dtype=F32)
            t["q_h"] = _to_heads(t["p"][:, :C], H).astype(MXU_DT)
            t["o_h"], t["lse"] = _fox_fwd(t["q_h"], k_h, v_h, cq, ck, tq=tq, name=f"fox{l}")
            memo = _mem_attn_fwd(t["p"], C // Mw, t["mkv"], 0, Mw=Mw, name=f"mem_attn{l}")
            t["cat"] = jnp.concatenate([_from_heads(t["o_h"]), memo], axis=1).astype(MXU_DT)
        t["x_mid"] = _mm(t["cat"], wl["w_out", l], name=f"out_proj{l}", out_dtype=F32, add=xs)
        t["h2"] = _rms_fwd(t["x_mid"], g_ffn[l], name=f"ffn_norm{l}")
        t["ug"] = _mm(t["h2"], wl["w_up", l][:, :Fw], name=f"up_gate{l}", out_dtype=F32)
        t["uv"] = _mm(t["h2"], wl["w_up", l][:, Fw:], name=f"up_val{l}", out_dtype=F32)
        t["act"] = _ffn_act_fwd(t["ug"], t["uv"], small["w_dw_f"][l], b_dw_f[l], name=f"ffn_act{l}")
        xs = _mm(t["act"], wl["w_down", l], name=f"down_proj{l}", out_dtype=F32, add=t["x_mid"])
        sv.append(t)
    loss_dev, dx, dg_final = _loss_bwd(xs, g_final, tgt, name="loss_head")

    M1 = {n: given["m_" + n] for n in WEIGHTS}
    V1 = {n: given["v_" + n] for n in WEIGHTS}
    res = {}

    def update(pk, recv, w32, m32, v32, name):
        for kind, buf in zip(("grad", "delta", "new_m", "new_v"), _reduce_adam(recv, w32, m32, v32, name=name)):
            for k, a in pk.unpack(buf).items():
                res[kind, k] = a

    def finish_layer(l, handle, after):
        srcs, lands = _xchg_wait(handle, after, [True], name=f"g_xchg_wait{l}")
        recv = _fill_own(lands[0], lax.dynamic_index_in_dim(srcs[0], me, 0, keepdims=False), me)
        update(pk_layer[l], recv, wl32[l], pack_layer(l, given, "m_"), pack_layer(l, given, "v_"), f"adam_layer{l}")

    G = {n: [None] * W[n].shape[0] for n in ("g_mix", "b_glu", "w_dw_a", "b_dw_a", "ln_g", "ln_b", "g_ffn",
                                              "w_dw_f", "b_dw_f")}
    mem_nt = mem_n.T
    dk_sum = dv_sum = dck_sum = dmem_n = pending = None
    btok = 0.0
    for l in reversed(range(depth)):
        t = sv[l]
        wl, gl = t["w"], {}
        dxb = (dx + btok).astype(MXU_DT)
        dact = _mm(dxb, wl["w_down", l], name=f"d_act{l}", out_dtype=F32, nt=True)
        gl["w_down", l] = _mm(t["act"].T, dxb, name=f"dw_down{l}", out_dtype=COMM_DT)
        dug, duv, G["w_dw_f"][l], db = _ffn_act_bwd(t["ug"], t["uv"], dact, small["w_dw_f"][l], b_dw_f[l],
                                                    name=f"d_ffn_act{l}")
        G["b_dw_f"][l] = db[0]
        dh2 = _mm(dug, wl["w_up", l][:, :Fw], name=f"d_up_gate{l}", out_dtype=F32, nt=True, mode="m")
        dh2 = _mm(duv, wl["w_up", l][:, Fw:], name=f"d_up_val{l}", out_dtype=F32, nt=True, mode="m", add=dh2)
        h2t = t["h2"].T
        gl["w_up", l] = jnp.concatenate([_mm(h2t, dug, name=f"dw_up_gate{l}", out_dtype=COMM_DT),
                                         _mm(h2t, duv, name=f"dw_up_val{l}", out_dtype=COMM_DT)], axis=1)
        dx, dg = _rms_bwd(t["x_mid"], g_ffn[l], dh2, dx, name=f"d_ffn_norm{l}")
        G["g_ffn"][l] = dg[0]

        dxb = dx.astype(MXU_DT)
        dcat = _mm(dxb, wl["w_out", l], name=f"d_cat{l}", out_dtype=F32, nt=True)
        gl["w_out", l] = _mm(t["cat"].T, dxb, name=f"dw_out{l}", out_dtype=COMM_DT)
        if l >= n_a:
            do_h = _to_heads(dcat[:, :C], H)
            dq_h, dk_h, dv_h, dcq, dck = _fox_bwd(t["q_h"], k_h, v_h, cq, ck, t["o_h"], t["lse"], do_h, tq=tq,
                                                  name=f"d_fox{l}")
            dck = dck.reshape(H, S) + dcq.reshape(H, S)
            dk_sum = dk_h if dk_sum is None else dk_sum + dk_h
            dv_sum = dv_h if dv_sum is None else dv_sum + dv_h
            dck_sum = dck if dck_sum is None else dck_sum + dck
            dqm, dmk, dmv = _mem_attn_bwd(t["p"], C // Mw, t["mkv"], 0, dcat, C // Mw, Mw=Mw,
                                          name=f"d_mem_attn{l}")
            dp = jnp.concatenate([_from_heads(dq_h), dqm], axis=1).astype(MXU_DT)
            key = ("w_in_b", l - n_a)
        else:
            dv2, dlg, dlb = _ln_silu_bwd(t["v2"], small["ln_g"][l], small["ln_b"][l], dcat, name=f"d_ln_silu{l}")
            G["ln_g"][l], G["ln_b"][l] = dlg[0], dlb[0]
            da, dgt, G["w_dw_a"][l], dbd, dba, dbg = _glu_conv_bwd(t["p"], small["b_glu"][l], small["w_dw_a"][l],
                                                                   dv2, name=f"d_glu_conv{l}")
            G["b_dw_a"][l] = dbd[0]
            G["b_glu"][l] = jnp.concatenate([dba[0], dbg[0]])
            dqm, dmk, dmv = _mem_attn_bwd(t["p"], 2 * C // Mw, t["mkv"], 0, dcat, C // Mw, Mw=Mw,
                                          name=f"d_mem_attn{l}")
            dp = jnp.concatenate([da, dgt, dqm], axis=1).astype(MXU_DT)
            key = ("w_in_a", l)
        dmkv = jnp.concatenate([dmk, dmv], axis=1).astype(MXU_DT)
        gl["w_mem_kv", l] = _mm(mem_nt, dmkv, name=f"dw_mem_kv{l}", out_dtype=COMM_DT)
        dmem_n = _mm(dmkv, wl["w_mem_kv", l], name=f"d_mem_kv{l}", out_dtype=F32, nt=True, add=dmem_n)
        dh = _mm(dp, wl[key], name=f"d_in_proj{l}", out_dtype=F32, nt=True)
        gl[key] = _mm(t["h"].T, dp, name=f"dw_in_proj{l}", out_dtype=COMM_DT)
        dx, dg = _rms_bwd(t["x_in"], g_mix[l], dh, dx, name=f"d_mix_norm{l}")
        G["g_mix"][l] = dg[0]
        if l == n_a:
            dcum = jnp.pad(dck_sum.T, ((0, 0), (0, GATE_LANES - H)))
            df, dbf = _fgate_bwd(fr, bf_pad, dcum, name="d_fgate")
            dkvf = jnp.concatenate([_from_heads(dk_sum), _from_heads(dv_sum), df], axis=1).astype(MXU_DT)
            dhk = _mm(dkvf, wkvf, name="d_kv_proj", out_dtype=F32, nt=True)
            gl["w_kvf", None] = _mm(hk.T, dkvf, name="dw_kv_proj", out_dtype=COMM_DT)[:, :w_kvf.shape[1]]
            dx, dg_kv = _rms_bwd(t["x_in"], g_kv, dhk, dx, name="d_kv_norm")

        src = pk_layer[l].pack({k: _to_shards(gl[k], key_ax(k)) for k in layer_keys(l)}, COMM_DT, (N_DEV,))
        if l > 0:
            handle, tk = _xchg_start([src], [True], name=f"g_xchg_start{l}")
            btok = tk[0, 0]
        if pending is not None:
            finish_layer(l + 1, pending, dx)
        pending = handle if l > 0 else None

    _, dg_mem = _rms_bwd(mem0, g_mem, dmem_n, None, name="d_mem_norm")
    grads = {n: jnp.stack(v) for n, v in G.items()}
    grads.update(g_kv=dg_kv[0], b_f=dbf[0, :H], g_mem=dg_mem[0], g_final=dg_final[0])
    gs8 = pk_small.pack({n: _to_shards(grads[n], ax) for n, ax in SMALL}, F32, (N_DEV,))
    gr = pk_rep.pack(grads, F32)
    scatter = [True, True, False]
    handle, _ = _xchg_start([src, gs8, gr], scatter, name="g_xchg_start0")
    srcs, lands = _xchg_wait(handle, dx, scatter, name="g_xchg_wait0")
    own = [lax.dynamic_index_in_dim(srcs[0], me, 0, keepdims=False),
           lax.dynamic_index_in_dim(srcs[1], me, 0, keepdims=False), srcs[2]]
    recv = [_fill_own(ld, o, me) for ld, o in zip(lands, own)]
    update(pk_layer[0], recv[0], wl32[0], pack_layer(0, given, "m_"), pack_layer(0, given, "v_"), "adam_layer0")
    update(pk_small, recv[1], ws32, pk_small.pack(M1, F32), pk_small.pack(V1, F32), "adam_small")
    update(pk_rep, recv[2], pk_rep.pack(W, F32), pk_rep.pack(M1, F32), pk_rep.pack(V1, F32), "adam_rep")

    def result(kind, n):
        if (kind, n) in res:
            return res[kind, n]
        if (kind, (n, None)) in res:
            return res[kind, (n, None)]
        return jnp.stack([res[kind, (n, i)] for i in range(W[n].shape[0])])

    loss = lax.psum(loss_dev, ("x", "y", "c"))
    return (loss, dx[None], *[result(kind, n) for kind in ("grad", "delta", "new_m", "new_v") for n in WEIGHTS])
```

```python
import jax
import jax.numpy as jnp
from jax import lax
from jax.experimental import pallas as pl
from jax.experimental.pallas import tpu as pltpu

F32 = jnp.float32
MXU_DT = jnp.bfloat16
COMM_DT = jnp.bfloat16

N_DEV = 8
HEAD_DIM = 64
RMS_EPS = 1e-6
LN_EPS = 1e-5
ADAM_LR = 0.001
ADAM_B1 = 0.9
ADAM_B2 = 0.999
ADAM_EPS = 1e-08
ADAM_WD = 0.01
ADAM_STEP = 10

PACK_LANES = 1024
GATE_LANES = 128
VMEM_LIMIT_V7X = 56 << 20
NEG = -1e30
MESH = pl.DeviceIdType.MESH
ANY = pl.BlockSpec(memory_space=pl.ANY)
NT = (((1,), (1,)), ((), ()))
NN = (((1,), (0,)), ((), ()))

BIG = (("w_in_a", 2), ("w_kvf", 0), ("w_in_b", 1), ("w_mem_kv", 1), ("w_out", 1), ("w_up", 2), ("w_down", 1))
SMALL = (("b_glu", 1), ("w_dw_a", 2), ("b_dw_a", 1), ("ln_g", 1), ("ln_b", 1), ("w_dw_f", 2))
REP = ("g_mix", "g_kv", "b_f", "g_mem", "g_ffn", "b_dw_f", "g_final")
WEIGHTS = ("g_mix", "w_in_a", "b_glu", "w_dw_a", "b_dw_a", "ln_g", "ln_b", "g_kv", "w_kvf", "b_f", "w_in_b",
           "g_mem", "w_mem_kv", "w_out", "g_ffn", "w_up", "w_dw_f", "b_dw_f", "w_down", "g_final")


def _sds(shape, dtype):
    return jax.ShapeDtypeStruct(tuple(shape), dtype)


def _call(body, *, name, out_shape, grid=(), in_specs=None, out_specs=None, scratch_shapes=()):
    params = dict(vmem_limit_bytes=VMEM_LIMIT_V7X)
    if grid:
        params["dimension_semantics"] = ("arbitrary",) * len(grid)
    kw = {}
    if in_specs is not None:
        kw["in_specs"] = in_specs
    if out_specs is not None:
        kw["out_specs"] = out_specs
    return pl.pallas_call(body, name=name, grid=grid, out_shape=out_shape, scratch_shapes=list(scratch_shapes),
                          compiler_params=pltpu.CompilerParams(**params), **kw)


def _res(shape):
    nd = len(shape)
    return pl.BlockSpec(tuple(shape), lambda *_: (0,) * nd)


def _colblk(rows, tc, off=0):
    return pl.BlockSpec((rows, tc), lambda j: (0, j + off))


def _rowblk(tm, cols, off=0):
    return pl.BlockSpec((tm, cols), lambda i: (i, off))


def _pick(n, opts=(512, 256, 128)):
    for t in opts:
        if n % t == 0:
            return t
    return n


def _sig(z):
    return 1.0 / (1.0 + jnp.exp(-z))


def _dot(a, b, dims=NN):
    return lax.dot_general(a, b, dims, preferred_element_type=F32)


def _dot_tn(a, b):
    return _dot(a.T.astype(b.dtype), b)


def _mm_tn(a, b, *, name, out_dtype):
    S, K = a.shape
    N = b.shape[1]
    tk, rc = _pick(K, (256, 128)), _pick(S)

    def body(a_ref, b_ref, o_ref, acc_ref):
        for n, r0 in enumerate(range(0, S, rc)):
            part = _dot(a_ref[r0:r0 + rc, :].astype(F32).T.astype(MXU_DT), b_ref[r0:r0 + rc, :])
            if n == 0:
                acc_ref[...] = part
            else:
                acc_ref[...] += part
        o_ref[...] = acc_ref[...].astype(out_dtype)

    return _call(body, name=name, grid=(K // tk,), in_specs=[_colblk(S, tk), _res((S, N))],
                 out_specs=pl.BlockSpec((tk, N), lambda j: (j, 0)), out_shape=_sds((K, N), out_dtype),
                 scratch_shapes=[pltpu.VMEM((tk, N), F32)])(a, b)


def _mm_nt2(a1, a2, w, *, name):
    M, Fw = a1.shape
    N = w.shape[0]
    tm, nc = _pick(M), _pick(N)

    def body(a1_ref, a2_ref, w_ref, o_ref):
        v1, v2 = a1_ref[...], a2_ref[...]
        for n0 in range(0, N, nc):
            o_ref[:, n0:n0 + nc] = (_dot(v1, w_ref[n0:n0 + nc, 0:Fw], NT) + _dot(v2, w_ref[n0:n0 + nc, Fw:2 * Fw], NT))

    return _call(body, name=name, grid=(M // tm,), in_specs=[_rowblk(tm, Fw), _rowblk(tm, Fw), _res(w.shape)],
                 out_specs=_rowblk(tm, N), out_shape=_sds((M, N), F32))(a1, a2, w)


def _mm(a, w, *, name, out_dtype, nt=False, add=None, mode="n", cols=None):
    M, K = a.shape
    N = w.shape[0] if nt else w.shape[1]
    assert (w.shape[1] if nt else w.shape[0]) == K
    dims = NT if nt else NN
    has_add = add is not None
    if cols is not None:
        assert mode == "n" and not nt and not has_add
        c0, N = cols
        tn, rc = _pick(N), _pick(M)
        assert c0 % tn == 0

        def body(a_ref, w_ref, o_ref):
            wv = w_ref[...]
            for r0 in range(0, M, rc):
                o_ref[r0:r0 + rc, :] = _dot(a_ref[r0:r0 + rc, :], wv).astype(out_dtype)

        return _call(body, name=name, grid=(N // tn,), in_specs=[_res((M, K)), _colblk(K, tn, c0 // tn)],
                     out_specs=_colblk(M, tn), out_shape=_sds((M, N), out_dtype))(a, w)
    if mode == "n":
        tn, rc = _pick(N), _pick(M)

        def body(*refs):
            a_ref, w_ref, o_ref = refs[0], refs[1], refs[-1]
            wv = w_ref[...]
            for r0 in range(0, M, rc):
                acc = _dot(a_ref[r0:r0 + rc, :], wv, dims)
                if has_add:
                    acc = acc + refs[2][r0:r0 + rc, :]
                o_ref[r0:r0 + rc, :] = acc.astype(out_dtype)

        w_spec = pl.BlockSpec((tn, K), lambda j: (j, 0)) if nt else pl.BlockSpec((K, tn), lambda j: (0, j))
        in_specs = [_res((M, K)), w_spec] + ([_colblk(M, tn)] if has_add else [])
        out_specs, grid = _colblk(M, tn), (N // tn,)
    else:
        tm, nc = _pick(M), _pick(N)

        def body(*refs):
            a_ref, w_ref, o_ref = refs[0], refs[1], refs[-1]
            av = a_ref[...]
            for n0 in range(0, N, nc):
                wv = w_ref[n0:n0 + nc, :] if nt else w_ref[:, n0:n0 + nc]
                acc = _dot(av, wv, dims)
                if has_add:
                    acc = acc + refs[2][:, n0:n0 + nc]
                o_ref[:, n0:n0 + nc] = acc.astype(out_dtype)

        in_specs = [_rowblk(tm, K), _res(w.shape)] + ([_rowblk(tm, N)] if has_add else [])
        out_specs, grid = _rowblk(tm, N), (M // tm,)
    args = (a, w) + ((add,) if has_add else ())
    return _call(body, name=name, grid=grid, in_specs=in_specs, out_specs=out_specs,
                 out_shape=_sds((M, N), out_dtype))(*args)


def _rms_fwd(x, g, *, name):
    M, D = x.shape
    tm = _pick(M, (256, 128))

    def body(x_ref, g_ref, h_ref):
        xf = x_ref[...]
        r = lax.rsqrt(jnp.mean(xf * xf, axis=-1, keepdims=True) + RMS_EPS)
        h_ref[...] = ((xf * r) * g_ref[...]).astype(MXU_DT)

    return _call(body, name=name, grid=(M // tm,), in_specs=[_rowblk(tm, D), _res((1, D))],
                 out_specs=_rowblk(tm, D), out_shape=_sds((M, D), MXU_DT))(x, g.reshape(1, D))


def _rms_bwd(x, g, dh, dx_in, *, name):
    M, D = x.shape
    tm = _pick(M, (256, 128))
    with_dx = dx_in is not None

    def body(*refs):
        if with_dx:
            x_ref, g_ref, dh_ref, dxin_ref, dx_ref, dg_ref = refs
        else:
            x_ref, g_ref, dh_ref, dg_ref = refs
        i = pl.program_id(0)
        xf = x_ref[...]
        r = lax.rsqrt(jnp.mean(xf * xf, axis=-1, keepdims=True) + RMS_EPS)
        y = xf * r
        dh_v = dh_ref[...]
        if with_dx:
            dy = dh_v * g_ref[...]
            dx_ref[...] = dxin_ref[...] + r * (dy - y * jnp.mean(dy * y, axis=-1, keepdims=True))
        part = jnp.sum(dh_v * y, axis=0, keepdims=True)

        @pl.when(i == 0)
        def _():
            dg_ref[...] = part

        @pl.when(i > 0)
        def _():
            dg_ref[...] += part

    ins = [x, g.reshape(1, D), dh] + ([dx_in] if with_dx else [])
    in_specs = [_rowblk(tm, D), _res((1, D)), _rowblk(tm, D)] + ([_rowblk(tm, D)] if with_dx else [])
    if with_dx:
        out_specs, out_shape = [_rowblk(tm, D), _res((1, D))], [_sds((M, D), F32), _sds((1, D), F32)]
    else:
        out_specs, out_shape = [_res((1, D))], [_sds((1, D), F32)]
    out = _call(body, name=name, grid=(M // tm,), in_specs=in_specs, out_specs=out_specs, out_shape=out_shape)(*ins)
    return out if with_dx else (None, out[0])


def _loss_bwd(x, g, t, *, name):
    M, D = x.shape
    tm = _pick(M, (256, 128))

    def body(x_ref, g_ref, t_ref, dx_ref, dg_ref, ls_ref):
        i = pl.program_id(0)
        xf = x_ref[...]
        r = lax.rsqrt(jnp.mean(xf * xf, axis=-1, keepdims=True) + RMS_EPS)
        xr = xf * r
        e = xr * g_ref[...] - t_ref[...]
        dout = e * (1.0 / D)
        dy = dout * g_ref[...]
        dx_ref[...] = r * (dy - xr * jnp.mean(dy * xr, axis=-1, keepdims=True))
        part = jnp.sum(dout * xr, axis=0, keepdims=True)
        lpart = jnp.zeros(ls_ref.shape, F32) + (0.5 / D) * jnp.sum(e * e, keepdims=True)

        @pl.when(i == 0)
        def _():
            dg_ref[...] = part
            ls_ref[...] = lpart

        @pl.when(i > 0)
        def _():
            dg_ref[...] += part
            ls_ref[...] += lpart

    dx, dg, ls = _call(body, name=name, grid=(M // tm,),
                       in_specs=[_rowblk(tm, D), _res((1, D)), _rowblk(tm, D)],
                       out_specs=[_rowblk(tm, D), _res((1, D)), _res((8, 128))],
                       out_shape=[_sds((M, D), F32), _sds((1, D), F32), _sds((8, 128), F32)])(x, g.reshape(1, D), t)
    return ls[0, 0], dx, dg


def _shift_rows(ext, off, rows):
    if off % 8 == 0:
        return ext[off:off + rows, :]
    return pltpu.roll(ext, ext.shape[0] - off, 0)[0:rows, :]


def _ext(pad_ref, c, rows, halo):
    return pad_ref[pl.ds(pl.multiple_of(c * rows, rows), rows + halo), :]


def _conv_chunk(pad_ref, c, rows, halo, w_ref, taps):
    ext = _ext(pad_ref, c, rows, halo)
    acc = None
    for k in range(taps):
        term = w_ref[k:k + 1, :] * _shift_rows(ext, halo - (taps - 1) + k, rows)
        acc = term if acc is None else acc + term
    return acc


def _conv_t_chunk(pad_ref, c, rows, halo, w_ref, taps):
    ext = _ext(pad_ref, c, rows, halo)
    acc = None
    for k in range(taps):
        term = w_ref[k:k + 1, :] * _shift_rows(ext, taps - 1 - k, rows)
        acc = term if acc is None else acc + term
    return acc


def _conv_wgrad_chunk(pad_ref, c, rows, halo, dy, dw_ref, taps):
    ext = _ext(pad_ref, c, rows, halo)
    for k in range(taps):
        dw_ref[k:k + 1, :] += jnp.sum(dy * _shift_rows(ext, halo - (taps - 1) + k, rows), axis=0, keepdims=True)


A_HALO, F_HALO = 32, 8


def _glu_conv_fwd(p, b_glu, w_dw, b_dw, *, name):
    S = p.shape[0]
    taps, C = w_dw.shape
    tc, rows = 128, _pick(S, (256, 128))
    nb, nch = C // tc, S // rows

    def body(a_ref, g_ref, ba_ref, bg_ref, w_ref, bd_ref, o_ref, pad_ref):
        pad_ref[0:A_HALO, :] = jnp.zeros((A_HALO, tc), F32)

        def fill(c, _):
            r = pl.ds(pl.multiple_of(c * rows, rows), rows)
            v1 = (a_ref[r, :] + ba_ref[...]) * _sig(g_ref[r, :] + bg_ref[...])
            pad_ref[pl.ds(pl.multiple_of(A_HALO + c * rows, 8), rows), :] = v1
            return 0

        lax.fori_loop(0, nch, fill, 0)

        def conv(c, _):
            o_ref[pl.ds(pl.multiple_of(c * rows, rows), rows), :] = (
                _conv_chunk(pad_ref, c, rows, A_HALO, w_ref, taps) + bd_ref[...])
            return 0

        lax.fori_loop(0, nch, conv, 0)

    b2 = b_glu.reshape(1, 2 * C)
    return _call(body, name=name, grid=(nb,),
                 in_specs=[_colblk(S, tc), _colblk(S, tc, nb), _colblk(1, tc), _colblk(1, tc, nb),
                           _colblk(taps, tc), _colblk(1, tc)],
                 out_specs=_colblk(S, tc), out_shape=_sds((S, C), F32),
                 scratch_shapes=[pltpu.VMEM((S + A_HALO, tc), F32)])(p, p, b2, b2, w_dw, b_dw.reshape(1, C))


def _glu_conv_bwd(p, b_glu, w_dw, dv2, *, name):
    S = p.shape[0]
    taps, C = w_dw.shape
    tc, rows = 128, _pick(S, (256, 128))
    nb, nch = C // tc, S // rows

    def body(a_ref, g_ref, ba_ref, bg_ref, w_ref, dy_ref, da_ref, dgt_ref, dw_ref, dbd_ref, dba_ref, dbg_ref,
             padx_ref, pady_ref):
        padx_ref[0:A_HALO, :] = jnp.zeros((A_HALO, tc), F32)
        pady_ref[S:S + A_HALO, :] = jnp.zeros((A_HALO, tc), F32)
        dw_ref[...] = jnp.zeros((taps, tc), F32)

        def fill(c, _):
            r = pl.ds(pl.multiple_of(c * rows, rows), rows)
            v1 = (a_ref[r, :] + ba_ref[...]) * _sig(g_ref[r, :] + bg_ref[...])
            padx_ref[pl.ds(pl.multiple_of(A_HALO + c * rows, 8), rows), :] = v1
            pady_ref[r, :] = dy_ref[r, :]
            return 0

        lax.fori_loop(0, nch, fill, 0)

        def back(c, carry):
            sd, sa, sg = carry
            r = pl.ds(pl.multiple_of(c * rows, rows), rows)
            dy = dy_ref[r, :]
            _conv_wgrad_chunk(padx_ref, c, rows, A_HALO, dy, dw_ref, taps)
            dv1 = _conv_t_chunk(pady_ref, c, rows, A_HALO, w_ref, taps)
            a = a_ref[r, :] + ba_ref[...]
            s = _sig(g_ref[r, :] + bg_ref[...])
            da = dv1 * s
            dgt = dv1 * a * s * (1.0 - s)
            da_ref[r, :] = da
            dgt_ref[r, :] = dgt
            return (sd + jnp.sum(dy, axis=0, keepdims=True), sa + jnp.sum(da, axis=0, keepdims=True),
                    sg + jnp.sum(dgt, axis=0, keepdims=True))

        z = jnp.zeros((1, tc), F32)
        sd, sa, sg = lax.fori_loop(0, nch, back, (z, z, z))
        dbd_ref[...] = sd
        dba_ref[...] = sa
        dbg_ref[...] = sg

    b2 = b_glu.reshape(1, 2 * C)
    return _call(body, name=name, grid=(nb,),
                 in_specs=[_colblk(S, tc), _colblk(S, tc, nb), _colblk(1, tc), _colblk(1, tc, nb),
                           _colblk(taps, tc), _colblk(S, tc)],
                 out_specs=[_colblk(S, tc), _colblk(S, tc), _colblk(taps, tc), _colblk(1, tc), _colblk(1, tc),
                            _colblk(1, tc)],
                 out_shape=[_sds((S, C), F32), _sds((S, C), F32), _sds((taps, C), F32), _sds((1, C), F32),
                            _sds((1, C), F32), _sds((1, C), F32)],
                 scratch_shapes=[pltpu.VMEM((S + A_HALO, tc), F32), pltpu.VMEM((S + A_HALO, tc), F32)])(
                     p, p, b2, b2, w_dw, dv2)


def _ln_silu_cat_fwd(v2, ln_g, ln_b, memo, *, name):
    S, C = v2.shape
    Mw = memo.shape[1]
    tm = _pick(S, (256, 128))

    def body(v_ref, g_ref, b_ref, m_ref, o_ref):
        v = v_ref[...]
        mu = jnp.mean(v, axis=-1, keepdims=True)
        d = v - mu
        y = d * lax.rsqrt(jnp.mean(d * d, axis=-1, keepdims=True) + LN_EPS) * g_ref[...] + b_ref[...]
        o_ref[:, 0:C] = (y * _sig(y)).astype(MXU_DT)
        o_ref[:, C:C + Mw] = m_ref[...].astype(MXU_DT)

    return _call(body, name=name, grid=(S // tm,),
                 in_specs=[_rowblk(tm, C), _res((1, C)), _res((1, C)), _rowblk(tm, Mw)],
                 out_specs=_rowblk(tm, C + Mw), out_shape=_sds((S, C + Mw), MXU_DT))(
                     v2, ln_g.reshape(1, C), ln_b.reshape(1, C), memo)


def _ln_silu_bwd(v2, ln_g, ln_b, dcat, *, name):
    S, C = v2.shape
    tm = _pick(S, (256, 128))

    def body(v_ref, g_ref, b_ref, dm_ref, dv_ref, dg_ref, db_ref):
        i = pl.program_id(0)
        v = v_ref[...]
        mu = jnp.mean(v, axis=-1, keepdims=True)
        d = v - mu
        rstd = lax.rsqrt(jnp.mean(d * d, axis=-1, keepdims=True) + LN_EPS)
        xh = d * rstd
        y = xh * g_ref[...] + b_ref[...]
        s = _sig(y)
        dyv = dm_ref[...] * (s * (1.0 + y * (1.0 - s)))
        dxh = dyv * g_ref[...]
        dv_ref[...] = rstd * (dxh - jnp.mean(dxh, axis=-1, keepdims=True)
                              - xh * jnp.mean(dxh * xh, axis=-1, keepdims=True))
        pg = jnp.sum(dyv * xh, axis=0, keepdims=True)
        pb = jnp.sum(dyv, axis=0, keepdims=True)

        @pl.when(i == 0)
        def _():
            dg_ref[...] = pg
            db_ref[...] = pb

        @pl.when(i > 0)
        def _():
            dg_ref[...] += pg
            db_ref[...] += pb

    return _call(body, name=name, grid=(S // tm,),
                 in_specs=[_rowblk(tm, C), _res((1, C)), _res((1, C)), _rowblk(tm, C)],
                 out_specs=[_rowblk(tm, C), _res((1, C)), _res((1, C))],
                 out_shape=[_sds((S, C), F32), _sds((1, C), F32), _sds((1, C), F32)])(
                     v2, ln_g.reshape(1, C), ln_b.reshape(1, C), dcat)


def _ffn_act_fwd(ug, uv, w_dw, b_dw, *, name):
    S, Fw = ug.shape
    taps = w_dw.shape[0]
    tc, rows = _pick(Fw, (256, 128)), _pick(S, (256, 128))
    nb, nch = Fw // tc, S // rows

    def body(ug_ref, uv_ref, wg_ref, wv_ref, bg_ref, bv_ref, o_ref, pg_ref, pv_ref):
        pg_ref[0:F_HALO, :] = jnp.zeros((F_HALO, tc), F32)
        pv_ref[0:F_HALO, :] = jnp.zeros((F_HALO, tc), F32)
        pg_ref[F_HALO:F_HALO + S, :] = ug_ref[...]
        pv_ref[F_HALO:F_HALO + S, :] = uv_ref[...]

        def act(c, _):
            gc = _conv_chunk(pg_ref, c, rows, F_HALO, wg_ref, taps) + bg_ref[...]
            vc = _conv_chunk(pv_ref, c, rows, F_HALO, wv_ref, taps) + bv_ref[...]
            o_ref[pl.ds(pl.multiple_of(c * rows, rows), rows), :] = (gc * _sig(gc) * vc).astype(MXU_DT)
            return 0

        lax.fori_loop(0, nch, act, 0)

    b2 = b_dw.reshape(1, 2 * Fw)
    return _call(body, name=name, grid=(nb,),
                 in_specs=[_colblk(S, tc), _colblk(S, tc), _colblk(taps, tc), _colblk(taps, tc, nb),
                           _colblk(1, tc), _colblk(1, tc, nb)],
                 out_specs=_colblk(S, tc), out_shape=_sds((S, Fw), MXU_DT),
                 scratch_shapes=[pltpu.VMEM((S + F_HALO, tc), F32), pltpu.VMEM((S + F_HALO, tc), F32)])(
                     ug, uv, w_dw, w_dw, b2, b2)


def _ffn_act_bwd(ug, uv, dact, w_dw, b_dw, *, name):
    S, Fw = ug.shape
    taps = w_dw.shape[0]
    tc, rows = _pick(Fw, (256, 128)), _pick(S, (256, 128))
    nb, nch = Fw // tc, S // rows

    def body(ug_ref, uv_ref, da_ref, wg_ref, wv_ref, bg_ref, bv_ref, dug_ref, duv_ref, dwg_ref, dwv_ref,
             dbg_ref, dbv_ref, pg_ref, pv_ref, qg_ref, qv_ref):
        pg_ref[0:F_HALO, :] = jnp.zeros((F_HALO, tc), F32)
        pv_ref[0:F_HALO, :] = jnp.zeros((F_HALO, tc), F32)
        qg_ref[S:S + F_HALO, :] = jnp.zeros((F_HALO, tc), F32)
        qv_ref[S:S + F_HALO, :] = jnp.zeros((F_HALO, tc), F32)
        pg_ref[F_HALO:F_HALO + S, :] = ug_ref[...]
        pv_ref[F_HALO:F_HALO + S, :] = uv_ref[...]
        dwg_ref[...] = jnp.zeros((taps, tc), F32)
        dwv_ref[...] = jnp.zeros((taps, tc), F32)

        def grads(c, carry):
            sg, sv = carry
            r = pl.ds(pl.multiple_of(c * rows, rows), rows)
            gc = _conv_chunk(pg_ref, c, rows, F_HALO, wg_ref, taps) + bg_ref[...]
            vc = _conv_chunk(pv_ref, c, rows, F_HALO, wv_ref, taps) + bv_ref[...]
            s = _sig(gc)
            da = da_ref[r, :]
            dgc = da * vc * (s * (1.0 + gc * (1.0 - s)))
            dvc = da * (gc * s)
            qg_ref[r, :] = dgc
            qv_ref[r, :] = dvc
            _conv_wgrad_chunk(pg_ref, c, rows, F_HALO, dgc, dwg_ref, taps)
            _conv_wgrad_chunk(pv_ref, c, rows, F_HALO, dvc, dwv_ref, taps)
            return sg + jnp.sum(dgc, axis=0, keepdims=True), sv + jnp.sum(dvc, axis=0, keepdims=True)

        z = jnp.zeros((1, tc), F32)
        sg, sv = lax.fori_loop(0, nch, grads, (z, z))
        dbg_ref[...] = sg
        dbv_ref[...] = sv

        def back(c, _):
            r = pl.ds(pl.multiple_of(c * rows, rows), rows)
            dug_ref[r, :] = _conv_t_chunk(qg_ref, c, rows, F_HALO, wg_ref, taps).astype(MXU_DT)
            duv_ref[r, :] = _conv_t_chunk(qv_ref, c, rows, F_HALO, wv_ref, taps).astype(MXU_DT)
            return 0

        lax.fori_loop(0, nch, back, 0)

    b2 = b_dw.reshape(1, 2 * Fw)
    pad = pltpu.VMEM((S + F_HALO, tc), F32)
    dug, duv, dwg, dwv, dbg, dbv = _call(
        body, name=name, grid=(nb,),
        in_specs=[_colblk(S, tc), _colblk(S, tc), _colblk(S, tc), _colblk(taps, tc), _colblk(taps, tc, nb),
                  _colblk(1, tc), _colblk(1, tc, nb)],
        out_specs=[_colblk(S, tc), _colblk(S, tc), _colblk(taps, tc), _colblk(taps, tc), _colblk(1, tc),
                   _colblk(1, tc)],
        out_shape=[_sds((S, Fw), MXU_DT), _sds((S, Fw), MXU_DT), _sds((taps, Fw), F32), _sds((taps, Fw), F32),
                   _sds((1, Fw), F32), _sds((1, Fw), F32)],
        scratch_shapes=[pad, pad, pad, pad])(ug, uv, dact, w_dw, w_dw, b2, b2)
    return dug, duv, jnp.concatenate([dwg, dwv], axis=1), jnp.concatenate([dbg, dbv], axis=1)


def _head_mask(h, width):
    lane = lax.broadcasted_iota(jnp.int32, (1, width), 1)
    return (lane >= h * HEAD_DIM) & (lane < (h + 1) * HEAD_DIM)


def _mem_attn_fwd(p, qblk, mkv, l, *, Mw, name):
    S, ML = p.shape[0], mkv.shape[0]
    tm, nh, scale = _pick(S, (256, 128)), Mw // HEAD_DIM, HEAD_DIM ** -0.5

    def body(q_ref, k_ref, v_ref, o_ref):
        q, kv, vv = q_ref[...], k_ref[...], v_ref[...]
        out = jnp.zeros((tm, Mw), F32)
        for h in range(nh):
            mk = _head_mask(h, Mw)
            s = _dot(jnp.where(mk, q, 0.0).astype(MXU_DT), kv, NT) * scale
            e = jnp.exp(s - jnp.max(s, axis=-1, keepdims=True))
            pr = e / jnp.sum(e, axis=-1, keepdims=True)
            out = out + _dot(pr.astype(MXU_DT), jnp.where(mk, vv, jnp.zeros_like(vv)))
        o_ref[...] = out

    return _call(body, name=name, grid=(S // tm,),
                 in_specs=[_rowblk(tm, Mw, qblk), pl.BlockSpec((ML, Mw), lambda i: (0, 2 * l)),
                           pl.BlockSpec((ML, Mw), lambda i: (0, 2 * l + 1))],
                 out_specs=_rowblk(tm, Mw), out_shape=_sds((S, Mw), F32))(p, mkv, mkv)


def _mem_attn_bwd(p, qblk, mkv, l, dcat, doblk, *, Mw, name):
    S, ML = p.shape[0], mkv.shape[0]
    tm, nh, scale = _pick(S, (256, 128)), Mw // HEAD_DIM, HEAD_DIM ** -0.5

    def body(q_ref, k_ref, v_ref, do_ref, dq_ref, dk_ref, dv_ref):
        i = pl.program_id(0)

        @pl.when(i == 0)
        def _():
            dk_ref[...] = jnp.zeros((ML, Mw), F32)
            dv_ref[...] = jnp.zeros((ML, Mw), F32)

        q, kv, vv, do = q_ref[...], k_ref[...], v_ref[...], do_ref[...]
        dq = jnp.zeros((tm, Mw), F32)
        for h in range(nh):
            mk = _head_mask(h, Mw)
            qh = jnp.where(mk, q, 0.0).astype(MXU_DT)
            s = _dot(qh, kv, NT) * scale
            e = jnp.exp(s - jnp.max(s, axis=-1, keepdims=True))
            pr = e / jnp.sum(e, axis=-1, keepdims=True)
            doh = jnp.where(mk, do, 0.0).astype(MXU_DT)
            dv_ref[...] += _dot_tn(pr, doh)
            dp = _dot(doh, vv, NT)
            ds = pr * (dp - jnp.sum(dp * pr, axis=-1, keepdims=True))
            dq = dq + _dot(ds.astype(MXU_DT), jnp.where(mk, kv, jnp.zeros_like(kv))) * scale
            dk_ref[...] += _dot_tn(ds, qh) * scale
        dq_ref[...] = dq

    return _call(body, name=name, grid=(S // tm,),
                 in_specs=[_rowblk(tm, Mw, qblk), pl.BlockSpec((ML, Mw), lambda i: (0, 2 * l)),
                           pl.BlockSpec((ML, Mw), lambda i: (0, 2 * l + 1)), _rowblk(tm, Mw, doblk)],
                 out_specs=[_rowblk(tm, Mw), _res((ML, Mw)), _res((ML, Mw))],
                 out_shape=[_sds((S, Mw), F32), _sds((ML, Mw), F32), _sds((ML, Mw), F32)])(p, mkv, mkv, dcat)


def _fox_specs(S, dh, tq):
    nb = S // tq
    qs = pl.BlockSpec((1, tq, dh), lambda h, i: (h, i, 0))
    ks = pl.BlockSpec((1, S, dh), lambda h, i: (h, 0, 0))
    cqs = pl.BlockSpec((1, tq, 1), lambda h, i: (h, i, 0))
    cks = pl.BlockSpec((1, nb, 1, tq), lambda h, i: (h, 0, 0, 0))
    return qs, ks, cqs, cks


def _fox_logits(qv, kv, cqv, ckv, i, j, tq, scale):
    s = _dot(qv, kv, NT) * scale + cqv - ckv
    rows = i * tq + lax.broadcasted_iota(jnp.int32, (tq, tq), 0)
    cols = j * tq + lax.broadcasted_iota(jnp.int32, (tq, tq), 1)
    return jnp.where(cols <= rows, s, NEG)


def _fox_fwd(q, k, v, cq, ck, *, tq, name):
    H, S, dh = q.shape
    scale = dh ** -0.5
    qs, ks, cqs, cks = _fox_specs(S, dh, tq)

    def body(q_ref, k_ref, v_ref, cq_ref, ck_ref, o_ref, lse_ref):
        i = pl.program_id(1)
        qv, cqv = q_ref[0], cq_ref[0]

        def kblock(j, carry):
            m, l, acc = carry
            r = pl.ds(pl.multiple_of(j * tq, tq), tq)
            s = _fox_logits(qv, k_ref[0, r, :], cqv, ck_ref[0, j], i, j, tq, scale)
            m2 = jnp.maximum(m, jnp.max(s, axis=-1, keepdims=True))
            pr = jnp.exp(s - m2)
            al = jnp.exp(m - m2)
            return (m2, al * l + jnp.sum(pr, axis=-1, keepdims=True),
                    al * acc + _dot(pr.astype(MXU_DT), v_ref[0, r, :]))

        init = (jnp.full((tq, 1), NEG, F32), jnp.zeros((tq, 1), F32), jnp.zeros((tq, dh), F32))
        m, l, acc = lax.fori_loop(0, i + 1, kblock, init)
        o_ref[0] = acc / l
        lse_ref[0] = m + jnp.log(l)

    return _call(body, name=name, grid=(H, S // tq), in_specs=[qs, ks, ks, cqs, cks], out_specs=[qs, cqs],
                 out_shape=[_sds((H, S, dh), F32), _sds((H, S, 1), F32)])(q, k, v, cq, ck)


def _fox_bwd(q, k, v, cq, ck, o, lse, do, *, tq, name):
    H, S, dh = q.shape
    nb, scale = S // tq, dh ** -0.5
    qs, ks, cqs, cks = _fox_specs(S, dh, tq)

    def body(q_ref, k_ref, v_ref, cq_ref, ck_ref, o_ref, lse_ref, do_ref, dq_ref, dk_ref, dv_ref, dcq_ref,
             dck_ref):
        i = pl.program_id(1)

        @pl.when(i == 0)
        def _():
            dk_ref[...] = jnp.zeros((1, S, dh), F32)
            dv_ref[...] = jnp.zeros((1, S, dh), F32)
            dck_ref[...] = jnp.zeros((1, nb, 1, tq), F32)

        qv, cqv, lsev = q_ref[0], cq_ref[0], lse_ref[0]
        dob = do_ref[0].astype(MXU_DT)
        delta = jnp.sum(dob.astype(F32) * o_ref[0], axis=-1, keepdims=True)

        def kblock(j, carry):
            dq, rs = carry
            r = pl.ds(pl.multiple_of(j * tq, tq), tq)
            kv, vv = k_ref[0, r, :], v_ref[0, r, :]
            pr = jnp.exp(_fox_logits(qv, kv, cqv, ck_ref[0, j], i, j, tq, scale) - lsev)
            ds = pr * (_dot(dob, vv, NT) - delta)
            dk_ref[0, r, :] += _dot_tn(ds, qv) * scale
            dv_ref[0, r, :] += _dot_tn(pr, dob)
            dck_ref[0, j] += -jnp.sum(ds, axis=0, keepdims=True)
            return dq + _dot(ds.astype(MXU_DT), kv), rs + jnp.sum(ds, axis=-1, keepdims=True)

        dq, rs = lax.fori_loop(0, i + 1, kblock, (jnp.zeros((tq, dh), F32), jnp.zeros((tq, 1), F32)))
        dq_ref[0] = dq * scale
        dcq_ref[0] = rs

    return _call(body, name=name, grid=(H, nb), in_specs=[qs, ks, ks, cqs, cks, qs, cqs, qs],
                 out_specs=[qs, ks, ks, cqs, cks],
                 out_shape=[_sds((H, S, dh), F32), _sds((H, S, dh), F32), _sds((H, S, dh), F32),
                            _sds((H, S, 1), F32), _sds((H, nb, 1, tq), F32)])(q, k, v, cq, ck, o, lse, do)


def _tri(n, lower):
    r = lax.broadcasted_iota(jnp.int32, (n, n), 0)
    c = lax.broadcasted_iota(jnp.int32, (n, n), 1)
    return ((c <= r) if lower else (c >= r)).astype(F32)


def _fgate_fwd(fr, bf, *, name):
    S, W = fr.shape
    B = _pick(S, (256, 128))

    def body(f_ref, b_ref, cum_ref):
        L = _tri(B, True)
        carry = jnp.zeros((1, W), F32)
        for blk in range(S // B):
            z = f_ref[blk * B:(blk + 1) * B, :] + b_ref[...]
            ls = jnp.minimum(z, 0.0) - jnp.log(1.0 + jnp.exp(-jnp.abs(z)))
            cum_ref[blk * B:(blk + 1) * B, :] = jnp.dot(L, ls, precision=lax.Precision.HIGHEST,
                                                        preferred_element_type=F32) + carry
            carry = carry + jnp.sum(ls, axis=0, keepdims=True)

    return _call(body, name=name, out_shape=_sds((S, W), F32))(fr, bf)


def _fgate_bwd(fr, bf, dcum, *, name):
    S, W = fr.shape
    B = _pick(S, (256, 128))

    def body(f_ref, b_ref, dc_ref, df_ref, db_ref):
        U = _tri(B, False)
        carry = jnp.zeros((1, W), F32)
        dbs = jnp.zeros((1, W), F32)
        for blk in reversed(range(S // B)):
            dc = dc_ref[blk * B:(blk + 1) * B, :]
            dls = jnp.dot(U, dc, precision=lax.Precision.HIGHEST, preferred_element_type=F32) + carry
            carry = carry + jnp.sum(dc, axis=0, keepdims=True)
            z = f_ref[blk * B:(blk + 1) * B, :] + b_ref[...]
            df = dls * (1.0 / (1.0 + jnp.exp(z)))
            df_ref[blk * B:(blk + 1) * B, :] = df
            dbs = dbs + jnp.sum(df, axis=0, keepdims=True)
        db_ref[...] = dbs

    return _call(body, name=name, out_shape=[_sds((S, W), F32), _sds((1, W), F32)])(fr, bf, dcum)


def _flip(v, bit):
    return 1 - v if bit else v


HBM_SPEC = pl.BlockSpec(memory_space=pltpu.HBM)
SEM_SPEC = pl.BlockSpec(memory_space=pltpu.SEMAPHORE)


def _xchg_copies(src, land, sems, scatter):
    n = len(src)
    send, recv, loc = sems[:7 * n], sems[7 * n:14 * n], sems[14 * n:15 * n]
    x, y, c = lax.axis_index("x"), lax.axis_index("y"), lax.axis_index("c")
    me = 4 * x + 2 * y + c

    def peer(m):
        return _flip(x, m & 4), _flip(y, m & 2), _flip(c, m & 1)

    def copy(i, m):
        px, py, pc = peer(m)
        return pltpu.make_async_remote_copy(
            src_ref=src[i].at[4 * px + 2 * py + pc] if scatter[i] else src[i], dst_ref=land[i].at[me],
            send_sem=send[7 * i + m - 1], recv_sem=recv[7 * i + m - 1], device_id=(px, py, pc), device_id_type=MESH)

    def arrival(i, m):
        px, py, pc = peer(m)
        slot = land[i].at[4 * px + 2 * py + pc]
        return pltpu.make_async_remote_copy(src_ref=slot, dst_ref=slot, send_sem=send[7 * i + m - 1],
                                            recv_sem=recv[7 * i + m - 1], device_id=(px, py, pc), device_id_type=MESH)

    def own(i):
        return pltpu.make_async_copy(src[i].at[me] if scatter[i] else src[i], land[i].at[me], loc[i])

    return copy, arrival, own


def _xchg_start(srcs, scatter, *, name):
    n = len(srcs)
    lands = [_sds((N_DEV,) + s.shape[-2:], s.dtype) for s in srcs]

    ns = 15 * n

    def body(*refs):
        src, land, sems, token = refs[:n], refs[n:2 * n], refs[2 * n:2 * n + ns], refs[-1]
        copy, _, own = _xchg_copies(src, land, sems, scatter)
        for i in range(n):
            own(i).start()
            for m in range(1, N_DEV):
                copy(i, m).start()
        token[...] = jnp.zeros(token.shape, F32)

    thru = [pltpu.HBM(s.shape, s.dtype) for s in srcs] + [pltpu.HBM(s.shape, s.dtype) for s in lands]
    out = pl.pallas_call(
        body, name=name,
        out_shape=(*[pltpu.SemaphoreType.DMA(())] * ns, *thru, _sds((8, 128), F32)),
        in_specs=[HBM_SPEC] * (2 * n),
        out_specs=(*[SEM_SPEC] * ns, *[HBM_SPEC] * (2 * n), pl.BlockSpec(memory_space=pltpu.VMEM)),
        input_output_aliases={i: ns + i for i in range(2 * n)},
        compiler_params=pltpu.CompilerParams(has_side_effects=pltpu.SideEffectType.DATAFLOW_SIDE_EFFECTING),
    )(*[pltpu.with_memory_space_constraint(s, pltpu.HBM) for s in srcs],
      *[pltpu.with_memory_space_constraint(lax.empty(s.shape, s.dtype), pltpu.HBM) for s in lands])
    bufs = list(out[ns:ns + 2 * n])
    return (list(out[:ns]), bufs[:n], bufs[n:]), out[-1]


def _xchg_wait(handle, after, scatter, *, name):
    sems, srcs, lands = handle
    n = len(srcs)
    ns = 15 * n

    def body(*refs):
        src, land = refs[:n], refs[n:2 * n]
        copy, arrival, own = _xchg_copies(src, land, refs[2 * n:2 * n + ns], scatter)
        for i in range(n):
            own(i).wait()
            for m in range(1, N_DEV):
                copy(i, m).wait_send()
                arrival(i, m).wait_recv()

    out = pl.pallas_call(
        body, name=name,
        out_shape=tuple(pltpu.HBM(s.shape, s.dtype) for s in srcs + lands),
        in_specs=[HBM_SPEC] * (2 * n) + [SEM_SPEC] * ns + [ANY],
        out_specs=tuple([HBM_SPEC] * (2 * n)),
        input_output_aliases={i: i for i in range(2 * n)},
        compiler_params=pltpu.CompilerParams(has_side_effects=pltpu.SideEffectType.DATAFLOW_SIDE_EFFECTING),
    )(*srcs, *lands, *sems, after)
    return list(out[n:])


def _reduce_adam_body(r_ref, w_ref, m_ref, v_ref, g_ref, d_ref, m2_ref, v2_ref):
    g = r_ref[0].astype(F32)
    for s in range(1, N_DEV):
        g = g + r_ref[s].astype(F32)
    mm = ADAM_B1 * m_ref[...] + (1.0 - ADAM_B1) * g
    vv = ADAM_B2 * v_ref[...] + (1.0 - ADAM_B2) * (g * g)
    m_hat = mm / (1.0 - ADAM_B1 ** ADAM_STEP)
    v_hat = vv / (1.0 - ADAM_B2 ** ADAM_STEP)
    g_ref[...] = g
    d_ref[...] = -ADAM_LR * (m_hat / (jnp.sqrt(v_hat) + ADAM_EPS) + ADAM_WD * w_ref[...])
    m2_ref[...] = mm
    v2_ref[...] = vv


def _reduce_adam(recv, w, m, v, *, name):
    R, L = w.shape
    tr = _pick(R, (256, 128, 64, 32, 16, 8))

    def body(*refs):
        _reduce_adam_body(*refs)

    blk = _rowblk(tr, L)
    return _call(body, name=name, grid=(R // tr,),
                 in_specs=[pl.BlockSpec((N_DEV, tr, L), lambda i: (0, i, 0)), blk, blk, blk],
                 out_specs=[blk, blk, blk, blk], out_shape=[_sds((R, L), F32)] * 4)(recv, w, m, v)


def _reduce_adam_layer(recv, w, m, v, idx, prev, *, name):
    r, c = w.shape[-2:]
    tr = _pick(r, (256, 128)) if r % 128 == 0 else r
    if prev is None:
        prev = [lax.empty(w.shape, F32) for _ in range(4)]

    def body(r_ref, w_ref, m_ref, v_ref, *rest):
        _reduce_adam_body(r_ref, w_ref, m_ref, v_ref, *rest[-4:])

    blk = pl.BlockSpec((None, tr, c), lambda i: (idx, i, 0))
    return pl.pallas_call(
        body, name=name, grid=(r // tr,),
        in_specs=[pl.BlockSpec((N_DEV, tr, c), lambda i: (0, i, 0)), blk, blk, blk] + [ANY] * 4,
        out_specs=[blk] * 4, out_shape=[_sds(w.shape, F32)] * 4, input_output_aliases={4 + j: j for j in range(4)},
        compiler_params=pltpu.CompilerParams(vmem_limit_bytes=VMEM_LIMIT_V7X, dimension_semantics=("arbitrary",)),
    )(recv, w, m, v, *prev)


class _Pack:
    def __init__(self, shapes, row_mult):
        self.shapes, self.offs, rows = dict(shapes), {}, 0
        for name, shp in shapes:
            size = 1
            for d in shp:
                size *= d
            nr = -(-size // (16 * PACK_LANES)) * 16
            self.offs[name] = (rows, size, nr)
            rows += nr
        self.used = rows
        self.rows = -(-rows // row_mult) * row_mult

    def pack(self, arrays, dtype, lead=()):
        parts = []
        for name, (r0, size, nr) in self.offs.items():
            flat = arrays[name].astype(dtype).reshape(lead + (size,))
            flat = jnp.pad(flat, [(0, 0)] * len(lead) + [(0, nr * PACK_LANES - size)])
            parts.append(flat.reshape(lead + (nr, PACK_LANES)))
        if self.rows > self.used:
            parts.append(jnp.zeros(lead + (self.rows - self.used, PACK_LANES), dtype))
        return jnp.concatenate(parts, axis=len(lead))

    def unpack(self, buf, lead=()):
        out = {}
        for name, (r0, size, nr) in self.offs.items():
            flat = buf[..., r0:r0 + nr, :].reshape(lead + (nr * PACK_LANES,))
            out[name] = flat[..., :size].reshape(lead + tuple(self.shapes[name]))
        return out


def _to_full(g8, ax):
    t = jnp.moveaxis(g8, 0, ax)
    return t.reshape(t.shape[:ax] + (t.shape[ax] * t.shape[ax + 1],) + t.shape[ax + 2:])


def _to_shards(full, ax):
    shp = full.shape
    return jnp.moveaxis(full.reshape(shp[:ax] + (N_DEV, shp[ax] // N_DEV) + shp[ax + 1:]), ax, 0)


def _to_heads(a, H):
    S = a.shape[0]
    return a.reshape(S, H, HEAD_DIM).transpose(1, 0, 2)


def _from_heads(a):
    H, S, dh = a.shape
    return a.transpose(1, 0, 2).reshape(S, H * dh)


def kernel(x, mem, g_mix, w_in_a, b_glu, w_dw_a, b_dw_a, ln_g, ln_b, g_kv, w_kvf, b_f, w_in_b, g_mem, w_mem_kv, w_out, g_ffn, w_up, w_dw_f, b_dw_f, w_down, g_final, loss_target, m_g_mix, m_w_in_a, m_b_glu, m_w_dw_a, m_b_dw_a, m_ln_g, m_ln_b, m_g_kv, m_w_kvf, m_b_f, m_w_in_b, m_g_mem, m_w_mem_kv, m_w_out, m_g_ffn, m_w_up, m_w_dw_f, m_b_dw_f, m_w_down, m_g_final, v_g_mix, v_w_in_a, v_b_glu, v_w_dw_a, v_b_dw_a, v_ln_g, v_ln_b, v_g_kv, v_w_kvf, v_b_f, v_w_in_b, v_g_mem, v_w_mem_kv, v_w_out, v_g_ffn, v_w_up, v_w_dw_f, v_b_dw_f, v_w_down, v_g_final):
    given = dict(locals())
    W = {n: given[n] for n in WEIGHTS}
    x0, mem0, tgt = x[0], mem[0], loss_target[0]
    S, D = x0.shape
    depth, n_a = g_mix.shape[0], w_in_a.shape[0]
    C = w_dw_a.shape[2] * N_DEV
    Mw = D - C
    Fw = w_down.shape[1] * N_DEV
    H = b_f.shape[0]
    assert C == H * HEAD_DIM and (2 * C) % Mw == 0 and C % Mw == 0 and H <= GATE_LANES
    tq = _pick(S, (256, 128))
    nkv = 2 * C + GATE_LANES

    def mix_keys(l):
        keys = [("w_in_a", l) if l < n_a else ("w_in_b", l - n_a), ("w_mem_kv", l), ("w_out", l)]
        return keys + ([("w_kvf", 0)] if l == n_a else [])

    def ffn_keys(l):
        return [("w_up", l), ("w_down", l)]

    def key_ax(key):
        return big_ax[key[0]] - 1

    W3 = {n: (W[n][None] if n == "w_kvf" else W[n]) for n, _ in BIG}
    big_ax = {n: (ax + 1 if n == "w_kvf" else ax) for n, ax in BIG}
    pk_small = _Pack([(n, W[n].shape) for n, _ in SMALL], 8)
    pk_rep = _Pack([(n, W[n].shape) for n in REP], 8)

    ws32 = pk_small.pack(W, F32)
    gathers, tok = {}, 0.0
    for l in range(depth):
        for tag, keys in (("mix", mix_keys(l)), ("ffn", ffn_keys(l))):
            srcs = [W3[n][i].astype(COMM_DT) for n, i in keys] + ([ws32] if (l, tag) == (0, "mix") else [])
            gathers[l, tag], t = _xchg_start(srcs, [False] * len(srcs), name=f"w_gather_start_{tag}{l}")
            tok = tok + t[0, 0]
    g_mix, g_mem = g_mix + tok, g_mem + tok
    bf_pad = jnp.pad(b_f, (0, GATE_LANES - H)).reshape(1, GATE_LANES)

    def gathered(l, tag, keys, after):
        n = len(gathers[l, tag][1])
        lands = _xchg_wait(gathers[l, tag], after, [False] * n, name=f"w_gather_wait_{tag}{l}")
        return {k: _to_full(a, key_ax(k)).astype(MXU_DT) for k, a in zip(keys, lands)}, lands[len(keys):]

    mem_n = _rms_fwd(mem0, g_mem, name="mem_norm")
    sv = []
    xs = x0
    for l in range(depth):
        wl, extra = gathered(l, "mix", mix_keys(l), xs)
        if extra:
            gs = pk_small.unpack(extra[0], (N_DEV,))
            small = {n: _to_full(gs[n], ax) for n, ax in SMALL}
        t = dict(x_in=xs, w=wl)
        t["mkv"] = _mm(mem_n, wl["w_mem_kv", l], name=f"mem_kv{l}", out_dtype=MXU_DT)
        t["h"] = _rms_fwd(xs, g_mix[l], name=f"mix_norm{l}")
        if l < n_a:
            t["p"] = _mm(t["h"], wl["w_in_a", l], name=f"in_proj{l}", out_dtype=F32)
            t["v2"] = _glu_conv_fwd(t["p"], small["b_glu"][l], small["w_dw_a"][l], small["b_dw_a"][l],
                                    name=f"glu_conv{l}")
            memo = _mem_attn_fwd(t["p"], 2 * C // Mw, t["mkv"], 0, Mw=Mw, name=f"mem_attn{l}")
            t["cat"] = _ln_silu_cat_fwd(t["v2"], small["ln_g"][l], small["ln_b"][l], memo, name=f"ln_silu{l}")
        else:
            if l == n_a:
                wkvf = jnp.pad(wl["w_kvf", 0], ((0, 0), (0, nkv - w_kvf.shape[1])))
                hk = _rms_fwd(xs, g_kv, name="kv_norm")
                kvf = _mm(hk, wkvf, name="kv_proj", out_dtype=F32)
                k_h = _to_heads(kvf[:, :C], H).astype(MXU_DT)
                v_h = _to_heads(kvf[:, C:2 * C], H).astype(MXU_DT)
                fr = kvf[:, 2 * C:]
                cum = _fgate_fwd(fr, bf_pad, name="fgate")
                cum_t = cum[:, :H].T
                cq, ck = cum_t.reshape(H, S, 1), cum_t.reshape(H, S // tq, 1, tq)
            t["p"] = _mm(t["h"], wl["w_in_b", l - n_a], name=f"in_proj{l}", out_dtype=F32)
            t["q_h"] = _to_heads(t["p"][:, :C], H).astype(MXU_DT)
            t["o_h"], t["lse"] = _fox_fwd(t["q_h"], k_h, v_h, cq, ck, tq=tq, name=f"fox{l}")
            memo = _mem_attn_fwd(t["p"], C // Mw, t["mkv"], 0, Mw=Mw, name=f"mem_attn{l}")
            t["cat"] = jnp.concatenate([_from_heads(t["o_h"]), memo], axis=1).astype(MXU_DT)
        t["x_mid"] = _mm(t["cat"], wl["w_out", l], name=f"out_proj{l}", out_dtype=F32, add=xs)
        wl.update(gathered(l, "ffn", ffn_keys(l), t["x_mid"])[0])
        t["h2"] = _rms_fwd(t["x_mid"], g_ffn[l], name=f"ffn_norm{l}")
        t["ug"] = _mm(t["h2"], wl["w_up", l], name=f"up_gate{l}", out_dtype=F32, cols=(0, Fw))
        t["uv"] = _mm(t["h2"], wl["w_up", l], name=f"up_val{l}", out_dtype=F32, cols=(Fw, Fw))
        t["act"] = _ffn_act_fwd(t["ug"], t["uv"], small["w_dw_f"][l], b_dw_f[l], name=f"ffn_act{l}")
        xs = _mm(t["act"], wl["w_down", l], name=f"down_proj{l}", out_dtype=F32, add=t["x_mid"])
        sv.append(t)
    loss_dev, dx, dg_final = _loss_bwd(xs, g_final, tgt, name="loss_head")

    M1 = {n: given["m_" + n] for n in WEIGHTS}
    V1 = {n: given["v_" + n] for n in WEIGHTS}
    M3 = {n: (M1[n][None] if n == "w_kvf" else M1[n]) for n, _ in BIG}
    V3 = {n: (V1[n][None] if n == "w_kvf" else V1[n]) for n, _ in BIG}
    res, chain, pending = {}, {}, []

    def start_grads(tag, l, keys, gl, extra=(), extra_scatter=()):
        srcs = [_to_shards(gl[k], key_ax(k)) for k in keys] + list(extra)
        scatter = [True] * len(keys) + list(extra_scatter)
        handle, tk = _xchg_start(srcs, scatter, name=f"g_xchg_start_{tag}{l}")
        pending.append((f"{tag}{l}", keys, handle, scatter))
        return tk[0, 0]

    def finish_grads(after):
        tag, keys, handle, scatter = pending.pop(0)
        lands = _xchg_wait(handle, after, scatter, name=f"g_xchg_wait_{tag}")
        for (n, i), recv in zip(keys, lands):
            chain[n] = _reduce_adam_layer(recv, W3[n], M3[n], V3[n], i, chain.get(n), name=f"adam_{n}{i}")
        return lands[len(keys):]

    G = {n: [None] * W[n].shape[0] for n in ("g_mix", "b_glu", "w_dw_a", "b_dw_a", "ln_g", "ln_b", "g_ffn",
                                              "w_dw_f", "b_dw_f")}
    dk_sum = dv_sum = dck_sum = dmem_n = None
    btok = 0.0
    for l in reversed(range(depth)):
        t = sv[l]
        wl, gl = t["w"], {}
        dxb = (dx + btok).astype(MXU_DT)
        dact = _mm(dxb, wl["w_down", l], name=f"d_act{l}", out_dtype=F32, nt=True)
        gl["w_down", l] = _mm_tn(t["act"], dxb, name=f"dw_down{l}", out_dtype=COMM_DT)
        dug, duv, G["w_dw_f"][l], db = _ffn_act_bwd(t["ug"], t["uv"], dact, small["w_dw_f"][l], b_dw_f[l],
                                                    name=f"d_ffn_act{l}")
        G["b_dw_f"][l] = db[0]
        dh2 = _mm_nt2(dug, duv, wl["w_up", l], name=f"d_up{l}")
        gl["w_up", l] = jnp.concatenate([_mm_tn(t["h2"], dug, name=f"dw_up_gate{l}", out_dtype=COMM_DT),
                                         _mm_tn(t["h2"], duv, name=f"dw_up_val{l}", out_dtype=COMM_DT)], axis=1)
        dx, dg = _rms_bwd(t["x_mid"], g_ffn[l], dh2, dx, name=f"d_ffn_norm{l}")
        G["g_ffn"][l] = dg[0]
        btok = start_grads("ffn", l, ffn_keys(l), gl)
        if len(pending) > 2:
            finish_grads(dx)

        dxb = (dx + btok).astype(MXU_DT)
        dcat = _mm(dxb, wl["w_out", l], name=f"d_cat{l}", out_dtype=F32, nt=True)
        gl["w_out", l] = _mm_tn(t["cat"], dxb, name=f"dw_out{l}", out_dtype=COMM_DT)
        if l >= n_a:
            do_h = _to_heads(dcat[:, :C], H)
            dq_h, dk_h, dv_h, dcq, dck = _fox_bwd(t["q_h"], k_h, v_h, cq, ck, t["o_h"], t["lse"], do_h, tq=tq,
                                                  name=f"d_fox{l}")
            dck = dck.reshape(H, S) + dcq.reshape(H, S)
            dk_sum = dk_h if dk_sum is None else dk_sum + dk_h
            dv_sum = dv_h if dv_sum is None else dv_sum + dv_h
            dck_sum = dck if dck_sum is None else dck_sum + dck
            dqm, dmk, dmv = _mem_attn_bwd(t["p"], C // Mw, t["mkv"], 0, dcat, C // Mw, Mw=Mw,
                                          name=f"d_mem_attn{l}")
            dp = jnp.concatenate([_from_heads(dq_h), dqm], axis=1).astype(MXU_DT)
            key = ("w_in_b", l - n_a)
        else:
            dv2, dlg, dlb = _ln_silu_bwd(t["v2"], small["ln_g"][l], small["ln_b"][l], dcat, name=f"d_ln_silu{l}")
            G["ln_g"][l], G["ln_b"][l] = dlg[0], dlb[0]
            da, dgt, G["w_dw_a"][l], dbd, dba, dbg = _glu_conv_bwd(t["p"], small["b_glu"][l], small["w_dw_a"][l],
                                                                   dv2, name=f"d_glu_conv{l}")
            G["b_dw_a"][l] = dbd[0]
            G["b_glu"][l] = jnp.concatenate([dba[0], dbg[0]])
            dqm, dmk, dmv = _mem_attn_bwd(t["p"], 2 * C // Mw, t["mkv"], 0, dcat, C // Mw, Mw=Mw,
                                          name=f"d_mem_attn{l}")
            dp = jnp.concatenate([da, dgt, dqm], axis=1).astype(MXU_DT)
            key = ("w_in_a", l)
        dmkv = jnp.concatenate([dmk, dmv], axis=1).astype(MXU_DT)
        gl["w_mem_kv", l] = _mm_tn(mem_n, dmkv, name=f"dw_mem_kv{l}", out_dtype=COMM_DT)
        dmem_n = _mm(dmkv, wl["w_mem_kv", l], name=f"d_mem_kv{l}", out_dtype=F32, nt=True, add=dmem_n)
        dh = _mm(dp, wl[key], name=f"d_in_proj{l}", out_dtype=F32, nt=True)
        gl[key] = _mm_tn(t["h"], dp, name=f"dw_in_proj{l}", out_dtype=COMM_DT)
        dx, dg = _rms_bwd(t["x_in"], g_mix[l], dh, dx, name=f"d_mix_norm{l}")
        G["g_mix"][l] = dg[0]
        if l == n_a:
            dcum = jnp.pad(dck_sum.T, ((0, 0), (0, GATE_LANES - H)))
            df, dbf = _fgate_bwd(fr, bf_pad, dcum, name="d_fgate")
            dkvf = jnp.concatenate([_from_heads(dk_sum), _from_heads(dv_sum), df], axis=1).astype(MXU_DT)
            dhk = _mm(dkvf, wkvf, name="d_kv_proj", out_dtype=F32, nt=True)
            gl["w_kvf", 0] = _mm_tn(hk, dkvf, name="dw_kv_proj", out_dtype=COMM_DT)[:, :w_kvf.shape[1]]
            dx, dg_kv = _rms_bwd(t["x_in"], g_kv, dhk, dx, name="d_kv_norm")
        if l > 0:
            btok = start_grads("mix", l, mix_keys(l), gl)
            if len(pending) > 2:
                finish_grads(dx)

    _, dg_mem = _rms_bwd(mem0, g_mem, dmem_n, None, name="d_mem_norm")
    grads = {n: jnp.stack(v) for n, v in G.items()}
    grads.update(g_kv=dg_kv[0], b_f=dbf[0, :H], g_mem=dg_mem[0], g_final=dg_final[0])
    gs8 = pk_small.pack({n: _to_shards(grads[n], ax) for n, ax in SMALL}, F32, (N_DEV,))
    start_grads("mix", 0, mix_keys(0), gl, [gs8, pk_rep.pack(grads, F32)], [True, False])
    while pending:
        extra = finish_grads(dx)
    for pk, recv, tag in ((pk_small, extra[0], "small"), (pk_rep, extra[1], "rep")):
        w32 = ws32 if tag == "small" else pk.pack(W, F32)
        outs = _reduce_adam(recv, w32, pk.pack(M1, F32), pk.pack(V1, F32), name=f"adam_{tag}")
        for kind, buf in zip(("grad", "delta", "new_m", "new_v"), outs):
            for n, a in pk.unpack(buf).items():
                res[kind, n] = a
    for n, outs in chain.items():
        for kind, a in zip(("grad", "delta", "new_m", "new_v"), outs):
            res[kind, n] = a.reshape(W[n].shape)

    loss = lax.psum(loss_dev, ("x", "y", "c"))
    return (loss, dx[None], *[res[kind, n] for kind in ("grad", "delta", "new_m", "new_v") for n in WEIGHTS])
```

```python
import jax
import jax.numpy as jnp
from jax import lax
from jax.experimental import pallas as pl
from jax.experimental.pallas import tpu as pltpu

F32 = jnp.float32
MXU_DT = jnp.bfloat16
COMM_DT = jnp.bfloat16

N_DEV = 8
HEAD_DIM = 64
RMS_EPS = 1e-6
LN_EPS = 1e-5
ADAM_LR = 0.001
ADAM_B1 = 0.9
ADAM_B2 = 0.999
ADAM_EPS = 1e-08
ADAM_WD = 0.01
ADAM_STEP = 10

PACK_LANES = 1024
GATE_LANES = 128
VMEM_LIMIT_V7X = 56 << 20
NEG = -1e30
MESH = pl.DeviceIdType.MESH
ANY = pl.BlockSpec(memory_space=pl.ANY)
NT = (((1,), (1,)), ((), ()))
NN = (((1,), (0,)), ((), ()))

BIG = (("w_in_a", 2), ("w_kvf", 0), ("w_in_b", 1), ("w_mem_kv", 1), ("w_out", 1), ("w_up", 2), ("w_down", 1))
SMALL = (("b_glu", 1), ("w_dw_a", 2), ("b_dw_a", 1), ("ln_g", 1), ("ln_b", 1), ("w_dw_f", 2))
REP = ("g_mix", "g_kv", "b_f", "g_mem", "g_ffn", "b_dw_f", "g_final")
WEIGHTS = ("g_mix", "w_in_a", "b_glu", "w_dw_a", "b_dw_a", "ln_g", "ln_b", "g_kv", "w_kvf", "b_f", "w_in_b",
           "g_mem", "w_mem_kv", "w_out", "g_ffn", "w_up", "w_dw_f", "b_dw_f", "w_down", "g_final")


def _sds(shape, dtype):
    return jax.ShapeDtypeStruct(tuple(shape), dtype)


def _call(body, *, name, out_shape, grid=(), in_specs=None, out_specs=None, scratch_shapes=()):
    params = dict(vmem_limit_bytes=VMEM_LIMIT_V7X)
    if grid:
        params["dimension_semantics"] = ("arbitrary",) * len(grid)
    kw = {}
    if in_specs is not None:
        kw["in_specs"] = in_specs
    if out_specs is not None:
        kw["out_specs"] = out_specs
    return pl.pallas_call(body, name=name, grid=grid, out_shape=out_shape, scratch_shapes=list(scratch_shapes),
                          compiler_params=pltpu.CompilerParams(**params), **kw)


def _res(shape):
    nd = len(shape)
    return pl.BlockSpec(tuple(shape), lambda *_: (0,) * nd)


def _colblk(rows, tc, off=0):
    return pl.BlockSpec((rows, tc), lambda j: (0, j + off))


def _rowblk(tm, cols, off=0):
    return pl.BlockSpec((tm, cols), lambda i: (i, off))


def _pick(n, opts=(512, 256, 128)):
    for t in opts:
        if n % t == 0:
            return t
    return n


def _sig(z):
    return 1.0 / (1.0 + jnp.exp(-z))


def _dot(a, b, dims=NN):
    return lax.dot_general(a, b, dims, preferred_element_type=F32)


def _dot_tn(a, b):
    return _dot(a.T.astype(b.dtype), b)


def _mm_tn(a, b, *, name, out_dtype):
    S, K = a.shape
    N = b.shape[1]
    tk, rc = _pick(K, (256, 128)), _pick(S)
    cast = b.dtype != MXU_DT

    def body(a_ref, b_ref, o_ref, acc_ref, *bb):
        if cast:
            @pl.when(pl.program_id(0) == 0)
            def _():
                for r0 in range(0, S, rc):
                    bb[0][r0:r0 + rc, :] = b_ref[r0:r0 + rc, :].astype(MXU_DT)
            b_ref = bb[0]
        for n, r0 in enumerate(range(0, S, rc)):
            part = _dot(a_ref[r0:r0 + rc, :].astype(F32).T.astype(MXU_DT), b_ref[r0:r0 + rc, :])
            if n == 0:
                acc_ref[...] = part
            else:
                acc_ref[...] += part
        o_ref[...] = acc_ref[...].astype(out_dtype)

    return _call(body, name=name, grid=(K // tk,), in_specs=[_colblk(S, tk), _res((S, N))],
                 out_specs=pl.BlockSpec((tk, N), lambda j: (j, 0)), out_shape=_sds((K, N), out_dtype),
                 scratch_shapes=[pltpu.VMEM((tk, N), F32)] + ([pltpu.VMEM((S, N), MXU_DT)] if cast else []))(a, b)


def _mm_nt2(a1, a2, w, *, name):
    M, Fw = a1.shape
    N = w.shape[0]
    tm, nc = _pick(M), _pick(N)

    def body(a1_ref, a2_ref, w_ref, o_ref):
        v1, v2 = a1_ref[...], a2_ref[...]
        for n0 in range(0, N, nc):
            o_ref[:, n0:n0 + nc] = (_dot(v1, w_ref[n0:n0 + nc, 0:Fw], NT) + _dot(v2, w_ref[n0:n0 + nc, Fw:2 * Fw], NT))

    return _call(body, name=name, grid=(M // tm,), in_specs=[_rowblk(tm, Fw), _rowblk(tm, Fw), _res(w.shape)],
                 out_specs=_rowblk(tm, N), out_shape=_sds((M, N), F32))(a1, a2, w)


def _mm(a, w, *, name, out_dtype, nt=False, add=None, mode="n", cols=None):
    M, K = a.shape
    N = w.shape[0] if nt else w.shape[1]
    assert (w.shape[1] if nt else w.shape[0]) == K
    dims = NT if nt else NN
    has_add = add is not None
    if cols is not None:
        assert mode == "n" and not nt and not has_add
        c0, N = cols
        tn, rc = _pick(N), _pick(M)
        assert c0 % tn == 0

        def body(a_ref, w_ref, o_ref):
            wv = w_ref[...]
            for r0 in range(0, M, rc):
                o_ref[r0:r0 + rc, :] = _dot(a_ref[r0:r0 + rc, :], wv).astype(out_dtype)

        return _call(body, name=name, grid=(N // tn,), in_specs=[_res((M, K)), _colblk(K, tn, c0 // tn)],
                     out_specs=_colblk(M, tn), out_shape=_sds((M, N), out_dtype))(a, w)
    scratch = []
    if mode == "n":
        tn, rc = _pick(N), _pick(M)
        cast = a.dtype != MXU_DT
        scratch = [pltpu.VMEM((M, K), MXU_DT)] if cast else []

        def body(*refs):
            a_ref, w_ref, o_ref = refs[0], refs[1], refs[2 + has_add]
            if cast:
                @pl.when(pl.program_id(0) == 0)
                def _():
                    for r0 in range(0, M, rc):
                        refs[-1][r0:r0 + rc, :] = a_ref[r0:r0 + rc, :].astype(MXU_DT)
                a_ref = refs[-1]
            wv = w_ref[...]
            for r0 in range(0, M, rc):
                acc = _dot(a_ref[r0:r0 + rc, :], wv, dims)
                if has_add:
                    acc = acc + refs[2][r0:r0 + rc, :]
                o_ref[r0:r0 + rc, :] = acc.astype(out_dtype)

        w_spec = pl.BlockSpec((tn, K), lambda j: (j, 0)) if nt else pl.BlockSpec((K, tn), lambda j: (0, j))
        in_specs = [_res((M, K)), w_spec] + ([_colblk(M, tn)] if has_add else [])
        out_specs, grid = _colblk(M, tn), (N // tn,)
    else:
        tm, nc = _pick(M), _pick(N)

        def body(*refs):
            a_ref, w_ref, o_ref = refs[0], refs[1], refs[-1]
            av = a_ref[...]
            for n0 in range(0, N, nc):
                wv = w_ref[n0:n0 + nc, :] if nt else w_ref[:, n0:n0 + nc]
                acc = _dot(av, wv, dims)
                if has_add:
                    acc = acc + refs[2][:, n0:n0 + nc]
                o_ref[:, n0:n0 + nc] = acc.astype(out_dtype)

        in_specs = [_rowblk(tm, K), _res(w.shape)] + ([_rowblk(tm, N)] if has_add else [])
        out_specs, grid = _rowblk(tm, N), (M // tm,)
    args = (a, w) + ((add,) if has_add else ())
    return _call(body, name=name, grid=grid, in_specs=in_specs, out_specs=out_specs,
                 out_shape=_sds((M, N), out_dtype), scratch_shapes=scratch)(*args)


def _rms_fwd(x, g, *, name):
    M, D = x.shape
    tm = _pick(M, (256, 128))

    def body(x_ref, g_ref, h_ref):
        xf = x_ref[...]
        r = lax.rsqrt(jnp.mean(xf * xf, axis=-1, keepdims=True) + RMS_EPS)
        h_ref[...] = ((xf * r) * g_ref[...]).astype(MXU_DT)

    return _call(body, name=name, grid=(M // tm,), in_specs=[_rowblk(tm, D), _res((1, D))],
                 out_specs=_rowblk(tm, D), out_shape=_sds((M, D), MXU_DT))(x, g.reshape(1, D))


def _rms_bwd(x, g, dh, dx_in, *, name):
    M, D = x.shape
    tm = _pick(M, (256, 128))
    with_dx = dx_in is not None

    def body(*refs):
        if with_dx:
            x_ref, g_ref, dh_ref, dxin_ref, dx_ref, dg_ref = refs
        else:
            x_ref, g_ref, dh_ref, dg_ref = refs
        i = pl.program_id(0)
        xf = x_ref[...]
        r = lax.rsqrt(jnp.mean(xf * xf, axis=-1, keepdims=True) + RMS_EPS)
        y = xf * r
        dh_v = dh_ref[...]
        if with_dx:
            dy = dh_v * g_ref[...]
            dx_ref[...] = dxin_ref[...] + r * (dy - y * jnp.mean(dy * y, axis=-1, keepdims=True))
        part = jnp.sum(dh_v * y, axis=0, keepdims=True)

        @pl.when(i == 0)
        def _():
            dg_ref[...] = part

        @pl.when(i > 0)
        def _():
            dg_ref[...] += part

    ins = [x, g.reshape(1, D), dh] + ([dx_in] if with_dx else [])
    in_specs = [_rowblk(tm, D), _res((1, D)), _rowblk(tm, D)] + ([_rowblk(tm, D)] if with_dx else [])
    if with_dx:
        out_specs, out_shape = [_rowblk(tm, D), _res((1, D))], [_sds((M, D), F32), _sds((1, D), F32)]
    else:
        out_specs, out_shape = [_res((1, D))], [_sds((1, D), F32)]
    out = _call(body, name=name, grid=(M // tm,), in_specs=in_specs, out_specs=out_specs, out_shape=out_shape)(*ins)
    return out if with_dx else (None, out[0])


def _loss_bwd(x, g, t, *, name):
    M, D = x.shape
    tm = _pick(M, (256, 128))

    def body(x_ref, g_ref, t_ref, dx_ref, dg_ref, ls_ref):
        i = pl.program_id(0)
        xf = x_ref[...]
        r = lax.rsqrt(jnp.mean(xf * xf, axis=-1, keepdims=True) + RMS_EPS)
        xr = xf * r
        e = xr * g_ref[...] - t_ref[...]
        dout = e * (1.0 / D)
        dy = dout * g_ref[...]
        dx_ref[...] = r * (dy - xr * jnp.mean(dy * xr, axis=-1, keepdims=True))
        part = jnp.sum(dout * xr, axis=0, keepdims=True)
        lpart = jnp.zeros(ls_ref.shape, F32) + (0.5 / D) * jnp.sum(e * e, keepdims=True)

        @pl.when(i == 0)
        def _():
            dg_ref[...] = part
            ls_ref[...] = lpart

        @pl.when(i > 0)
        def _():
            dg_ref[...] += part
            ls_ref[...] += lpart

    dx, dg, ls = _call(body, name=name, grid=(M // tm,),
                       in_specs=[_rowblk(tm, D), _res((1, D)), _rowblk(tm, D)],
                       out_specs=[_rowblk(tm, D), _res((1, D)), _res((8, 128))],
                       out_shape=[_sds((M, D), F32), _sds((1, D), F32), _sds((8, 128), F32)])(x, g.reshape(1, D), t)
    return ls[0, 0], dx, dg


def _shift_rows(ext, off, rows):
    if off % 8 == 0:
        return ext[off:off + rows, :]
    return pltpu.roll(ext, ext.shape[0] - off, 0)[0:rows, :]


def _ext(pad_ref, c, rows, halo):
    return pad_ref[pl.ds(pl.multiple_of(c * rows, rows), rows + halo), :]


def _conv_chunk(pad_ref, c, rows, halo, w_ref, taps):
    ext = _ext(pad_ref, c, rows, halo)
    acc = None
    for k in range(taps):
        term = w_ref[k:k + 1, :] * _shift_rows(ext, halo - (taps - 1) + k, rows)
        acc = term if acc is None else acc + term
    return acc


def _conv_t_chunk(pad_ref, c, rows, halo, w_ref, taps):
    ext = _ext(pad_ref, c, rows, halo)
    acc = None
    for k in range(taps):
        term = w_ref[k:k + 1, :] * _shift_rows(ext, taps - 1 - k, rows)
        acc = term if acc is None else acc + term
    return acc


def _conv_wgrad_chunk(pad_ref, c, rows, halo, dy, dw_ref, taps):
    ext = _ext(pad_ref, c, rows, halo)
    for k in range(taps):
        dw_ref[k:k + 1, :] += jnp.sum(dy * _shift_rows(ext, halo - (taps - 1) + k, rows), axis=0, keepdims=True)


A_HALO, F_HALO = 32, 8


def _glu_conv_fwd(p, b_glu, w_dw, b_dw, *, name):
    S = p.shape[0]
    taps, C = w_dw.shape
    tc, rows = 128, _pick(S, (256, 128))
    nb, nch = C // tc, S // rows

    def body(a_ref, g_ref, ba_ref, bg_ref, w_ref, bd_ref, o_ref, pad_ref):
        pad_ref[0:A_HALO, :] = jnp.zeros((A_HALO, tc), F32)

        def fill(c, _):
            r = pl.ds(pl.multiple_of(c * rows, rows), rows)
            v1 = (a_ref[r, :] + ba_ref[...]) * _sig(g_ref[r, :] + bg_ref[...])
            pad_ref[pl.ds(pl.multiple_of(A_HALO + c * rows, 8), rows), :] = v1
            return 0

        lax.fori_loop(0, nch, fill, 0)

        def conv(c, _):
            o_ref[pl.ds(pl.multiple_of(c * rows, rows), rows), :] = (
                _conv_chunk(pad_ref, c, rows, A_HALO, w_ref, taps) + bd_ref[...])
            return 0

        lax.fori_loop(0, nch, conv, 0)

    b2 = b_glu.reshape(1, 2 * C)
    return _call(body, name=name, grid=(nb,),
                 in_specs=[_colblk(S, tc), _colblk(S, tc, nb), _colblk(1, tc), _colblk(1, tc, nb),
                           _colblk(taps, tc), _colblk(1, tc)],
                 out_specs=_colblk(S, tc), out_shape=_sds((S, C), F32),
                 scratch_shapes=[pltpu.VMEM((S + A_HALO, tc), F32)])(p, p, b2, b2, w_dw, b_dw.reshape(1, C))


def _glu_conv_bwd(p, b_glu, w_dw, dv2, *, name):
    S = p.shape[0]
    taps, C = w_dw.shape
    tc, rows = 128, _pick(S, (256, 128))
    nb, nch = C // tc, S // rows

    def body(a_ref, g_ref, ba_ref, bg_ref, w_ref, dy_ref, da_ref, dgt_ref, dw_ref, dbd_ref, dba_ref, dbg_ref,
             padx_ref, pady_ref):
        padx_ref[0:A_HALO, :] = jnp.zeros((A_HALO, tc), F32)
        pady_ref[S:S + A_HALO, :] = jnp.zeros((A_HALO, tc), F32)
        dw_ref[...] = jnp.zeros((taps, tc), F32)

        def fill(c, _):
            r = pl.ds(pl.multiple_of(c * rows, rows), rows)
            v1 = (a_ref[r, :] + ba_ref[...]) * _sig(g_ref[r, :] + bg_ref[...])
            padx_ref[pl.ds(pl.multiple_of(A_HALO + c * rows, 8), rows), :] = v1
            pady_ref[r, :] = dy_ref[r, :]
            return 0

        lax.fori_loop(0, nch, fill, 0)

        def back(c, carry):
            sd, sa, sg = carry
            r = pl.ds(pl.multiple_of(c * rows, rows), rows)
            dy = dy_ref[r, :]
            _conv_wgrad_chunk(padx_ref, c, rows, A_HALO, dy, dw_ref, taps)
            dv1 = _conv_t_chunk(pady_ref, c, rows, A_HALO, w_ref, taps)
            a = a_ref[r, :] + ba_ref[...]
            s = _sig(g_ref[r, :] + bg_ref[...])
            da = dv1 * s
            dgt = dv1 * a * s * (1.0 - s)
            da_ref[r, :] = da
            dgt_ref[r, :] = dgt
            return (sd + jnp.sum(dy, axis=0, keepdims=True), sa + jnp.sum(da, axis=0, keepdims=True),
                    sg + jnp.sum(dgt, axis=0, keepdims=True))

        z = jnp.zeros((1, tc), F32)
        sd, sa, sg = lax.fori_loop(0, nch, back, (z, z, z))
        dbd_ref[...] = sd
        dba_ref[...] = sa
        dbg_ref[...] = sg

    b2 = b_glu.reshape(1, 2 * C)
    return _call(body, name=name, grid=(nb,),
                 in_specs=[_colblk(S, tc), _colblk(S, tc, nb), _colblk(1, tc), _colblk(1, tc, nb),
                           _colblk(taps, tc), _colblk(S, tc)],
                 out_specs=[_colblk(S, tc), _colblk(S, tc), _colblk(taps, tc), _colblk(1, tc), _colblk(1, tc),
                            _colblk(1, tc)],
                 out_shape=[_sds((S, C), F32), _sds((S, C), F32), _sds((taps, C), F32), _sds((1, C), F32),
                            _sds((1, C), F32), _sds((1, C), F32)],
                 scratch_shapes=[pltpu.VMEM((S + A_HALO, tc), F32), pltpu.VMEM((S + A_HALO, tc), F32)])(
                     p, p, b2, b2, w_dw, dv2)


def _ln_silu_cat_fwd(v2, ln_g, ln_b, memo, *, name):
    S, C = v2.shape
    Mw = memo.shape[1]
    tm = _pick(S, (256, 128))

    def body(v_ref, g_ref, b_ref, m_ref, o_ref):
        v = v_ref[...]
        mu = jnp.mean(v, axis=-1, keepdims=True)
        d = v - mu
        y = d * lax.rsqrt(jnp.mean(d * d, axis=-1, keepdims=True) + LN_EPS) * g_ref[...] + b_ref[...]
        o_ref[:, 0:C] = (y * _sig(y)).astype(MXU_DT)
        o_ref[:, C:C + Mw] = m_ref[...].astype(MXU_DT)

    return _call(body, name=name, grid=(S // tm,),
                 in_specs=[_rowblk(tm, C), _res((1, C)), _res((1, C)), _rowblk(tm, Mw)],
                 out_specs=_rowblk(tm, C + Mw), out_shape=_sds((S, C + Mw), MXU_DT))(
                     v2, ln_g.reshape(1, C), ln_b.reshape(1, C), memo)


def _ln_silu_bwd(v2, ln_g, ln_b, dcat, *, name):
    S, C = v2.shape
    tm = _pick(S, (256, 128))

    def body(v_ref, g_ref, b_ref, dm_ref, dv_ref, dg_ref, db_ref):
        i = pl.program_id(0)
        v = v_ref[...]
        mu = jnp.mean(v, axis=-1, keepdims=True)
        d = v - mu
        rstd = lax.rsqrt(jnp.mean(d * d, axis=-1, keepdims=True) + LN_EPS)
        xh = d * rstd
        y = xh * g_ref[...] + b_ref[...]
        s = _sig(y)
        dyv = dm_ref[...] * (s * (1.0 + y * (1.0 - s)))
        dxh = dyv * g_ref[...]
        dv_ref[...] = rstd * (dxh - jnp.mean(dxh, axis=-1, keepdims=True)
                              - xh * jnp.mean(dxh * xh, axis=-1, keepdims=True))
        pg = jnp.sum(dyv * xh, axis=0, keepdims=True)
        pb = jnp.sum(dyv, axis=0, keepdims=True)

        @pl.when(i == 0)
        def _():
            dg_ref[...] = pg
            db_ref[...] = pb

        @pl.when(i > 0)
        def _():
            dg_ref[...] += pg
            db_ref[...] += pb

    return _call(body, name=name, grid=(S // tm,),
                 in_specs=[_rowblk(tm, C), _res((1, C)), _res((1, C)), _rowblk(tm, C)],
                 out_specs=[_rowblk(tm, C), _res((1, C)), _res((1, C))],
                 out_shape=[_sds((S, C), F32), _sds((1, C), F32), _sds((1, C), F32)])(
                     v2, ln_g.reshape(1, C), ln_b.reshape(1, C), dcat)


def _ffn_act_fwd(ug, uv, w_dw, b_dw, *, name):
    S, Fw = ug.shape
    taps = w_dw.shape[0]
    tc, rows = _pick(Fw, (256, 128)), _pick(S, (256, 128))
    nb, nch = Fw // tc, S // rows

    def body(ug_ref, uv_ref, wg_ref, wv_ref, bg_ref, bv_ref, o_ref, pg_ref, pv_ref):
        pg_ref[0:F_HALO, :] = jnp.zeros((F_HALO, tc), F32)
        pv_ref[0:F_HALO, :] = jnp.zeros((F_HALO, tc), F32)
        pg_ref[F_HALO:F_HALO + S, :] = ug_ref[...]
        pv_ref[F_HALO:F_HALO + S, :] = uv_ref[...]

        def act(c, _):
            gc = _conv_chunk(pg_ref, c, rows, F_HALO, wg_ref, taps) + bg_ref[...]
            vc = _conv_chunk(pv_ref, c, rows, F_HALO, wv_ref, taps) + bv_ref[...]
            o_ref[pl.ds(pl.multiple_of(c * rows, rows), rows), :] = (gc * _sig(gc) * vc).astype(MXU_DT)
            return 0

        lax.fori_loop(0, nch, act, 0)

    b2 = b_dw.reshape(1, 2 * Fw)
    return _call(body, name=name, grid=(nb,),
                 in_specs=[_colblk(S, tc), _colblk(S, tc), _colblk(taps, tc), _colblk(taps, tc, nb),
                           _colblk(1, tc), _colblk(1, tc, nb)],
                 out_specs=_colblk(S, tc), out_shape=_sds((S, Fw), MXU_DT),
                 scratch_shapes=[pltpu.VMEM((S + F_HALO, tc), F32), pltpu.VMEM((S + F_HALO, tc), F32)])(
                     ug, uv, w_dw, w_dw, b2, b2)


def _ffn_act_bwd(ug, uv, dact, w_dw, b_dw, *, name):
    S, Fw = ug.shape
    taps = w_dw.shape[0]
    tc, rows = _pick(Fw, (256, 128)), _pick(S, (256, 128))
    nb, nch = Fw // tc, S // rows

    def body(ug_ref, uv_ref, da_ref, wg_ref, wv_ref, bg_ref, bv_ref, dug_ref, duv_ref, dwg_ref, dwv_ref,
             dbg_ref, dbv_ref, pg_ref, pv_ref, qg_ref, qv_ref):
        pg_ref[0:F_HALO, :] = jnp.zeros((F_HALO, tc), F32)
        pv_ref[0:F_HALO, :] = jnp.zeros((F_HALO, tc), F32)
        qg_ref[S:S + F_HALO, :] = jnp.zeros((F_HALO, tc), F32)
        qv_ref[S:S + F_HALO, :] = jnp.zeros((F_HALO, tc), F32)
        pg_ref[F_HALO:F_HALO + S, :] = ug_ref[...]
        pv_ref[F_HALO:F_HALO + S, :] = uv_ref[...]
        dwg_ref[...] = jnp.zeros((taps, tc), F32)
        dwv_ref[...] = jnp.zeros((taps, tc), F32)

        def grads(c, carry):
            sg, sv = carry
            r = pl.ds(pl.multiple_of(c * rows, rows), rows)
            gc = _conv_chunk(pg_ref, c, rows, F_HALO, wg_ref, taps) + bg_ref[...]
            vc = _conv_chunk(pv_ref, c, rows, F_HALO, wv_ref, taps) + bv_ref[...]
            s = _sig(gc)
            da = da_ref[r, :]
            dgc = da * vc * (s * (1.0 + gc * (1.0 - s)))
            dvc = da * (gc * s)
            qg_ref[r, :] = dgc
            qv_ref[r, :] = dvc
            _conv_wgrad_chunk(pg_ref, c, rows, F_HALO, dgc, dwg_ref, taps)
            _conv_wgrad_chunk(pv_ref, c, rows, F_HALO, dvc, dwv_ref, taps)
            return sg + jnp.sum(dgc, axis=0, keepdims=True), sv + jnp.sum(dvc, axis=0, keepdims=True)

        z = jnp.zeros((1, tc), F32)
        sg, sv = lax.fori_loop(0, nch, grads, (z, z))
        dbg_ref[...] = sg
        dbv_ref[...] = sv

        def back(c, _):
            r = pl.ds(pl.multiple_of(c * rows, rows), rows)
            dug_ref[r, :] = _conv_t_chunk(qg_ref, c, rows, F_HALO, wg_ref, taps).astype(MXU_DT)
            duv_ref[r, :] = _conv_t_chunk(qv_ref, c, rows, F_HALO, wv_ref, taps).astype(MXU_DT)
            return 0

        lax.fori_loop(0, nch, back, 0)

    b2 = b_dw.reshape(1, 2 * Fw)
    pad = pltpu.VMEM((S + F_HALO, tc), F32)
    dug, duv, dwg, dwv, dbg, dbv = _call(
        body, name=name, grid=(nb,),
        in_specs=[_colblk(S, tc), _colblk(S, tc), _colblk(S, tc), _colblk(taps, tc), _colblk(taps, tc, nb),
                  _colblk(1, tc), _colblk(1, tc, nb)],
        out_specs=[_colblk(S, tc), _colblk(S, tc), _colblk(taps, tc), _colblk(taps, tc), _colblk(1, tc),
                   _colblk(1, tc)],
        out_shape=[_sds((S, Fw), MXU_DT), _sds((S, Fw), MXU_DT), _sds((taps, Fw), F32), _sds((taps, Fw), F32),
                   _sds((1, Fw), F32), _sds((1, Fw), F32)],
        scratch_shapes=[pad, pad, pad, pad])(ug, uv, dact, w_dw, w_dw, b2, b2)
    return dug, duv, jnp.concatenate([dwg, dwv], axis=1), jnp.concatenate([dbg, dbv], axis=1)


def _head_mask(h, width):
    lane = lax.broadcasted_iota(jnp.int32, (1, width), 1)
    return (lane >= h * HEAD_DIM) & (lane < (h + 1) * HEAD_DIM)


def _mem_attn_fwd(p, qblk, mkv, l, *, Mw, name):
    S, ML = p.shape[0], mkv.shape[0]
    tm, nh, scale = _pick(S, (256, 128)), Mw // HEAD_DIM, HEAD_DIM ** -0.5

    def body(q_ref, k_ref, v_ref, o_ref):
        q, kv, vv = q_ref[...], k_ref[...], v_ref[...]
        out = jnp.zeros((tm, Mw), F32)
        for h in range(nh):
            mk = _head_mask(h, Mw)
            s = _dot(jnp.where(mk, q, 0.0).astype(MXU_DT), kv, NT) * scale
            e = jnp.exp(s - jnp.max(s, axis=-1, keepdims=True))
            pr = e / jnp.sum(e, axis=-1, keepdims=True)
            out = out + _dot(pr.astype(MXU_DT), jnp.where(mk, vv, jnp.zeros_like(vv)))
        o_ref[...] = out

    return _call(body, name=name, grid=(S // tm,),
                 in_specs=[_rowblk(tm, Mw, qblk), pl.BlockSpec((ML, Mw), lambda i: (0, 2 * l)),
                           pl.BlockSpec((ML, Mw), lambda i: (0, 2 * l + 1))],
                 out_specs=_rowblk(tm, Mw), out_shape=_sds((S, Mw), F32))(p, mkv, mkv)


def _mem_attn_bwd(p, qblk, mkv, l, dcat, doblk, *, Mw, name):
    S, ML = p.shape[0], mkv.shape[0]
    tm, nh, scale = _pick(S, (256, 128)), Mw // HEAD_DIM, HEAD_DIM ** -0.5

    def body(q_ref, k_ref, v_ref, do_ref, dq_ref, dk_ref, dv_ref):
        i = pl.program_id(0)

        @pl.when(i == 0)
        def _():
            dk_ref[...] = jnp.zeros((ML, Mw), F32)
            dv_ref[...] = jnp.zeros((ML, Mw), F32)

        q, kv, vv, do = q_ref[...], k_ref[...], v_ref[...], do_ref[...]
        dq = jnp.zeros((tm, Mw), F32)
        for h in range(nh):
            mk = _head_mask(h, Mw)
            qh = jnp.where(mk, q, 0.0).astype(MXU_DT)
            s = _dot(qh, kv, NT) * scale
            e = jnp.exp(s - jnp.max(s, axis=-1, keepdims=True))
            pr = e / jnp.sum(e, axis=-1, keepdims=True)
            doh = jnp.where(mk, do, 0.0).astype(MXU_DT)
            dv_ref[...] += _dot_tn(pr, doh)
            dp = _dot(doh, vv, NT)
            ds = pr * (dp - jnp.sum(dp * pr, axis=-1, keepdims=True))
            dq = dq + _dot(ds.astype(MXU_DT), jnp.where(mk, kv, jnp.zeros_like(kv))) * scale
            dk_ref[...] += _dot_tn(ds, qh) * scale
        dq_ref[...] = dq

    return _call(body, name=name, grid=(S // tm,),
                 in_specs=[_rowblk(tm, Mw, qblk), pl.BlockSpec((ML, Mw), lambda i: (0, 2 * l)),
                           pl.BlockSpec((ML, Mw), lambda i: (0, 2 * l + 1)), _rowblk(tm, Mw, doblk)],
                 out_specs=[_rowblk(tm, Mw), _res((ML, Mw)), _res((ML, Mw))],
                 out_shape=[_sds((S, Mw), F32), _sds((ML, Mw), F32), _sds((ML, Mw), F32)])(p, mkv, mkv, dcat)


FOX_GROUP = 2


def _fox_specs(S, dh, tq):
    nb, G = S // tq, FOX_GROUP
    qs = pl.BlockSpec((G, tq, dh), lambda h, i: (h, i, 0))
    ks = pl.BlockSpec((G, S, dh), lambda h, i: (h, 0, 0))
    cqs = pl.BlockSpec((G, tq, 1), lambda h, i: (h, i, 0))
    cks = pl.BlockSpec((G, nb, 1, tq), lambda h, i: (h, 0, 0, 0))
    return qs, ks, cqs, cks


def _fox_logits(qv, kv, cqv, ckv, tq, scale, diag):
    s = _dot(qv, kv, NT) * scale + cqv - ckv
    if not diag:
        return s
    rows = lax.broadcasted_iota(jnp.int32, (tq, tq), 0)
    cols = lax.broadcasted_iota(jnp.int32, (tq, tq), 1)
    return jnp.where(cols <= rows, s, NEG)


def _fox_fwd(q, k, v, cq, ck, *, tq, name):
    H, S, dh = q.shape
    scale, G = dh ** -0.5, FOX_GROUP
    assert H % G == 0
    qs, ks, cqs, cks = _fox_specs(S, dh, tq)

    def body(q_ref, k_ref, v_ref, cq_ref, ck_ref, o_ref, lse_ref):
        i = pl.program_id(1)
        qv, cqv = [q_ref[e] for e in range(G)], [cq_ref[e] for e in range(G)]

        def kblock(j, carry, diag):
            r = pl.ds(pl.multiple_of(j * tq, tq), tq)
            out = []
            for e in range(G):
                m, l, acc = carry[e]
                s = _fox_logits(qv[e], k_ref[e, r, :], cqv[e], ck_ref[e, j], tq, scale, diag)
                m2 = jnp.maximum(m, jnp.max(s, axis=-1, keepdims=True))
                pr = jnp.exp(s - m2)
                al = jnp.exp(m - m2)
                out.append((m2, al * l + jnp.sum(pr, axis=-1, keepdims=True),
                            al * acc + _dot(pr.astype(MXU_DT), v_ref[e, r, :])))
            return tuple(out)

        init = tuple((jnp.full((tq, 1), NEG, F32), jnp.zeros((tq, 1), F32), jnp.zeros((tq, dh), F32))
                     for _ in range(G))
        carry = lax.fori_loop(0, i, lambda j, c: kblock(j, c, False), init)
        for e, (m, l, acc) in enumerate(kblock(i, carry, True)):
            o_ref[e] = acc / l
            lse_ref[e] = m + jnp.log(l)

    return _call(body, name=name, grid=(H // G, S // tq), in_specs=[qs, ks, ks, cqs, cks], out_specs=[qs, cqs],
                 out_shape=[_sds((H, S, dh), F32), _sds((H, S, 1), F32)])(q, k, v, cq, ck)


def _fox_bwd(q, k, v, cq, ck, o, lse, do, *, tq, name):
    H, S, dh = q.shape
    nb, scale, G = S // tq, dh ** -0.5, FOX_GROUP
    qs, ks, cqs, cks = _fox_specs(S, dh, tq)

    def body(q_ref, k_ref, v_ref, cq_ref, ck_ref, o_ref, lse_ref, do_ref, dq_ref, dk_ref, dv_ref, dcq_ref,
             dck_ref):
        i = pl.program_id(1)

        @pl.when(i == 0)
        def _():
            dk_ref[...] = jnp.zeros((G, S, dh), F32)
            dv_ref[...] = jnp.zeros((G, S, dh), F32)
            dck_ref[...] = jnp.zeros((G, nb, 1, tq), F32)

        qv, cqv, lsev = ([ref[e] for e in range(G)] for ref in (q_ref, cq_ref, lse_ref))
        dob = [do_ref[e].astype(MXU_DT) for e in range(G)]
        delta = [jnp.sum(dob[e].astype(F32) * o_ref[e], axis=-1, keepdims=True) for e in range(G)]

        def kblock(j, carry, diag):
            r = pl.ds(pl.multiple_of(j * tq, tq), tq)
            out = []
            for e in range(G):
                dq, rs = carry[e]
                kv, vv = k_ref[e, r, :], v_ref[e, r, :]
                pr = jnp.exp(_fox_logits(qv[e], kv, cqv[e], ck_ref[e, j], tq, scale, diag) - lsev[e])
                ds = pr * (_dot(dob[e], vv, NT) - delta[e])
                dk_ref[e, r, :] += _dot_tn(ds, qv[e]) * scale
                dv_ref[e, r, :] += _dot_tn(pr, dob[e])
                dck_ref[e, j] += -jnp.sum(ds, axis=0, keepdims=True)
                out.append((dq + _dot(ds.astype(MXU_DT), kv), rs + jnp.sum(ds, axis=-1, keepdims=True)))
            return tuple(out)

        init = tuple((jnp.zeros((tq, dh), F32), jnp.zeros((tq, 1), F32)) for _ in range(G))
        carry = lax.fori_loop(0, i, lambda j, c: kblock(j, c, False), init)
        for e, (dq, rs) in enumerate(kblock(i, carry, True)):
            dq_ref[e] = dq * scale
            dcq_ref[e] = rs

    return _call(body, name=name, grid=(H // G, nb), in_specs=[qs, ks, ks, cqs, cks, qs, cqs, qs],
                 out_specs=[qs, ks, ks, cqs, cks],
                 out_shape=[_sds((H, S, dh), F32), _sds((H, S, dh), F32), _sds((H, S, dh), F32),
                            _sds((H, S, 1), F32), _sds((H, nb, 1, tq), F32)])(q, k, v, cq, ck, o, lse, do)


def _tri(n, lower):
    r = lax.broadcasted_iota(jnp.int32, (n, n), 0)
    c = lax.broadcasted_iota(jnp.int32, (n, n), 1)
    return ((c <= r) if lower else (c >= r)).astype(F32)


def _fgate_fwd(fr, bf, *, name):
    S, W = fr.shape
    B = _pick(S, (256, 128))

    def body(f_ref, b_ref, cum_ref):
        L = _tri(B, True)
        carry = jnp.zeros((1, W), F32)
        for blk in range(S // B):
            z = f_ref[blk * B:(blk + 1) * B, :] + b_ref[...]
            ls = jnp.minimum(z, 0.0) - jnp.log(1.0 + jnp.exp(-jnp.abs(z)))
            cum_ref[blk * B:(blk + 1) * B, :] = jnp.dot(L, ls, precision=lax.Precision.HIGHEST,
                                                        preferred_element_type=F32) + carry
            carry = carry + jnp.sum(ls, axis=0, keepdims=True)

    return _call(body, name=name, out_shape=_sds((S, W), F32))(fr, bf)


def _fgate_bwd(fr, bf, dcum, *, name):
    S, W = fr.shape
    B = _pick(S, (256, 128))

    def body(f_ref, b_ref, dc_ref, df_ref, db_ref):
        U = _tri(B, False)
        carry = jnp.zeros((1, W), F32)
        dbs = jnp.zeros((1, W), F32)
        for blk in reversed(range(S // B)):
            dc = dc_ref[blk * B:(blk + 1) * B, :]
            dls = jnp.dot(U, dc, precision=lax.Precision.HIGHEST, preferred_element_type=F32) + carry
            carry = carry + jnp.sum(dc, axis=0, keepdims=True)
            z = f_ref[blk * B:(blk + 1) * B, :] + b_ref[...]
            df = dls * (1.0 / (1.0 + jnp.exp(z)))
            df_ref[blk * B:(blk + 1) * B, :] = df
            dbs = dbs + jnp.sum(df, axis=0, keepdims=True)
        db_ref[...] = dbs

    return _call(body, name=name, out_shape=[_sds((S, W), F32), _sds((1, W), F32)])(fr, bf, dcum)


def _flip(v, bit):
    return 1 - v if bit else v


HBM_SPEC = pl.BlockSpec(memory_space=pltpu.HBM)
SEM_SPEC = pl.BlockSpec(memory_space=pltpu.SEMAPHORE)


def _xchg_copies(src, land, sems, scatter):
    n = len(src)
    send, recv, loc = sems[:7 * n], sems[7 * n:14 * n], sems[14 * n:15 * n]
    x, y, c = lax.axis_index("x"), lax.axis_index("y"), lax.axis_index("c")
    me = 4 * x + 2 * y + c

    def peer(m):
        return _flip(x, m & 4), _flip(y, m & 2), _flip(c, m & 1)

    def copy(i, m):
        px, py, pc = peer(m)
        return pltpu.make_async_remote_copy(
            src_ref=src[i].at[4 * px + 2 * py + pc] if scatter[i] else src[i], dst_ref=land[i].at[me],
            send_sem=send[7 * i + m - 1], recv_sem=recv[7 * i + m - 1], device_id=(px, py, pc), device_id_type=MESH)

    def arrival(i, m):
        px, py, pc = peer(m)
        slot = land[i].at[4 * px + 2 * py + pc]
        return pltpu.make_async_remote_copy(src_ref=slot, dst_ref=slot, send_sem=send[7 * i + m - 1],
                                            recv_sem=recv[7 * i + m - 1], device_id=(px, py, pc), device_id_type=MESH)

    def own(i):
        return pltpu.make_async_copy(src[i].at[me] if scatter[i] else src[i], land[i].at[me], loc[i])

    return copy, arrival, own


def _xchg_start(srcs, scatter, *, name):
    n = len(srcs)
    lands = [_sds((N_DEV,) + s.shape[-2:], s.dtype) for s in srcs]

    ns = 15 * n

    def body(*refs):
        src, land, sems, token = refs[:n], refs[n:2 * n], refs[2 * n:2 * n + ns], refs[-1]
        copy, _, own = _xchg_copies(src, land, sems, scatter)
        for i in range(n):
            own(i).start()
            for m in range(1, N_DEV):
                copy(i, m).start()
        token[...] = jnp.zeros(token.shape, F32)

    thru = [pltpu.HBM(s.shape, s.dtype) for s in srcs] + [pltpu.HBM(s.shape, s.dtype) for s in lands]
    out = pl.pallas_call(
        body, name=name,
        out_shape=(*[pltpu.SemaphoreType.DMA(())] * ns, *thru, _sds((8, 128), F32)),
        in_specs=[HBM_SPEC] * (2 * n),
        out_specs=(*[SEM_SPEC] * ns, *[HBM_SPEC] * (2 * n), pl.BlockSpec(memory_space=pltpu.VMEM)),
        input_output_aliases={i: ns + i for i in range(2 * n)},
        compiler_params=pltpu.CompilerParams(has_side_effects=pltpu.SideEffectType.DATAFLOW_SIDE_EFFECTING),
    )(*[pltpu.with_memory_space_constraint(s, pltpu.HBM) for s in srcs],
      *[pltpu.with_memory_space_constraint(lax.empty(s.shape, s.dtype), pltpu.HBM) for s in lands])
    bufs = list(out[ns:ns + 2 * n])
    return (list(out[:ns]), bufs[:n], bufs[n:]), out[-1]


def _xchg_wait(handle, after, scatter, *, name):
    sems, srcs, lands = handle
    n = len(srcs)
    ns = 15 * n

    def body(*refs):
        src, land = refs[:n], refs[n:2 * n]
        copy, arrival, own = _xchg_copies(src, land, refs[2 * n:2 * n + ns], scatter)
        for i in range(n):
            own(i).wait()
            for m in range(1, N_DEV):
                copy(i, m).wait_send()
                arrival(i, m).wait_recv()

    out = pl.pallas_call(
        body, name=name,
        out_shape=tuple(pltpu.HBM(s.shape, s.dtype) for s in srcs + lands),
        in_specs=[HBM_SPEC] * (2 * n) + [SEM_SPEC] * ns + [ANY],
        out_specs=tuple([HBM_SPEC] * (2 * n)),
        input_output_aliases={i: i for i in range(2 * n)},
        compiler_params=pltpu.CompilerParams(has_side_effects=pltpu.SideEffectType.DATAFLOW_SIDE_EFFECTING),
    )(*srcs, *lands, *sems, after)
    return list(out[n:])


def _reduce_adam_body(r_ref, w_ref, m_ref, v_ref, g_ref, d_ref, m2_ref, v2_ref):
    g = r_ref[0].astype(F32)
    for s in range(1, N_DEV):
        g = g + r_ref[s].astype(F32)
    mm = ADAM_B1 * m_ref[...] + (1.0 - ADAM_B1) * g
    vv = ADAM_B2 * v_ref[...] + (1.0 - ADAM_B2) * (g * g)
    m_hat = mm / (1.0 - ADAM_B1 ** ADAM_STEP)
    v_hat = vv / (1.0 - ADAM_B2 ** ADAM_STEP)
    g_ref[...] = g
    d_ref[...] = -ADAM_LR * (m_hat / (jnp.sqrt(v_hat) + ADAM_EPS) + ADAM_WD * w_ref[...])
    m2_ref[...] = mm
    v2_ref[...] = vv


def _reduce_adam(recv, w, m, v, *, name):
    R, L = w.shape
    tr = _pick(R, (256, 128, 64, 32, 16, 8))

    def body(*refs):
        _reduce_adam_body(*refs)

    blk = _rowblk(tr, L)
    return _call(body, name=name, grid=(R // tr,),
                 in_specs=[pl.BlockSpec((N_DEV, tr, L), lambda i: (0, i, 0)), blk, blk, blk],
                 out_specs=[blk, blk, blk, blk], out_shape=[_sds((R, L), F32)] * 4)(recv, w, m, v)


def _reduce_adam_layer(recv, w, m, v, idx, prev, *, name):
    r, c = w.shape[-2:]
    tr = _pick(r, (256, 128)) if r % 128 == 0 else r
    if prev is None:
        prev = [lax.empty(w.shape, F32) for _ in range(4)]

    def body(r_ref, w_ref, m_ref, v_ref, *rest):
        _reduce_adam_body(r_ref, w_ref, m_ref, v_ref, *rest[-4:])

    blk = pl.BlockSpec((None, tr, c), lambda i: (idx, i, 0))
    return pl.pallas_call(
        body, name=name, grid=(r // tr,),
        in_specs=[pl.BlockSpec((N_DEV, tr, c), lambda i: (0, i, 0)), blk, blk, blk] + [ANY] * 4,
        out_specs=[blk] * 4, out_shape=[_sds(w.shape, F32)] * 4, input_output_aliases={4 + j: j for j in range(4)},
        compiler_params=pltpu.CompilerParams(vmem_limit_bytes=VMEM_LIMIT_V7X, dimension_semantics=("arbitrary",)),
    )(recv, w, m, v, *prev)


class _Pack:
    def __init__(self, shapes, row_mult):
        self.shapes, self.offs, rows = dict(shapes), {}, 0
        for name, shp in shapes:
            size = 1
            for d in shp:
                size *= d
            nr = -(-size // (16 * PACK_LANES)) * 16
            self.offs[name] = (rows, size, nr)
            rows += nr
        self.used = rows
        self.rows = -(-rows // row_mult) * row_mult

    def pack(self, arrays, dtype, lead=()):
        parts = []
        for name, (r0, size, nr) in self.offs.items():
            flat = arrays[name].astype(dtype).reshape(lead + (size,))
            flat = jnp.pad(flat, [(0, 0)] * len(lead) + [(0, nr * PACK_LANES - size)])
            parts.append(flat.reshape(lead + (nr, PACK_LANES)))
        if self.rows > self.used:
            parts.append(jnp.zeros(lead + (self.rows - self.used, PACK_LANES), dtype))
        return jnp.concatenate(parts, axis=len(lead))

    def unpack(self, buf, lead=()):
        out = {}
        for name, (r0, size, nr) in self.offs.items():
            flat = buf[..., r0:r0 + nr, :].reshape(lead + (nr * PACK_LANES,))
            out[name] = flat[..., :size].reshape(lead + tuple(self.shapes[name]))
        return out


def _to_full(g8, ax):
    t = jnp.moveaxis(g8, 0, ax)
    return t.reshape(t.shape[:ax] + (t.shape[ax] * t.shape[ax + 1],) + t.shape[ax + 2:])


def _to_shards(full, ax):
    shp = full.shape
    return jnp.moveaxis(full.reshape(shp[:ax] + (N_DEV, shp[ax] // N_DEV) + shp[ax + 1:]), ax, 0)


def _to_heads(a, H):
    S = a.shape[0]
    return a.reshape(S, H, HEAD_DIM).transpose(1, 0, 2)


def _from_heads(a):
    H, S, dh = a.shape
    return a.transpose(1, 0, 2).reshape(S, H * dh)


def kernel(x, mem, g_mix, w_in_a, b_glu, w_dw_a, b_dw_a, ln_g, ln_b, g_kv, w_kvf, b_f, w_in_b, g_mem, w_mem_kv, w_out, g_ffn, w_up, w_dw_f, b_dw_f, w_down, g_final, loss_target, m_g_mix, m_w_in_a, m_b_glu, m_w_dw_a, m_b_dw_a, m_ln_g, m_ln_b, m_g_kv, m_w_kvf, m_b_f, m_w_in_b, m_g_mem, m_w_mem_kv, m_w_out, m_g_ffn, m_w_up, m_w_dw_f, m_b_dw_f, m_w_down, m_g_final, v_g_mix, v_w_in_a, v_b_glu, v_w_dw_a, v_b_dw_a, v_ln_g, v_ln_b, v_g_kv, v_w_kvf, v_b_f, v_w_in_b, v_g_mem, v_w_mem_kv, v_w_out, v_g_ffn, v_w_up, v_w_dw_f, v_b_dw_f, v_w_down, v_g_final):
    given = dict(locals())
    W = {n: given[n] for n in WEIGHTS}
    x0, mem0, tgt = x[0], mem[0], loss_target[0]
    S, D = x0.shape
    depth, n_a = g_mix.shape[0], w_in_a.shape[0]
    C = w_dw_a.shape[2] * N_DEV
    Mw = D - C
    Fw = w_down.shape[1] * N_DEV
    H = b_f.shape[0]
    assert C == H * HEAD_DIM and (2 * C) % Mw == 0 and C % Mw == 0 and H <= GATE_LANES
    tq = _pick(S, (256, 128))
    nkv = 2 * C + GATE_LANES

    def mix_keys(l):
        keys = [("w_in_a", l) if l < n_a else ("w_in_b", l - n_a), ("w_mem_kv", l), ("w_out", l)]
        return keys + ([("w_kvf", 0)] if l == n_a else [])

    def ffn_keys(l):
        return [("w_up", l), ("w_down", l)]

    def key_ax(key):
        return big_ax[key[0]] - 1

    W3 = {n: (W[n][None] if n == "w_kvf" else W[n]) for n, _ in BIG}
    big_ax = {n: (ax + 1 if n == "w_kvf" else ax) for n, ax in BIG}
    pk_small = _Pack([(n, W[n].shape) for n, _ in SMALL], 8)
    pk_rep = _Pack([(n, W[n].shape) for n in REP] + [("loss", (1,))], 8)

    ws32 = pk_small.pack(W, F32)
    gathers, toks = {}, []
    for l in range(depth):
        for tag, keys in (("mix", mix_keys(l)), ("ffn", ffn_keys(l))):
            srcs = [W3[n][i].astype(COMM_DT) for n, i in keys] + ([ws32] if (l, tag) == (0, "mix") else [])
            gathers[l, tag], t = _xchg_start(srcs, [False] * len(srcs), name=f"w_gather_start_{tag}{l}")
            toks.append(t)
    x0, mem0 = lax.optimization_barrier((x0, mem0, *toks))[:2]
    bf_pad = jnp.pad(b_f, (0, GATE_LANES - H)).reshape(1, GATE_LANES)

    def gathered(l, tag, keys, after):
        n = len(gathers[l, tag][1])
        lands = _xchg_wait(gathers[l, tag], after, [False] * n, name=f"w_gather_wait_{tag}{l}")
        return {k: _to_full(a, key_ax(k)).astype(MXU_DT) for k, a in zip(keys, lands)}, lands[len(keys):]

    mem_n = _rms_fwd(mem0, g_mem, name="mem_norm")
    sv = []
    xs = x0
    for l in range(depth):
        wl, extra = gathered(l, "mix", mix_keys(l), xs)
        if extra:
            gs = pk_small.unpack(extra[0], (N_DEV,))
            small = {n: _to_full(gs[n], ax) for n, ax in SMALL}
        t = dict(x_in=xs, w=wl)
        t["mkv"] = _mm(mem_n, wl["w_mem_kv", l], name=f"mem_kv{l}", out_dtype=MXU_DT)
        t["h"] = _rms_fwd(xs, g_mix[l], name=f"mix_norm{l}")
        if l < n_a:
            t["p"] = _mm(t["h"], wl["w_in_a", l], name=f"in_proj{l}", out_dtype=F32)
            t["v2"] = _glu_conv_fwd(t["p"], small["b_glu"][l], small["w_dw_a"][l], small["b_dw_a"][l],
                                    name=f"glu_conv{l}")
            memo = _mem_attn_fwd(t["p"], 2 * C // Mw, t["mkv"], 0, Mw=Mw, name=f"mem_attn{l}")
            t["cat"] = _ln_silu_cat_fwd(t["v2"], small["ln_g"][l], small["ln_b"][l], memo, name=f"ln_silu{l}")
        else:
            if l == n_a:
                wkvf = jnp.pad(wl["w_kvf", 0], ((0, 0), (0, nkv - w_kvf.shape[1])))
                hk = _rms_fwd(xs, g_kv, name="kv_norm")
                kvf = _mm(hk, wkvf, name="kv_proj", out_dtype=F32)
                k_h = _to_heads(kvf[:, :C], H).astype(MXU_DT)
                v_h = _to_heads(kvf[:, C:2 * C], H).astype(MXU_DT)
                fr = kvf[:, 2 * C:]
                cum = _fgate_fwd(fr, bf_pad, name="fgate")
                cum_t = cum[:, :H].T
                cq, ck = cum_t.reshape(H, S, 1), cum_t.reshape(H, S // tq, 1, tq)
            t["p"] = _mm(t["h"], wl["w_in_b", l - n_a], name=f"in_proj{l}", out_dtype=F32)
            t["q_h"] = _to_heads(t["p"][:, :C], H).astype(MXU_DT)
            t["o_h"], t["lse"] = _fox_fwd(t["q_h"], k_h, v_h, cq, ck, tq=tq, name=f"fox{l}")
            memo = _mem_attn_fwd(t["p"], C // Mw, t["mkv"], 0, Mw=Mw, name=f"mem_attn{l}")
            t["cat"] = jnp.concatenate([_from_heads(t["o_h"]), memo], axis=1).astype(MXU_DT)
        t["x_mid"] = _mm(t["cat"], wl["w_out", l], name=f"out_proj{l}", out_dtype=F32, add=xs)
        wl.update(gathered(l, "ffn", ffn_keys(l), t["x_mid"])[0])
        t["h2"] = _rms_fwd(t["x_mid"], g_ffn[l], name=f"ffn_norm{l}")
        t["ug"] = _mm(t["h2"], wl["w_up", l], name=f"up_gate{l}", out_dtype=F32, cols=(0, Fw))
        t["uv"] = _mm(t["h2"], wl["w_up", l], name=f"up_val{l}", out_dtype=F32, cols=(Fw, Fw))
        t["act"] = _ffn_act_fwd(t["ug"], t["uv"], small["w_dw_f"][l], b_dw_f[l], name=f"ffn_act{l}")
        xs = _mm(t["act"], wl["w_down", l], name=f"down_proj{l}", out_dtype=F32, add=t["x_mid"])
        sv.append(t)
    loss_dev, dx, dg_final = _loss_bwd(xs, g_final, tgt, name="loss_head")

    M1 = {n: given["m_" + n] for n in WEIGHTS}
    V1 = {n: given["v_" + n] for n in WEIGHTS}
    M3 = {n: (M1[n][None] if n == "w_kvf" else M1[n]) for n, _ in BIG}
    V3 = {n: (V1[n][None] if n == "w_kvf" else V1[n]) for n, _ in BIG}
    res, chain, pending = {}, {}, []

    def start_grads(tag, l, keys, gl, extra=(), extra_scatter=()):
        srcs = [_to_shards(gl[k], key_ax(k)) for k in keys] + list(extra)
        scatter = [True] * len(keys) + list(extra_scatter)
        handle, tk = _xchg_start(srcs, scatter, name=f"g_xchg_start_{tag}{l}")
        pending.append((f"{tag}{l}", keys, handle, scatter))
        return tk

    def finish_grads(after):
        tag, keys, handle, scatter = pending.pop(0)
        lands = _xchg_wait(handle, after, scatter, name=f"g_xchg_wait_{tag}")
        for (n, i), recv in zip(keys, lands):
            chain[n] = _reduce_adam_layer(recv, W3[n], M3[n], V3[n], i, chain.get(n), name=f"adam_{n}{i}")
        return lands[len(keys):]

    G = {n: [None] * W[n].shape[0] for n in ("g_mix", "b_glu", "w_dw_a", "b_dw_a", "ln_g", "ln_b", "g_ffn",
                                              "w_dw_f", "b_dw_f")}
    dk_sum = dv_sum = dck_sum = dmem_n = None
    for l in reversed(range(depth)):
        t = sv[l]
        wl, gl = t["w"], {}
        dact = _mm(dx, wl["w_down", l], name=f"d_act{l}", out_dtype=F32, nt=True)
        gl["w_down", l] = _mm_tn(t["act"], dx, name=f"dw_down{l}", out_dtype=COMM_DT)
        dug, duv, G["w_dw_f"][l], db = _ffn_act_bwd(t["ug"], t["uv"], dact, small["w_dw_f"][l], b_dw_f[l],
                                                    name=f"d_ffn_act{l}")
        G["b_dw_f"][l] = db[0]
        dh2 = _mm_nt2(dug, duv, wl["w_up", l], name=f"d_up{l}")
        gl["w_up", l] = jnp.concatenate([_mm_tn(t["h2"], dug, name=f"dw_up_gate{l}", out_dtype=COMM_DT),
                                         _mm_tn(t["h2"], duv, name=f"dw_up_val{l}", out_dtype=COMM_DT)], axis=1)
        dx, dg = _rms_bwd(t["x_mid"], g_ffn[l], dh2, dx, name=f"d_ffn_norm{l}")
        G["g_ffn"][l] = dg[0]
        dx = lax.optimization_barrier((dx, start_grads("ffn", l, ffn_keys(l), gl)))[0]
        if len(pending) > 2:
            finish_grads(dx)

        dcat = _mm(dx, wl["w_out", l], name=f"d_cat{l}", out_dtype=F32, nt=True)
        gl["w_out", l] = _mm_tn(t["cat"], dx, name=f"dw_out{l}", out_dtype=COMM_DT)
        if l >= n_a:
            do_h = _to_heads(dcat[:, :C], H)
            dq_h, dk_h, dv_h, dcq, dck = _fox_bwd(t["q_h"], k_h, v_h, cq, ck, t["o_h"], t["lse"], do_h, tq=tq,
                                                  name=f"d_fox{l}")
            dck = dck.reshape(H, S) + dcq.reshape(H, S)
            dk_sum = dk_h if dk_sum is None else dk_sum + dk_h
            dv_sum = dv_h if dv_sum is None else dv_sum + dv_h
            dck_sum = dck if dck_sum is None else dck_sum + dck
            dqm, dmk, dmv = _mem_attn_bwd(t["p"], C // Mw, t["mkv"], 0, dcat, C // Mw, Mw=Mw,
                                          name=f"d_mem_attn{l}")
            dp = jnp.concatenate([_from_heads(dq_h), dqm], axis=1).astype(MXU_DT)
            key = ("w_in_b", l - n_a)
        else:
            dv2, dlg, dlb = _ln_silu_bwd(t["v2"], small["ln_g"][l], small["ln_b"][l], dcat, name=f"d_ln_silu{l}")
            G["ln_g"][l], G["ln_b"][l] = dlg[0], dlb[0]
            da, dgt, G["w_dw_a"][l], dbd, dba, dbg = _glu_conv_bwd(t["p"], small["b_glu"][l], small["w_dw_a"][l],
                                                                   dv2, name=f"d_glu_conv{l}")
            G["b_dw_a"][l] = dbd[0]
            G["b_glu"][l] = jnp.concatenate([dba[0], dbg[0]])
            dqm, dmk, dmv = _mem_attn_bwd(t["p"], 2 * C // Mw, t["mkv"], 0, dcat, C // Mw, Mw=Mw,
                                          name=f"d_mem_attn{l}")
            dp = jnp.concatenate([da, dgt, dqm], axis=1).astype(MXU_DT)
            key = ("w_in_a", l)
        dmkv = jnp.concatenate([dmk, dmv], axis=1).astype(MXU_DT)
        gl["w_mem_kv", l] = _mm_tn(mem_n, dmkv, name=f"dw_mem_kv{l}", out_dtype=COMM_DT)
        dmem_n = _mm(dmkv, wl["w_mem_kv", l], name=f"d_mem_kv{l}", out_dtype=F32, nt=True, add=dmem_n)
        dh = _mm(dp, wl[key], name=f"d_in_proj{l}", out_dtype=F32, nt=True)
        gl[key] = _mm_tn(t["h"], dp, name=f"dw_in_proj{l}", out_dtype=COMM_DT)
        dx, dg = _rms_bwd(t["x_in"], g_mix[l], dh, dx, name=f"d_mix_norm{l}")
        G["g_mix"][l] = dg[0]
        if l == n_a:
            dcum = jnp.pad(dck_sum.T, ((0, 0), (0, GATE_LANES - H)))
            df, dbf = _fgate_bwd(fr, bf_pad, dcum, name="d_fgate")
            dkvf = jnp.concatenate([_from_heads(dk_sum), _from_heads(dv_sum), df], axis=1).astype(MXU_DT)
            dhk = _mm(dkvf, wkvf, name="d_kv_proj", out_dtype=F32, nt=True)
            gl["w_kvf", 0] = _mm_tn(hk, dkvf, name="dw_kv_proj", out_dtype=COMM_DT)[:, :w_kvf.shape[1]]
            dx, dg_kv = _rms_bwd(t["x_in"], g_kv, dhk, dx, name="d_kv_norm")
        if l > 0:
            dx = lax.optimization_barrier((dx, start_grads("mix", l, mix_keys(l), gl)))[0]
            if len(pending) > 2:
                finish_grads(dx)

    _, dg_mem = _rms_bwd(mem0, g_mem, dmem_n, None, name="d_mem_norm")
    grads = {n: jnp.stack(v) for n, v in G.items()}
    grads.update(g_kv=dg_kv[0], b_f=dbf[0, :H], g_mem=dg_mem[0], g_final=dg_final[0], loss=loss_dev.reshape(1))
    gs8 = pk_small.pack({n: _to_shards(grads[n], ax) for n, ax in SMALL}, F32, (N_DEV,))
    start_grads("mix", 0, mix_keys(0), gl, [gs8, pk_rep.pack(grads, F32)], [True, False])
    while pending:
        extra = finish_grads(dx)
    no_state = dict(loss=jnp.zeros((1,), F32))
    for pk, recv, tag in ((pk_small, extra[0], "small"), (pk_rep, extra[1], "rep")):
        w32 = ws32 if tag == "small" else pk.pack({**W, **no_state}, F32)
        outs = _reduce_adam(recv, w32, pk.pack({**M1, **no_state}, F32), pk.pack({**V1, **no_state}, F32),
                            name=f"adam_{tag}")
        for kind, buf in zip(("grad", "delta", "new_m", "new_v"), outs):
            for n, a in pk.unpack(buf).items():
                res[kind, n] = a
    for n, outs in chain.items():
        for kind, a in zip(("grad", "delta", "new_m", "new_v"), outs):
            res[kind, n] = a.reshape(W[n].shape)

    loss = res["grad", "loss"][0]
    return (loss, dx[None], *[res[kind, n] for kind in ("grad", "delta", "new_m", "new_v") for n in WEIGHTS])
```

```python
import jax
import jax.numpy as jnp
from jax import lax
from jax.experimental import pallas as pl
from jax.experimental.pallas import tpu as pltpu

F32 = jnp.float32
MXU_DT = jnp.bfloat16
COMM_DT = jnp.bfloat16

N_DEV = 8
HEAD_DIM = 64
RMS_EPS = 1e-6
LN_EPS = 1e-5
ADAM_LR = 0.001
ADAM_B1 = 0.9
ADAM_B2 = 0.999
ADAM_EPS = 1e-08
ADAM_WD = 0.01
ADAM_STEP = 10

PACK_LANES = 1024
GATE_LANES = 128
VMEM_LIMIT_V7X = 56 << 20
NEG = -1e30
MESH = pl.DeviceIdType.MESH
ANY = pl.BlockSpec(memory_space=pl.ANY)
NT = (((1,), (1,)), ((), ()))
NN = (((1,), (0,)), ((), ()))

BIG = (("w_in_a", 2), ("w_kvf", 0), ("w_in_b", 1), ("w_mem_kv", 1), ("w_out", 1), ("w_up", 2), ("w_down", 1))
SMALL = (("b_glu", 1), ("w_dw_a", 2), ("b_dw_a", 1), ("ln_g", 1), ("ln_b", 1), ("w_dw_f", 2))
REP = ("g_mix", "g_kv", "b_f", "g_mem", "g_ffn", "b_dw_f", "g_final")
WEIGHTS = ("g_mix", "w_in_a", "b_glu", "w_dw_a", "b_dw_a", "ln_g", "ln_b", "g_kv", "w_kvf", "b_f", "w_in_b",
           "g_mem", "w_mem_kv", "w_out", "g_ffn", "w_up", "w_dw_f", "b_dw_f", "w_down", "g_final")


def _sds(shape, dtype):
    return jax.ShapeDtypeStruct(tuple(shape), dtype)


def _call(body, *, name, out_shape, grid=(), in_specs=None, out_specs=None, scratch_shapes=()):
    params = dict(vmem_limit_bytes=VMEM_LIMIT_V7X)
    if grid:
        params["dimension_semantics"] = ("arbitrary",) * len(grid)
    kw = {}
    if in_specs is not None:
        kw["in_specs"] = in_specs
    if out_specs is not None:
        kw["out_specs"] = out_specs
    return pl.pallas_call(body, name=name, grid=grid, out_shape=out_shape, scratch_shapes=list(scratch_shapes),
                          compiler_params=pltpu.CompilerParams(**params), **kw)


def _res(shape):
    nd = len(shape)
    return pl.BlockSpec(tuple(shape), lambda *_: (0,) * nd)


def _colblk(rows, tc, off=0):
    return pl.BlockSpec((rows, tc), lambda j: (0, j + off))


def _rowblk(tm, cols, off=0):
    return pl.BlockSpec((tm, cols), lambda i: (i, off))


def _pick(n, opts=(512, 256, 128)):
    for t in opts:
        if n % t == 0:
            return t
    return n


def _sig(z):
    return 1.0 / (1.0 + jnp.exp(-z))


def _dot(a, b, dims=NN):
    return lax.dot_general(a, b, dims, preferred_element_type=F32)


def _dot_tn(a, b):
    return _dot(a.T.astype(b.dtype), b)


def _mm_tn(a, b, *, name, out_dtype):
    S, K = a.shape
    N = b.shape[1]
    tk, rc = _pick(K, (256, 128)), _pick(S)
    cast = b.dtype != MXU_DT

    def body(a_ref, b_ref, o_ref, acc_ref, *bb):
        if cast:
            @pl.when(pl.program_id(0) == 0)
            def _():
                for r0 in range(0, S, rc):
                    bb[0][r0:r0 + rc, :] = b_ref[r0:r0 + rc, :].astype(MXU_DT)
            b_ref = bb[0]
        for n, r0 in enumerate(range(0, S, rc)):
            part = _dot(a_ref[r0:r0 + rc, :].astype(F32).T.astype(MXU_DT), b_ref[r0:r0 + rc, :])
            if n == 0:
                acc_ref[...] = part
            else:
                acc_ref[...] += part
        o_ref[...] = acc_ref[...].astype(out_dtype)

    return _call(body, name=name, grid=(K // tk,), in_specs=[_colblk(S, tk), _res((S, N))],
                 out_specs=pl.BlockSpec((tk, N), lambda j: (j, 0)), out_shape=_sds((K, N), out_dtype),
                 scratch_shapes=[pltpu.VMEM((tk, N), F32)] + ([pltpu.VMEM((S, N), MXU_DT)] if cast else []))(a, b)


def _mm_nt2(a1, a2, w, *, name):
    M, Fw = a1.shape
    N = w.shape[0]
    tm, nc = _pick(M), _pick(N)

    def body(a1_ref, a2_ref, w_ref, o_ref):
        v1, v2 = a1_ref[...], a2_ref[...]
        for n0 in range(0, N, nc):
            o_ref[:, n0:n0 + nc] = (_dot(v1, w_ref[n0:n0 + nc, 0:Fw], NT) + _dot(v2, w_ref[n0:n0 + nc, Fw:2 * Fw], NT))

    return _call(body, name=name, grid=(M // tm,), in_specs=[_rowblk(tm, Fw), _rowblk(tm, Fw), _res(w.shape)],
                 out_specs=_rowblk(tm, N), out_shape=_sds((M, N), F32))(a1, a2, w)


def _mm(a, w, *, name, out_dtype, nt=False, add=None, cols=None, after=None):
    M, K = a.shape
    N = w.shape[0] if nt else w.shape[1]
    assert (w.shape[1] if nt else w.shape[0]) == K
    dims = NT if nt else NN
    has_add, has_after = add is not None, after is not None
    if cols is not None:
        assert not nt and not has_add and not has_after
        c0, N = cols
        tn, rc = _pick(N), _pick(M)
        assert c0 % tn == 0

        def body(a_ref, w_ref, o_ref):
            wv = w_ref[...]
            for r0 in range(0, M, rc):
                o_ref[r0:r0 + rc, :] = _dot(a_ref[r0:r0 + rc, :], wv).astype(out_dtype)

        return _call(body, name=name, grid=(N // tn,), in_specs=[_res((M, K)), _colblk(K, tn, c0 // tn)],
                     out_specs=_colblk(M, tn), out_shape=_sds((M, N), out_dtype))(a, w)
    tn, rc = _pick(N), _pick(M)
    cast = a.dtype != MXU_DT

    def body(*refs):
        a_ref, w_ref, o_ref = refs[0], refs[1], refs[2 + has_add + has_after]
        if cast:
            @pl.when(pl.program_id(0) == 0)
            def _():
                for r0 in range(0, M, rc):
                    refs[-1][r0:r0 + rc, :] = a_ref[r0:r0 + rc, :].astype(MXU_DT)
            a_ref = refs[-1]
        wv = w_ref[...]
        for r0 in range(0, M, rc):
            acc = _dot(a_ref[r0:r0 + rc, :], wv, dims)
            if has_add:
                acc = acc + refs[2][r0:r0 + rc, :]
            o_ref[r0:r0 + rc, :] = acc.astype(out_dtype)

    w_spec = pl.BlockSpec((tn, K), lambda j: (j, 0)) if nt else pl.BlockSpec((K, tn), lambda j: (0, j))
    in_specs = ([_res((M, K)), w_spec] + ([_colblk(M, tn)] if has_add else [])
                + ([_res(after.shape)] if has_after else []))
    args = (a, w) + ((add,) if has_add else ()) + ((after,) if has_after else ())
    return _call(body, name=name, grid=(N // tn,), in_specs=in_specs, out_specs=_colblk(M, tn),
                 out_shape=_sds((M, N), out_dtype),
                 scratch_shapes=[pltpu.VMEM((M, K), MXU_DT)] if cast else [])(*args)


def _rms_fwd(x, g, *, name):
    M, D = x.shape
    tm = _pick(M, (256, 128))

    def body(x_ref, g_ref, h_ref):
        xf = x_ref[...]
        r = lax.rsqrt(jnp.mean(xf * xf, axis=-1, keepdims=True) + RMS_EPS)
        h_ref[...] = ((xf * r) * g_ref[...]).astype(MXU_DT)

    return _call(body, name=name, grid=(M // tm,), in_specs=[_rowblk(tm, D), _res((1, D))],
                 out_specs=_rowblk(tm, D), out_shape=_sds((M, D), MXU_DT))(x, g.reshape(1, D))


def _rms_bwd(x, g, dh, dx_in, *, name):
    M, D = x.shape
    tm = _pick(M, (256, 128))
    with_dx = dx_in is not None

    def body(*refs):
        if with_dx:
            x_ref, g_ref, dh_ref, dxin_ref, dx_ref, dg_ref = refs
        else:
            x_ref, g_ref, dh_ref, dg_ref = refs
        i = pl.program_id(0)
        xf = x_ref[...]
        r = lax.rsqrt(jnp.mean(xf * xf, axis=-1, keepdims=True) + RMS_EPS)
        y = xf * r
        dh_v = dh_ref[...]
        if with_dx:
            dy = dh_v * g_ref[...]
            dx_ref[...] = dxin_ref[...] + r * (dy - y * jnp.mean(dy * y, axis=-1, keepdims=True))
        part = jnp.sum(dh_v * y, axis=0, keepdims=True)

        @pl.when(i == 0)
        def _():
            dg_ref[...] = part

        @pl.when(i > 0)
        def _():
            dg_ref[...] += part

    ins = [x, g.reshape(1, D), dh] + ([dx_in] if with_dx else [])
    in_specs = [_rowblk(tm, D), _res((1, D)), _rowblk(tm, D)] + ([_rowblk(tm, D)] if with_dx else [])
    if with_dx:
        out_specs, out_shape = [_rowblk(tm, D), _res((1, D))], [_sds((M, D), F32), _sds((1, D), F32)]
    else:
        out_specs, out_shape = [_res((1, D))], [_sds((1, D), F32)]
    out = _call(body, name=name, grid=(M // tm,), in_specs=in_specs, out_specs=out_specs, out_shape=out_shape)(*ins)
    return out if with_dx else (None, out[0])


def _loss_bwd(x, g, t, *, name):
    M, D = x.shape
    tm = _pick(M, (256, 128))

    def body(x_ref, g_ref, t_ref, dx_ref, dg_ref, ls_ref):
        i = pl.program_id(0)
        xf = x_ref[...]
        r = lax.rsqrt(jnp.mean(xf * xf, axis=-1, keepdims=True) + RMS_EPS)
        xr = xf * r
        e = xr * g_ref[...] - t_ref[...]
        dout = e * (1.0 / D)
        dy = dout * g_ref[...]
        dx_ref[...] = r * (dy - xr * jnp.mean(dy * xr, axis=-1, keepdims=True))
        part = jnp.sum(dout * xr, axis=0, keepdims=True)
        lpart = jnp.zeros(ls_ref.shape, F32) + (0.5 / D) * jnp.sum(e * e, keepdims=True)

        @pl.when(i == 0)
        def _():
            dg_ref[...] = part
            ls_ref[...] = lpart

        @pl.when(i > 0)
        def _():
            dg_ref[...] += part
            ls_ref[...] += lpart

    dx, dg, ls = _call(body, name=name, grid=(M // tm,),
                       in_specs=[_rowblk(tm, D), _res((1, D)), _rowblk(tm, D)],
                       out_specs=[_rowblk(tm, D), _res((1, D)), _res((8, 128))],
                       out_shape=[_sds((M, D), F32), _sds((1, D), F32), _sds((8, 128), F32)])(x, g.reshape(1, D), t)
    return ls[0, 0], dx, dg


def _shift_rows(ext, off, rows):
    if off % 8 == 0:
        return ext[off:off + rows, :]
    return pltpu.roll(ext, ext.shape[0] - off, 0)[0:rows, :]


def _ext(pad_ref, c, rows, halo):
    return pad_ref[pl.ds(pl.multiple_of(c * rows, rows), rows + halo), :]


def _conv_chunk(pad_ref, c, rows, halo, w_ref, taps):
    ext = _ext(pad_ref, c, rows, halo)
    acc = None
    for k in range(taps):
        term = w_ref[k:k + 1, :] * _shift_rows(ext, halo - (taps - 1) + k, rows)
        acc = term if acc is None else acc + term
    return acc


def _conv_t_chunk(pad_ref, c, rows, halo, w_ref, taps):
    ext = _ext(pad_ref, c, rows, halo)
    acc = None
    for k in range(taps):
        term = w_ref[k:k + 1, :] * _shift_rows(ext, taps - 1 - k, rows)
        acc = term if acc is None else acc + term
    return acc


def _conv_wgrad_chunk(pad_ref, c, rows, halo, dy, dw_ref, taps):
    ext = _ext(pad_ref, c, rows, halo)
    for k in range(taps):
        dw_ref[k:k + 1, :] += jnp.sum(dy * _shift_rows(ext, halo - (taps - 1) + k, rows), axis=0, keepdims=True)


A_HALO, F_HALO = 32, 8


def _glu_conv_fwd(p, b_glu, w_dw, b_dw, *, name):
    S = p.shape[0]
    taps, C = w_dw.shape
    tc, rows = 128, _pick(S, (256, 128))
    nb, nch = C // tc, S // rows

    def body(a_ref, g_ref, ba_ref, bg_ref, w_ref, bd_ref, o_ref, pad_ref):
        pad_ref[0:A_HALO, :] = jnp.zeros((A_HALO, tc), F32)

        def fill(c, _):
            r = pl.ds(pl.multiple_of(c * rows, rows), rows)
            v1 = (a_ref[r, :] + ba_ref[...]) * _sig(g_ref[r, :] + bg_ref[...])
            pad_ref[pl.ds(pl.multiple_of(A_HALO + c * rows, 8), rows), :] = v1
            return 0

        lax.fori_loop(0, nch, fill, 0)

        def conv(c, _):
            o_ref[pl.ds(pl.multiple_of(c * rows, rows), rows), :] = (
                _conv_chunk(pad_ref, c, rows, A_HALO, w_ref, taps) + bd_ref[...])
            return 0

        lax.fori_loop(0, nch, conv, 0)

    b2 = b_glu.reshape(1, 2 * C)
    return _call(body, name=name, grid=(nb,),
                 in_specs=[_colblk(S, tc), _colblk(S, tc, nb), _colblk(1, tc), _colblk(1, tc, nb),
                           _colblk(taps, tc), _colblk(1, tc)],
                 out_specs=_colblk(S, tc), out_shape=_sds((S, C), F32),
                 scratch_shapes=[pltpu.VMEM((S + A_HALO, tc), F32)])(p, p, b2, b2, w_dw, b_dw.reshape(1, C))


def _glu_conv_bwd(p, b_glu, w_dw, dv2, *, name):
    S = p.shape[0]
    taps, C = w_dw.shape
    tc, rows = 128, _pick(S, (256, 128))
    nb, nch = C // tc, S // rows

    def body(a_ref, g_ref, ba_ref, bg_ref, w_ref, dy_ref, da_ref, dgt_ref, dw_ref, dbd_ref, dba_ref, dbg_ref,
             padx_ref, pady_ref):
        padx_ref[0:A_HALO, :] = jnp.zeros((A_HALO, tc), F32)
        pady_ref[S:S + A_HALO, :] = jnp.zeros((A_HALO, tc), F32)
        dw_ref[...] = jnp.zeros((taps, tc), F32)

        def fill(c, _):
            r = pl.ds(pl.multiple_of(c * rows, rows), rows)
            v1 = (a_ref[r, :] + ba_ref[...]) * _sig(g_ref[r, :] + bg_ref[...])
            padx_ref[pl.ds(pl.multiple_of(A_HALO + c * rows, 8), rows), :] = v1
            pady_ref[r, :] = dy_ref[r, :]
            return 0

        lax.fori_loop(0, nch, fill, 0)

        def back(c, carry):
            sd, sa, sg = carry
            r = pl.ds(pl.multiple_of(c * rows, rows), rows)
            dy = dy_ref[r, :]
            _conv_wgrad_chunk(padx_ref, c, rows, A_HALO, dy, dw_ref, taps)
            dv1 = _conv_t_chunk(pady_ref, c, rows, A_HALO, w_ref, taps)
            a = a_ref[r, :] + ba_ref[...]
            s = _sig(g_ref[r, :] + bg_ref[...])
            da = dv1 * s
            dgt = dv1 * a * s * (1.0 - s)
            da_ref[r, :] = da
            dgt_ref[r, :] = dgt
            return (sd + jnp.sum(dy, axis=0, keepdims=True), sa + jnp.sum(da, axis=0, keepdims=True),
                    sg + jnp.sum(dgt, axis=0, keepdims=True))

        z = jnp.zeros((1, tc), F32)
        sd, sa, sg = lax.fori_loop(0, nch, back, (z, z, z))
        dbd_ref[...] = sd
        dba_ref[...] = sa
        dbg_ref[...] = sg

    b2 = b_glu.reshape(1, 2 * C)
    return _call(body, name=name, grid=(nb,),
                 in_specs=[_colblk(S, tc), _colblk(S, tc, nb), _colblk(1, tc), _colblk(1, tc, nb),
                           _colblk(taps, tc), _colblk(S, tc)],
                 out_specs=[_colblk(S, tc), _colblk(S, tc), _colblk(taps, tc), _colblk(1, tc), _colblk(1, tc),
                            _colblk(1, tc)],
                 out_shape=[_sds((S, C), F32), _sds((S, C), F32), _sds((taps, C), F32), _sds((1, C), F32),
                            _sds((1, C), F32), _sds((1, C), F32)],
                 scratch_shapes=[pltpu.VMEM((S + A_HALO, tc), F32), pltpu.VMEM((S + A_HALO, tc), F32)])(
                     p, p, b2, b2, w_dw, dv2)


def _ln_silu_cat_fwd(v2, ln_g, ln_b, memo, *, name):
    S, C = v2.shape
    Mw = memo.shape[1]
    tm = _pick(S, (256, 128))

    def body(v_ref, g_ref, b_ref, m_ref, o_ref):
        v = v_ref[...]
        mu = jnp.mean(v, axis=-1, keepdims=True)
        d = v - mu
        y = d * lax.rsqrt(jnp.mean(d * d, axis=-1, keepdims=True) + LN_EPS) * g_ref[...] + b_ref[...]
        o_ref[:, 0:C] = (y * _sig(y)).astype(MXU_DT)
        o_ref[:, C:C + Mw] = m_ref[...].astype(MXU_DT)

    return _call(body, name=name, grid=(S // tm,),
                 in_specs=[_rowblk(tm, C), _res((1, C)), _res((1, C)), _rowblk(tm, Mw)],
                 out_specs=_rowblk(tm, C + Mw), out_shape=_sds((S, C + Mw), MXU_DT))(
                     v2, ln_g.reshape(1, C), ln_b.reshape(1, C), memo)


def _ln_silu_bwd(v2, ln_g, ln_b, dcat, *, name):
    S, C = v2.shape
    tm = _pick(S, (256, 128))

    def body(v_ref, g_ref, b_ref, dm_ref, dv_ref, dg_ref, db_ref):
        i = pl.program_id(0)
        v = v_ref[...]
        mu = jnp.mean(v, axis=-1, keepdims=True)
        d = v - mu
        rstd = lax.rsqrt(jnp.mean(d * d, axis=-1, keepdims=True) + LN_EPS)
        xh = d * rstd
        y = xh * g_ref[...] + b_ref[...]
        s = _sig(y)
        dyv = dm_ref[...] * (s * (1.0 + y * (1.0 - s)))
        dxh = dyv * g_ref[...]
        dv_ref[...] = rstd * (dxh - jnp.mean(dxh, axis=-1, keepdims=True)
                              - xh * jnp.mean(dxh * xh, axis=-1, keepdims=True))
        pg = jnp.sum(dyv * xh, axis=0, keepdims=True)
        pb = jnp.sum(dyv, axis=0, keepdims=True)

        @pl.when(i == 0)
        def _():
            dg_ref[...] = pg
            db_ref[...] = pb

        @pl.when(i > 0)
        def _():
            dg_ref[...] += pg
            db_ref[...] += pb

    return _call(body, name=name, grid=(S // tm,),
                 in_specs=[_rowblk(tm, C), _res((1, C)), _res((1, C)), _rowblk(tm, C)],
                 out_specs=[_rowblk(tm, C), _res((1, C)), _res((1, C))],
                 out_shape=[_sds((S, C), F32), _sds((1, C), F32), _sds((1, C), F32)])(
                     v2, ln_g.reshape(1, C), ln_b.reshape(1, C), dcat)


def _ffn_act_fwd(ug, uv, w_dw, b_dw, *, name):
    S, Fw = ug.shape
    taps = w_dw.shape[0]
    tc, rows = _pick(Fw, (256, 128)), _pick(S, (256, 128))
    nb, nch = Fw // tc, S // rows

    def body(ug_ref, uv_ref, wg_ref, wv_ref, bg_ref, bv_ref, o_ref, pg_ref, pv_ref):
        pg_ref[0:F_HALO, :] = jnp.zeros((F_HALO, tc), F32)
        pv_ref[0:F_HALO, :] = jnp.zeros((F_HALO, tc), F32)
        pg_ref[F_HALO:F_HALO + S, :] = ug_ref[...]
        pv_ref[F_HALO:F_HALO + S, :] = uv_ref[...]

        def act(c, _):
            gc = _conv_chunk(pg_ref, c, rows, F_HALO, wg_ref, taps) + bg_ref[...]
            vc = _conv_chunk(pv_ref, c, rows, F_HALO, wv_ref, taps) + bv_ref[...]
            o_ref[pl.ds(pl.multiple_of(c * rows, rows), rows), :] = (gc * _sig(gc) * vc).astype(MXU_DT)
            return 0

        lax.fori_loop(0, nch, act, 0)

    b2 = b_dw.reshape(1, 2 * Fw)
    return _call(body, name=name, grid=(nb,),
                 in_specs=[_colblk(S, tc), _colblk(S, tc), _colblk(taps, tc), _colblk(taps, tc, nb),
                           _colblk(1, tc), _colblk(1, tc, nb)],
                 out_specs=_colblk(S, tc), out_shape=_sds((S, Fw), MXU_DT),
                 scratch_shapes=[pltpu.VMEM((S + F_HALO, tc), F32), pltpu.VMEM((S + F_HALO, tc), F32)])(
                     ug, uv, w_dw, w_dw, b2, b2)


def _ffn_act_bwd(ug, uv, dact, w_dw, b_dw, *, name):
    S, Fw = ug.shape
    taps = w_dw.shape[0]
    tc, rows = _pick(Fw, (256, 128)), _pick(S, (256, 128))
    nb, nch = Fw // tc, S // rows

    def body(ug_ref, uv_ref, da_ref, wg_ref, wv_ref, bg_ref, bv_ref, dug_ref, duv_ref, dwg_ref, dwv_ref,
             dbg_ref, dbv_ref, pg_ref, pv_ref, qg_ref, qv_ref):
        pg_ref[0:F_HALO, :] = jnp.zeros((F_HALO, tc), F32)
        pv_ref[0:F_HALO, :] = jnp.zeros((F_HALO, tc), F32)
        qg_ref[S:S + F_HALO, :] = jnp.zeros((F_HALO, tc), F32)
        qv_ref[S:S + F_HALO, :] = jnp.zeros((F_HALO, tc), F32)
        pg_ref[F_HALO:F_HALO + S, :] = ug_ref[...]
        pv_ref[F_HALO:F_HALO + S, :] = uv_ref[...]
        dwg_ref[...] = jnp.zeros((taps, tc), F32)
        dwv_ref[...] = jnp.zeros((taps, tc), F32)

        def grads(c, carry):
            sg, sv = carry
            r = pl.ds(pl.multiple_of(c * rows, rows), rows)
            gc = _conv_chunk(pg_ref, c, rows, F_HALO, wg_ref, taps) + bg_ref[...]
            vc = _conv_chunk(pv_ref, c, rows, F_HALO, wv_ref, taps) + bv_ref[...]
            s = _sig(gc)
            da = da_ref[r, :]
            dgc = da * vc * (s * (1.0 + gc * (1.0 - s)))
            dvc = da * (gc * s)
            qg_ref[r, :] = dgc
            qv_ref[r, :] = dvc
            _conv_wgrad_chunk(pg_ref, c, rows, F_HALO, dgc, dwg_ref, taps)
            _conv_wgrad_chunk(pv_ref, c, rows, F_HALO, dvc, dwv_ref, taps)
            return sg + jnp.sum(dgc, axis=0, keepdims=True), sv + jnp.sum(dvc, axis=0, keepdims=True)

        z = jnp.zeros((1, tc), F32)
        sg, sv = lax.fori_loop(0, nch, grads, (z, z))
        dbg_ref[...] = sg
        dbv_ref[...] = sv

        def back(c, _):
            r = pl.ds(pl.multiple_of(c * rows, rows), rows)
            dug_ref[r, :] = _conv_t_chunk(qg_ref, c, rows, F_HALO, wg_ref, taps).astype(MXU_DT)
            duv_ref[r, :] = _conv_t_chunk(qv_ref, c, rows, F_HALO, wv_ref, taps).astype(MXU_DT)
            return 0

        lax.fori_loop(0, nch, back, 0)

    b2 = b_dw.reshape(1, 2 * Fw)
    pad = pltpu.VMEM((S + F_HALO, tc), F32)
    dug, duv, dwg, dwv, dbg, dbv = _call(
        body, name=name, grid=(nb,),
        in_specs=[_colblk(S, tc), _colblk(S, tc), _colblk(S, tc), _colblk(taps, tc), _colblk(taps, tc, nb),
                  _colblk(1, tc), _colblk(1, tc, nb)],
        out_specs=[_colblk(S, tc), _colblk(S, tc), _colblk(taps, tc), _colblk(taps, tc), _colblk(1, tc),
                   _colblk(1, tc)],
        out_shape=[_sds((S, Fw), MXU_DT), _sds((S, Fw), MXU_DT), _sds((taps, Fw), F32), _sds((taps, Fw), F32),
                   _sds((1, Fw), F32), _sds((1, Fw), F32)],
        scratch_shapes=[pad, pad, pad, pad])(ug, uv, dact, w_dw, w_dw, b2, b2)
    return dug, duv, jnp.concatenate([dwg, dwv], axis=1), jnp.concatenate([dbg, dbv], axis=1)


def _head_mask(h, width):
    lane = lax.broadcasted_iota(jnp.int32, (1, width), 1)
    return (lane >= h * HEAD_DIM) & (lane < (h + 1) * HEAD_DIM)


def _mem_attn_fwd(p, qblk, mkv, l, *, Mw, name):
    S, ML = p.shape[0], mkv.shape[0]
    tm, nh, scale = _pick(S, (256, 128)), Mw // HEAD_DIM, HEAD_DIM ** -0.5

    def body(q_ref, k_ref, v_ref, o_ref):
        q, kv, vv = q_ref[...], k_ref[...], v_ref[...]
        out = jnp.zeros((tm, Mw), F32)
        for h in range(nh):
            mk = _head_mask(h, Mw)
            s = _dot(jnp.where(mk, q, 0.0).astype(MXU_DT), kv, NT) * scale
            e = jnp.exp(s - jnp.max(s, axis=-1, keepdims=True))
            pr = e / jnp.sum(e, axis=-1, keepdims=True)
            out = out + _dot(pr.astype(MXU_DT), jnp.where(mk, vv, jnp.zeros_like(vv)))
        o_ref[...] = out

    return _call(body, name=name, grid=(S // tm,),
                 in_specs=[_rowblk(tm, Mw, qblk), pl.BlockSpec((ML, Mw), lambda i: (0, 2 * l)),
                           pl.BlockSpec((ML, Mw), lambda i: (0, 2 * l + 1))],
                 out_specs=_rowblk(tm, Mw), out_shape=_sds((S, Mw), F32))(p, mkv, mkv)


def _mem_attn_bwd(p, qblk, mkv, l, dcat, doblk, *, Mw, name):
    S, ML = p.shape[0], mkv.shape[0]
    tm, nh, scale = _pick(S, (256, 128)), Mw // HEAD_DIM, HEAD_DIM ** -0.5

    def body(q_ref, k_ref, v_ref, do_ref, dq_ref, dk_ref, dv_ref):
        i = pl.program_id(0)

        @pl.when(i == 0)
        def _():
            dk_ref[...] = jnp.zeros((ML, Mw), F32)
            dv_ref[...] = jnp.zeros((ML, Mw), F32)

        q, kv, vv, do = q_ref[...], k_ref[...], v_ref[...], do_ref[...]
        dq = jnp.zeros((tm, Mw), F32)
        for h in range(nh):
            mk = _head_mask(h, Mw)
            qh = jnp.where(mk, q, 0.0).astype(MXU_DT)
            s = _dot(qh, kv, NT) * scale
            e = jnp.exp(s - jnp.max(s, axis=-1, keepdims=True))
            pr = e / jnp.sum(e, axis=-1, keepdims=True)
            doh = jnp.where(mk, do, 0.0).astype(MXU_DT)
            dv_ref[...] += _dot_tn(pr, doh)
            dp = _dot(doh, vv, NT)
            ds = pr * (dp - jnp.sum(dp * pr, axis=-1, keepdims=True))
            dq = dq + _dot(ds.astype(MXU_DT), jnp.where(mk, kv, jnp.zeros_like(kv))) * scale
            dk_ref[...] += _dot_tn(ds, qh) * scale
        dq_ref[...] = dq

    return _call(body, name=name, grid=(S // tm,),
                 in_specs=[_rowblk(tm, Mw, qblk), pl.BlockSpec((ML, Mw), lambda i: (0, 2 * l)),
                           pl.BlockSpec((ML, Mw), lambda i: (0, 2 * l + 1)), _rowblk(tm, Mw, doblk)],
                 out_specs=[_rowblk(tm, Mw), _res((ML, Mw)), _res((ML, Mw))],
                 out_shape=[_sds((S, Mw), F32), _sds((ML, Mw), F32), _sds((ML, Mw), F32)])(p, mkv, mkv, dcat)


FOX_GROUP = 2


def _fox_specs(S, dh, tq):
    nb, G = S // tq, FOX_GROUP
    qs = pl.BlockSpec((G, tq, dh), lambda h, i: (h, i, 0))
    ks = pl.BlockSpec((G, S, dh), lambda h, i: (h, 0, 0))
    cqs = pl.BlockSpec((G, tq, 1), lambda h, i: (h, i, 0))
    cks = pl.BlockSpec((G, nb, 1, tq), lambda h, i: (h, 0, 0, 0))
    return qs, ks, cqs, cks


def _fox_logits(qv, kv, cqv, ckv, tq, scale, diag):
    s = _dot(qv, kv, NT) * scale + cqv - ckv
    if not diag:
        return s
    rows = lax.broadcasted_iota(jnp.int32, (tq, tq), 0)
    cols = lax.broadcasted_iota(jnp.int32, (tq, tq), 1)
    return jnp.where(cols <= rows, s, NEG)


def _fox_fwd(q, k, v, cq, ck, *, tq, name):
    H, S, dh = q.shape
    scale, G = dh ** -0.5, FOX_GROUP
    assert H % G == 0
    qs, ks, cqs, cks = _fox_specs(S, dh, tq)

    def body(q_ref, k_ref, v_ref, cq_ref, ck_ref, o_ref, lse_ref):
        i = pl.program_id(1)
        qv, cqv = [q_ref[e] for e in range(G)], [cq_ref[e] for e in range(G)]

        def kblock(j, carry, diag):
            r = pl.ds(pl.multiple_of(j * tq, tq), tq)
            out = []
            for e in range(G):
                m, l, acc = carry[e]
                s = _fox_logits(qv[e], k_ref[e, r, :], cqv[e], ck_ref[e, j], tq, scale, diag)
                m2 = jnp.maximum(m, jnp.max(s, axis=-1, keepdims=True))
                pr = jnp.exp(s - m2)
                al = jnp.exp(m - m2)
                out.append((m2, al * l + jnp.sum(pr, axis=-1, keepdims=True),
                            al * acc + _dot(pr.astype(MXU_DT), v_ref[e, r, :])))
            return tuple(out)

        init = tuple((jnp.full((tq, 1), NEG, F32), jnp.zeros((tq, 1), F32), jnp.zeros((tq, dh), F32))
                     for _ in range(G))
        carry = lax.fori_loop(0, i, lambda j, c: kblock(j, c, False), init)
        for e, (m, l, acc) in enumerate(kblock(i, carry, True)):
            o_ref[e] = acc / l
            lse_ref[e] = m + jnp.log(l)

    return _call(body, name=name, grid=(H // G, S // tq), in_specs=[qs, ks, ks, cqs, cks], out_specs=[qs, cqs],
                 out_shape=[_sds((H, S, dh), F32), _sds((H, S, 1), F32)])(q, k, v, cq, ck)


def _fox_bwd(q, k, v, cq, ck, o, lse, do, *, tq, name):
    H, S, dh = q.shape
    nb, scale, G = S // tq, dh ** -0.5, FOX_GROUP
    qs, ks, cqs, cks = _fox_specs(S, dh, tq)

    def body(q_ref, k_ref, v_ref, cq_ref, ck_ref, o_ref, lse_ref, do_ref, dq_ref, dk_ref, dv_ref, dcq_ref,
             dck_ref):
        i = pl.program_id(1)

        @pl.when(i == 0)
        def _():
            dk_ref[...] = jnp.zeros((G, S, dh), F32)
            dv_ref[...] = jnp.zeros((G, S, dh), F32)
            dck_ref[...] = jnp.zeros((G, nb, 1, tq), F32)

        qv, cqv, lsev = ([ref[e] for e in range(G)] for ref in (q_ref, cq_ref, lse_ref))
        dob = [do_ref[e].astype(MXU_DT) for e in range(G)]
        delta = [jnp.sum(dob[e].astype(F32) * o_ref[e], axis=-1, keepdims=True) for e in range(G)]

        def kblock(j, carry, diag):
            r = pl.ds(pl.multiple_of(j * tq, tq), tq)
            out = []
            for e in range(G):
                dq, rs = carry[e]
                kv, vv = k_ref[e, r, :], v_ref[e, r, :]
                pr = jnp.exp(_fox_logits(qv[e], kv, cqv[e], ck_ref[e, j], tq, scale, diag) - lsev[e])
                ds = pr * (_dot(dob[e], vv, NT) - delta[e])
                dk_ref[e, r, :] += _dot_tn(ds, qv[e]) * scale
                dv_ref[e, r, :] += _dot_tn(pr, dob[e])
                dck_ref[e, j] += -jnp.sum(ds, axis=0, keepdims=True)
                out.append((dq + _dot(ds.astype(MXU_DT), kv), rs + jnp.sum(ds, axis=-1, keepdims=True)))
            return tuple(out)

        init = tuple((jnp.zeros((tq, dh), F32), jnp.zeros((tq, 1), F32)) for _ in range(G))
        carry = lax.fori_loop(0, i, lambda j, c: kblock(j, c, False), init)
        for e, (dq, rs) in enumerate(kblock(i, carry, True)):
            dq_ref[e] = dq * scale
            dcq_ref[e] = rs

    return _call(body, name=name, grid=(H // G, nb), in_specs=[qs, ks, ks, cqs, cks, qs, cqs, qs],
                 out_specs=[qs, ks, ks, cqs, cks],
                 out_shape=[_sds((H, S, dh), F32), _sds((H, S, dh), F32), _sds((H, S, dh), F32),
                            _sds((H, S, 1), F32), _sds((H, nb, 1, tq), F32)])(q, k, v, cq, ck, o, lse, do)


def _tri(n, lower):
    r = lax.broadcasted_iota(jnp.int32, (n, n), 0)
    c = lax.broadcasted_iota(jnp.int32, (n, n), 1)
    return ((c <= r) if lower else (c >= r)).astype(F32)


def _fgate_fwd(fr, bf, *, name):
    S, W = fr.shape
    B = _pick(S, (256, 128))

    def body(f_ref, b_ref, cum_ref):
        L = _tri(B, True)
        carry = jnp.zeros((1, W), F32)
        for blk in range(S // B):
            z = f_ref[blk * B:(blk + 1) * B, :] + b_ref[...]
            ls = jnp.minimum(z, 0.0) - jnp.log(1.0 + jnp.exp(-jnp.abs(z)))
            cum_ref[blk * B:(blk + 1) * B, :] = jnp.dot(L, ls, precision=lax.Precision.HIGHEST,
                                                        preferred_element_type=F32) + carry
            carry = carry + jnp.sum(ls, axis=0, keepdims=True)

    return _call(body, name=name, out_shape=_sds((S, W), F32))(fr, bf)


def _fgate_bwd(fr, bf, dcum, *, name):
    S, W = fr.shape
    B = _pick(S, (256, 128))

    def body(f_ref, b_ref, dc_ref, df_ref, db_ref):
        U = _tri(B, False)
        carry = jnp.zeros((1, W), F32)
        dbs = jnp.zeros((1, W), F32)
        for blk in reversed(range(S // B)):
            dc = dc_ref[blk * B:(blk + 1) * B, :]
            dls = jnp.dot(U, dc, precision=lax.Precision.HIGHEST, preferred_element_type=F32) + carry
            carry = carry + jnp.sum(dc, axis=0, keepdims=True)
            z = f_ref[blk * B:(blk + 1) * B, :] + b_ref[...]
            df = dls * (1.0 / (1.0 + jnp.exp(z)))
            df_ref[blk * B:(blk + 1) * B, :] = df
            dbs = dbs + jnp.sum(df, axis=0, keepdims=True)
        db_ref[...] = dbs

    return _call(body, name=name, out_shape=[_sds((S, W), F32), _sds((1, W), F32)])(fr, bf, dcum)


def _flip(v, bit):
    return 1 - v if bit else v


HBM_SPEC = pl.BlockSpec(memory_space=pltpu.HBM)
SEM_SPEC = pl.BlockSpec(memory_space=pltpu.SEMAPHORE)


def _xchg_copies(src, land, sems, scatter):
    n = len(src)
    send, recv, loc = sems[:7 * n], sems[7 * n:14 * n], sems[14 * n:15 * n]
    x, y, c = lax.axis_index("x"), lax.axis_index("y"), lax.axis_index("c")
    me = 4 * x + 2 * y + c

    def peer(m):
        return _flip(x, m & 4), _flip(y, m & 2), _flip(c, m & 1)

    def copy(i, m):
        px, py, pc = peer(m)
        return pltpu.make_async_remote_copy(
            src_ref=src[i].at[4 * px + 2 * py + pc] if scatter[i] else src[i], dst_ref=land[i].at[me],
            send_sem=send[7 * i + m - 1], recv_sem=recv[7 * i + m - 1], device_id=(px, py, pc), device_id_type=MESH)

    def arrival(i, m):
        px, py, pc = peer(m)
        slot = land[i].at[4 * px + 2 * py + pc]
        return pltpu.make_async_remote_copy(src_ref=slot, dst_ref=slot, send_sem=send[7 * i + m - 1],
                                            recv_sem=recv[7 * i + m - 1], device_id=(px, py, pc), device_id_type=MESH)

    def own(i):
        return pltpu.make_async_copy(src[i].at[me] if scatter[i] else src[i], land[i].at[me], loc[i])

    return copy, arrival, own


def _xchg_start(srcs, scatter, *, name):
    n = len(srcs)
    lands = [_sds((N_DEV,) + s.shape[-2:], s.dtype) for s in srcs]

    ns = 15 * n

    def body(*refs):
        src, land, sems, token = refs[:n], refs[n:2 * n], refs[2 * n:2 * n + ns], refs[-1]
        copy, _, own = _xchg_copies(src, land, sems, scatter)
        for i in range(n):
            own(i).start()
            for m in range(1, N_DEV):
                copy(i, m).start()
        token[...] = jnp.zeros(token.shape, F32)

    thru = [pltpu.HBM(s.shape, s.dtype) for s in srcs] + [pltpu.HBM(s.shape, s.dtype) for s in lands]
    out = pl.pallas_call(
        body, name=name,
        out_shape=(*[pltpu.SemaphoreType.DMA(())] * ns, *thru, _sds((8, 128), F32)),
        in_specs=[HBM_SPEC] * (2 * n),
        out_specs=(*[SEM_SPEC] * ns, *[HBM_SPEC] * (2 * n), pl.BlockSpec(memory_space=pltpu.VMEM)),
        input_output_aliases={i: ns + i for i in range(2 * n)},
        compiler_params=pltpu.CompilerParams(has_side_effects=pltpu.SideEffectType.DATAFLOW_SIDE_EFFECTING),
    )(*[pltpu.with_memory_space_constraint(s, pltpu.HBM) for s in srcs],
      *[pltpu.with_memory_space_constraint(lax.empty(s.shape, s.dtype), pltpu.HBM) for s in lands])
    bufs = list(out[ns:ns + 2 * n])
    return (list(out[:ns]), bufs[:n], bufs[n:]), out[-1]


def _xchg_wait(handle, after, scatter, *, name):
    sems, srcs, lands = handle
    n = len(srcs)
    ns = 15 * n

    def body(*refs):
        src, land = refs[:n], refs[n:2 * n]
        copy, arrival, own = _xchg_copies(src, land, refs[2 * n:2 * n + ns], scatter)
        for i in range(n):
            own(i).wait()
            for m in range(1, N_DEV):
                copy(i, m).wait_send()
                arrival(i, m).wait_recv()

    out = pl.pallas_call(
        body, name=name,
        out_shape=tuple(pltpu.HBM(s.shape, s.dtype) for s in srcs + lands),
        in_specs=[HBM_SPEC] * (2 * n) + [SEM_SPEC] * ns + [ANY],
        out_specs=tuple([HBM_SPEC] * (2 * n)),
        input_output_aliases={i: i for i in range(2 * n)},
        compiler_params=pltpu.CompilerParams(has_side_effects=pltpu.SideEffectType.DATAFLOW_SIDE_EFFECTING),
    )(*srcs, *lands, *sems, after)
    return list(out[n:])


def _reduce_adam_body(r_ref, w_ref, m_ref, v_ref, g_ref, d_ref, m2_ref, v2_ref):
    g = r_ref[0].astype(F32)
    for s in range(1, N_DEV):
        g = g + r_ref[s].astype(F32)
    mm = ADAM_B1 * m_ref[...] + (1.0 - ADAM_B1) * g
    vv = ADAM_B2 * v_ref[...] + (1.0 - ADAM_B2) * (g * g)
    m_hat = mm / (1.0 - ADAM_B1 ** ADAM_STEP)
    v_hat = vv / (1.0 - ADAM_B2 ** ADAM_STEP)
    g_ref[...] = g
    d_ref[...] = -ADAM_LR * (m_hat / (jnp.sqrt(v_hat) + ADAM_EPS) + ADAM_WD * w_ref[...])
    m2_ref[...] = mm
    v2_ref[...] = vv


def _reduce_adam(recv, w, m, v, *, name):
    R, L = w.shape
    tr = _pick(R, (256, 128, 64, 32, 16, 8))

    def body(*refs):
        _reduce_adam_body(*refs)

    blk = _rowblk(tr, L)
    return _call(body, name=name, grid=(R // tr,),
                 in_specs=[pl.BlockSpec((N_DEV, tr, L), lambda i: (0, i, 0)), blk, blk, blk],
                 out_specs=[blk, blk, blk, blk], out_shape=[_sds((R, L), F32)] * 4)(recv, w, m, v)


def _reduce_adam_layer(recv, w, m, v, idx, prev, *, name):
    r, c = w.shape[-2:]
    tr = _pick(r, (256, 128)) if r % 128 == 0 else r
    if prev is None:
        prev = [lax.empty(w.shape, F32) for _ in range(4)]

    def body(r_ref, w_ref, m_ref, v_ref, *rest):
        _reduce_adam_body(r_ref, w_ref, m_ref, v_ref, *rest[-4:])

    blk = pl.BlockSpec((None, tr, c), lambda i: (idx, i, 0))
    return pl.pallas_call(
        body, name=name, grid=(r // tr,),
        in_specs=[pl.BlockSpec((N_DEV, tr, c), lambda i: (0, i, 0)), blk, blk, blk] + [ANY] * 4,
        out_specs=[blk] * 4, out_shape=[_sds(w.shape, F32)] * 4, input_output_aliases={4 + j: j for j in range(4)},
        compiler_params=pltpu.CompilerParams(vmem_limit_bytes=VMEM_LIMIT_V7X, dimension_semantics=("arbitrary",)),
    )(recv, w, m, v, *prev)


class _Pack:
    def __init__(self, shapes, row_mult):
        self.shapes, self.offs, rows = dict(shapes), {}, 0
        for name, shp in shapes:
            size = 1
            for d in shp:
                size *= d
            nr = -(-size // (16 * PACK_LANES)) * 16
            self.offs[name] = (rows, size, nr)
            rows += nr
        self.used = rows
        self.rows = -(-rows // row_mult) * row_mult

    def pack(self, arrays, dtype, lead=()):
        parts = []
        for name, (r0, size, nr) in self.offs.items():
            flat = arrays[name].astype(dtype).reshape(lead + (size,))
            flat = jnp.pad(flat, [(0, 0)] * len(lead) + [(0, nr * PACK_LANES - size)])
            parts.append(flat.reshape(lead + (nr, PACK_LANES)))
        if self.rows > self.used:
            parts.append(jnp.zeros(lead + (self.rows - self.used, PACK_LANES), dtype))
        return jnp.concatenate(parts, axis=len(lead))

    def unpack(self, buf, lead=()):
        out = {}
        for name, (r0, size, nr) in self.offs.items():
            flat = buf[..., r0:r0 + nr, :].reshape(lead + (nr * PACK_LANES,))
            out[name] = flat[..., :size].reshape(lead + tuple(self.shapes[name]))
        return out


def _to_full(g8, ax):
    t = jnp.moveaxis(g8, 0, ax)
    return t.reshape(t.shape[:ax] + (t.shape[ax] * t.shape[ax + 1],) + t.shape[ax + 2:])


def _to_shards(full, ax):
    shp = full.shape
    return jnp.moveaxis(full.reshape(shp[:ax] + (N_DEV, shp[ax] // N_DEV) + shp[ax + 1:]), ax, 0)


def _to_heads(a, H):
    S = a.shape[0]
    return a.reshape(S, H, HEAD_DIM).transpose(1, 0, 2)


def _from_heads(a):
    H, S, dh = a.shape
    return a.transpose(1, 0, 2).reshape(S, H * dh)


def kernel(x, mem, g_mix, w_in_a, b_glu, w_dw_a, b_dw_a, ln_g, ln_b, g_kv, w_kvf, b_f, w_in_b, g_mem, w_mem_kv, w_out, g_ffn, w_up, w_dw_f, b_dw_f, w_down, g_final, loss_target, m_g_mix, m_w_in_a, m_b_glu, m_w_dw_a, m_b_dw_a, m_ln_g, m_ln_b, m_g_kv, m_w_kvf, m_b_f, m_w_in_b, m_g_mem, m_w_mem_kv, m_w_out, m_g_ffn, m_w_up, m_w_dw_f, m_b_dw_f, m_w_down, m_g_final, v_g_mix, v_w_in_a, v_b_glu, v_w_dw_a, v_b_dw_a, v_ln_g, v_ln_b, v_g_kv, v_w_kvf, v_b_f, v_w_in_b, v_g_mem, v_w_mem_kv, v_w_out, v_g_ffn, v_w_up, v_w_dw_f, v_b_dw_f, v_w_down, v_g_final):
    given = dict(locals())
    W = {n: given[n] for n in WEIGHTS}
    x0, mem0, tgt = x[0], mem[0], loss_target[0]
    S, D = x0.shape
    depth, n_a = g_mix.shape[0], w_in_a.shape[0]
    C = w_dw_a.shape[2] * N_DEV
    Mw = D - C
    Fw = w_down.shape[1] * N_DEV
    H = b_f.shape[0]
    assert C == H * HEAD_DIM and (2 * C) % Mw == 0 and C % Mw == 0 and H <= GATE_LANES
    tq = _pick(S, (256, 128))
    nkv = 2 * C + GATE_LANES

    def mix_keys(l):
        keys = [("w_in_a", l) if l < n_a else ("w_in_b", l - n_a), ("w_mem_kv", l), ("w_out", l)]
        return keys + ([("w_kvf", 0)] if l == n_a else [])

    def ffn_keys(l):
        return [("w_up", l), ("w_down", l)]

    def key_ax(key):
        return big_ax[key[0]] - 1

    W3 = {n: (W[n][None] if n == "w_kvf" else W[n]) for n, _ in BIG}
    big_ax = {n: (ax + 1 if n == "w_kvf" else ax) for n, ax in BIG}
    pk_small = _Pack([(n, W[n].shape) for n, _ in SMALL], 8)
    pk_rep = _Pack([(n, W[n].shape) for n in REP] + [("loss", (1,))], 8)

    ws32 = pk_small.pack(W, F32)
    gathers, toks = {}, []
    for l in range(depth):
        for tag, keys in (("mix", mix_keys(l)), ("ffn", ffn_keys(l))):
            srcs = [W3[n][i].astype(COMM_DT) for n, i in keys] + ([ws32] if (l, tag) == (0, "mix") else [])
            gathers[l, tag], t = _xchg_start(srcs, [False] * len(srcs), name=f"w_gather_start_{tag}{l}")
            toks.append(t)
    tok = sum(t[0, 0] for t in toks)
    g_mix, g_mem = g_mix + tok, g_mem + tok
    bf_pad = jnp.pad(b_f, (0, GATE_LANES - H)).reshape(1, GATE_LANES)

    def gathered(l, tag, keys, after):
        n = len(gathers[l, tag][1])
        lands = _xchg_wait(gathers[l, tag], after, [False] * n, name=f"w_gather_wait_{tag}{l}")
        return {k: _to_full(a, key_ax(k)).astype(MXU_DT) for k, a in zip(keys, lands)}, lands[len(keys):]

    mem_n = _rms_fwd(mem0, g_mem, name="mem_norm")
    sv = []
    xs = x0
    for l in range(depth):
        wl, extra = gathered(l, "mix", mix_keys(l), xs)
        if extra:
            gs = pk_small.unpack(extra[0], (N_DEV,))
            small = {n: _to_full(gs[n], ax) for n, ax in SMALL}
        t = dict(x_in=xs, w=wl)
        t["mkv"] = _mm(mem_n, wl["w_mem_kv", l], name=f"mem_kv{l}", out_dtype=MXU_DT)
        t["h"] = _rms_fwd(xs, g_mix[l], name=f"mix_norm{l}")
        if l < n_a:
            t["p"] = _mm(t["h"], wl["w_in_a", l], name=f"in_proj{l}", out_dtype=F32)
            t["v2"] = _glu_conv_fwd(t["p"], small["b_glu"][l], small["w_dw_a"][l], small["b_dw_a"][l],
                                    name=f"glu_conv{l}")
            memo = _mem_attn_fwd(t["p"], 2 * C // Mw, t["mkv"], 0, Mw=Mw, name=f"mem_attn{l}")
            t["cat"] = _ln_silu_cat_fwd(t["v2"], small["ln_g"][l], small["ln_b"][l], memo, name=f"ln_silu{l}")
        else:
            if l == n_a:
                wkvf = jnp.pad(wl["w_kvf", 0], ((0, 0), (0, nkv - w_kvf.shape[1])))
                hk = _rms_fwd(xs, g_kv, name="kv_norm")
                kvf = _mm(hk, wkvf, name="kv_proj", out_dtype=F32)
                k_h = _to_heads(kvf[:, :C], H).astype(MXU_DT)
                v_h = _to_heads(kvf[:, C:2 * C], H).astype(MXU_DT)
                fr = kvf[:, 2 * C:]
                cum = _fgate_fwd(fr, bf_pad, name="fgate")
                cum_t = cum[:, :H].T
                cq, ck = cum_t.reshape(H, S, 1), cum_t.reshape(H, S // tq, 1, tq)
            t["p"] = _mm(t["h"], wl["w_in_b", l - n_a], name=f"in_proj{l}", out_dtype=F32)
            t["q_h"] = _to_heads(t["p"][:, :C], H).astype(MXU_DT)
            t["o_h"], t["lse"] = _fox_fwd(t["q_h"], k_h, v_h, cq, ck, tq=tq, name=f"fox{l}")
            memo = _mem_attn_fwd(t["p"], C // Mw, t["mkv"], 0, Mw=Mw, name=f"mem_attn{l}")
            t["cat"] = jnp.concatenate([_from_heads(t["o_h"]), memo], axis=1).astype(MXU_DT)
        t["x_mid"] = _mm(t["cat"], wl["w_out", l], name=f"out_proj{l}", out_dtype=F32, add=xs)
        wl.update(gathered(l, "ffn", ffn_keys(l), t["x_mid"])[0])
        t["h2"] = _rms_fwd(t["x_mid"], g_ffn[l], name=f"ffn_norm{l}")
        t["ug"] = _mm(t["h2"], wl["w_up", l], name=f"up_gate{l}", out_dtype=F32, cols=(0, Fw))
        t["uv"] = _mm(t["h2"], wl["w_up", l], name=f"up_val{l}", out_dtype=F32, cols=(Fw, Fw))
        t["act"] = _ffn_act_fwd(t["ug"], t["uv"], small["w_dw_f"][l], b_dw_f[l], name=f"ffn_act{l}")
        xs = _mm(t["act"], wl["w_down", l], name=f"down_proj{l}", out_dtype=F32, add=t["x_mid"])
        sv.append(t)
    loss_dev, dx, dg_final = _loss_bwd(xs, g_final, tgt, name="loss_head")

    M1 = {n: given["m_" + n] for n in WEIGHTS}
    V1 = {n: given["v_" + n] for n in WEIGHTS}
    M3 = {n: (M1[n][None] if n == "w_kvf" else M1[n]) for n, _ in BIG}
    V3 = {n: (V1[n][None] if n == "w_kvf" else V1[n]) for n, _ in BIG}
    res, chain, pending = {}, {}, []

    def start_grads(tag, l, keys, gl, extra=(), extra_scatter=()):
        srcs = [_to_shards(gl[k], key_ax(k)) for k in keys] + list(extra)
        scatter = [True] * len(keys) + list(extra_scatter)
        handle, tk = _xchg_start(srcs, scatter, name=f"g_xchg_start_{tag}{l}")
        pending.append((f"{tag}{l}", keys, handle, scatter))
        return tk

    def finish_grads(after):
        tag, keys, handle, scatter = pending.pop(0)
        lands = _xchg_wait(handle, after, scatter, name=f"g_xchg_wait_{tag}")
        for (n, i), recv in zip(keys, lands):
            chain[n] = _reduce_adam_layer(recv, W3[n], M3[n], V3[n], i, chain.get(n), name=f"adam_{n}{i}")
        return lands[len(keys):]

    G = {n: [None] * W[n].shape[0] for n in ("g_mix", "b_glu", "w_dw_a", "b_dw_a", "ln_g", "ln_b", "g_ffn",
                                              "w_dw_f", "b_dw_f")}
    dk_sum = dv_sum = dck_sum = dmem_n = None
    started = None
    for l in reversed(range(depth)):
        t = sv[l]
        wl, gl = t["w"], {}
        dact = _mm(dx, wl["w_down", l], name=f"d_act{l}", out_dtype=F32, nt=True, after=started)
        gl["w_down", l] = _mm_tn(t["act"], dx, name=f"dw_down{l}", out_dtype=COMM_DT)
        dug, duv, G["w_dw_f"][l], db = _ffn_act_bwd(t["ug"], t["uv"], dact, small["w_dw_f"][l], b_dw_f[l],
                                                    name=f"d_ffn_act{l}")
        G["b_dw_f"][l] = db[0]
        dh2 = _mm_nt2(dug, duv, wl["w_up", l], name=f"d_up{l}")
        gl["w_up", l] = jnp.concatenate([_mm_tn(t["h2"], dug, name=f"dw_up_gate{l}", out_dtype=COMM_DT),
                                         _mm_tn(t["h2"], duv, name=f"dw_up_val{l}", out_dtype=COMM_DT)], axis=1)
        dx, dg = _rms_bwd(t["x_mid"], g_ffn[l], dh2, dx, name=f"d_ffn_norm{l}")
        G["g_ffn"][l] = dg[0]
        started = start_grads("ffn", l, ffn_keys(l), gl)
        if len(pending) > 2:
            finish_grads(dx)

        dcat = _mm(dx, wl["w_out", l], name=f"d_cat{l}", out_dtype=F32, nt=True, after=started)
        gl["w_out", l] = _mm_tn(t["cat"], dx, name=f"dw_out{l}", out_dtype=COMM_DT)
        if l >= n_a:
            do_h = _to_heads(dcat[:, :C], H)
            dq_h, dk_h, dv_h, dcq, dck = _fox_bwd(t["q_h"], k_h, v_h, cq, ck, t["o_h"], t["lse"], do_h, tq=tq,
                                                  name=f"d_fox{l}")
            dck = dck.reshape(H, S) + dcq.reshape(H, S)
            dk_sum = dk_h if dk_sum is None else dk_sum + dk_h
            dv_sum = dv_h if dv_sum is None else dv_sum + dv_h
            dck_sum = dck if dck_sum is None else dck_sum + dck
            dqm, dmk, dmv = _mem_attn_bwd(t["p"], C // Mw, t["mkv"], 0, dcat, C // Mw, Mw=Mw,
                                          name=f"d_mem_attn{l}")
            dp = jnp.concatenate([_from_heads(dq_h), dqm], axis=1).astype(MXU_DT)
            key = ("w_in_b", l - n_a)
        else:
            dv2, dlg, dlb = _ln_silu_bwd(t["v2"], small["ln_g"][l], small["ln_b"][l], dcat, name=f"d_ln_silu{l}")
            G["ln_g"][l], G["ln_b"][l] = dlg[0], dlb[0]
            da, dgt, G["w_dw_a"][l], dbd, dba, dbg = _glu_conv_bwd(t["p"], small["b_glu"][l], small["w_dw_a"][l],
                                                                   dv2, name=f"d_glu_conv{l}")
            G["b_dw_a"][l] = dbd[0]
            G["b_glu"][l] = jnp.concatenate([dba[0], dbg[0]])
            dqm, dmk, dmv = _mem_attn_bwd(t["p"], 2 * C // Mw, t["mkv"], 0, dcat, C // Mw, Mw=Mw,
                                          name=f"d_mem_attn{l}")
            dp = jnp.concatenate([da, dgt, dqm], axis=1).astype(MXU_DT)
            key = ("w_in_a", l)
        dmkv = jnp.concatenate([dmk, dmv], axis=1).astype(MXU_DT)
        gl["w_mem_kv", l] = _mm_tn(mem_n, dmkv, name=f"dw_mem_kv{l}", out_dtype=COMM_DT)
        dmem_n = _mm(dmkv, wl["w_mem_kv", l], name=f"d_mem_kv{l}", out_dtype=F32, nt=True, add=dmem_n)
        dh = _mm(dp, wl[key], name=f"d_in_proj{l}", out_dtype=F32, nt=True)
        gl[key] = _mm_tn(t["h"], dp, name=f"dw_in_proj{l}", out_dtype=COMM_DT)
        dx, dg = _rms_bwd(t["x_in"], g_mix[l], dh, dx, name=f"d_mix_norm{l}")
        G["g_mix"][l] = dg[0]
        if l == n_a:
            dcum = jnp.pad(dck_sum.T, ((0, 0), (0, GATE_LANES - H)))
            df, dbf = _fgate_bwd(fr, bf_pad, dcum, name="d_fgate")
            dkvf = jnp.concatenate([_from_heads(dk_sum), _from_heads(dv_sum), df], axis=1).astype(MXU_DT)
            dhk = _mm(dkvf, wkvf, name="d_kv_proj", out_dtype=F32, nt=True)
            gl["w_kvf", 0] = _mm_tn(hk, dkvf, name="dw_kv_proj", out_dtype=COMM_DT)[:, :w_kvf.shape[1]]
            dx, dg_kv = _rms_bwd(t["x_in"], g_kv, dhk, dx, name="d_kv_norm")
        if l > 0:
            started = start_grads("mix", l, mix_keys(l), gl)
            if len(pending) > 2:
                finish_grads(dx)

    _, dg_mem = _rms_bwd(mem0, g_mem, dmem_n, None, name="d_mem_norm")
    grads = {n: jnp.stack(v) for n, v in G.items()}
    grads.update(g_kv=dg_kv[0], b_f=dbf[0, :H], g_mem=dg_mem[0], g_final=dg_final[0], loss=loss_dev.reshape(1))
    gs8 = pk_small.pack({n: _to_shards(grads[n], ax) for n, ax in SMALL}, F32, (N_DEV,))
    start_grads("mix", 0, mix_keys(0), gl, [gs8, pk_rep.pack(grads, F32)], [True, False])
    while pending:
        extra = finish_grads(dx)
    no_state = dict(loss=jnp.zeros((1,), F32))
    for pk, recv, tag in ((pk_small, extra[0], "small"), (pk_rep, extra[1], "rep")):
        w32 = ws32 if tag == "small" else pk.pack({**W, **no_state}, F32)
        outs = _reduce_adam(recv, w32, pk.pack({**M1, **no_state}, F32), pk.pack({**V1, **no_state}, F32),
                            name=f"adam_{tag}")
        for kind, buf in zip(("grad", "delta", "new_m", "new_v"), outs):
            for n, a in pk.unpack(buf).items():
                res[kind, n] = a
    for n, outs in chain.items():
        for kind, a in zip(("grad", "delta", "new_m", "new_v"), outs):
            res[kind, n] = a.reshape(W[n].shape)

    loss = res["grad", "loss"][0]
    return (loss, dx[None], *[res[kind, n] for kind in ("grad", "delta", "new_m", "new_v") for n in WEIGHTS])
```

```python
import jax
import jax.numpy as jnp
from jax import lax
from jax.experimental import pallas as pl
from jax.experimental.pallas import tpu as pltpu

F32 = jnp.float32
MXU_DT = jnp.bfloat16
COMM_DT = jnp.bfloat16

N_DEV = 8
HEAD_DIM = 64
RMS_EPS = 1e-6
LN_EPS = 1e-5
ADAM_LR = 0.001
ADAM_B1 = 0.9
ADAM_B2 = 0.999
ADAM_EPS = 1e-08
ADAM_WD = 0.01
ADAM_STEP = 10

PACK_LANES = 1024
GATE_LANES = 128
VMEM_LIMIT_V7X = 56 << 20
NEG = -1e30
MESH = pl.DeviceIdType.MESH
ANY = pl.BlockSpec(memory_space=pl.ANY)
NT = (((1,), (1,)), ((), ()))
NN = (((1,), (0,)), ((), ()))

BIG = (("w_in_a", 2), ("w_kvf", 0), ("w_in_b", 1), ("w_mem_kv", 1), ("w_out", 1), ("w_up", 2), ("w_down", 1))
SMALL = (("b_glu", 1), ("w_dw_a", 2), ("b_dw_a", 1), ("ln_g", 1), ("ln_b", 1), ("w_dw_f", 2))
REP = ("g_mix", "g_kv", "b_f", "g_mem", "g_ffn", "b_dw_f", "g_final")
WEIGHTS = ("g_mix", "w_in_a", "b_glu", "w_dw_a", "b_dw_a", "ln_g", "ln_b", "g_kv", "w_kvf", "b_f", "w_in_b",
           "g_mem", "w_mem_kv", "w_out", "g_ffn", "w_up", "w_dw_f", "b_dw_f", "w_down", "g_final")


def _sds(shape, dtype):
    return jax.ShapeDtypeStruct(tuple(shape), dtype)


def _call(body, *, name, out_shape, grid=(), in_specs=None, out_specs=None, scratch_shapes=()):
    params = dict(vmem_limit_bytes=VMEM_LIMIT_V7X)
    if grid:
        params["dimension_semantics"] = ("arbitrary",) * len(grid)
    kw = {}
    if in_specs is not None:
        kw["in_specs"] = in_specs
    if out_specs is not None:
        kw["out_specs"] = out_specs
    return pl.pallas_call(body, name=name, grid=grid, out_shape=out_shape, scratch_shapes=list(scratch_shapes),
                          compiler_params=pltpu.CompilerParams(**params), **kw)


def _res(shape):
    nd = len(shape)
    return pl.BlockSpec(tuple(shape), lambda *_: (0,) * nd)


def _colblk(rows, tc, off=0):
    return pl.BlockSpec((rows, tc), lambda j: (0, j + off))


def _rowblk(tm, cols, off=0):
    return pl.BlockSpec((tm, cols), lambda i: (i, off))


def _pick(n, opts=(512, 256, 128)):
    for t in opts:
        if n % t == 0:
            return t
    return n


def _sig(z):
    return 1.0 / (1.0 + jnp.exp(-z))


def _dot(a, b, dims=NN):
    return lax.dot_general(a, b, dims, preferred_element_type=F32)


def _dot_tn(a, b):
    return _dot(a.T.astype(b.dtype), b)


def _mm_tn(a, b, *, name, out_dtype):
    S, K = a.shape
    N = b.shape[1]
    tk, rc = _pick(K, (256, 128)), _pick(S)
    cast = b.dtype != MXU_DT

    def body(a_ref, b_ref, o_ref, acc_ref, *bb):
        if cast:
            @pl.when(pl.program_id(0) == 0)
            def _():
                for r0 in range(0, S, rc):
                    bb[0][r0:r0 + rc, :] = b_ref[r0:r0 + rc, :].astype(MXU_DT)
            b_ref = bb[0]
        for n, r0 in enumerate(range(0, S, rc)):
            part = _dot(a_ref[r0:r0 + rc, :].astype(F32).T.astype(MXU_DT), b_ref[r0:r0 + rc, :])
            if n == 0:
                acc_ref[...] = part
            else:
                acc_ref[...] += part
        o_ref[...] = acc_ref[...].astype(out_dtype)

    return _call(body, name=name, grid=(K // tk,), in_specs=[_colblk(S, tk), _res((S, N))],
                 out_specs=pl.BlockSpec((tk, N), lambda j: (j, 0)), out_shape=_sds((K, N), out_dtype),
                 scratch_shapes=[pltpu.VMEM((tk, N), F32)] + ([pltpu.VMEM((S, N), MXU_DT)] if cast else []))(a, b)


def _mm_nt2(a1, a2, w, *, name):
    M, Fw = a1.shape
    N = w.shape[0]
    tm, nc = _pick(M), _pick(N)

    def body(a1_ref, a2_ref, w_ref, o_ref):
        v1, v2 = a1_ref[...], a2_ref[...]
        for n0 in range(0, N, nc):
            o_ref[:, n0:n0 + nc] = (_dot(v1, w_ref[n0:n0 + nc, 0:Fw], NT) + _dot(v2, w_ref[n0:n0 + nc, Fw:2 * Fw], NT))

    return _call(body, name=name, grid=(M // tm,), in_specs=[_rowblk(tm, Fw), _rowblk(tm, Fw), _res(w.shape)],
                 out_specs=_rowblk(tm, N), out_shape=_sds((M, N), F32))(a1, a2, w)


def _mm(a, w, *, name, out_dtype, nt=False, add=None, cols=None, after=None):
    M, K = a.shape
    N = w.shape[0] if nt else w.shape[1]
    assert (w.shape[1] if nt else w.shape[0]) == K
    dims = NT if nt else NN
    has_add, has_after = add is not None, after is not None
    if cols is not None:
        assert not nt and not has_add and not has_after
        c0, N = cols
        tn, rc = _pick(N), _pick(M)
        assert c0 % tn == 0

        def body(a_ref, w_ref, o_ref):
            wv = w_ref[...]
            for r0 in range(0, M, rc):
                o_ref[r0:r0 + rc, :] = _dot(a_ref[r0:r0 + rc, :], wv).astype(out_dtype)

        return _call(body, name=name, grid=(N // tn,), in_specs=[_res((M, K)), _colblk(K, tn, c0 // tn)],
                     out_specs=_colblk(M, tn), out_shape=_sds((M, N), out_dtype))(a, w)
    tn, rc = _pick(N), _pick(M)
    cast = a.dtype != MXU_DT

    def body(*refs):
        a_ref, w_ref, o_ref = refs[0], refs[1], refs[2 + has_add + has_after]
        if cast:
            @pl.when(pl.program_id(0) == 0)
            def _():
                for r0 in range(0, M, rc):
                    refs[-1][r0:r0 + rc, :] = a_ref[r0:r0 + rc, :].astype(MXU_DT)
            a_ref = refs[-1]
        wv = w_ref[...]
        for r0 in range(0, M, rc):
            acc = _dot(a_ref[r0:r0 + rc, :], wv, dims)
            if has_add:
                acc = acc + refs[2][r0:r0 + rc, :]
            o_ref[r0:r0 + rc, :] = acc.astype(out_dtype)

    w_spec = pl.BlockSpec((tn, K), lambda j: (j, 0)) if nt else pl.BlockSpec((K, tn), lambda j: (0, j))
    in_specs = ([_res((M, K)), w_spec] + ([_colblk(M, tn)] if has_add else [])
                + ([_res(after.shape)] if has_after else []))
    args = (a, w) + ((add,) if has_add else ()) + ((after,) if has_after else ())
    return _call(body, name=name, grid=(N // tn,), in_specs=in_specs, out_specs=_colblk(M, tn),
                 out_shape=_sds((M, N), out_dtype),
                 scratch_shapes=[pltpu.VMEM((M, K), MXU_DT)] if cast else [])(*args)


def _rms_fwd(x, g, *, name):
    M, D = x.shape
    tm = _pick(M, (256, 128))

    def body(x_ref, g_ref, h_ref):
        xf = x_ref[...]
        r = lax.rsqrt(jnp.mean(xf * xf, axis=-1, keepdims=True) + RMS_EPS)
        h_ref[...] = ((xf * r) * g_ref[...]).astype(MXU_DT)

    return _call(body, name=name, grid=(M // tm,), in_specs=[_rowblk(tm, D), _res((1, D))],
                 out_specs=_rowblk(tm, D), out_shape=_sds((M, D), MXU_DT))(x, g.reshape(1, D))


def _rms_bwd(x, g, dh, dx_in, *, name):
    M, D = x.shape
    tm = _pick(M, (256, 128))
    with_dx = dx_in is not None

    def body(*refs):
        if with_dx:
            x_ref, g_ref, dh_ref, dxin_ref, dx_ref, dg_ref = refs
        else:
            x_ref, g_ref, dh_ref, dg_ref = refs
        i = pl.program_id(0)
        xf = x_ref[...]
        r = lax.rsqrt(jnp.mean(xf * xf, axis=-1, keepdims=True) + RMS_EPS)
        y = xf * r
        dh_v = dh_ref[...]
        if with_dx:
            dy = dh_v * g_ref[...]
            dx_ref[...] = dxin_ref[...] + r * (dy - y * jnp.mean(dy * y, axis=-1, keepdims=True))
        part = jnp.sum(dh_v * y, axis=0, keepdims=True)

        @pl.when(i == 0)
        def _():
            dg_ref[...] = part

        @pl.when(i > 0)
        def _():
            dg_ref[...] += part

    ins = [x, g.reshape(1, D), dh] + ([dx_in] if with_dx else [])
    in_specs = [_rowblk(tm, D), _res((1, D)), _rowblk(tm, D)] + ([_rowblk(tm, D)] if with_dx else [])
    if with_dx:
        out_specs, out_shape = [_rowblk(tm, D), _res((1, D))], [_sds((M, D), F32), _sds((1, D), F32)]
    else:
        out_specs, out_shape = [_res((1, D))], [_sds((1, D), F32)]
    out = _call(body, name=name, grid=(M // tm,), in_specs=in_specs, out_specs=out_specs, out_shape=out_shape)(*ins)
    return out if with_dx else (None, out[0])


def _loss_bwd(x, g, t, *, name):
    M, D = x.shape
    tm = _pick(M, (256, 128))

    def body(x_ref, g_ref, t_ref, dx_ref, dg_ref, ls_ref):
        i = pl.program_id(0)
        xf = x_ref[...]
        r = lax.rsqrt(jnp.mean(xf * xf, axis=-1, keepdims=True) + RMS_EPS)
        xr = xf * r
        e = xr * g_ref[...] - t_ref[...]
        dout = e * (1.0 / D)
        dy = dout * g_ref[...]
        dx_ref[...] = r * (dy - xr * jnp.mean(dy * xr, axis=-1, keepdims=True))
        part = jnp.sum(dout * xr, axis=0, keepdims=True)
        lpart = jnp.zeros(ls_ref.shape, F32) + (0.5 / D) * jnp.sum(e * e, keepdims=True)

        @pl.when(i == 0)
        def _():
            dg_ref[...] = part
            ls_ref[...] = lpart

        @pl.when(i > 0)
        def _():
            dg_ref[...] += part
            ls_ref[...] += lpart

    dx, dg, ls = _call(body, name=name, grid=(M // tm,),
                       in_specs=[_rowblk(tm, D), _res((1, D)), _rowblk(tm, D)],
                       out_specs=[_rowblk(tm, D), _res((1, D)), _res((8, 128))],
                       out_shape=[_sds((M, D), F32), _sds((1, D), F32), _sds((8, 128), F32)])(x, g.reshape(1, D), t)
    return ls[0, 0], dx, dg


def _shift_rows(ext, off, rows):
    if off % 8 == 0:
        return ext[off:off + rows, :]
    return pltpu.roll(ext, ext.shape[0] - off, 0)[0:rows, :]


def _ext(pad_ref, c, rows, halo):
    return pad_ref[pl.ds(pl.multiple_of(c * rows, rows), rows + halo), :]


def _conv_chunk(pad_ref, c, rows, halo, w_ref, taps):
    ext = _ext(pad_ref, c, rows, halo)
    acc = None
    for k in range(taps):
        term = w_ref[k:k + 1, :] * _shift_rows(ext, halo - (taps - 1) + k, rows)
        acc = term if acc is None else acc + term
    return acc


def _conv_t_chunk(pad_ref, c, rows, halo, w_ref, taps):
    ext = _ext(pad_ref, c, rows, halo)
    acc = None
    for k in range(taps):
        term = w_ref[k:k + 1, :] * _shift_rows(ext, taps - 1 - k, rows)
        acc = term if acc is None else acc + term
    return acc


def _conv_wgrad_chunk(pad_ref, c, rows, halo, dy, dw_ref, taps):
    ext = _ext(pad_ref, c, rows, halo)
    for k in range(taps):
        dw_ref[k:k + 1, :] += jnp.sum(dy * _shift_rows(ext, halo - (taps - 1) + k, rows), axis=0, keepdims=True)


A_HALO, F_HALO = 32, 8


def _glu_conv_fwd(p, b_glu, w_dw, b_dw, *, name):
    S = p.shape[0]
    taps, C = w_dw.shape
    tc, rows = 128, _pick(S, (256, 128))
    nb, nch = C // tc, S // rows

    def body(a_ref, g_ref, ba_ref, bg_ref, w_ref, bd_ref, o_ref, pad_ref):
        pad_ref[0:A_HALO, :] = jnp.zeros((A_HALO, tc), F32)

        def fill(c, _):
            r = pl.ds(pl.multiple_of(c * rows, rows), rows)
            v1 = (a_ref[r, :] + ba_ref[...]) * _sig(g_ref[r, :] + bg_ref[...])
            pad_ref[pl.ds(pl.multiple_of(A_HALO + c * rows, 8), rows), :] = v1
            return 0

        lax.fori_loop(0, nch, fill, 0)

        def conv(c, _):
            o_ref[pl.ds(pl.multiple_of(c * rows, rows), rows), :] = (
                _conv_chunk(pad_ref, c, rows, A_HALO, w_ref, taps) + bd_ref[...])
            return 0

        lax.fori_loop(0, nch, conv, 0)

    b2 = b_glu.reshape(1, 2 * C)
    return _call(body, name=name, grid=(nb,),
                 in_specs=[_colblk(S, tc), _colblk(S, tc, nb), _colblk(1, tc), _colblk(1, tc, nb),
                           _colblk(taps, tc), _colblk(1, tc)],
                 out_specs=_colblk(S, tc), out_shape=_sds((S, C), F32),
                 scratch_shapes=[pltpu.VMEM((S + A_HALO, tc), F32)])(p, p, b2, b2, w_dw, b_dw.reshape(1, C))


def _glu_conv_bwd(p, b_glu, w_dw, dv2, *, name):
    S = p.shape[0]
    taps, C = w_dw.shape
    tc, rows = 128, _pick(S, (256, 128))
    nb, nch = C // tc, S // rows

    def body(a_ref, g_ref, ba_ref, bg_ref, w_ref, dy_ref, da_ref, dgt_ref, dw_ref, dbd_ref, dba_ref, dbg_ref,
             padx_ref, pady_ref):
        padx_ref[0:A_HALO, :] = jnp.zeros((A_HALO, tc), F32)
        pady_ref[S:S + A_HALO, :] = jnp.zeros((A_HALO, tc), F32)
        dw_ref[...] = jnp.zeros((taps, tc), F32)

        def fill(c, _):
            r = pl.ds(pl.multiple_of(c * rows, rows), rows)
            v1 = (a_ref[r, :] + ba_ref[...]) * _sig(g_ref[r, :] + bg_ref[...])
            padx_ref[pl.ds(pl.multiple_of(A_HALO + c * rows, 8), rows), :] = v1
            pady_ref[r, :] = dy_ref[r, :]
            return 0

        lax.fori_loop(0, nch, fill, 0)

        def back(c, carry):
            sd, sa, sg = carry
            r = pl.ds(pl.multiple_of(c * rows, rows), rows)
            dy = dy_ref[r, :]
            _conv_wgrad_chunk(padx_ref, c, rows, A_HALO, dy, dw_ref, taps)
            dv1 = _conv_t_chunk(pady_ref, c, rows, A_HALO, w_ref, taps)
            a = a_ref[r, :] + ba_ref[...]
            s = _sig(g_ref[r, :] + bg_ref[...])
            da = dv1 * s
            dgt = dv1 * a * s * (1.0 - s)
            da_ref[r, :] = da
            dgt_ref[r, :] = dgt
            return (sd + jnp.sum(dy, axis=0, keepdims=True), sa + jnp.sum(da, axis=0, keepdims=True),
                    sg + jnp.sum(dgt, axis=0, keepdims=True))

        z = jnp.zeros((1, tc), F32)
        sd, sa, sg = lax.fori_loop(0, nch, back, (z, z, z))
        dbd_ref[...] = sd
        dba_ref[...] = sa
        dbg_ref[...] = sg

    b2 = b_glu.reshape(1, 2 * C)
    return _call(body, name=name, grid=(nb,),
                 in_specs=[_colblk(S, tc), _colblk(S, tc, nb), _colblk(1, tc), _colblk(1, tc, nb),
                           _colblk(taps, tc), _colblk(S, tc)],
                 out_specs=[_colblk(S, tc), _colblk(S, tc), _colblk(taps, tc), _colblk(1, tc), _colblk(1, tc),
                            _colblk(1, tc)],
                 out_shape=[_sds((S, C), F32), _sds((S, C), F32), _sds((taps, C), F32), _sds((1, C), F32),
                            _sds((1, C), F32), _sds((1, C), F32)],
                 scratch_shapes=[pltpu.VMEM((S + A_HALO, tc), F32), pltpu.VMEM((S + A_HALO, tc), F32)])(
                     p, p, b2, b2, w_dw, dv2)


def _ln_silu_cat_fwd(v2, ln_g, ln_b, memo, *, name):
    S, C = v2.shape
    Mw = memo.shape[1]
    tm = _pick(S, (256, 128))

    def body(v_ref, g_ref, b_ref, m_ref, o_ref):
        v = v_ref[...]
        mu = jnp.mean(v, axis=-1, keepdims=True)
        d = v - mu
        y = d * lax.rsqrt(jnp.mean(d * d, axis=-1, keepdims=True) + LN_EPS) * g_ref[...] + b_ref[...]
        o_ref[:, 0:C] = (y * _sig(y)).astype(MXU_DT)
        o_ref[:, C:C + Mw] = m_ref[...].astype(MXU_DT)

    return _call(body, name=name, grid=(S // tm,),
                 in_specs=[_rowblk(tm, C), _res((1, C)), _res((1, C)), _rowblk(tm, Mw)],
                 out_specs=_rowblk(tm, C + Mw), out_shape=_sds((S, C + Mw), MXU_DT))(
                     v2, ln_g.reshape(1, C), ln_b.reshape(1, C), memo)


def _ln_silu_bwd(v2, ln_g, ln_b, dcat, *, name):
    S, C = v2.shape
    tm = _pick(S, (256, 128))

    def body(v_ref, g_ref, b_ref, dm_ref, dv_ref, dg_ref, db_ref):
        i = pl.program_id(0)
        v = v_ref[...]
        mu = jnp.mean(v, axis=-1, keepdims=True)
        d = v - mu
        rstd = lax.rsqrt(jnp.mean(d * d, axis=-1, keepdims=True) + LN_EPS)
        xh = d * rstd
        y = xh * g_ref[...] + b_ref[...]
        s = _sig(y)
        dyv = dm_ref[...] * (s * (1.0 + y * (1.0 - s)))
        dxh = dyv * g_ref[...]
        dv_ref[...] = rstd * (dxh - jnp.mean(dxh, axis=-1, keepdims=True)
                              - xh * jnp.mean(dxh * xh, axis=-1, keepdims=True))
        pg = jnp.sum(dyv * xh, axis=0, keepdims=True)
        pb = jnp.sum(dyv, axis=0, keepdims=True)

        @pl.when(i == 0)
        def _():
            dg_ref[...] = pg
            db_ref[...] = pb

        @pl.when(i > 0)
        def _():
            dg_ref[...] += pg
            db_ref[...] += pb

    return _call(body, name=name, grid=(S // tm,),
                 in_specs=[_rowblk(tm, C), _res((1, C)), _res((1, C)), _rowblk(tm, C)],
                 out_specs=[_rowblk(tm, C), _res((1, C)), _res((1, C))],
                 out_shape=[_sds((S, C), F32), _sds((1, C), F32), _sds((1, C), F32)])(
                     v2, ln_g.reshape(1, C), ln_b.reshape(1, C), dcat)


def _ffn_act_fwd(ug, uv, w_dw, b_dw, *, name):
    S, Fw = ug.shape
    taps = w_dw.shape[0]
    tc, rows = _pick(Fw, (256, 128)), _pick(S, (256, 128))
    nb, nch = Fw // tc, S // rows

    def body(ug_ref, uv_ref, wg_ref, wv_ref, bg_ref, bv_ref, o_ref, pg_ref, pv_ref):
        pg_ref[0:F_HALO, :] = jnp.zeros((F_HALO, tc), F32)
        pv_ref[0:F_HALO, :] = jnp.zeros((F_HALO, tc), F32)
        pg_ref[F_HALO:F_HALO + S, :] = ug_ref[...]
        pv_ref[F_HALO:F_HALO + S, :] = uv_ref[...]

        def act(c, _):
            gc = _conv_chunk(pg_ref, c, rows, F_HALO, wg_ref, taps) + bg_ref[...]
            vc = _conv_chunk(pv_ref, c, rows, F_HALO, wv_ref, taps) + bv_ref[...]
            o_ref[pl.ds(pl.multiple_of(c * rows, rows), rows), :] = (gc * _sig(gc) * vc).astype(MXU_DT)
            return 0

        lax.fori_loop(0, nch, act, 0)

    b2 = b_dw.reshape(1, 2 * Fw)
    return _call(body, name=name, grid=(nb,),
                 in_specs=[_colblk(S, tc), _colblk(S, tc), _colblk(taps, tc), _colblk(taps, tc, nb),
                           _colblk(1, tc), _colblk(1, tc, nb)],
                 out_specs=_colblk(S, tc), out_shape=_sds((S, Fw), MXU_DT),
                 scratch_shapes=[pltpu.VMEM((S + F_HALO, tc), F32), pltpu.VMEM((S + F_HALO, tc), F32)])(
                     ug, uv, w_dw, w_dw, b2, b2)


def _ffn_act_bwd(ug, uv, dact, w_dw, b_dw, *, name):
    S, Fw = ug.shape
    taps = w_dw.shape[0]
    tc, rows = _pick(Fw, (256, 128)), _pick(S, (256, 128))
    nb, nch = Fw // tc, S // rows

    def body(ug_ref, uv_ref, da_ref, wg_ref, wv_ref, bg_ref, bv_ref, dug_ref, duv_ref, dwg_ref, dwv_ref,
             dbg_ref, dbv_ref, pg_ref, pv_ref, qg_ref, qv_ref):
        pg_ref[0:F_HALO, :] = jnp.zeros((F_HALO, tc), F32)
        pv_ref[0:F_HALO, :] = jnp.zeros((F_HALO, tc), F32)
        qg_ref[S:S + F_HALO, :] = jnp.zeros((F_HALO, tc), F32)
        qv_ref[S:S + F_HALO, :] = jnp.zeros((F_HALO, tc), F32)
        pg_ref[F_HALO:F_HALO + S, :] = ug_ref[...]
        pv_ref[F_HALO:F_HALO + S, :] = uv_ref[...]
        dwg_ref[...] = jnp.zeros((taps, tc), F32)
        dwv_ref[...] = jnp.zeros((taps, tc), F32)

        def grads(c, carry):
            sg, sv = carry
            r = pl.ds(pl.multiple_of(c * rows, rows), rows)
            gc = _conv_chunk(pg_ref, c, rows, F_HALO, wg_ref, taps) + bg_ref[...]
            vc = _conv_chunk(pv_ref, c, rows, F_HALO, wv_ref, taps) + bv_ref[...]
            s = _sig(gc)
            da = da_ref[r, :]
            dgc = da * vc * (s * (1.0 + gc * (1.0 - s)))
            dvc = da * (gc * s)
            qg_ref[r, :] = dgc
            qv_ref[r, :] = dvc
            _conv_wgrad_chunk(pg_ref, c, rows, F_HALO, dgc, dwg_ref, taps)
            _conv_wgrad_chunk(pv_ref, c, rows, F_HALO, dvc, dwv_ref, taps)
            return sg + jnp.sum(dgc, axis=0, keepdims=True), sv + jnp.sum(dvc, axis=0, keepdims=True)

        z = jnp.zeros((1, tc), F32)
        sg, sv = lax.fori_loop(0, nch, grads, (z, z))
        dbg_ref[...] = sg
        dbv_ref[...] = sv

        def back(c, _):
            r = pl.ds(pl.multiple_of(c * rows, rows), rows)
            dug_ref[r, :] = _conv_t_chunk(qg_ref, c, rows, F_HALO, wg_ref, taps).astype(MXU_DT)
            duv_ref[r, :] = _conv_t_chunk(qv_ref, c, rows, F_HALO, wv_ref, taps).astype(MXU_DT)
            return 0

        lax.fori_loop(0, nch, back, 0)

    b2 = b_dw.reshape(1, 2 * Fw)
    pad = pltpu.VMEM((S + F_HALO, tc), F32)
    dug, duv, dwg, dwv, dbg, dbv = _call(
        body, name=name, grid=(nb,),
        in_specs=[_colblk(S, tc), _colblk(S, tc), _colblk(S, tc), _colblk(taps, tc), _colblk(taps, tc, nb),
                  _colblk(1, tc), _colblk(1, tc, nb)],
        out_specs=[_colblk(S, tc), _colblk(S, tc), _colblk(taps, tc), _colblk(taps, tc), _colblk(1, tc),
                   _colblk(1, tc)],
        out_shape=[_sds((S, Fw), MXU_DT), _sds((S, Fw), MXU_DT), _sds((taps, Fw), F32), _sds((taps, Fw), F32),
                   _sds((1, Fw), F32), _sds((1, Fw), F32)],
        scratch_shapes=[pad, pad, pad, pad])(ug, uv, dact, w_dw, w_dw, b2, b2)
    return dug, duv, jnp.concatenate([dwg, dwv], axis=1), jnp.concatenate([dbg, dbv], axis=1)


def _head_mask(h, width):
    lane = lax.broadcasted_iota(jnp.int32, (1, width), 1)
    return (lane >= h * HEAD_DIM) & (lane < (h + 1) * HEAD_DIM)


def _mem_attn_fwd(p, qblk, mkv, l, *, Mw, name):
    S, ML = p.shape[0], mkv.shape[0]
    tm, nh, scale = _pick(S, (256, 128)), Mw // HEAD_DIM, HEAD_DIM ** -0.5

    def body(q_ref, k_ref, v_ref, o_ref):
        q, kv, vv = q_ref[...], k_ref[...], v_ref[...]
        out = jnp.zeros((tm, Mw), F32)
        for h in range(nh):
            mk = _head_mask(h, Mw)
            s = _dot(jnp.where(mk, q, 0.0).astype(MXU_DT), kv, NT) * scale
            e = jnp.exp(s - jnp.max(s, axis=-1, keepdims=True))
            pr = e / jnp.sum(e, axis=-1, keepdims=True)
            out = out + _dot(pr.astype(MXU_DT), jnp.where(mk, vv, jnp.zeros_like(vv)))
        o_ref[...] = out

    return _call(body, name=name, grid=(S // tm,),
                 in_specs=[_rowblk(tm, Mw, qblk), pl.BlockSpec((ML, Mw), lambda i: (0, 2 * l)),
                           pl.BlockSpec((ML, Mw), lambda i: (0, 2 * l + 1))],
                 out_specs=_rowblk(tm, Mw), out_shape=_sds((S, Mw), F32))(p, mkv, mkv)


def _mem_attn_bwd(p, qblk, mkv, l, dcat, doblk, *, Mw, name):
    S, ML = p.shape[0], mkv.shape[0]
    tm, nh, scale = _pick(S, (256, 128)), Mw // HEAD_DIM, HEAD_DIM ** -0.5

    def body(q_ref, k_ref, v_ref, do_ref, dq_ref, dk_ref, dv_ref):
        i = pl.program_id(0)

        @pl.when(i == 0)
        def _():
            dk_ref[...] = jnp.zeros((ML, Mw), F32)
            dv_ref[...] = jnp.zeros((ML, Mw), F32)

        q, kv, vv, do = q_ref[...], k_ref[...], v_ref[...], do_ref[...]
        dq = jnp.zeros((tm, Mw), F32)
        for h in range(nh):
            mk = _head_mask(h, Mw)
            qh = jnp.where(mk, q, 0.0).astype(MXU_DT)
            s = _dot(qh, kv, NT) * scale
            e = jnp.exp(s - jnp.max(s, axis=-1, keepdims=True))
            pr = e / jnp.sum(e, axis=-1, keepdims=True)
            doh = jnp.where(mk, do, 0.0).astype(MXU_DT)
            dv_ref[...] += _dot_tn(pr, doh)
            dp = _dot(doh, vv, NT)
            ds = pr * (dp - jnp.sum(dp * pr, axis=-1, keepdims=True))
            dq = dq + _dot(ds.astype(MXU_DT), jnp.where(mk, kv, jnp.zeros_like(kv))) * scale
            dk_ref[...] += _dot_tn(ds, qh) * scale
        dq_ref[...] = dq

    return _call(body, name=name, grid=(S // tm,),
                 in_specs=[_rowblk(tm, Mw, qblk), pl.BlockSpec((ML, Mw), lambda i: (0, 2 * l)),
                           pl.BlockSpec((ML, Mw), lambda i: (0, 2 * l + 1)), _rowblk(tm, Mw, doblk)],
                 out_specs=[_rowblk(tm, Mw), _res((ML, Mw)), _res((ML, Mw))],
                 out_shape=[_sds((S, Mw), F32), _sds((ML, Mw), F32), _sds((ML, Mw), F32)])(p, mkv, mkv, dcat)


FOX_GROUP = 3


def _fox_specs(S, dh, tq):
    nb, G = S // tq, FOX_GROUP
    qs = pl.BlockSpec((G, tq, dh), lambda h, i: (h, i, 0))
    ks = pl.BlockSpec((G, S, dh), lambda h, i: (h, 0, 0))
    cqs = pl.BlockSpec((G, tq, 1), lambda h, i: (h, i, 0))
    cks = pl.BlockSpec((G, nb, 1, tq), lambda h, i: (h, 0, 0, 0))
    return qs, ks, cqs, cks


def _fox_logits(qv, kv, cqv, ckv, scale, diag):
    s = _dot(qv, kv, NT) * scale + cqv - ckv
    if not diag:
        return s
    rows = lax.broadcasted_iota(jnp.int32, s.shape, 0)
    cols = lax.broadcasted_iota(jnp.int32, s.shape, 1)
    return jnp.where(cols <= rows, s, NEG)


def _fox_fwd(q, k, v, cq, ck, *, tq, name):
    H, S, dh = q.shape
    scale, G = dh ** -0.5, FOX_GROUP
    assert H % G == 0
    qs, ks, cqs, cks = _fox_specs(S, dh, tq)

    def body(q_ref, k_ref, v_ref, cq_ref, ck_ref, o_ref, lse_ref):
        i = pl.program_id(1)
        qv, cqv = [q_ref[e] for e in range(G)], [cq_ref[e] for e in range(G)]

        def kblock(j, carry, diag):
            r = pl.ds(pl.multiple_of(j * tq, tq), tq)
            out = []
            for e in range(G):
                m, l, acc = carry[e]
                s = _fox_logits(qv[e], k_ref[e, r, :], cqv[e], ck_ref[e, j], scale, diag)
                m2 = jnp.maximum(m, jnp.max(s, axis=-1, keepdims=True))
                pr = jnp.exp(s - m2)
                al = jnp.exp(m - m2)
                out.append((m2, al * l + jnp.sum(pr, axis=-1, keepdims=True),
                            al * acc + _dot(pr.astype(MXU_DT), v_ref[e, r, :])))
            return tuple(out)

        init = tuple((jnp.full((tq, 1), NEG, F32), jnp.zeros((tq, 1), F32), jnp.zeros((tq, dh), F32))
                     for _ in range(G))
        carry = lax.fori_loop(0, i, lambda j, c: kblock(j, c, False), init)
        for e, (m, l, acc) in enumerate(kblock(i, carry, True)):
            o_ref[e] = acc / l
            lse_ref[e] = m + jnp.log(l)

    return _call(body, name=name, grid=(H // G, S // tq), in_specs=[qs, ks, ks, cqs, cks], out_specs=[qs, cqs],
                 out_shape=[_sds((H, S, dh), F32), _sds((H, S, 1), F32)])(q, k, v, cq, ck)


def _fox_bwd(q, k, v, cq, ck, o, lse, do, *, tq, name):
    H, S, dh = q.shape
    nb, scale, G = S // tq, dh ** -0.5, FOX_GROUP
    qs, ks, cqs, cks = _fox_specs(S, dh, tq)

    def body(q_ref, k_ref, v_ref, cq_ref, ck_ref, o_ref, lse_ref, do_ref, dq_ref, dk_ref, dv_ref, dcq_ref,
             dck_ref):
        i = pl.program_id(1)

        @pl.when(i == 0)
        def _():
            dk_ref[...] = jnp.zeros((G, S, dh), F32)
            dv_ref[...] = jnp.zeros((G, S, dh), F32)
            dck_ref[...] = jnp.zeros((G, nb, 1, tq), F32)

        qv, cqv, lsev = ([ref[e] for e in range(G)] for ref in (q_ref, cq_ref, lse_ref))
        dob = [do_ref[e].astype(MXU_DT) for e in range(G)]
        delta = [jnp.sum(dob[e].astype(F32) * o_ref[e], axis=-1, keepdims=True) for e in range(G)]

        def kblock(j, carry, diag):
            r = pl.ds(pl.multiple_of(j * tq, tq), tq)
            out = []
            for e in range(G):
                dq, rs = carry[e]
                kv, vv = k_ref[e, r, :], v_ref[e, r, :]
                pr = jnp.exp(_fox_logits(qv[e], kv, cqv[e], ck_ref[e, j], scale, diag) - lsev[e])
                ds = pr * (_dot(dob[e], vv, NT) - delta[e])
                dk_ref[e, r, :] += _dot_tn(ds, qv[e]) * scale
                dv_ref[e, r, :] += _dot_tn(pr, dob[e])
                dck_ref[e, j] += -jnp.sum(ds, axis=0, keepdims=True)
                out.append((dq + _dot(ds.astype(MXU_DT), kv), rs + jnp.sum(ds, axis=-1, keepdims=True)))
            return tuple(out)

        init = tuple((jnp.zeros((tq, dh), F32), jnp.zeros((tq, 1), F32)) for _ in range(G))
        carry = lax.fori_loop(0, i, lambda j, c: kblock(j, c, False), init)
        for e, (dq, rs) in enumerate(kblock(i, carry, True)):
            dq_ref[e] = dq * scale
            dcq_ref[e] = rs

    return _call(body, name=name, grid=(H // G, nb), in_specs=[qs, ks, ks, cqs, cks, qs, cqs, qs],
                 out_specs=[qs, ks, ks, cqs, cks],
                 out_shape=[_sds((H, S, dh), F32), _sds((H, S, dh), F32), _sds((H, S, dh), F32),
                            _sds((H, S, 1), F32), _sds((H, nb, 1, tq), F32)])(q, k, v, cq, ck, o, lse, do)


def _tri(n, lower):
    r = lax.broadcasted_iota(jnp.int32, (n, n), 0)
    c = lax.broadcasted_iota(jnp.int32, (n, n), 1)
    return ((c <= r) if lower else (c >= r)).astype(F32)


def _fgate_fwd(fr, bf, *, name):
    S, W = fr.shape
    B = _pick(S, (256, 128))

    def body(f_ref, b_ref, cum_ref):
        L = _tri(B, True)
        carry = jnp.zeros((1, W), F32)
        for blk in range(S // B):
            z = f_ref[blk * B:(blk + 1) * B, :] + b_ref[...]
            ls = jnp.minimum(z, 0.0) - jnp.log(1.0 + jnp.exp(-jnp.abs(z)))
            cum_ref[blk * B:(blk + 1) * B, :] = jnp.dot(L, ls, precision=lax.Precision.HIGHEST,
                                                        preferred_element_type=F32) + carry
            carry = carry + jnp.sum(ls, axis=0, keepdims=True)

    return _call(body, name=name, out_shape=_sds((S, W), F32))(fr, bf)


def _fgate_bwd(fr, bf, dcum, *, name):
    S, W = fr.shape
    B = _pick(S, (256, 128))

    def body(f_ref, b_ref, dc_ref, df_ref, db_ref):
        U = _tri(B, False)
        carry = jnp.zeros((1, W), F32)
        dbs = jnp.zeros((1, W), F32)
        for blk in reversed(range(S // B)):
            dc = dc_ref[blk * B:(blk + 1) * B, :]
            dls = jnp.dot(U, dc, precision=lax.Precision.HIGHEST, preferred_element_type=F32) + carry
            carry = carry + jnp.sum(dc, axis=0, keepdims=True)
            z = f_ref[blk * B:(blk + 1) * B, :] + b_ref[...]
            df = dls * (1.0 / (1.0 + jnp.exp(z)))
            df_ref[blk * B:(blk + 1) * B, :] = df
            dbs = dbs + jnp.sum(df, axis=0, keepdims=True)
        db_ref[...] = dbs

    return _call(body, name=name, out_shape=[_sds((S, W), F32), _sds((1, W), F32)])(fr, bf, dcum)


def _flip(v, bit):
    return 1 - v if bit else v


HBM_SPEC = pl.BlockSpec(memory_space=pltpu.HBM)
SEM_SPEC = pl.BlockSpec(memory_space=pltpu.SEMAPHORE)


def _xchg_copies(src, land, sems, scatter):
    n = len(src)
    send, recv, loc = sems[:7 * n], sems[7 * n:14 * n], sems[14 * n:15 * n]
    x, y, c = lax.axis_index("x"), lax.axis_index("y"), lax.axis_index("c")
    me = 4 * x + 2 * y + c

    def peer(m):
        return _flip(x, m & 4), _flip(y, m & 2), _flip(c, m & 1)

    def copy(i, m):
        px, py, pc = peer(m)
        return pltpu.make_async_remote_copy(
            src_ref=src[i].at[4 * px + 2 * py + pc] if scatter[i] else src[i], dst_ref=land[i].at[me],
            send_sem=send[7 * i + m - 1], recv_sem=recv[7 * i + m - 1], device_id=(px, py, pc), device_id_type=MESH)

    def arrival(i, m):
        px, py, pc = peer(m)
        slot = land[i].at[4 * px + 2 * py + pc]
        return pltpu.make_async_remote_copy(src_ref=slot, dst_ref=slot, send_sem=send[7 * i + m - 1],
                                            recv_sem=recv[7 * i + m - 1], device_id=(px, py, pc), device_id_type=MESH)

    def own(i):
        return pltpu.make_async_copy(src[i].at[me] if scatter[i] else src[i], land[i].at[me], loc[i])

    return copy, arrival, own


def _xchg_start(srcs, scatter, *, name):
    n = len(srcs)
    lands = [_sds((N_DEV,) + s.shape[-2:], s.dtype) for s in srcs]

    ns = 15 * n

    def body(*refs):
        src, land, sems, token = refs[:n], refs[n:2 * n], refs[2 * n:2 * n + ns], refs[-1]
        copy, _, own = _xchg_copies(src, land, sems, scatter)
        for i in range(n):
            own(i).start()
            for m in range(1, N_DEV):
                copy(i, m).start()
        token[...] = jnp.zeros(token.shape, F32)

    thru = [pltpu.HBM(s.shape, s.dtype) for s in srcs] + [pltpu.HBM(s.shape, s.dtype) for s in lands]
    out = pl.pallas_call(
        body, name=name,
        out_shape=(*[pltpu.SemaphoreType.DMA(())] * ns, *thru, _sds((8, 128), F32)),
        in_specs=[HBM_SPEC] * (2 * n),
        out_specs=(*[SEM_SPEC] * ns, *[HBM_SPEC] * (2 * n), pl.BlockSpec(memory_space=pltpu.VMEM)),
        input_output_aliases={i: ns + i for i in range(2 * n)},
        compiler_params=pltpu.CompilerParams(has_side_effects=pltpu.SideEffectType.DATAFLOW_SIDE_EFFECTING),
    )(*[pltpu.with_memory_space_constraint(s, pltpu.HBM) for s in srcs],
      *[pltpu.with_memory_space_constraint(lax.empty(s.shape, s.dtype), pltpu.HBM) for s in lands])
    bufs = list(out[ns:ns + 2 * n])
    return (list(out[:ns]), bufs[:n], bufs[n:]), out[-1]


def _xchg_wait(handle, after, scatter, *, name):
    sems, srcs, lands = handle
    n = len(srcs)
    ns = 15 * n

    def body(*refs):
        src, land = refs[:n], refs[n:2 * n]
        copy, arrival, own = _xchg_copies(src, land, refs[2 * n:2 * n + ns], scatter)
        for i in range(n):
            own(i).wait()
            for m in range(1, N_DEV):
                copy(i, m).wait_send()
                arrival(i, m).wait_recv()

    out = pl.pallas_call(
        body, name=name,
        out_shape=tuple(pltpu.HBM(s.shape, s.dtype) for s in srcs + lands),
        in_specs=[HBM_SPEC] * (2 * n) + [SEM_SPEC] * ns + [ANY],
        out_specs=tuple([HBM_SPEC] * (2 * n)),
        input_output_aliases={i: i for i in range(2 * n)},
        compiler_params=pltpu.CompilerParams(has_side_effects=pltpu.SideEffectType.DATAFLOW_SIDE_EFFECTING),
    )(*srcs, *lands, *sems, after)
    return list(out[n:])


def _reduce_adam_body(r_ref, w_ref, m_ref, v_ref, g_ref, d_ref, m2_ref, v2_ref):
    g = r_ref[0].astype(F32)
    for s in range(1, N_DEV):
        g = g + r_ref[s].astype(F32)
    mm = ADAM_B1 * m_ref[...] + (1.0 - ADAM_B1) * g
    vv = ADAM_B2 * v_ref[...] + (1.0 - ADAM_B2) * (g * g)
    m_hat = mm / (1.0 - ADAM_B1 ** ADAM_STEP)
    v_hat = vv / (1.0 - ADAM_B2 ** ADAM_STEP)
    g_ref[...] = g
    d_ref[...] = -ADAM_LR * (m_hat / (jnp.sqrt(v_hat) + ADAM_EPS) + ADAM_WD * w_ref[...])
    m2_ref[...] = mm
    v2_ref[...] = vv


def _reduce_adam(recv, w, m, v, *, name):
    R, L = w.shape
    tr = _pick(R, (256, 128, 64, 32, 16, 8))

    def body(*refs):
        _reduce_adam_body(*refs)

    blk = _rowblk(tr, L)
    return _call(body, name=name, grid=(R // tr,),
                 in_specs=[pl.BlockSpec((N_DEV, tr, L), lambda i: (0, i, 0)), blk, blk, blk],
                 out_specs=[blk, blk, blk, blk], out_shape=[_sds((R, L), F32)] * 4)(recv, w, m, v)


def _reduce_adam_layer(recv, w, m, v, idx, prev, *, name, after=None):
    r, c = w.shape[-2:]
    tr = _pick(r, (256, 128)) if r % 128 == 0 else r
    if prev is None:
        prev = [lax.empty(w.shape, F32) for _ in range(4)]
    behind = [] if after is None else [after]

    def body(r_ref, w_ref, m_ref, v_ref, *rest):
        _reduce_adam_body(r_ref, w_ref, m_ref, v_ref, *rest[-4:])

    blk = pl.BlockSpec((None, tr, c), lambda i: (idx, i, 0))
    return pl.pallas_call(
        body, name=name, grid=(r // tr,),
        in_specs=[pl.BlockSpec((N_DEV, tr, c), lambda i: (0, i, 0)), blk, blk, blk] + [ANY] * (4 + len(behind)),
        out_specs=[blk] * 4, out_shape=[_sds(w.shape, F32)] * 4, input_output_aliases={4 + j: j for j in range(4)},
        compiler_params=pltpu.CompilerParams(vmem_limit_bytes=VMEM_LIMIT_V7X, dimension_semantics=("arbitrary",)),
    )(recv, w, m, v, *prev, *behind)


class _Pack:
    def __init__(self, shapes, row_mult):
        self.shapes, self.offs, rows = dict(shapes), {}, 0
        for name, shp in shapes:
            size = 1
            for d in shp:
                size *= d
            nr = -(-size // (16 * PACK_LANES)) * 16
            self.offs[name] = (rows, size, nr)
            rows += nr
        self.used = rows
        self.rows = -(-rows // row_mult) * row_mult

    def pack(self, arrays, dtype, lead=()):
        parts = []
        for name, (r0, size, nr) in self.offs.items():
            flat = arrays[name].astype(dtype).reshape(lead + (size,))
            flat = jnp.pad(flat, [(0, 0)] * len(lead) + [(0, nr * PACK_LANES - size)])
            parts.append(flat.reshape(lead + (nr, PACK_LANES)))
        if self.rows > self.used:
            parts.append(jnp.zeros(lead + (self.rows - self.used, PACK_LANES), dtype))
        return jnp.concatenate(parts, axis=len(lead))

    def unpack(self, buf, lead=()):
        out = {}
        for name, (r0, size, nr) in self.offs.items():
            flat = buf[..., r0:r0 + nr, :].reshape(lead + (nr * PACK_LANES,))
            out[name] = flat[..., :size].reshape(lead + tuple(self.shapes[name]))
        return out


def _to_full(g8, ax):
    t = jnp.moveaxis(g8, 0, ax)
    return t.reshape(t.shape[:ax] + (t.shape[ax] * t.shape[ax + 1],) + t.shape[ax + 2:])


def _to_shards(full, ax):
    shp = full.shape
    return jnp.moveaxis(full.reshape(shp[:ax] + (N_DEV, shp[ax] // N_DEV) + shp[ax + 1:]), ax, 0)


def _to_heads(a, H):
    S = a.shape[0]
    return a.reshape(S, H, HEAD_DIM).transpose(1, 0, 2)


def _from_heads(a):
    H, S, dh = a.shape
    return a.transpose(1, 0, 2).reshape(S, H * dh)


def kernel(x, mem, g_mix, w_in_a, b_glu, w_dw_a, b_dw_a, ln_g, ln_b, g_kv, w_kvf, b_f, w_in_b, g_mem, w_mem_kv, w_out, g_ffn, w_up, w_dw_f, b_dw_f, w_down, g_final, loss_target, m_g_mix, m_w_in_a, m_b_glu, m_w_dw_a, m_b_dw_a, m_ln_g, m_ln_b, m_g_kv, m_w_kvf, m_b_f, m_w_in_b, m_g_mem, m_w_mem_kv, m_w_out, m_g_ffn, m_w_up, m_w_dw_f, m_b_dw_f, m_w_down, m_g_final, v_g_mix, v_w_in_a, v_b_glu, v_w_dw_a, v_b_dw_a, v_ln_g, v_ln_b, v_g_kv, v_w_kvf, v_b_f, v_w_in_b, v_g_mem, v_w_mem_kv, v_w_out, v_g_ffn, v_w_up, v_w_dw_f, v_b_dw_f, v_w_down, v_g_final):
    given = dict(locals())
    W = {n: given[n] for n in WEIGHTS}
    x0, mem0, tgt = x[0], mem[0], loss_target[0]
    S, D = x0.shape
    depth, n_a = g_mix.shape[0], w_in_a.shape[0]
    C = w_dw_a.shape[2] * N_DEV
    Mw = D - C
    Fw = w_down.shape[1] * N_DEV
    H = b_f.shape[0]
    assert C == H * HEAD_DIM and (2 * C) % Mw == 0 and C % Mw == 0 and H <= GATE_LANES
    tq = _pick(S, (256, 128))
    nkv = 2 * C + GATE_LANES

    def mix_keys(l):
        keys = [("w_in_a", l) if l < n_a else ("w_in_b", l - n_a), ("w_mem_kv", l), ("w_out", l)]
        return keys + ([("w_kvf", 0)] if l == n_a else [])

    def ffn_keys(l):
        return [("w_up", l), ("w_down", l)]

    def key_ax(key):
        return big_ax[key[0]] - 1

    W3 = {n: (W[n][None] if n == "w_kvf" else W[n]) for n, _ in BIG}
    big_ax = {n: (ax + 1 if n == "w_kvf" else ax) for n, ax in BIG}
    pk_small = _Pack([(n, W[n].shape) for n, _ in SMALL], 8)
    pk_rep = _Pack([(n, W[n].shape) for n in REP] + [("loss", (1,))], 8)

    ws32 = pk_small.pack(W, F32)
    gathers, toks = {}, []
    for l in range(depth):
        for tag, keys in (("mix", mix_keys(l)), ("ffn", ffn_keys(l))):
            srcs = [W3[n][i].astype(COMM_DT) for n, i in keys] + ([ws32] if (l, tag) == (0, "mix") else [])
            gathers[l, tag], t = _xchg_start(srcs, [False] * len(srcs), name=f"w_gather_start_{tag}{l}")
            toks.append(t)
    tok = sum(t[0, 0] for t in toks)
    g_mix, g_mem = g_mix + tok, g_mem + tok
    bf_pad = jnp.pad(b_f, (0, GATE_LANES - H)).reshape(1, GATE_LANES)

    def gathered(l, tag, keys, after):
        n = len(gathers[l, tag][1])
        lands = _xchg_wait(gathers[l, tag], after, [False] * n, name=f"w_gather_wait_{tag}{l}")
        return {k: _to_full(a, key_ax(k)).astype(MXU_DT) for k, a in zip(keys, lands)}, lands[len(keys):]

    mem_n = _rms_fwd(mem0, g_mem, name="mem_norm")
    sv = []
    xs = x0
    for l in range(depth):
        wl, extra = gathered(l, "mix", mix_keys(l), xs)
        if extra:
            gs = pk_small.unpack(extra[0], (N_DEV,))
            small = {n: _to_full(gs[n], ax) for n, ax in SMALL}
        t = dict(x_in=xs, w=wl)
        t["mkv"] = _mm(mem_n, wl["w_mem_kv", l], name=f"mem_kv{l}", out_dtype=MXU_DT)
        t["h"] = _rms_fwd(xs, g_mix[l], name=f"mix_norm{l}")
        if l < n_a:
            t["p"] = _mm(t["h"], wl["w_in_a", l], name=f"in_proj{l}", out_dtype=F32)
            t["v2"] = _glu_conv_fwd(t["p"], small["b_glu"][l], small["w_dw_a"][l], small["b_dw_a"][l],
                                    name=f"glu_conv{l}")
            memo = _mem_attn_fwd(t["p"], 2 * C // Mw, t["mkv"], 0, Mw=Mw, name=f"mem_attn{l}")
            t["cat"] = _ln_silu_cat_fwd(t["v2"], small["ln_g"][l], small["ln_b"][l], memo, name=f"ln_silu{l}")
        else:
            if l == n_a:
                wkvf = jnp.pad(wl["w_kvf", 0], ((0, 0), (0, nkv - w_kvf.shape[1])))
                hk = _rms_fwd(xs, g_kv, name="kv_norm")
                kvf = _mm(hk, wkvf, name="kv_proj", out_dtype=F32)
                k_h = _to_heads(kvf[:, :C], H).astype(MXU_DT)
                v_h = _to_heads(kvf[:, C:2 * C], H).astype(MXU_DT)
                fr = kvf[:, 2 * C:]
                cum = _fgate_fwd(fr, bf_pad, name="fgate")
                cum_t = cum[:, :H].T
                cq, ck = cum_t.reshape(H, S, 1), cum_t.reshape(H, S // tq, 1, tq)
            t["p"] = _mm(t["h"], wl["w_in_b", l - n_a], name=f"in_proj{l}", out_dtype=F32)
            t["q_h"] = _to_heads(t["p"][:, :C], H).astype(MXU_DT)
            t["o_h"], t["lse"] = _fox_fwd(t["q_h"], k_h, v_h, cq, ck, tq=tq, name=f"fox{l}")
            memo = _mem_attn_fwd(t["p"], C // Mw, t["mkv"], 0, Mw=Mw, name=f"mem_attn{l}")
            t["cat"] = jnp.concatenate([_from_heads(t["o_h"]), memo], axis=1).astype(MXU_DT)
        t["x_mid"] = _mm(t["cat"], wl["w_out", l], name=f"out_proj{l}", out_dtype=F32, add=xs)
        wl.update(gathered(l, "ffn", ffn_keys(l), t["x_mid"])[0])
        t["h2"] = _rms_fwd(t["x_mid"], g_ffn[l], name=f"ffn_norm{l}")
        t["ug"] = _mm(t["h2"], wl["w_up", l], name=f"up_gate{l}", out_dtype=F32, cols=(0, Fw))
        t["uv"] = _mm(t["h2"], wl["w_up", l], name=f"up_val{l}", out_dtype=F32, cols=(Fw, Fw))
        t["act"] = _ffn_act_fwd(t["ug"], t["uv"], small["w_dw_f"][l], b_dw_f[l], name=f"ffn_act{l}")
        xs = _mm(t["act"], wl["w_down", l], name=f"down_proj{l}", out_dtype=F32, add=t["x_mid"])
        sv.append(t)
    loss_dev, dx, dg_final = _loss_bwd(xs, g_final, tgt, name="loss_head")

    M1 = {n: given["m_" + n] for n in WEIGHTS}
    V1 = {n: given["v_" + n] for n in WEIGHTS}
    M3 = {n: (M1[n][None] if n == "w_kvf" else M1[n]) for n, _ in BIG}
    V3 = {n: (V1[n][None] if n == "w_kvf" else V1[n]) for n, _ in BIG}
    res, chain, pending, waited = {}, {}, [], []

    def start_grads(tag, l, keys, gl, extra=(), extra_scatter=()):
        srcs = [_to_shards(gl[k], key_ax(k)) for k in keys] + list(extra)
        scatter = [True] * len(keys) + list(extra_scatter)
        handle, tk = _xchg_start(srcs, scatter, name=f"g_xchg_start_{tag}{l}")
        pending.append((f"{tag}{l}", keys, handle, scatter))
        return tk

    def finish_grads(after):
        tag, keys, handle, scatter = pending.pop(0)
        lands = _xchg_wait(handle, after, scatter, name=f"g_xchg_wait_{tag}")
        waited.extend(zip(keys, lands))
        return lands[len(keys):]

    def update_waited(after=None):
        for (n, i), recv in waited:
            chain[n] = _reduce_adam_layer(recv, W3[n], M3[n], V3[n], i, chain.get(n), name=f"adam_{n}{i}",
                                          after=after)
        waited.clear()

    G = {n: [None] * W[n].shape[0] for n in ("g_mix", "b_glu", "w_dw_a", "b_dw_a", "ln_g", "ln_b", "g_ffn",
                                              "w_dw_f", "b_dw_f")}
    dk_sum = dv_sum = dck_sum = dmem_n = None
    started = None
    for l in reversed(range(depth)):
        t = sv[l]
        wl, gl = t["w"], {}
        dact = _mm(dx, wl["w_down", l], name=f"d_act{l}", out_dtype=F32, nt=True, after=started)
        gl["w_down", l] = _mm_tn(t["act"], dx, name=f"dw_down{l}", out_dtype=COMM_DT)
        dug, duv, G["w_dw_f"][l], db = _ffn_act_bwd(t["ug"], t["uv"], dact, small["w_dw_f"][l], b_dw_f[l],
                                                    name=f"d_ffn_act{l}")
        G["b_dw_f"][l] = db[0]
        dh2 = _mm_nt2(dug, duv, wl["w_up", l], name=f"d_up{l}")
        gl["w_up", l] = jnp.concatenate([_mm_tn(t["h2"], dug, name=f"dw_up_gate{l}", out_dtype=COMM_DT),
                                         _mm_tn(t["h2"], duv, name=f"dw_up_val{l}", out_dtype=COMM_DT)], axis=1)
        dx, dg = _rms_bwd(t["x_mid"], g_ffn[l], dh2, dx, name=f"d_ffn_norm{l}")
        G["g_ffn"][l] = dg[0]
        started = start_grads("ffn", l, ffn_keys(l), gl)
        if len(pending) > 2:
            finish_grads(dx)

        dcat = _mm(dx, wl["w_out", l], name=f"d_cat{l}", out_dtype=F32, nt=True, after=started)
        gl["w_out", l] = _mm_tn(t["cat"], dx, name=f"dw_out{l}", out_dtype=COMM_DT)
        if l >= n_a:
            do_h = _to_heads(dcat[:, :C], H)
            dq_h, dk_h, dv_h, dcq, dck = _fox_bwd(t["q_h"], k_h, v_h, cq, ck, t["o_h"], t["lse"], do_h, tq=tq,
                                                  name=f"d_fox{l}")
            dck = dck.reshape(H, S) + dcq.reshape(H, S)
            dk_sum = dk_h if dk_sum is None else dk_sum + dk_h
            dv_sum = dv_h if dv_sum is None else dv_sum + dv_h
            dck_sum = dck if dck_sum is None else dck_sum + dck
            dqm, dmk, dmv = _mem_attn_bwd(t["p"], C // Mw, t["mkv"], 0, dcat, C // Mw, Mw=Mw,
                                          name=f"d_mem_attn{l}")
            dp = jnp.concatenate([_from_heads(dq_h), dqm], axis=1).astype(MXU_DT)
            key = ("w_in_b", l - n_a)
        else:
            dv2, dlg, dlb = _ln_silu_bwd(t["v2"], small["ln_g"][l], small["ln_b"][l], dcat, name=f"d_ln_silu{l}")
            G["ln_g"][l], G["ln_b"][l] = dlg[0], dlb[0]
            da, dgt, G["w_dw_a"][l], dbd, dba, dbg = _glu_conv_bwd(t["p"], small["b_glu"][l], small["w_dw_a"][l],
                                                                   dv2, name=f"d_glu_conv{l}")
            G["b_dw_a"][l] = dbd[0]
            G["b_glu"][l] = jnp.concatenate([dba[0], dbg[0]])
            dqm, dmk, dmv = _mem_attn_bwd(t["p"], 2 * C // Mw, t["mkv"], 0, dcat, C // Mw, Mw=Mw,
                                          name=f"d_mem_attn{l}")
            dp = jnp.concatenate([da, dgt, dqm], axis=1).astype(MXU_DT)
            key = ("w_in_a", l)
        dmkv = jnp.concatenate([dmk, dmv], axis=1).astype(MXU_DT)
        gl["w_mem_kv", l] = _mm_tn(mem_n, dmkv, name=f"dw_mem_kv{l}", out_dtype=COMM_DT)
        dmem_n = _mm(dmkv, wl["w_mem_kv", l], name=f"d_mem_kv{l}", out_dtype=F32, nt=True, add=dmem_n)
        dh = _mm(dp, wl[key], name=f"d_in_proj{l}", out_dtype=F32, nt=True)
        gl[key] = _mm_tn(t["h"], dp, name=f"dw_in_proj{l}", out_dtype=COMM_DT)
        dx, dg = _rms_bwd(t["x_in"], g_mix[l], dh, dx, name=f"d_mix_norm{l}")
        G["g_mix"][l] = dg[0]
        if l == n_a:
            dcum = jnp.pad(dck_sum.T, ((0, 0), (0, GATE_LANES - H)))
            df, dbf = _fgate_bwd(fr, bf_pad, dcum, name="d_fgate")
            dkvf = jnp.concatenate([_from_heads(dk_sum), _from_heads(dv_sum), df], axis=1).astype(MXU_DT)
            dhk = _mm(dkvf, wkvf, name="d_kv_proj", out_dtype=F32, nt=True)
            gl["w_kvf", 0] = _mm_tn(hk, dkvf, name="dw_kv_proj", out_dtype=COMM_DT)[:, :w_kvf.shape[1]]
            dx, dg_kv = _rms_bwd(t["x_in"], g_kv, dhk, dx, name="d_kv_norm")
        if l > 0:
            started = start_grads("mix", l, mix_keys(l), gl)
            if len(pending) > 2:
                finish_grads(dx)

    _, dg_mem = _rms_bwd(mem0, g_mem, dmem_n, None, name="d_mem_norm")
    grads = {n: jnp.stack(v) for n, v in G.items()}
    grads.update(g_kv=dg_kv[0], b_f=dbf[0, :H], g_mem=dg_mem[0], g_final=dg_final[0], loss=loss_dev.reshape(1))
    gs8 = pk_small.pack({n: _to_shards(grads[n], ax) for n, ax in SMALL}, F32, (N_DEV,))
    last = start_grads("mix", 0, mix_keys(0), gl, [gs8, pk_rep.pack(grads, F32)], [True, False])
    while len(pending) > 1:
        finish_grads(dx)
    update_waited(after=last)
    extra = finish_grads(chain["w_down"][0])
    update_waited()
    no_state = dict(loss=jnp.zeros((1,), F32))
    for pk, recv, tag in ((pk_small, extra[0], "small"), (pk_rep, extra[1], "rep")):
        w32 = ws32 if tag == "small" else pk.pack({**W, **no_state}, F32)
        outs = _reduce_adam(recv, w32, pk.pack({**M1, **no_state}, F32), pk.pack({**V1, **no_state}, F32),
                            name=f"adam_{tag}")
        for kind, buf in zip(("grad", "delta", "new_m", "new_v"), outs):
            for n, a in pk.unpack(buf).items():
                res[kind, n] = a
    for n, outs in chain.items():
        for kind, a in zip(("grad", "delta", "new_m", "new_v"), outs):
            res[kind, n] = a.reshape(W[n].shape)

    loss = res["grad", "loss"][0]
    return (loss, dx[None], *[res[kind, n] for kind in ("grad", "delta", "new_m", "new_v") for n in WEIGHTS])
```

```python
import jax
import jax.numpy as jnp
from jax import lax
from jax.experimental import pallas as pl
from jax.experimental.pallas import tpu as pltpu

F32 = jnp.float32
MXU_DT = jnp.bfloat16
COMM_DT = jnp.bfloat16

N_DEV = 8
HEAD_DIM = 64
RMS_EPS = 1e-6
LN_EPS = 1e-5
ADAM_LR = 0.001
ADAM_B1 = 0.9
ADAM_B2 = 0.999
ADAM_EPS = 1e-08
ADAM_WD = 0.01
ADAM_STEP = 10

PACK_LANES = 1024
GATE_LANES = 128
VMEM_LIMIT_V7X = 56 << 20
NEG = -1e30
MESH = pl.DeviceIdType.MESH
ANY = pl.BlockSpec(memory_space=pl.ANY)
NT = (((1,), (1,)), ((), ()))
NN = (((1,), (0,)), ((), ()))

BIG = (("w_in_a", 2), ("w_kvf", 0), ("w_in_b", 1), ("w_mem_kv", 1), ("w_out", 1), ("w_up", 2), ("w_down", 1))
SMALL = (("b_glu", 1), ("w_dw_a", 2), ("b_dw_a", 1), ("ln_g", 1), ("ln_b", 1), ("w_dw_f", 2))
REP = ("g_mix", "g_kv", "b_f", "g_mem", "g_ffn", "b_dw_f", "g_final")
WEIGHTS = ("g_mix", "w_in_a", "b_glu", "w_dw_a", "b_dw_a", "ln_g", "ln_b", "g_kv", "w_kvf", "b_f", "w_in_b",
           "g_mem", "w_mem_kv", "w_out", "g_ffn", "w_up", "w_dw_f", "b_dw_f", "w_down", "g_final")


def _sds(shape, dtype):
    return jax.ShapeDtypeStruct(tuple(shape), dtype)


def _call(body, *, name, out_shape, grid=(), in_specs=None, out_specs=None, scratch_shapes=()):
    params = dict(vmem_limit_bytes=VMEM_LIMIT_V7X)
    if grid:
        params["dimension_semantics"] = ("arbitrary",) * len(grid)
    kw = {}
    if in_specs is not None:
        kw["in_specs"] = in_specs
    if out_specs is not None:
        kw["out_specs"] = out_specs
    return pl.pallas_call(body, name=name, grid=grid, out_shape=out_shape, scratch_shapes=list(scratch_shapes),
                          compiler_params=pltpu.CompilerParams(**params), **kw)


def _res(shape):
    nd = len(shape)
    return pl.BlockSpec(tuple(shape), lambda *_: (0,) * nd)


def _colblk(rows, tc, off=0):
    return pl.BlockSpec((rows, tc), lambda j: (0, j + off))


def _rowblk(tm, cols, off=0):
    return pl.BlockSpec((tm, cols), lambda i: (i, off))


def _pick(n, opts=(512, 256, 128)):
    for t in opts:
        if n % t == 0:
            return t
    return n


def _sig(z):
    return 1.0 / (1.0 + jnp.exp(-z))


def _dot(a, b, dims=NN):
    return lax.dot_general(a, b, dims, preferred_element_type=F32)


def _dot_tn(a, b):
    return _dot(a.T.astype(b.dtype), b)


def _mm_tn(a, b, *, name, out_dtype):
    S, K = a.shape
    N = b.shape[1]
    tk, rc = _pick(K, (256, 128)), _pick(S)
    cast = b.dtype != MXU_DT

    def body(a_ref, b_ref, o_ref, acc_ref, *bb):
        if cast:
            @pl.when(pl.program_id(0) == 0)
            def _():
                for r0 in range(0, S, rc):
                    bb[0][r0:r0 + rc, :] = b_ref[r0:r0 + rc, :].astype(MXU_DT)
            b_ref = bb[0]
        for n, r0 in enumerate(range(0, S, rc)):
            part = _dot(a_ref[r0:r0 + rc, :].astype(F32).T.astype(MXU_DT), b_ref[r0:r0 + rc, :])
            if n == 0:
                acc_ref[...] = part
            else:
                acc_ref[...] += part
        o_ref[...] = acc_ref[...].astype(out_dtype)

    return _call(body, name=name, grid=(K // tk,), in_specs=[_colblk(S, tk), _res((S, N))],
                 out_specs=pl.BlockSpec((tk, N), lambda j: (j, 0)), out_shape=_sds((K, N), out_dtype),
                 scratch_shapes=[pltpu.VMEM((tk, N), F32)] + ([pltpu.VMEM((S, N), MXU_DT)] if cast else []))(a, b)


def _mm_nt2(a1, a2, w, *, name):
    M, Fw = a1.shape
    N = w.shape[0]
    tm, nc = _pick(M), _pick(N)

    def body(a1_ref, a2_ref, w_ref, o_ref):
        v1, v2 = a1_ref[...], a2_ref[...]
        for n0 in range(0, N, nc):
            o_ref[:, n0:n0 + nc] = (_dot(v1, w_ref[n0:n0 + nc, 0:Fw], NT) + _dot(v2, w_ref[n0:n0 + nc, Fw:2 * Fw], NT))

    return _call(body, name=name, grid=(M // tm,), in_specs=[_rowblk(tm, Fw), _rowblk(tm, Fw), _res(w.shape)],
                 out_specs=_rowblk(tm, N), out_shape=_sds((M, N), F32))(a1, a2, w)


def _mm(a, w, *, name, out_dtype, nt=False, add=None, cols=None, after=None):
    M, K = a.shape
    N = w.shape[0] if nt else w.shape[1]
    assert (w.shape[1] if nt else w.shape[0]) == K
    dims = NT if nt else NN
    has_add, has_after = add is not None, after is not None
    if cols is not None:
        assert not nt and not has_add and not has_after
        c0, N = cols
        tn, rc = _pick(N), _pick(M)
        assert c0 % tn == 0

        def body(a_ref, w_ref, o_ref):
            wv = w_ref[...]
            for r0 in range(0, M, rc):
                o_ref[r0:r0 + rc, :] = _dot(a_ref[r0:r0 + rc, :], wv).astype(out_dtype)

        return _call(body, name=name, grid=(N // tn,), in_specs=[_res((M, K)), _colblk(K, tn, c0 // tn)],
                     out_specs=_colblk(M, tn), out_shape=_sds((M, N), out_dtype))(a, w)
    tn, rc = _pick(N), _pick(M)
    cast = a.dtype != MXU_DT

    def body(*refs):
        a_ref, w_ref, o_ref = refs[0], refs[1], refs[2 + has_add + has_after]
        if cast:
            @pl.when(pl.program_id(0) == 0)
            def _():
                for r0 in range(0, M, rc):
                    refs[-1][r0:r0 + rc, :] = a_ref[r0:r0 + rc, :].astype(MXU_DT)
            a_ref = refs[-1]
        wv = w_ref[...]
        for r0 in range(0, M, rc):
            acc = _dot(a_ref[r0:r0 + rc, :], wv, dims)
            if has_add:
                acc = acc + refs[2][r0:r0 + rc, :]
            o_ref[r0:r0 + rc, :] = acc.astype(out_dtype)

    w_spec = pl.BlockSpec((tn, K), lambda j: (j, 0)) if nt else pl.BlockSpec((K, tn), lambda j: (0, j))
    in_specs = ([_res((M, K)), w_spec] + ([_colblk(M, tn)] if has_add else [])
                + ([_res(after.shape)] if has_after else []))
    args = (a, w) + ((add,) if has_add else ()) + ((after,) if has_after else ())
    return _call(body, name=name, grid=(N // tn,), in_specs=in_specs, out_specs=_colblk(M, tn),
                 out_shape=_sds((M, N), out_dtype),
                 scratch_shapes=[pltpu.VMEM((M, K), MXU_DT)] if cast else [])(*args)


def _rms_fwd(x, g, *, name):
    M, D = x.shape
    tm = _pick(M, (256, 128))

    def body(x_ref, g_ref, h_ref):
        xf = x_ref[...]
        r = lax.rsqrt(jnp.mean(xf * xf, axis=-1, keepdims=True) + RMS_EPS)
        h_ref[...] = ((xf * r) * g_ref[...]).astype(MXU_DT)

    return _call(body, name=name, grid=(M // tm,), in_specs=[_rowblk(tm, D), _res((1, D))],
                 out_specs=_rowblk(tm, D), out_shape=_sds((M, D), MXU_DT))(x, g.reshape(1, D))


def _rms_bwd(x, g, dh, dx_in, *, name):
    M, D = x.shape
    tm = _pick(M, (256, 128))
    with_dx = dx_in is not None

    def body(*refs):
        if with_dx:
            x_ref, g_ref, dh_ref, dxin_ref, dx_ref, dg_ref = refs
        else:
            x_ref, g_ref, dh_ref, dg_ref = refs
        i = pl.program_id(0)
        xf = x_ref[...]
        r = lax.rsqrt(jnp.mean(xf * xf, axis=-1, keepdims=True) + RMS_EPS)
        y = xf * r
        dh_v = dh_ref[...]
        if with_dx:
            dy = dh_v * g_ref[...]
            dx_ref[...] = dxin_ref[...] + r * (dy - y * jnp.mean(dy * y, axis=-1, keepdims=True))
        part = jnp.sum(dh_v * y, axis=0, keepdims=True)

        @pl.when(i == 0)
        def _():
            dg_ref[...] = part

        @pl.when(i > 0)
        def _():
            dg_ref[...] += part

    ins = [x, g.reshape(1, D), dh] + ([dx_in] if with_dx else [])
    in_specs = [_rowblk(tm, D), _res((1, D)), _rowblk(tm, D)] + ([_rowblk(tm, D)] if with_dx else [])
    if with_dx:
        out_specs, out_shape = [_rowblk(tm, D), _res((1, D))], [_sds((M, D), F32), _sds((1, D), F32)]
    else:
        out_specs, out_shape = [_res((1, D))], [_sds((1, D), F32)]
    out = _call(body, name=name, grid=(M // tm,), in_specs=in_specs, out_specs=out_specs, out_shape=out_shape)(*ins)
    return out if with_dx else (None, out[0])


def _loss_bwd(x, g, t, *, name):
    M, D = x.shape
    tm = _pick(M, (256, 128))

    def body(x_ref, g_ref, t_ref, dx_ref, dg_ref, ls_ref):
        i = pl.program_id(0)
        xf = x_ref[...]
        r = lax.rsqrt(jnp.mean(xf * xf, axis=-1, keepdims=True) + RMS_EPS)
        xr = xf * r
        e = xr * g_ref[...] - t_ref[...]
        dout = e * (1.0 / D)
        dy = dout * g_ref[...]
        dx_ref[...] = r * (dy - xr * jnp.mean(dy * xr, axis=-1, keepdims=True))
        part = jnp.sum(dout * xr, axis=0, keepdims=True)
        lpart = jnp.zeros(ls_ref.shape, F32) + (0.5 / D) * jnp.sum(e * e, keepdims=True)

        @pl.when(i == 0)
        def _():
            dg_ref[...] = part
            ls_ref[...] = lpart

        @pl.when(i > 0)
        def _():
            dg_ref[...] += part
            ls_ref[...] += lpart

    dx, dg, ls = _call(body, name=name, grid=(M // tm,),
                       in_specs=[_rowblk(tm, D), _res((1, D)), _rowblk(tm, D)],
                       out_specs=[_rowblk(tm, D), _res((1, D)), _res((8, 128))],
                       out_shape=[_sds((M, D), F32), _sds((1, D), F32), _sds((8, 128), F32)])(x, g.reshape(1, D), t)
    return ls[0, 0], dx, dg


def _shift_rows(ext, off, rows):
    if off % 8 == 0:
        return ext[off:off + rows, :]
    return pltpu.roll(ext, ext.shape[0] - off, 0)[0:rows, :]


def _ext(pad_ref, c, rows, halo):
    return pad_ref[pl.ds(pl.multiple_of(c * rows, rows), rows + halo), :]


def _conv_chunk(pad_ref, c, rows, halo, w_ref, taps):
    ext = _ext(pad_ref, c, rows, halo)
    acc = None
    for k in range(taps):
        term = w_ref[k:k + 1, :] * _shift_rows(ext, halo - (taps - 1) + k, rows)
        acc = term if acc is None else acc + term
    return acc


def _conv_t_chunk(pad_ref, c, rows, halo, w_ref, taps):
    ext = _ext(pad_ref, c, rows, halo)
    acc = None
    for k in range(taps):
        term = w_ref[k:k + 1, :] * _shift_rows(ext, taps - 1 - k, rows)
        acc = term if acc is None else acc + term
    return acc


def _conv_wgrad_chunk(pad_ref, c, rows, halo, dy, dw_ref, taps):
    ext = _ext(pad_ref, c, rows, halo)
    for k in range(taps):
        dw_ref[k:k + 1, :] += jnp.sum(dy * _shift_rows(ext, halo - (taps - 1) + k, rows), axis=0, keepdims=True)


A_HALO, F_HALO = 32, 8


def _glu_conv_fwd(p, b_glu, w_dw, b_dw, *, name):
    S = p.shape[0]
    taps, C = w_dw.shape
    tc, rows = 128, _pick(S, (256, 128))
    nb, nch = C // tc, S // rows

    def body(a_ref, g_ref, ba_ref, bg_ref, w_ref, bd_ref, o_ref, pad_ref):
        pad_ref[0:A_HALO, :] = jnp.zeros((A_HALO, tc), F32)

        def fill(c, _):
            r = pl.ds(pl.multiple_of(c * rows, rows), rows)
            v1 = (a_ref[r, :] + ba_ref[...]) * _sig(g_ref[r, :] + bg_ref[...])
            pad_ref[pl.ds(pl.multiple_of(A_HALO + c * rows, 8), rows), :] = v1
            return 0

        lax.fori_loop(0, nch, fill, 0)

        def conv(c, _):
            o_ref[pl.ds(pl.multiple_of(c * rows, rows), rows), :] = (
                _conv_chunk(pad_ref, c, rows, A_HALO, w_ref, taps) + bd_ref[...])
            return 0

        lax.fori_loop(0, nch, conv, 0)

    b2 = b_glu.reshape(1, 2 * C)
    return _call(body, name=name, grid=(nb,),
                 in_specs=[_colblk(S, tc), _colblk(S, tc, nb), _colblk(1, tc), _colblk(1, tc, nb),
                           _colblk(taps, tc), _colblk(1, tc)],
                 out_specs=_colblk(S, tc), out_shape=_sds((S, C), F32),
                 scratch_shapes=[pltpu.VMEM((S + A_HALO, tc), F32)])(p, p, b2, b2, w_dw, b_dw.reshape(1, C))


def _glu_conv_bwd(p, b_glu, w_dw, dv2, *, name):
    S = p.shape[0]
    taps, C = w_dw.shape
    tc, rows = 128, _pick(S, (256, 128))
    nb, nch = C // tc, S // rows

    def body(a_ref, g_ref, ba_ref, bg_ref, w_ref, dy_ref, da_ref, dgt_ref, dw_ref, dbd_ref, dba_ref, dbg_ref,
             padx_ref, pady_ref):
        padx_ref[0:A_HALO, :] = jnp.zeros((A_HALO, tc), F32)
        pady_ref[S:S + A_HALO, :] = jnp.zeros((A_HALO, tc), F32)
        dw_ref[...] = jnp.zeros((taps, tc), F32)

        def fill(c, _):
            r = pl.ds(pl.multiple_of(c * rows, rows), rows)
            v1 = (a_ref[r, :] + ba_ref[...]) * _sig(g_ref[r, :] + bg_ref[...])
            padx_ref[pl.ds(pl.multiple_of(A_HALO + c * rows, 8), rows), :] = v1
            pady_ref[r, :] = dy_ref[r, :]
            return 0

        lax.fori_loop(0, nch, fill, 0)

        def back(c, carry):
            sd, sa, sg = carry
            r = pl.ds(pl.multiple_of(c * rows, rows), rows)
            dy = dy_ref[r, :]
            _conv_wgrad_chunk(padx_ref, c, rows, A_HALO, dy, dw_ref, taps)
            dv1 = _conv_t_chunk(pady_ref, c, rows, A_HALO, w_ref, taps)
            a = a_ref[r, :] + ba_ref[...]
            s = _sig(g_ref[r, :] + bg_ref[...])
            da = dv1 * s
            dgt = dv1 * a * s * (1.0 - s)
            da_ref[r, :] = da
            dgt_ref[r, :] = dgt
            return (sd + jnp.sum(dy, axis=0, keepdims=True), sa + jnp.sum(da, axis=0, keepdims=True),
                    sg + jnp.sum(dgt, axis=0, keepdims=True))

        z = jnp.zeros((1, tc), F32)
        sd, sa, sg = lax.fori_loop(0, nch, back, (z, z, z))
        dbd_ref[...] = sd
        dba_ref[...] = sa
        dbg_ref[...] = sg

    b2 = b_glu.reshape(1, 2 * C)
    return _call(body, name=name, grid=(nb,),
                 in_specs=[_colblk(S, tc), _colblk(S, tc, nb), _colblk(1, tc), _colblk(1, tc, nb),
                           _colblk(taps, tc), _colblk(S, tc)],
                 out_specs=[_colblk(S, tc), _colblk(S, tc), _colblk(taps, tc), _colblk(1, tc), _colblk(1, tc),
                            _colblk(1, tc)],
                 out_shape=[_sds((S, C), F32), _sds((S, C), F32), _sds((taps, C), F32), _sds((1, C), F32),
                            _sds((1, C), F32), _sds((1, C), F32)],
                 scratch_shapes=[pltpu.VMEM((S + A_HALO, tc), F32), pltpu.VMEM((S + A_HALO, tc), F32)])(
                     p, p, b2, b2, w_dw, dv2)


def _ln_silu_cat_fwd(v2, ln_g, ln_b, memo, *, name):
    S, C = v2.shape
    Mw = memo.shape[1]
    tm = _pick(S, (256, 128))

    def body(v_ref, g_ref, b_ref, m_ref, o_ref):
        v = v_ref[...]
        mu = jnp.mean(v, axis=-1, keepdims=True)
        d = v - mu
        y = d * lax.rsqrt(jnp.mean(d * d, axis=-1, keepdims=True) + LN_EPS) * g_ref[...] + b_ref[...]
        o_ref[:, 0:C] = (y * _sig(y)).astype(MXU_DT)
        o_ref[:, C:C + Mw] = m_ref[...].astype(MXU_DT)

    return _call(body, name=name, grid=(S // tm,),
                 in_specs=[_rowblk(tm, C), _res((1, C)), _res((1, C)), _rowblk(tm, Mw)],
                 out_specs=_rowblk(tm, C + Mw), out_shape=_sds((S, C + Mw), MXU_DT))(
                     v2, ln_g.reshape(1, C), ln_b.reshape(1, C), memo)


def _ln_silu_bwd(v2, ln_g, ln_b, dcat, *, name):
    S, C = v2.shape
    tm = _pick(S, (256, 128))

    def body(v_ref, g_ref, b_ref, dm_ref, dv_ref, dg_ref, db_ref):
        i = pl.program_id(0)
        v = v_ref[...]
        mu = jnp.mean(v, axis=-1, keepdims=True)
        d = v - mu
        rstd = lax.rsqrt(jnp.mean(d * d, axis=-1, keepdims=True) + LN_EPS)
        xh = d * rstd
        y = xh * g_ref[...] + b_ref[...]
        s = _sig(y)
        dyv = dm_ref[...] * (s * (1.0 + y * (1.0 - s)))
        dxh = dyv * g_ref[...]
        dv_ref[...] = rstd * (dxh - jnp.mean(dxh, axis=-1, keepdims=True)
                              - xh * jnp.mean(dxh * xh, axis=-1, keepdims=True))
        pg = jnp.sum(dyv * xh, axis=0, keepdims=True)
        pb = jnp.sum(dyv, axis=0, keepdims=True)

        @pl.when(i == 0)
        def _():
            dg_ref[...] = pg
            db_ref[...] = pb

        @pl.when(i > 0)
        def _():
            dg_ref[...] += pg
            db_ref[...] += pb

    return _call(body, name=name, grid=(S // tm,),
                 in_specs=[_rowblk(tm, C), _res((1, C)), _res((1, C)), _rowblk(tm, C)],
                 out_specs=[_rowblk(tm, C), _res((1, C)), _res((1, C))],
                 out_shape=[_sds((S, C), F32), _sds((1, C), F32), _sds((1, C), F32)])(
                     v2, ln_g.reshape(1, C), ln_b.reshape(1, C), dcat)


def _ffn_act_fwd(ug, uv, w_dw, b_dw, *, name):
    S, Fw = ug.shape
    taps = w_dw.shape[0]
    tc, rows = _pick(Fw, (256, 128)), _pick(S, (256, 128))
    nb, nch = Fw // tc, S // rows

    def body(ug_ref, uv_ref, wg_ref, wv_ref, bg_ref, bv_ref, o_ref, pg_ref, pv_ref):
        pg_ref[0:F_HALO, :] = jnp.zeros((F_HALO, tc), F32)
        pv_ref[0:F_HALO, :] = jnp.zeros((F_HALO, tc), F32)
        pg_ref[F_HALO:F_HALO + S, :] = ug_ref[...]
        pv_ref[F_HALO:F_HALO + S, :] = uv_ref[...]

        def act(c, _):
            gc = _conv_chunk(pg_ref, c, rows, F_HALO, wg_ref, taps) + bg_ref[...]
            vc = _conv_chunk(pv_ref, c, rows, F_HALO, wv_ref, taps) + bv_ref[...]
            o_ref[pl.ds(pl.multiple_of(c * rows, rows), rows), :] = (gc * _sig(gc) * vc).astype(MXU_DT)
            return 0

        lax.fori_loop(0, nch, act, 0)

    b2 = b_dw.reshape(1, 2 * Fw)
    return _call(body, name=name, grid=(nb,),
                 in_specs=[_colblk(S, tc), _colblk(S, tc), _colblk(taps, tc), _colblk(taps, tc, nb),
                           _colblk(1, tc), _colblk(1, tc, nb)],
                 out_specs=_colblk(S, tc), out_shape=_sds((S, Fw), MXU_DT),
                 scratch_shapes=[pltpu.VMEM((S + F_HALO, tc), F32), pltpu.VMEM((S + F_HALO, tc), F32)])(
                     ug, uv, w_dw, w_dw, b2, b2)


def _ffn_act_bwd(ug, uv, dact, w_dw, b_dw, *, name):
    S, Fw = ug.shape
    taps = w_dw.shape[0]
    tc, rows = _pick(Fw, (256, 128)), _pick(S, (256, 128))
    nb, nch = Fw // tc, S // rows

    def body(ug_ref, uv_ref, da_ref, wg_ref, wv_ref, bg_ref, bv_ref, dug_ref, duv_ref, dwg_ref, dwv_ref,
             dbg_ref, dbv_ref, pg_ref, pv_ref, qg_ref, qv_ref):
        pg_ref[0:F_HALO, :] = jnp.zeros((F_HALO, tc), F32)
        pv_ref[0:F_HALO, :] = jnp.zeros((F_HALO, tc), F32)
        qg_ref[S:S + F_HALO, :] = jnp.zeros((F_HALO, tc), F32)
        qv_ref[S:S + F_HALO, :] = jnp.zeros((F_HALO, tc), F32)
        pg_ref[F_HALO:F_HALO + S, :] = ug_ref[...]
        pv_ref[F_HALO:F_HALO + S, :] = uv_ref[...]
        dwg_ref[...] = jnp.zeros((taps, tc), F32)
        dwv_ref[...] = jnp.zeros((taps, tc), F32)

        def grads(c, carry):
            sg, sv = carry
            r = pl.ds(pl.multiple_of(c * rows, rows), rows)
            gc = _conv_chunk(pg_ref, c, rows, F_HALO, wg_ref, taps) + bg_ref[...]
            vc = _conv_chunk(pv_ref, c, rows, F_HALO, wv_ref, taps) + bv_ref[...]
            s = _sig(gc)
            da = da_ref[r, :]
            dgc = da * vc * (s * (1.0 + gc * (1.0 - s)))
            dvc = da * (gc * s)
            qg_ref[r, :] = dgc
            qv_ref[r, :] = dvc
            _conv_wgrad_chunk(pg_ref, c, rows, F_HALO, dgc, dwg_ref, taps)
            _conv_wgrad_chunk(pv_ref, c, rows, F_HALO, dvc, dwv_ref, taps)
            return sg + jnp.sum(dgc, axis=0, keepdims=True), sv + jnp.sum(dvc, axis=0, keepdims=True)

        z = jnp.zeros((1, tc), F32)
        sg, sv = lax.fori_loop(0, nch, grads, (z, z))
        dbg_ref[...] = sg
        dbv_ref[...] = sv

        def back(c, _):
            r = pl.ds(pl.multiple_of(c * rows, rows), rows)
            dug_ref[r, :] = _conv_t_chunk(qg_ref, c, rows, F_HALO, wg_ref, taps).astype(MXU_DT)
            duv_ref[r, :] = _conv_t_chunk(qv_ref, c, rows, F_HALO, wv_ref, taps).astype(MXU_DT)
            return 0

        lax.fori_loop(0, nch, back, 0)

    b2 = b_dw.reshape(1, 2 * Fw)
    pad = pltpu.VMEM((S + F_HALO, tc), F32)
    dug, duv, dwg, dwv, dbg, dbv = _call(
        body, name=name, grid=(nb,),
        in_specs=[_colblk(S, tc), _colblk(S, tc), _colblk(S, tc), _colblk(taps, tc), _colblk(taps, tc, nb),
                  _colblk(1, tc), _colblk(1, tc, nb)],
        out_specs=[_colblk(S, tc), _colblk(S, tc), _colblk(taps, tc), _colblk(taps, tc), _colblk(1, tc),
                   _colblk(1, tc)],
        out_shape=[_sds((S, Fw), MXU_DT), _sds((S, Fw), MXU_DT), _sds((taps, Fw), F32), _sds((taps, Fw), F32),
                   _sds((1, Fw), F32), _sds((1, Fw), F32)],
        scratch_shapes=[pad, pad, pad, pad])(ug, uv, dact, w_dw, w_dw, b2, b2)
    return dug, duv, jnp.concatenate([dwg, dwv], axis=1), jnp.concatenate([dbg, dbv], axis=1)


def _head_mask(h, width):
    lane = lax.broadcasted_iota(jnp.int32, (1, width), 1)
    return (lane >= h * HEAD_DIM) & (lane < (h + 1) * HEAD_DIM)


def _mem_attn_fwd(p, qblk, mkv, l, *, Mw, name):
    S, ML = p.shape[0], mkv.shape[0]
    tm, nh, scale = _pick(S, (256, 128)), Mw // HEAD_DIM, HEAD_DIM ** -0.5

    def body(q_ref, k_ref, v_ref, o_ref):
        q, kv, vv = q_ref[...], k_ref[...], v_ref[...]
        out = jnp.zeros((tm, Mw), F32)
        for h in range(nh):
            mk = _head_mask(h, Mw)
            s = _dot(jnp.where(mk, q, 0.0).astype(MXU_DT), kv, NT) * scale
            e = jnp.exp(s - jnp.max(s, axis=-1, keepdims=True))
            pr = e / jnp.sum(e, axis=-1, keepdims=True)
            out = out + _dot(pr.astype(MXU_DT), jnp.where(mk, vv, jnp.zeros_like(vv)))
        o_ref[...] = out

    return _call(body, name=name, grid=(S // tm,),
                 in_specs=[_rowblk(tm, Mw, qblk), pl.BlockSpec((ML, Mw), lambda i: (0, 2 * l)),
                           pl.BlockSpec((ML, Mw), lambda i: (0, 2 * l + 1))],
                 out_specs=_rowblk(tm, Mw), out_shape=_sds((S, Mw), F32))(p, mkv, mkv)


def _mem_attn_bwd(p, qblk, mkv, l, dcat, doblk, *, Mw, name):
    S, ML = p.shape[0], mkv.shape[0]
    tm, nh, scale = _pick(S, (256, 128)), Mw // HEAD_DIM, HEAD_DIM ** -0.5

    def body(q_ref, k_ref, v_ref, do_ref, dq_ref, dk_ref, dv_ref):
        i = pl.program_id(0)

        @pl.when(i == 0)
        def _():
            dk_ref[...] = jnp.zeros((ML, Mw), F32)
            dv_ref[...] = jnp.zeros((ML, Mw), F32)

        q, kv, vv, do = q_ref[...], k_ref[...], v_ref[...], do_ref[...]
        dq = jnp.zeros((tm, Mw), F32)
        for h in range(nh):
            mk = _head_mask(h, Mw)
            qh = jnp.where(mk, q, 0.0).astype(MXU_DT)
            s = _dot(qh, kv, NT) * scale
            e = jnp.exp(s - jnp.max(s, axis=-1, keepdims=True))
            pr = e / jnp.sum(e, axis=-1, keepdims=True)
            doh = jnp.where(mk, do, 0.0).astype(MXU_DT)
            dv_ref[...] += _dot_tn(pr, doh)
            dp = _dot(doh, vv, NT)
            ds = pr * (dp - jnp.sum(dp * pr, axis=-1, keepdims=True))
            dq = dq + _dot(ds.astype(MXU_DT), jnp.where(mk, kv, jnp.zeros_like(kv))) * scale
            dk_ref[...] += _dot_tn(ds, qh) * scale
        dq_ref[...] = dq

    return _call(body, name=name, grid=(S // tm,),
                 in_specs=[_rowblk(tm, Mw, qblk), pl.BlockSpec((ML, Mw), lambda i: (0, 2 * l)),
                           pl.BlockSpec((ML, Mw), lambda i: (0, 2 * l + 1)), _rowblk(tm, Mw, doblk)],
                 out_specs=[_rowblk(tm, Mw), _res((ML, Mw)), _res((ML, Mw))],
                 out_shape=[_sds((S, Mw), F32), _sds((ML, Mw), F32), _sds((ML, Mw), F32)])(p, mkv, mkv, dcat)


FOX_GROUP = 3


def _foxt_specs(S, dh, tq):
    nb, G = S // tq, FOX_GROUP
    rows = pl.BlockSpec((G, tq, dh), lambda h, i: (h, i, 0))
    seq = pl.BlockSpec((G, S, dh), lambda h, i: (h, 0, 0))
    seq_t = pl.BlockSpec((G, nb, dh, tq), lambda h, i: (h, 0, 0, 0))
    blk_t = pl.BlockSpec((G, dh, tq), lambda h, i: (h, 0, i))
    col = pl.BlockSpec((G, S, 1), lambda h, i: (h, 0, 0))
    row_all = pl.BlockSpec((G, nb, 1, tq), lambda h, i: (h, 0, 0, 0))
    row = pl.BlockSpec((G, 1, 1, tq), lambda h, i: (h, i, 0, 0))
    return rows, seq, seq_t, blk_t, col, row_all, row


def _foxt_logits(kv, qv, cq_row, ck_col, scale, diag):
    s = _dot(kv, qv, NT) * scale + cq_row - ck_col
    if not diag:
        return s
    keys = lax.broadcasted_iota(jnp.int32, s.shape, 0)
    queries = lax.broadcasted_iota(jnp.int32, s.shape, 1)
    return jnp.where(keys <= queries, s, NEG)


def _foxt_fwd(q, k, vt, cq_row, ck_col, *, tq, name):
    H, S, dh = q.shape
    nb, scale, G = S // tq, dh ** -0.5, FOX_GROUP
    assert H % G == 0
    rows, seq, seq_t, blk_t, col, row_all, row = _foxt_specs(S, dh, tq)

    def body(q_ref, k_ref, vt_ref, cq_ref, ck_ref, o_ref, lse_ref):
        i = pl.program_id(1)
        qv, cqv = [q_ref[e] for e in range(G)], [cq_ref[e, i] for e in range(G)]

        def kblock(j, carry, diag):
            r = pl.ds(pl.multiple_of(j * tq, tq), tq)
            out = []
            for e in range(G):
                m, l, acc = carry[e]
                s = _foxt_logits(k_ref[e, r, :], qv[e], cqv[e], ck_ref[e, r, :], scale, diag)
                m2 = jnp.maximum(m, jnp.max(s, axis=0, keepdims=True))
                pr = jnp.exp(s - m2)
                al = jnp.exp(m - m2)
                out.append((m2, al * l + jnp.sum(pr, axis=0, keepdims=True),
                            al * acc + _dot(vt_ref[e, j], pr.astype(MXU_DT))))
            return tuple(out)

        init = tuple((jnp.full((1, tq), NEG, F32), jnp.zeros((1, tq), F32), jnp.zeros((dh, tq), F32))
                     for _ in range(G))
        carry = lax.fori_loop(0, i, lambda j, c: kblock(j, c, False), init)
        for e, (m, l, acc) in enumerate(kblock(i, carry, True)):
            o_ref[e] = acc / l
            lse_ref[e, 0] = m + jnp.log(l)

    return _call(body, name=name, grid=(H // G, nb), in_specs=[rows, seq, seq_t, row_all, col],
                 out_specs=[blk_t, row],
                 out_shape=[_sds((H, dh, S), F32), _sds((H, nb, 1, tq), F32)])(q, k, vt, cq_row, ck_col)


def _foxt_bwd(q, k, kt, v, cq_row, ck_col, ot, lse, dot_, *, tq, name):
    H, S, dh = q.shape
    nb, scale, G = S // tq, dh ** -0.5, FOX_GROUP
    rows, seq, seq_t, blk_t, col, row_all, row = _foxt_specs(S, dh, tq)

    def body(q_ref, k_ref, kt_ref, v_ref, cq_ref, ck_ref, ot_ref, lse_ref, dot_ref, dq_ref, dk_ref, dv_ref,
             dcq_ref, dck_ref):
        i = pl.program_id(1)

        @pl.when(i == 0)
        def _():
            dk_ref[...] = jnp.zeros((G, S, dh), F32)
            dv_ref[...] = jnp.zeros((G, S, dh), F32)
            dck_ref[...] = jnp.zeros((G, S, 1), F32)

        qv, cqv, lsev = [q_ref[e] for e in range(G)], [cq_ref[e, i] for e in range(G)], [lse_ref[e, 0] for e in range(G)]
        dob = [dot_ref[e].astype(MXU_DT) for e in range(G)]
        delta = [jnp.sum(dob[e].astype(F32) * ot_ref[e], axis=0, keepdims=True) for e in range(G)]

        def kblock(j, carry, diag):
            r = pl.ds(pl.multiple_of(j * tq, tq), tq)
            out = []
            for e in range(G):
                dq, rs = carry[e]
                pr = jnp.exp(_foxt_logits(k_ref[e, r, :], qv[e], cqv[e], ck_ref[e, r, :], scale, diag) - lsev[e])
                ds = pr * (_dot(v_ref[e, r, :], dob[e]) - delta[e])
                dsb = ds.astype(MXU_DT)
                dk_ref[e, r, :] += _dot(dsb, qv[e]) * scale
                dv_ref[e, r, :] += _dot(pr.astype(MXU_DT), dob[e], NT)
                dck_ref[e, r, :] += -jnp.sum(ds, axis=1, keepdims=True)
                out.append((dq + _dot(kt_ref[e, j], dsb), rs + jnp.sum(ds, axis=0, keepdims=True)))
            return tuple(out)

        init = tuple((jnp.zeros((dh, tq), F32), jnp.zeros((1, tq), F32)) for _ in range(G))
        carry = lax.fori_loop(0, i, lambda j, c: kblock(j, c, False), init)
        for e, (dq, rs) in enumerate(kblock(i, carry, True)):
            dq_ref[e] = dq * scale
            dcq_ref[e, 0] = rs

    return _call(body, name=name, grid=(H // G, nb),
                 in_specs=[rows, seq, seq_t, seq, row_all, col, blk_t, row, blk_t],
                 out_specs=[blk_t, seq, seq, row, col],
                 out_shape=[_sds((H, dh, S), F32), _sds((H, S, dh), F32), _sds((H, S, dh), F32),
                            _sds((H, nb, 1, tq), F32), _sds((H, S, 1), F32)])(
                                q, k, kt, v, cq_row, ck_col, ot, lse, dot_)


def _tri(n, lower):
    r = lax.broadcasted_iota(jnp.int32, (n, n), 0)
    c = lax.broadcasted_iota(jnp.int32, (n, n), 1)
    return ((c <= r) if lower else (c >= r)).astype(F32)


def _fgate_fwd(fr, bf, *, name):
    S, W = fr.shape
    B = _pick(S, (256, 128))

    def body(f_ref, b_ref, cum_ref):
        L = _tri(B, True)
        carry = jnp.zeros((1, W), F32)
        for blk in range(S // B):
            z = f_ref[blk * B:(blk + 1) * B, :] + b_ref[...]
            ls = jnp.minimum(z, 0.0) - jnp.log(1.0 + jnp.exp(-jnp.abs(z)))
            cum_ref[blk * B:(blk + 1) * B, :] = jnp.dot(L, ls, precision=lax.Precision.HIGHEST,
                                                        preferred_element_type=F32) + carry
            carry = carry + jnp.sum(ls, axis=0, keepdims=True)

    return _call(body, name=name, out_shape=_sds((S, W), F32))(fr, bf)


def _fgate_bwd(fr, bf, dcum, *, name):
    S, W = fr.shape
    B = _pick(S, (256, 128))

    def body(f_ref, b_ref, dc_ref, df_ref, db_ref):
        U = _tri(B, False)
        carry = jnp.zeros((1, W), F32)
        dbs = jnp.zeros((1, W), F32)
        for blk in reversed(range(S // B)):
            dc = dc_ref[blk * B:(blk + 1) * B, :]
            dls = jnp.dot(U, dc, precision=lax.Precision.HIGHEST, preferred_element_type=F32) + carry
            carry = carry + jnp.sum(dc, axis=0, keepdims=True)
            z = f_ref[blk * B:(blk + 1) * B, :] + b_ref[...]
            df = dls * (1.0 / (1.0 + jnp.exp(z)))
            df_ref[blk * B:(blk + 1) * B, :] = df
            dbs = dbs + jnp.sum(df, axis=0, keepdims=True)
        db_ref[...] = dbs

    return _call(body, name=name, out_shape=[_sds((S, W), F32), _sds((1, W), F32)])(fr, bf, dcum)


def _flip(v, bit):
    return 1 - v if bit else v


HBM_SPEC = pl.BlockSpec(memory_space=pltpu.HBM)
SEM_SPEC = pl.BlockSpec(memory_space=pltpu.SEMAPHORE)


def _xchg_copies(src, land, sems, scatter):
    n = len(src)
    send, recv, loc = sems[:7 * n], sems[7 * n:14 * n], sems[14 * n:15 * n]
    x, y, c = lax.axis_index("x"), lax.axis_index("y"), lax.axis_index("c")
    me = 4 * x + 2 * y + c

    def peer(m):
        return _flip(x, m & 4), _flip(y, m & 2), _flip(c, m & 1)

    def copy(i, m):
        px, py, pc = peer(m)
        return pltpu.make_async_remote_copy(
            src_ref=src[i].at[4 * px + 2 * py + pc] if scatter[i] else src[i], dst_ref=land[i].at[me],
            send_sem=send[7 * i + m - 1], recv_sem=recv[7 * i + m - 1], device_id=(px, py, pc), device_id_type=MESH)

    def arrival(i, m):
        px, py, pc = peer(m)
        slot = land[i].at[4 * px + 2 * py + pc]
        return pltpu.make_async_remote_copy(src_ref=slot, dst_ref=slot, send_sem=send[7 * i + m - 1],
                                            recv_sem=recv[7 * i + m - 1], device_id=(px, py, pc), device_id_type=MESH)

    def own(i):
        return pltpu.make_async_copy(src[i].at[me] if scatter[i] else src[i], land[i].at[me], loc[i])

    return copy, arrival, own


def _xchg_start(srcs, scatter, *, name):
    n = len(srcs)
    lands = [_sds((N_DEV,) + s.shape[-2:], s.dtype) for s in srcs]

    ns = 15 * n

    def body(*refs):
        src, land, sems, token = refs[:n], refs[n:2 * n], refs[2 * n:2 * n + ns], refs[-1]
        copy, _, own = _xchg_copies(src, land, sems, scatter)
        for i in range(n):
            own(i).start()
            for m in range(1, N_DEV):
                copy(i, m).start()
        token[...] = jnp.zeros(token.shape, F32)

    thru = [pltpu.HBM(s.shape, s.dtype) for s in srcs] + [pltpu.HBM(s.shape, s.dtype) for s in lands]
    out = pl.pallas_call(
        body, name=name,
        out_shape=(*[pltpu.SemaphoreType.DMA(())] * ns, *thru, _sds((8, 128), F32)),
        in_specs=[HBM_SPEC] * (2 * n),
        out_specs=(*[SEM_SPEC] * ns, *[HBM_SPEC] * (2 * n), pl.BlockSpec(memory_space=pltpu.VMEM)),
        input_output_aliases={i: ns + i for i in range(2 * n)},
        compiler_params=pltpu.CompilerParams(has_side_effects=pltpu.SideEffectType.DATAFLOW_SIDE_EFFECTING),
    )(*[pltpu.with_memory_space_constraint(s, pltpu.HBM) for s in srcs],
      *[pltpu.with_memory_space_constraint(lax.empty(s.shape, s.dtype), pltpu.HBM) for s in lands])
    bufs = list(out[ns:ns + 2 * n])
    return (list(out[:ns]), bufs[:n], bufs[n:]), out[-1]


def _xchg_wait(handle, after, scatter, *, name):
    sems, srcs, lands = handle
    n = len(srcs)
    ns = 15 * n

    def body(*refs):
        src, land = refs[:n], refs[n:2 * n]
        copy, arrival, own = _xchg_copies(src, land, refs[2 * n:2 * n + ns], scatter)
        for i in range(n):
            own(i).wait()
            for m in range(1, N_DEV):
                copy(i, m).wait_send()
                arrival(i, m).wait_recv()

    out = pl.pallas_call(
        body, name=name,
        out_shape=tuple(pltpu.HBM(s.shape, s.dtype) for s in srcs + lands),
        in_specs=[HBM_SPEC] * (2 * n) + [SEM_SPEC] * ns + [ANY],
        out_specs=tuple([HBM_SPEC] * (2 * n)),
        input_output_aliases={i: i for i in range(2 * n)},
        compiler_params=pltpu.CompilerParams(has_side_effects=pltpu.SideEffectType.DATAFLOW_SIDE_EFFECTING),
    )(*srcs, *lands, *sems, after)
    return list(out[n:])


def _reduce_adam_body(r_ref, w_ref, m_ref, v_ref, g_ref, d_ref, m2_ref, v2_ref):
    g = r_ref[0].astype(F32)
    for s in range(1, N_DEV):
        g = g + r_ref[s].astype(F32)
    mm = ADAM_B1 * m_ref[...] + (1.0 - ADAM_B1) * g
    vv = ADAM_B2 * v_ref[...] + (1.0 - ADAM_B2) * (g * g)
    m_hat = mm / (1.0 - ADAM_B1 ** ADAM_STEP)
    v_hat = vv / (1.0 - ADAM_B2 ** ADAM_STEP)
    g_ref[...] = g
    d_ref[...] = -ADAM_LR * (m_hat / (jnp.sqrt(v_hat) + ADAM_EPS) + ADAM_WD * w_ref[...])
    m2_ref[...] = mm
    v2_ref[...] = vv


def _reduce_adam(recv, w, m, v, *, name):
    R, L = w.shape
    tr = _pick(R, (256, 128, 64, 32, 16, 8))

    def body(*refs):
        _reduce_adam_body(*refs)

    blk = _rowblk(tr, L)
    return _call(body, name=name, grid=(R // tr,),
                 in_specs=[pl.BlockSpec((N_DEV, tr, L), lambda i: (0, i, 0)), blk, blk, blk],
                 out_specs=[blk, blk, blk, blk], out_shape=[_sds((R, L), F32)] * 4)(recv, w, m, v)


def _reduce_adam_layer(recv, w, m, v, idx, prev, *, name, after=None):
    r, c = w.shape[-2:]
    tr = _pick(r, (256, 128)) if r % 128 == 0 else r
    if prev is None:
        prev = [lax.empty(w.shape, F32) for _ in range(4)]
    behind = [] if after is None else [after]

    def body(r_ref, w_ref, m_ref, v_ref, *rest):
        _reduce_adam_body(r_ref, w_ref, m_ref, v_ref, *rest[-4:])

    blk = pl.BlockSpec((None, tr, c), lambda i: (idx, i, 0))
    return pl.pallas_call(
        body, name=name, grid=(r // tr,),
        in_specs=[pl.BlockSpec((N_DEV, tr, c), lambda i: (0, i, 0)), blk, blk, blk] + [ANY] * (4 + len(behind)),
        out_specs=[blk] * 4, out_shape=[_sds(w.shape, F32)] * 4, input_output_aliases={4 + j: j for j in range(4)},
        compiler_params=pltpu.CompilerParams(vmem_limit_bytes=VMEM_LIMIT_V7X, dimension_semantics=("arbitrary",)),
    )(recv, w, m, v, *prev, *behind)


class _Pack:
    def __init__(self, shapes, row_mult):
        self.shapes, self.offs, rows = dict(shapes), {}, 0
        for name, shp in shapes:
            size = 1
            for d in shp:
                size *= d
            nr = -(-size // (16 * PACK_LANES)) * 16
            self.offs[name] = (rows, size, nr)
            rows += nr
        self.used = rows
        self.rows = -(-rows // row_mult) * row_mult

    def pack(self, arrays, dtype, lead=()):
        parts = []
        for name, (r0, size, nr) in self.offs.items():
            flat = arrays[name].astype(dtype).reshape(lead + (size,))
            flat = jnp.pad(flat, [(0, 0)] * len(lead) + [(0, nr * PACK_LANES - size)])
            parts.append(flat.reshape(lead + (nr, PACK_LANES)))
        if self.rows > self.used:
            parts.append(jnp.zeros(lead + (self.rows - self.used, PACK_LANES), dtype))
        return jnp.concatenate(parts, axis=len(lead))

    def unpack(self, buf, lead=()):
        out = {}
        for name, (r0, size, nr) in self.offs.items():
            flat = buf[..., r0:r0 + nr, :].reshape(lead + (nr * PACK_LANES,))
            out[name] = flat[..., :size].reshape(lead + tuple(self.shapes[name]))
        return out


def _to_full(g8, ax):
    t = jnp.moveaxis(g8, 0, ax)
    return t.reshape(t.shape[:ax] + (t.shape[ax] * t.shape[ax + 1],) + t.shape[ax + 2:])


def _to_shards(full, ax):
    shp = full.shape
    return jnp.moveaxis(full.reshape(shp[:ax] + (N_DEV, shp[ax] // N_DEV) + shp[ax + 1:]), ax, 0)


def _to_heads(a, H):
    S = a.shape[0]
    return a.reshape(S, H, HEAD_DIM).transpose(1, 0, 2)


def _from_heads(a):
    H, S, dh = a.shape
    return a.transpose(1, 0, 2).reshape(S, H * dh)


def _to_heads_t(a, H, tq):
    S = a.shape[0]
    return a.reshape(S // tq, tq, H, HEAD_DIM).transpose(2, 0, 3, 1)


def _from_heads_t(a):
    H, dh, S = a.shape
    return a.transpose(2, 0, 1).reshape(S, H * dh)


def kernel(x, mem, g_mix, w_in_a, b_glu, w_dw_a, b_dw_a, ln_g, ln_b, g_kv, w_kvf, b_f, w_in_b, g_mem, w_mem_kv, w_out, g_ffn, w_up, w_dw_f, b_dw_f, w_down, g_final, loss_target, m_g_mix, m_w_in_a, m_b_glu, m_w_dw_a, m_b_dw_a, m_ln_g, m_ln_b, m_g_kv, m_w_kvf, m_b_f, m_w_in_b, m_g_mem, m_w_mem_kv, m_w_out, m_g_ffn, m_w_up, m_w_dw_f, m_b_dw_f, m_w_down, m_g_final, v_g_mix, v_w_in_a, v_b_glu, v_w_dw_a, v_b_dw_a, v_ln_g, v_ln_b, v_g_kv, v_w_kvf, v_b_f, v_w_in_b, v_g_mem, v_w_mem_kv, v_w_out, v_g_ffn, v_w_up, v_w_dw_f, v_b_dw_f, v_w_down, v_g_final):
    given = dict(locals())
    W = {n: given[n] for n in WEIGHTS}
    x0, mem0, tgt = x[0], mem[0], loss_target[0]
    S, D = x0.shape
    depth, n_a = g_mix.shape[0], w_in_a.shape[0]
    C = w_dw_a.shape[2] * N_DEV
    Mw = D - C
    Fw = w_down.shape[1] * N_DEV
    H = b_f.shape[0]
    assert C == H * HEAD_DIM and (2 * C) % Mw == 0 and C % Mw == 0 and H <= GATE_LANES
    tq = _pick(S, (256, 128))
    nkv = 2 * C + GATE_LANES

    def mix_keys(l):
        keys = [("w_in_a", l) if l < n_a else ("w_in_b", l - n_a), ("w_mem_kv", l), ("w_out", l)]
        return keys + ([("w_kvf", 0)] if l == n_a else [])

    def ffn_keys(l):
        return [("w_up", l), ("w_down", l)]

    def key_ax(key):
        return big_ax[key[0]] - 1

    W3 = {n: (W[n][None] if n == "w_kvf" else W[n]) for n, _ in BIG}
    big_ax = {n: (ax + 1 if n == "w_kvf" else ax) for n, ax in BIG}
    pk_small = _Pack([(n, W[n].shape) for n, _ in SMALL], 8)
    pk_rep = _Pack([(n, W[n].shape) for n in REP] + [("loss", (1,))], 8)

    ws32 = pk_small.pack(W, F32)
    gathers, toks = {}, []
    for l in range(depth):
        for tag, keys in (("mix", mix_keys(l)), ("ffn", ffn_keys(l))):
            srcs = [W3[n][i].astype(COMM_DT) for n, i in keys] + ([ws32] if (l, tag) == (0, "mix") else [])
            gathers[l, tag], t = _xchg_start(srcs, [False] * len(srcs), name=f"w_gather_start_{tag}{l}")
            toks.append(t)
    tok = sum(t[0, 0] for t in toks)
    g_mix, g_mem = g_mix + tok, g_mem + tok
    bf_pad = jnp.pad(b_f, (0, GATE_LANES - H)).reshape(1, GATE_LANES)

    def gathered(l, tag, keys, after):
        n = len(gathers[l, tag][1])
        lands = _xchg_wait(gathers[l, tag], after, [False] * n, name=f"w_gather_wait_{tag}{l}")
        return {k: _to_full(a, key_ax(k)).astype(MXU_DT) for k, a in zip(keys, lands)}, lands[len(keys):]

    mem_n = _rms_fwd(mem0, g_mem, name="mem_norm")
    sv = []
    xs = x0
    for l in range(depth):
        wl, extra = gathered(l, "mix", mix_keys(l), xs)
        if extra:
            gs = pk_small.unpack(extra[0], (N_DEV,))
            small = {n: _to_full(gs[n], ax) for n, ax in SMALL}
        t = dict(x_in=xs, w=wl)
        t["mkv"] = _mm(mem_n, wl["w_mem_kv", l], name=f"mem_kv{l}", out_dtype=MXU_DT)
        t["h"] = _rms_fwd(xs, g_mix[l], name=f"mix_norm{l}")
        if l < n_a:
            t["p"] = _mm(t["h"], wl["w_in_a", l], name=f"in_proj{l}", out_dtype=F32)
            t["v2"] = _glu_conv_fwd(t["p"], small["b_glu"][l], small["w_dw_a"][l], small["b_dw_a"][l],
                                    name=f"glu_conv{l}")
            memo = _mem_attn_fwd(t["p"], 2 * C // Mw, t["mkv"], 0, Mw=Mw, name=f"mem_attn{l}")
            t["cat"] = _ln_silu_cat_fwd(t["v2"], small["ln_g"][l], small["ln_b"][l], memo, name=f"ln_silu{l}")
        else:
            if l == n_a:
                wkvf = jnp.pad(wl["w_kvf", 0], ((0, 0), (0, nkv - w_kvf.shape[1])))
                hk = _rms_fwd(xs, g_kv, name="kv_norm")
                kvf = _mm(hk, wkvf, name="kv_proj", out_dtype=F32)
                kb, vb = kvf[:, :C].astype(MXU_DT), kvf[:, C:2 * C].astype(MXU_DT)
                k_h, v_h = _to_heads(kb, H), _to_heads(vb, H)
                kt_h, vt_h = _to_heads_t(kb, H, tq), _to_heads_t(vb, H, tq)
                fr = kvf[:, 2 * C:]
                cum = _fgate_fwd(fr, bf_pad, name="fgate")
                cum_t = cum[:, :H].T
                cq_row, ck_col = cum_t.reshape(H, S // tq, 1, tq), cum_t.reshape(H, S, 1)
            t["p"] = _mm(t["h"], wl["w_in_b", l - n_a], name=f"in_proj{l}", out_dtype=F32)
            t["q_h"] = _to_heads(t["p"][:, :C], H).astype(MXU_DT)
            t["ot"], t["lse"] = _foxt_fwd(t["q_h"], k_h, vt_h, cq_row, ck_col, tq=tq, name=f"fox{l}")
            memo = _mem_attn_fwd(t["p"], C // Mw, t["mkv"], 0, Mw=Mw, name=f"mem_attn{l}")
            t["cat"] = jnp.concatenate([_from_heads_t(t["ot"]), memo], axis=1).astype(MXU_DT)
        t["x_mid"] = _mm(t["cat"], wl["w_out", l], name=f"out_proj{l}", out_dtype=F32, add=xs)
        wl.update(gathered(l, "ffn", ffn_keys(l), t["x_mid"])[0])
        t["h2"] = _rms_fwd(t["x_mid"], g_ffn[l], name=f"ffn_norm{l}")
        t["ug"] = _mm(t["h2"], wl["w_up", l], name=f"up_gate{l}", out_dtype=F32, cols=(0, Fw))
        t["uv"] = _mm(t["h2"], wl["w_up", l], name=f"up_val{l}", out_dtype=F32, cols=(Fw, Fw))
        t["act"] = _ffn_act_fwd(t["ug"], t["uv"], small["w_dw_f"][l], b_dw_f[l], name=f"ffn_act{l}")
        xs = _mm(t["act"], wl["w_down", l], name=f"down_proj{l}", out_dtype=F32, add=t["x_mid"])
        sv.append(t)
    loss_dev, dx, dg_final = _loss_bwd(xs, g_final, tgt, name="loss_head")

    M1 = {n: given["m_" + n] for n in WEIGHTS}
    V1 = {n: given["v_" + n] for n in WEIGHTS}
    M3 = {n: (M1[n][None] if n == "w_kvf" else M1[n]) for n, _ in BIG}
    V3 = {n: (V1[n][None] if n == "w_kvf" else V1[n]) for n, _ in BIG}
    res, chain, pending, waited = {}, {}, [], []

    def start_grads(tag, l, keys, gl, extra=(), extra_scatter=()):
        srcs = [_to_shards(gl[k], key_ax(k)) for k in keys] + list(extra)
        scatter = [True] * len(keys) + list(extra_scatter)
        handle, tk = _xchg_start(srcs, scatter, name=f"g_xchg_start_{tag}{l}")
        pending.append((f"{tag}{l}", keys, handle, scatter))
        return tk

    def finish_grads(after):
        tag, keys, handle, scatter = pending.pop(0)
        lands = _xchg_wait(handle, after, scatter, name=f"g_xchg_wait_{tag}")
        waited.extend(zip(keys, lands))
        return lands[len(keys):]

    def update_waited(after=None):
        for (n, i), recv in waited:
            chain[n] = _reduce_adam_layer(recv, W3[n], M3[n], V3[n], i, chain.get(n), name=f"adam_{n}{i}",
                                          after=after)
        waited.clear()

    G = {n: [None] * W[n].shape[0] for n in ("g_mix", "b_glu", "w_dw_a", "b_dw_a", "ln_g", "ln_b", "g_ffn",
                                              "w_dw_f", "b_dw_f")}
    dk_sum = dv_sum = dck_sum = dmem_n = None
    started = None
    for l in reversed(range(depth)):
        t = sv[l]
        wl, gl = t["w"], {}
        dact = _mm(dx, wl["w_down", l], name=f"d_act{l}", out_dtype=F32, nt=True, after=started)
        gl["w_down", l] = _mm_tn(t["act"], dx, name=f"dw_down{l}", out_dtype=COMM_DT)
        dug, duv, G["w_dw_f"][l], db = _ffn_act_bwd(t["ug"], t["uv"], dact, small["w_dw_f"][l], b_dw_f[l],
                                                    name=f"d_ffn_act{l}")
        G["b_dw_f"][l] = db[0]
        dh2 = _mm_nt2(dug, duv, wl["w_up", l], name=f"d_up{l}")
        gl["w_up", l] = jnp.concatenate([_mm_tn(t["h2"], dug, name=f"dw_up_gate{l}", out_dtype=COMM_DT),
                                         _mm_tn(t["h2"], duv, name=f"dw_up_val{l}", out_dtype=COMM_DT)], axis=1)
        dx, dg = _rms_bwd(t["x_mid"], g_ffn[l], dh2, dx, name=f"d_ffn_norm{l}")
        G["g_ffn"][l] = dg[0]
        started = start_grads("ffn", l, ffn_keys(l), gl)
        if len(pending) > 2:
            finish_grads(dx)

        dcat = _mm(dx, wl["w_out", l], name=f"d_cat{l}", out_dtype=F32, nt=True, after=started)
        gl["w_out", l] = _mm_tn(t["cat"], dx, name=f"dw_out{l}", out_dtype=COMM_DT)
        if l >= n_a:
            dot_h = dcat[:, :C].T.reshape(H, HEAD_DIM, S)
            dqt_h, dk_h, dv_h, dcq, dck = _foxt_bwd(t["q_h"], k_h, kt_h, v_h, cq_row, ck_col, t["ot"], t["lse"],
                                                    dot_h, tq=tq, name=f"d_fox{l}")
            dck = dck.reshape(H, S) + dcq.reshape(H, S)
            dk_sum = dk_h if dk_sum is None else dk_sum + dk_h
            dv_sum = dv_h if dv_sum is None else dv_sum + dv_h
            dck_sum = dck if dck_sum is None else dck_sum + dck
            dqm, dmk, dmv = _mem_attn_bwd(t["p"], C // Mw, t["mkv"], 0, dcat, C // Mw, Mw=Mw,
                                          name=f"d_mem_attn{l}")
            dp = jnp.concatenate([_from_heads_t(dqt_h), dqm], axis=1).astype(MXU_DT)
            key = ("w_in_b", l - n_a)
        else:
            dv2, dlg, dlb = _ln_silu_bwd(t["v2"], small["ln_g"][l], small["ln_b"][l], dcat, name=f"d_ln_silu{l}")
            G["ln_g"][l], G["ln_b"][l] = dlg[0], dlb[0]
            da, dgt, G["w_dw_a"][l], dbd, dba, dbg = _glu_conv_bwd(t["p"], small["b_glu"][l], small["w_dw_a"][l],
                                                                   dv2, name=f"d_glu_conv{l}")
            G["b_dw_a"][l] = dbd[0]
            G["b_glu"][l] = jnp.concatenate([dba[0], dbg[0]])
            dqm, dmk, dmv = _mem_attn_bwd(t["p"], 2 * C // Mw, t["mkv"], 0, dcat, C // Mw, Mw=Mw,
                                          name=f"d_mem_attn{l}")
            dp = jnp.concatenate([da, dgt, dqm], axis=1).astype(MXU_DT)
            key = ("w_in_a", l)
        dmkv = jnp.concatenate([dmk, dmv], axis=1).astype(MXU_DT)
        gl["w_mem_kv", l] = _mm_tn(mem_n, dmkv, name=f"dw_mem_kv{l}", out_dtype=COMM_DT)
        dmem_n = _mm(dmkv, wl["w_mem_kv", l], name=f"d_mem_kv{l}", out_dtype=F32, nt=True, add=dmem_n)
        dh = _mm(dp, wl[key], name=f"d_in_proj{l}", out_dtype=F32, nt=True)
        gl[key] = _mm_tn(t["h"], dp, name=f"dw_in_proj{l}", out_dtype=COMM_DT)
        dx, dg = _rms_bwd(t["x_in"], g_mix[l], dh, dx, name=f"d_mix_norm{l}")
        G["g_mix"][l] = dg[0]
        if l == n_a:
            dcum = jnp.pad(dck_sum.T, ((0, 0), (0, GATE_LANES - H)))
            df, dbf = _fgate_bwd(fr, bf_pad, dcum, name="d_fgate")
            dkvf = jnp.concatenate([_from_heads(dk_sum), _from_heads(dv_sum), df], axis=1).astype(MXU_DT)
            dhk = _mm(dkvf, wkvf, name="d_kv_proj", out_dtype=F32, nt=True)
            gl["w_kvf", 0] = _mm_tn(hk, dkvf, name="dw_kv_proj", out_dtype=COMM_DT)[:, :w_kvf.shape[1]]
            dx, dg_kv = _rms_bwd(t["x_in"], g_kv, dhk, dx, name="d_kv_norm")
        if l > 0:
            started = start_grads("mix", l, mix_keys(l), gl)
            if len(pending) > 2:
                finish_grads(dx)

    _, dg_mem = _rms_bwd(mem0, g_mem, dmem_n, None, name="d_mem_norm")
    grads = {n: jnp.stack(v) for n, v in G.items()}
    grads.update(g_kv=dg_kv[0], b_f=dbf[0, :H], g_mem=dg_mem[0], g_final=dg_final[0], loss=loss_dev.reshape(1))
    gs8 = pk_small.pack({n: _to_shards(grads[n], ax) for n, ax in SMALL}, F32, (N_DEV,))
    last = start_grads("mix", 0, mix_keys(0), gl, [gs8, pk_rep.pack(grads, F32)], [True, False])
    while len(pending) > 1:
        finish_grads(dx)
    update_waited(after=last)
    extra = finish_grads(chain["w_down"][0])
    update_waited()
    no_state = dict(loss=jnp.zeros((1,), F32))
    for pk, recv, tag in ((pk_small, extra[0], "small"), (pk_rep, extra[1], "rep")):
        w32 = ws32 if tag == "small" else pk.pack({**W, **no_state}, F32)
        outs = _reduce_adam(recv, w32, pk.pack({**M1, **no_state}, F32), pk.pack({**V1, **no_state}, F32),
                            name=f"adam_{tag}")
        for kind, buf in zip(("grad", "delta", "new_m", "new_v"), outs):
            for n, a in pk.unpack(buf).items():
                res[kind, n] = a
    for n, outs in chain.items():
        for kind, a in zip(("grad", "delta", "new_m", "new_v"), outs):
            res[kind, n] = a.reshape(W[n].shape)

    loss = res["grad", "loss"][0]
    return (loss, dx[None], *[res[kind, n] for kind in ("grad", "delta", "new_m", "new_v") for n in WEIGHTS])
```

```python
import jax
import jax.numpy as jnp
from jax import lax
from jax.experimental import pallas as pl
from jax.experimental.pallas import tpu as pltpu

F32 = jnp.float32
MXU_DT = jnp.bfloat16
COMM_DT = jnp.bfloat16

N_DEV = 8
HEAD_DIM = 64
RMS_EPS = 1e-6
LN_EPS = 1e-5
ADAM_LR = 0.001
ADAM_B1 = 0.9
ADAM_B2 = 0.999
ADAM_EPS = 1e-08
ADAM_WD = 0.01
ADAM_STEP = 10

PACK_LANES = 1024
GATE_LANES = 128
VMEM_LIMIT_V7X = 56 << 20
NEG = -1e30
MESH = pl.DeviceIdType.MESH
ANY = pl.BlockSpec(memory_space=pl.ANY)
NT = (((1,), (1,)), ((), ()))
NN = (((1,), (0,)), ((), ()))

BIG = (("w_in_a", 2), ("w_kvf", 0), ("w_in_b", 1), ("w_mem_kv", 1), ("w_out", 1), ("w_up", 2), ("w_down", 1))
SMALL = (("b_glu", 1), ("w_dw_a", 2), ("b_dw_a", 1), ("ln_g", 1), ("ln_b", 1), ("w_dw_f", 2))
REP = ("g_mix", "g_kv", "b_f", "g_mem", "g_ffn", "b_dw_f", "g_final")
WEIGHTS = ("g_mix", "w_in_a", "b_glu", "w_dw_a", "b_dw_a", "ln_g", "ln_b", "g_kv", "w_kvf", "b_f", "w_in_b",
           "g_mem", "w_mem_kv", "w_out", "g_ffn", "w_up", "w_dw_f", "b_dw_f", "w_down", "g_final")


def _sds(shape, dtype):
    return jax.ShapeDtypeStruct(tuple(shape), dtype)


def _call(body, *, name, out_shape, grid=(), in_specs=None, out_specs=None, scratch_shapes=()):
    params = dict(vmem_limit_bytes=VMEM_LIMIT_V7X)
    if grid:
        params["dimension_semantics"] = ("arbitrary",) * len(grid)
    kw = {}
    if in_specs is not None:
        kw["in_specs"] = in_specs
    if out_specs is not None:
        kw["out_specs"] = out_specs
    return pl.pallas_call(body, name=name, grid=grid, out_shape=out_shape, scratch_shapes=list(scratch_shapes),
                          compiler_params=pltpu.CompilerParams(**params), **kw)


def _res(shape):
    nd = len(shape)
    return pl.BlockSpec(tuple(shape), lambda *_: (0,) * nd)


def _colblk(rows, tc, off=0):
    return pl.BlockSpec((rows, tc), lambda j: (0, j + off))


def _rowblk(tm, cols, off=0):
    return pl.BlockSpec((tm, cols), lambda i: (i, off))


def _pick(n, opts=(512, 256, 128)):
    for t in opts:
        if n % t == 0:
            return t
    return n


def _sig(z):
    return 1.0 / (1.0 + jnp.exp(-z))


def _dot(a, b, dims=NN):
    return lax.dot_general(a, b, dims, preferred_element_type=F32)


def _dot_tn(a, b):
    return _dot(a.T.astype(b.dtype), b)


def _mm_tn(a, b, *, name, out_dtype):
    S, K = a.shape
    N = b.shape[1]
    tk, rc = _pick(K, (256, 128)), _pick(S)
    cast = b.dtype != MXU_DT

    def body(a_ref, b_ref, o_ref, acc_ref, *bb):
        if cast:
            @pl.when(pl.program_id(0) == 0)
            def _():
                for r0 in range(0, S, rc):
                    bb[0][r0:r0 + rc, :] = b_ref[r0:r0 + rc, :].astype(MXU_DT)
            b_ref = bb[0]
        for n, r0 in enumerate(range(0, S, rc)):
            part = _dot(a_ref[r0:r0 + rc, :].astype(F32).T.astype(MXU_DT), b_ref[r0:r0 + rc, :])
            if n == 0:
                acc_ref[...] = part
            else:
                acc_ref[...] += part
        o_ref[...] = acc_ref[...].astype(out_dtype)

    return _call(body, name=name, grid=(K // tk,), in_specs=[_colblk(S, tk), _res((S, N))],
                 out_specs=pl.BlockSpec((tk, N), lambda j: (j, 0)), out_shape=_sds((K, N), out_dtype),
                 scratch_shapes=[pltpu.VMEM((tk, N), F32)] + ([pltpu.VMEM((S, N), MXU_DT)] if cast else []))(a, b)


def _mm_nn2(a1, a2, w, *, name):
    M, Fw = a1.shape
    N = w.shape[1]
    tm, nc = _pick(M), _pick(N)

    def body(a1_ref, a2_ref, w_ref, o_ref):
        v1, v2 = a1_ref[...], a2_ref[...]
        for n0 in range(0, N, nc):
            o_ref[:, n0:n0 + nc] = _dot(v1, w_ref[0:Fw, n0:n0 + nc]) + _dot(v2, w_ref[Fw:2 * Fw, n0:n0 + nc])

    return _call(body, name=name, grid=(M // tm,), in_specs=[_rowblk(tm, Fw), _rowblk(tm, Fw), _res(w.shape)],
                 out_specs=_rowblk(tm, N), out_shape=_sds((M, N), F32))(a1, a2, w)


def _mm(a, w, *, name, out_dtype, nt=False, add=None, cols=None, after=None):
    M, K = a.shape
    N = w.shape[0] if nt else w.shape[1]
    assert (w.shape[1] if nt else w.shape[0]) == K
    dims = NT if nt else NN
    has_add, has_after = add is not None, after is not None
    if cols is not None:
        assert not has_add and not has_after
        c0, N = cols
        tn, rc = _pick(N), _pick(M)
        assert c0 % tn == 0

        def body(a_ref, w_ref, o_ref):
            wv = w_ref[...]
            for r0 in range(0, M, rc):
                o_ref[r0:r0 + rc, :] = _dot(a_ref[r0:r0 + rc, :], wv, dims).astype(out_dtype)

        w_spec = (pl.BlockSpec((tn, K), lambda j: (j + c0 // tn, 0)) if nt else _colblk(K, tn, c0 // tn))
        return _call(body, name=name, grid=(N // tn,), in_specs=[_res((M, K)), w_spec],
                     out_specs=_colblk(M, tn), out_shape=_sds((M, N), out_dtype))(a, w)
    tn, rc = _pick(N), _pick(M)
    cast = a.dtype != MXU_DT

    def body(*refs):
        a_ref, w_ref, o_ref = refs[0], refs[1], refs[2 + has_add + has_after]
        if cast:
            @pl.when(pl.program_id(0) == 0)
            def _():
                for r0 in range(0, M, rc):
                    refs[-1][r0:r0 + rc, :] = a_ref[r0:r0 + rc, :].astype(MXU_DT)
            a_ref = refs[-1]
        wv = w_ref[...]
        for r0 in range(0, M, rc):
            acc = _dot(a_ref[r0:r0 + rc, :], wv, dims)
            if has_add:
                acc = acc + refs[2][r0:r0 + rc, :]
            o_ref[r0:r0 + rc, :] = acc.astype(out_dtype)

    w_spec = pl.BlockSpec((tn, K), lambda j: (j, 0)) if nt else pl.BlockSpec((K, tn), lambda j: (0, j))
    in_specs = ([_res((M, K)), w_spec] + ([_colblk(M, tn)] if has_add else [])
                + ([_res(after.shape)] if has_after else []))
    args = (a, w) + ((add,) if has_add else ()) + ((after,) if has_after else ())
    return _call(body, name=name, grid=(N // tn,), in_specs=in_specs, out_specs=_colblk(M, tn),
                 out_shape=_sds((M, N), out_dtype),
                 scratch_shapes=[pltpu.VMEM((M, K), MXU_DT)] if cast else [])(*args)


def _rms_fwd(x, g, *, name):
    M, D = x.shape
    tm = _pick(M, (256, 128))

    def body(x_ref, g_ref, h_ref):
        xf = x_ref[...]
        r = lax.rsqrt(jnp.mean(xf * xf, axis=-1, keepdims=True) + RMS_EPS)
        h_ref[...] = ((xf * r) * g_ref[...]).astype(MXU_DT)

    return _call(body, name=name, grid=(M // tm,), in_specs=[_rowblk(tm, D), _res((1, D))],
                 out_specs=_rowblk(tm, D), out_shape=_sds((M, D), MXU_DT))(x, g.reshape(1, D))


def _rms_bwd(x, g, dh, dx_in, *, name):
    M, D = x.shape
    tm = _pick(M, (256, 128))
    with_dx = dx_in is not None

    def body(*refs):
        if with_dx:
            x_ref, g_ref, dh_ref, dxin_ref, dx_ref, dg_ref = refs
        else:
            x_ref, g_ref, dh_ref, dg_ref = refs
        i = pl.program_id(0)
        xf = x_ref[...]
        r = lax.rsqrt(jnp.mean(xf * xf, axis=-1, keepdims=True) + RMS_EPS)
        y = xf * r
        dh_v = dh_ref[...]
        if with_dx:
            dy = dh_v * g_ref[...]
            dx_ref[...] = dxin_ref[...] + r * (dy - y * jnp.mean(dy * y, axis=-1, keepdims=True))
        part = jnp.sum(dh_v * y, axis=0, keepdims=True)

        @pl.when(i == 0)
        def _():
            dg_ref[...] = part

        @pl.when(i > 0)
        def _():
            dg_ref[...] += part

    ins = [x, g.reshape(1, D), dh] + ([dx_in] if with_dx else [])
    in_specs = [_rowblk(tm, D), _res((1, D)), _rowblk(tm, D)] + ([_rowblk(tm, D)] if with_dx else [])
    if with_dx:
        out_specs, out_shape = [_rowblk(tm, D), _res((1, D))], [_sds((M, D), F32), _sds((1, D), F32)]
    else:
        out_specs, out_shape = [_res((1, D))], [_sds((1, D), F32)]
    out = _call(body, name=name, grid=(M // tm,), in_specs=in_specs, out_specs=out_specs, out_shape=out_shape)(*ins)
    return out if with_dx else (None, out[0])


def _loss_bwd(x, g, t, *, name):
    M, D = x.shape
    tm = _pick(M, (256, 128))

    def body(x_ref, g_ref, t_ref, dx_ref, dg_ref, ls_ref):
        i = pl.program_id(0)
        xf = x_ref[...]
        r = lax.rsqrt(jnp.mean(xf * xf, axis=-1, keepdims=True) + RMS_EPS)
        xr = xf * r
        e = xr * g_ref[...] - t_ref[...]
        dout = e * (1.0 / D)
        dy = dout * g_ref[...]
        dx_ref[...] = r * (dy - xr * jnp.mean(dy * xr, axis=-1, keepdims=True))
        part = jnp.sum(dout * xr, axis=0, keepdims=True)
        lpart = jnp.zeros(ls_ref.shape, F32) + (0.5 / D) * jnp.sum(e * e, keepdims=True)

        @pl.when(i == 0)
        def _():
            dg_ref[...] = part
            ls_ref[...] = lpart

        @pl.when(i > 0)
        def _():
            dg_ref[...] += part
            ls_ref[...] += lpart

    dx, dg, ls = _call(body, name=name, grid=(M // tm,),
                       in_specs=[_rowblk(tm, D), _res((1, D)), _rowblk(tm, D)],
                       out_specs=[_rowblk(tm, D), _res((1, D)), _res((8, 128))],
                       out_shape=[_sds((M, D), F32), _sds((1, D), F32), _sds((8, 128), F32)])(x, g.reshape(1, D), t)
    return ls[0, 0], dx, dg


def _shift_rows(ext, off, rows):
    if off % 8 == 0:
        return ext[off:off + rows, :]
    return pltpu.roll(ext, ext.shape[0] - off, 0)[0:rows, :]


def _ext(pad_ref, c, rows, halo):
    return pad_ref[pl.ds(pl.multiple_of(c * rows, rows), rows + halo), :]


def _conv_chunk(pad_ref, c, rows, halo, w_ref, taps):
    ext = _ext(pad_ref, c, rows, halo)
    acc = None
    for k in range(taps):
        term = w_ref[k:k + 1, :] * _shift_rows(ext, halo - (taps - 1) + k, rows)
        acc = term if acc is None else acc + term
    return acc


def _conv_t_chunk(pad_ref, c, rows, halo, w_ref, taps):
    ext = _ext(pad_ref, c, rows, halo)
    acc = None
    for k in range(taps):
        term = w_ref[k:k + 1, :] * _shift_rows(ext, taps - 1 - k, rows)
        acc = term if acc is None else acc + term
    return acc


def _conv_wgrad_chunk(pad_ref, c, rows, halo, dy, dw_ref, taps):
    ext = _ext(pad_ref, c, rows, halo)
    for k in range(taps):
        dw_ref[k:k + 1, :] += jnp.sum(dy * _shift_rows(ext, halo - (taps - 1) + k, rows), axis=0, keepdims=True)


A_HALO, F_HALO = 32, 8


def _glu_conv_fwd(p, b_glu, w_dw, b_dw, *, name):
    S = p.shape[0]
    taps, C = w_dw.shape
    tc, rows = 128, _pick(S, (256, 128))
    nb, nch = C // tc, S // rows

    def body(a_ref, g_ref, ba_ref, bg_ref, w_ref, bd_ref, o_ref, pad_ref):
        pad_ref[0:A_HALO, :] = jnp.zeros((A_HALO, tc), F32)

        def fill(c, _):
            r = pl.ds(pl.multiple_of(c * rows, rows), rows)
            v1 = (a_ref[r, :] + ba_ref[...]) * _sig(g_ref[r, :] + bg_ref[...])
            pad_ref[pl.ds(pl.multiple_of(A_HALO + c * rows, 8), rows), :] = v1
            return 0

        lax.fori_loop(0, nch, fill, 0)

        def conv(c, _):
            o_ref[pl.ds(pl.multiple_of(c * rows, rows), rows), :] = (
                _conv_chunk(pad_ref, c, rows, A_HALO, w_ref, taps) + bd_ref[...])
            return 0

        lax.fori_loop(0, nch, conv, 0)

    b2 = b_glu.reshape(1, 2 * C)
    return _call(body, name=name, grid=(nb,),
                 in_specs=[_colblk(S, tc), _colblk(S, tc, nb), _colblk(1, tc), _colblk(1, tc, nb),
                           _colblk(taps, tc), _colblk(1, tc)],
                 out_specs=_colblk(S, tc), out_shape=_sds((S, C), F32),
                 scratch_shapes=[pltpu.VMEM((S + A_HALO, tc), F32)])(p, p, b2, b2, w_dw, b_dw.reshape(1, C))


def _glu_conv_bwd(p, b_glu, w_dw, dv2, *, name):
    S = p.shape[0]
    taps, C = w_dw.shape
    tc, rows = 128, _pick(S, (256, 128))
    nb, nch = C // tc, S // rows

    def body(a_ref, g_ref, ba_ref, bg_ref, w_ref, dy_ref, da_ref, dgt_ref, dw_ref, dbd_ref, dba_ref, dbg_ref,
             padx_ref, pady_ref):
        padx_ref[0:A_HALO, :] = jnp.zeros((A_HALO, tc), F32)
        pady_ref[S:S + A_HALO, :] = jnp.zeros((A_HALO, tc), F32)
        dw_ref[...] = jnp.zeros((taps, tc), F32)

        def fill(c, _):
            r = pl.ds(pl.multiple_of(c * rows, rows), rows)
            v1 = (a_ref[r, :] + ba_ref[...]) * _sig(g_ref[r, :] + bg_ref[...])
            padx_ref[pl.ds(pl.multiple_of(A_HALO + c * rows, 8), rows), :] = v1
            pady_ref[r, :] = dy_ref[r, :]
            return 0

        lax.fori_loop(0, nch, fill, 0)

        def back(c, carry):
            sd, sa, sg = carry
            r = pl.ds(pl.multiple_of(c * rows, rows), rows)
            dy = dy_ref[r, :]
            _conv_wgrad_chunk(padx_ref, c, rows, A_HALO, dy, dw_ref, taps)
            dv1 = _conv_t_chunk(pady_ref, c, rows, A_HALO, w_ref, taps)
            a = a_ref[r, :] + ba_ref[...]
            s = _sig(g_ref[r, :] + bg_ref[...])
            da = dv1 * s
            dgt = dv1 * a * s * (1.0 - s)
            da_ref[r, :] = da
            dgt_ref[r, :] = dgt
            return (sd + jnp.sum(dy, axis=0, keepdims=True), sa + jnp.sum(da, axis=0, keepdims=True),
                    sg + jnp.sum(dgt, axis=0, keepdims=True))

        z = jnp.zeros((1, tc), F32)
        sd, sa, sg = lax.fori_loop(0, nch, back, (z, z, z))
        dbd_ref[...] = sd
        dba_ref[...] = sa
        dbg_ref[...] = sg

    b2 = b_glu.reshape(1, 2 * C)
    return _call(body, name=name, grid=(nb,),
                 in_specs=[_colblk(S, tc), _colblk(S, tc, nb), _colblk(1, tc), _colblk(1, tc, nb),
                           _colblk(taps, tc), _colblk(S, tc)],
                 out_specs=[_colblk(S, tc), _colblk(S, tc), _colblk(taps, tc), _colblk(1, tc), _colblk(1, tc),
                            _colblk(1, tc)],
                 out_shape=[_sds((S, C), F32), _sds((S, C), F32), _sds((taps, C), F32), _sds((1, C), F32),
                            _sds((1, C), F32), _sds((1, C), F32)],
                 scratch_shapes=[pltpu.VMEM((S + A_HALO, tc), F32), pltpu.VMEM((S + A_HALO, tc), F32)])(
                     p, p, b2, b2, w_dw, dv2)


def _ln_silu_cat_fwd(v2, ln_g, ln_b, memo, *, name):
    S, C = v2.shape
    Mw = memo.shape[1]
    tm = _pick(S, (256, 128))

    def body(v_ref, g_ref, b_ref, m_ref, o_ref):
        v = v_ref[...]
        mu = jnp.mean(v, axis=-1, keepdims=True)
        d = v - mu
        y = d * lax.rsqrt(jnp.mean(d * d, axis=-1, keepdims=True) + LN_EPS) * g_ref[...] + b_ref[...]
        o_ref[:, 0:C] = (y * _sig(y)).astype(MXU_DT)
        o_ref[:, C:C + Mw] = m_ref[...].astype(MXU_DT)

    return _call(body, name=name, grid=(S // tm,),
                 in_specs=[_rowblk(tm, C), _res((1, C)), _res((1, C)), _rowblk(tm, Mw)],
                 out_specs=_rowblk(tm, C + Mw), out_shape=_sds((S, C + Mw), MXU_DT))(
                     v2, ln_g.reshape(1, C), ln_b.reshape(1, C), memo)


def _ln_silu_bwd(v2, ln_g, ln_b, dcat, *, name):
    S, C = v2.shape
    tm = _pick(S, (256, 128))

    def body(v_ref, g_ref, b_ref, dm_ref, dv_ref, dg_ref, db_ref):
        i = pl.program_id(0)
        v = v_ref[...]
        mu = jnp.mean(v, axis=-1, keepdims=True)
        d = v - mu
        rstd = lax.rsqrt(jnp.mean(d * d, axis=-1, keepdims=True) + LN_EPS)
        xh = d * rstd
        y = xh * g_ref[...] + b_ref[...]
        s = _sig(y)
        dyv = dm_ref[...] * (s * (1.0 + y * (1.0 - s)))
        dxh = dyv * g_ref[...]
        dv_ref[...] = rstd * (dxh - jnp.mean(dxh, axis=-1, keepdims=True)
                              - xh * jnp.mean(dxh * xh, axis=-1, keepdims=True))
        pg = jnp.sum(dyv * xh, axis=0, keepdims=True)
        pb = jnp.sum(dyv, axis=0, keepdims=True)

        @pl.when(i == 0)
        def _():
            dg_ref[...] = pg
            db_ref[...] = pb

        @pl.when(i > 0)
        def _():
            dg_ref[...] += pg
            db_ref[...] += pb

    return _call(body, name=name, grid=(S // tm,),
                 in_specs=[_rowblk(tm, C), _res((1, C)), _res((1, C)), _rowblk(tm, C)],
                 out_specs=[_rowblk(tm, C), _res((1, C)), _res((1, C))],
                 out_shape=[_sds((S, C), F32), _sds((1, C), F32), _sds((1, C), F32)])(
                     v2, ln_g.reshape(1, C), ln_b.reshape(1, C), dcat)


def _ffn_act_fwd(ug, uv, w_dw, b_dw, *, name):
    S, Fw = ug.shape
    taps = w_dw.shape[0]
    tc, rows = _pick(Fw, (256, 128)), _pick(S, (256, 128))
    nb, nch = Fw // tc, S // rows

    def body(ug_ref, uv_ref, wg_ref, wv_ref, bg_ref, bv_ref, o_ref, pg_ref, pv_ref):
        pg_ref[0:F_HALO, :] = jnp.zeros((F_HALO, tc), F32)
        pv_ref[0:F_HALO, :] = jnp.zeros((F_HALO, tc), F32)
        pg_ref[F_HALO:F_HALO + S, :] = ug_ref[...]
        pv_ref[F_HALO:F_HALO + S, :] = uv_ref[...]

        def act(c, _):
            gc = _conv_chunk(pg_ref, c, rows, F_HALO, wg_ref, taps) + bg_ref[...]
            vc = _conv_chunk(pv_ref, c, rows, F_HALO, wv_ref, taps) + bv_ref[...]
            o_ref[pl.ds(pl.multiple_of(c * rows, rows), rows), :] = (gc * _sig(gc) * vc).astype(MXU_DT)
            return 0

        lax.fori_loop(0, nch, act, 0)

    b2 = b_dw.reshape(1, 2 * Fw)
    return _call(body, name=name, grid=(nb,),
                 in_specs=[_colblk(S, tc), _colblk(S, tc), _colblk(taps, tc), _colblk(taps, tc, nb),
                           _colblk(1, tc), _colblk(1, tc, nb)],
                 out_specs=_colblk(S, tc), out_shape=_sds((S, Fw), MXU_DT),
                 scratch_shapes=[pltpu.VMEM((S + F_HALO, tc), F32), pltpu.VMEM((S + F_HALO, tc), F32)])(
                     ug, uv, w_dw, w_dw, b2, b2)


def _ffn_act_bwd(ug, uv, dact, w_dw, b_dw, *, name):
    S, Fw = ug.shape
    taps = w_dw.shape[0]
    tc, rows = _pick(Fw, (256, 128)), _pick(S, (256, 128))
    nb, nch = Fw // tc, S // rows

    def body(ug_ref, uv_ref, da_ref, wg_ref, wv_ref, bg_ref, bv_ref, dug_ref, duv_ref, dwg_ref, dwv_ref,
             dbg_ref, dbv_ref, pg_ref, pv_ref, qg_ref, qv_ref):
        pg_ref[0:F_HALO, :] = jnp.zeros((F_HALO, tc), F32)
        pv_ref[0:F_HALO, :] = jnp.zeros((F_HALO, tc), F32)
        qg_ref[S:S + F_HALO, :] = jnp.zeros((F_HALO, tc), F32)
        qv_ref[S:S + F_HALO, :] = jnp.zeros((F_HALO, tc), F32)
        pg_ref[F_HALO:F_HALO + S, :] = ug_ref[...]
        pv_ref[F_HALO:F_HALO + S, :] = uv_ref[...]
        dwg_ref[...] = jnp.zeros((taps, tc), F32)
        dwv_ref[...] = jnp.zeros((taps, tc), F32)

        def grads(c, carry):
            sg, sv = carry
            r = pl.ds(pl.multiple_of(c * rows, rows), rows)
            gc = _conv_chunk(pg_ref, c, rows, F_HALO, wg_ref, taps) + bg_ref[...]
            vc = _conv_chunk(pv_ref, c, rows, F_HALO, wv_ref, taps) + bv_ref[...]
            s = _sig(gc)
            da = da_ref[r, :]
            dgc = da * vc * (s * (1.0 + gc * (1.0 - s)))
            dvc = da * (gc * s)
            qg_ref[r, :] = dgc
            qv_ref[r, :] = dvc
            _conv_wgrad_chunk(pg_ref, c, rows, F_HALO, dgc, dwg_ref, taps)
            _conv_wgrad_chunk(pv_ref, c, rows, F_HALO, dvc, dwv_ref, taps)
            return sg + jnp.sum(dgc, axis=0, keepdims=True), sv + jnp.sum(dvc, axis=0, keepdims=True)

        z = jnp.zeros((1, tc), F32)
        sg, sv = lax.fori_loop(0, nch, grads, (z, z))
        dbg_ref[...] = sg
        dbv_ref[...] = sv

        def back(c, _):
            r = pl.ds(pl.multiple_of(c * rows, rows), rows)
            dug_ref[r, :] = _conv_t_chunk(qg_ref, c, rows, F_HALO, wg_ref, taps).astype(MXU_DT)
            duv_ref[r, :] = _conv_t_chunk(qv_ref, c, rows, F_HALO, wv_ref, taps).astype(MXU_DT)
            return 0

        lax.fori_loop(0, nch, back, 0)

    b2 = b_dw.reshape(1, 2 * Fw)
    pad = pltpu.VMEM((S + F_HALO, tc), F32)
    dug, duv, dwg, dwv, dbg, dbv = _call(
        body, name=name, grid=(nb,),
        in_specs=[_colblk(S, tc), _colblk(S, tc), _colblk(S, tc), _colblk(taps, tc), _colblk(taps, tc, nb),
                  _colblk(1, tc), _colblk(1, tc, nb)],
        out_specs=[_colblk(S, tc), _colblk(S, tc), _colblk(taps, tc), _colblk(taps, tc), _colblk(1, tc),
                   _colblk(1, tc)],
        out_shape=[_sds((S, Fw), MXU_DT), _sds((S, Fw), MXU_DT), _sds((taps, Fw), F32), _sds((taps, Fw), F32),
                   _sds((1, Fw), F32), _sds((1, Fw), F32)],
        scratch_shapes=[pad, pad, pad, pad])(ug, uv, dact, w_dw, w_dw, b2, b2)
    return dug, duv, jnp.concatenate([dwg, dwv], axis=1), jnp.concatenate([dbg, dbv], axis=1)


def _head_mask(h, width):
    lane = lax.broadcasted_iota(jnp.int32, (1, width), 1)
    return (lane >= h * HEAD_DIM) & (lane < (h + 1) * HEAD_DIM)


def _mem_attn_fwd(p, qblk, mkv, l, *, Mw, name):
    S, ML = p.shape[0], mkv.shape[0]
    tm, nh, scale = _pick(S, (256, 128)), Mw // HEAD_DIM, HEAD_DIM ** -0.5

    def body(q_ref, k_ref, v_ref, o_ref):
        q, kv, vv = q_ref[...], k_ref[...], v_ref[...]
        out = jnp.zeros((tm, Mw), F32)
        for h in range(nh):
            mk = _head_mask(h, Mw)
            s = _dot(jnp.where(mk, q, 0.0).astype(MXU_DT), kv, NT) * scale
            e = jnp.exp(s - jnp.max(s, axis=-1, keepdims=True))
            pr = e / jnp.sum(e, axis=-1, keepdims=True)
            out = out + _dot(pr.astype(MXU_DT), jnp.where(mk, vv, jnp.zeros_like(vv)))
        o_ref[...] = out

    return _call(body, name=name, grid=(S // tm,),
                 in_specs=[_rowblk(tm, Mw, qblk), pl.BlockSpec((ML, Mw), lambda i: (0, 2 * l)),
                           pl.BlockSpec((ML, Mw), lambda i: (0, 2 * l + 1))],
                 out_specs=_rowblk(tm, Mw), out_shape=_sds((S, Mw), F32))(p, mkv, mkv)


def _mem_attn_bwd(p, qblk, mkv, l, dcat, doblk, *, Mw, name):
    S, ML = p.shape[0], mkv.shape[0]
    tm, nh, scale = _pick(S, (256, 128)), Mw // HEAD_DIM, HEAD_DIM ** -0.5

    def body(q_ref, k_ref, v_ref, do_ref, dq_ref, dk_ref, dv_ref):
        i = pl.program_id(0)

        @pl.when(i == 0)
        def _():
            dk_ref[...] = jnp.zeros((ML, Mw), F32)
            dv_ref[...] = jnp.zeros((ML, Mw), F32)

        q, kv, vv, do = q_ref[...], k_ref[...], v_ref[...], do_ref[...]
        dq = jnp.zeros((tm, Mw), F32)
        for h in range(nh):
            mk = _head_mask(h, Mw)
            qh = jnp.where(mk, q, 0.0).astype(MXU_DT)
            s = _dot(qh, kv, NT) * scale
            e = jnp.exp(s - jnp.max(s, axis=-1, keepdims=True))
            pr = e / jnp.sum(e, axis=-1, keepdims=True)
            doh = jnp.where(mk, do, 0.0).astype(MXU_DT)
            dv_ref[...] += _dot_tn(pr, doh)
            dp = _dot(doh, vv, NT)
            ds = pr * (dp - jnp.sum(dp * pr, axis=-1, keepdims=True))
            dq = dq + _dot(ds.astype(MXU_DT), jnp.where(mk, kv, jnp.zeros_like(kv))) * scale
            dk_ref[...] += _dot_tn(ds, qh) * scale
        dq_ref[...] = dq

    return _call(body, name=name, grid=(S // tm,),
                 in_specs=[_rowblk(tm, Mw, qblk), pl.BlockSpec((ML, Mw), lambda i: (0, 2 * l)),
                           pl.BlockSpec((ML, Mw), lambda i: (0, 2 * l + 1)), _rowblk(tm, Mw, doblk)],
                 out_specs=[_rowblk(tm, Mw), _res((ML, Mw)), _res((ML, Mw))],
                 out_shape=[_sds((S, Mw), F32), _sds((ML, Mw), F32), _sds((ML, Mw), F32)])(p, mkv, mkv, dcat)


FOX_GROUP = 3


def _foxt_specs(S, dh, tq):
    nb, G = S // tq, FOX_GROUP
    rows = pl.BlockSpec((G, tq, dh), lambda h, i: (h, i, 0))
    seq = pl.BlockSpec((G, S, dh), lambda h, i: (h, 0, 0))
    seq_t = pl.BlockSpec((G, nb, dh, tq), lambda h, i: (h, 0, 0, 0))
    blk_t = pl.BlockSpec((G, dh, tq), lambda h, i: (h, 0, i))
    col = pl.BlockSpec((G, S, 1), lambda h, i: (h, 0, 0))
    row_all = pl.BlockSpec((G, nb, 1, tq), lambda h, i: (h, 0, 0, 0))
    row = pl.BlockSpec((G, 1, 1, tq), lambda h, i: (h, i, 0, 0))
    return rows, seq, seq_t, blk_t, col, row_all, row


def _foxt_logits(kv, qv, cq_row, ck_col, scale, diag):
    s = _dot(kv, qv, NT) * scale + cq_row - ck_col
    if not diag:
        return s
    keys = lax.broadcasted_iota(jnp.int32, s.shape, 0)
    queries = lax.broadcasted_iota(jnp.int32, s.shape, 1)
    return jnp.where(keys <= queries, s, NEG)


def _foxt_fwd(q, k, vt, cq_row, ck_col, *, tq, name):
    H, S, dh = q.shape
    nb, scale, G = S // tq, dh ** -0.5, FOX_GROUP
    assert H % G == 0
    rows, seq, seq_t, blk_t, col, row_all, row = _foxt_specs(S, dh, tq)

    def body(q_ref, k_ref, vt_ref, cq_ref, ck_ref, o_ref, lse_ref):
        i = pl.program_id(1)
        qv, cqv = [q_ref[e] for e in range(G)], [cq_ref[e, i] for e in range(G)]

        def kblock(j, carry, diag):
            r = pl.ds(pl.multiple_of(j * tq, tq), tq)
            out = []
            for e in range(G):
                m, l, acc = carry[e]
                s = _foxt_logits(k_ref[e, r, :], qv[e], cqv[e], ck_ref[e, r, :], scale, diag)
                m2 = jnp.maximum(m, jnp.max(s, axis=0, keepdims=True))
                pr = jnp.exp(s - m2)
                al = jnp.exp(m - m2)
                out.append((m2, al * l + jnp.sum(pr, axis=0, keepdims=True),
                            al * acc + _dot(vt_ref[e, j], pr.astype(MXU_DT))))
            return tuple(out)

        init = tuple((jnp.full((1, tq), NEG, F32), jnp.zeros((1, tq), F32), jnp.zeros((dh, tq), F32))
                     for _ in range(G))
        carry = lax.fori_loop(0, i, lambda j, c: kblock(j, c, False), init)
        for e, (m, l, acc) in enumerate(kblock(i, carry, True)):
            o_ref[e] = acc / l
            lse_ref[e, 0] = m + jnp.log(l)

    return _call(body, name=name, grid=(H // G, nb), in_specs=[rows, seq, seq_t, row_all, col],
                 out_specs=[blk_t, row],
                 out_shape=[_sds((H, dh, S), F32), _sds((H, nb, 1, tq), F32)])(q, k, vt, cq_row, ck_col)


def _foxt_bwd(q, k, kt, v, cq_row, ck_col, ot, lse, dot_, *, tq, name):
    H, S, dh = q.shape
    nb, scale, G = S // tq, dh ** -0.5, FOX_GROUP
    rows, seq, seq_t, blk_t, col, row_all, row = _foxt_specs(S, dh, tq)

    def body(q_ref, k_ref, kt_ref, v_ref, cq_ref, ck_ref, ot_ref, lse_ref, dot_ref, dq_ref, dk_ref, dv_ref,
             dcq_ref, dck_ref):
        i = pl.program_id(1)

        @pl.when(i == 0)
        def _():
            dk_ref[...] = jnp.zeros((G, S, dh), F32)
            dv_ref[...] = jnp.zeros((G, S, dh), F32)
            dck_ref[...] = jnp.zeros((G, S, 1), F32)

        qv, cqv, lsev = [q_ref[e] for e in range(G)], [cq_ref[e, i] for e in range(G)], [lse_ref[e, 0] for e in range(G)]
        dob = [dot_ref[e].astype(MXU_DT) for e in range(G)]
        delta = [jnp.sum(dob[e].astype(F32) * ot_ref[e], axis=0, keepdims=True) for e in range(G)]

        def kblock(j, carry, diag):
            r = pl.ds(pl.multiple_of(j * tq, tq), tq)
            out = []
            for e in range(G):
                dq, rs = carry[e]
                pr = jnp.exp(_foxt_logits(k_ref[e, r, :], qv[e], cqv[e], ck_ref[e, r, :], scale, diag) - lsev[e])
                ds = pr * (_dot(v_ref[e, r, :], dob[e]) - delta[e])
                dsb = ds.astype(MXU_DT)
                dk_ref[e, r, :] += _dot(dsb, qv[e]) * scale
                dv_ref[e, r, :] += _dot(pr.astype(MXU_DT), dob[e], NT)
                dck_ref[e, r, :] += -jnp.sum(ds, axis=1, keepdims=True)
                out.append((dq + _dot(kt_ref[e, j], dsb), rs + jnp.sum(ds, axis=0, keepdims=True)))
            return tuple(out)

        init = tuple((jnp.zeros((dh, tq), F32), jnp.zeros((1, tq), F32)) for _ in range(G))
        carry = lax.fori_loop(0, i, lambda j, c: kblock(j, c, False), init)
        for e, (dq, rs) in enumerate(kblock(i, carry, True)):
            dq_ref[e] = dq * scale
            dcq_ref[e, 0] = rs

    return _call(body, name=name, grid=(H // G, nb),
                 in_specs=[rows, seq, seq_t, seq, row_all, col, blk_t, row, blk_t],
                 out_specs=[blk_t, seq, seq, row, col],
                 out_shape=[_sds((H, dh, S), F32), _sds((H, S, dh), F32), _sds((H, S, dh), F32),
                            _sds((H, nb, 1, tq), F32), _sds((H, S, 1), F32)])(
                                q, k, kt, v, cq_row, ck_col, ot, lse, dot_)


def _tri(n, lower):
    r = lax.broadcasted_iota(jnp.int32, (n, n), 0)
    c = lax.broadcasted_iota(jnp.int32, (n, n), 1)
    return ((c <= r) if lower else (c >= r)).astype(F32)


def _fgate_fwd(fr, bf, *, name):
    S, W = fr.shape
    B = _pick(S, (256, 128))

    def body(f_ref, b_ref, cum_ref):
        L = _tri(B, True)
        carry = jnp.zeros((1, W), F32)
        for blk in range(S // B):
            z = f_ref[blk * B:(blk + 1) * B, :] + b_ref[...]
            ls = jnp.minimum(z, 0.0) - jnp.log(1.0 + jnp.exp(-jnp.abs(z)))
            cum_ref[blk * B:(blk + 1) * B, :] = jnp.dot(L, ls, precision=lax.Precision.HIGHEST,
                                                        preferred_element_type=F32) + carry
            carry = carry + jnp.sum(ls, axis=0, keepdims=True)

    return _call(body, name=name, out_shape=_sds((S, W), F32))(fr, bf)


def _fgate_bwd(fr, bf, dcum, *, name):
    S, W = fr.shape
    B = _pick(S, (256, 128))

    def body(f_ref, b_ref, dc_ref, df_ref, db_ref):
        U = _tri(B, False)
        carry = jnp.zeros((1, W), F32)
        dbs = jnp.zeros((1, W), F32)
        for blk in reversed(range(S // B)):
            dc = dc_ref[blk * B:(blk + 1) * B, :]
            dls = jnp.dot(U, dc, precision=lax.Precision.HIGHEST, preferred_element_type=F32) + carry
            carry = carry + jnp.sum(dc, axis=0, keepdims=True)
            z = f_ref[blk * B:(blk + 1) * B, :] + b_ref[...]
            df = dls * (1.0 / (1.0 + jnp.exp(z)))
            df_ref[blk * B:(blk + 1) * B, :] = df
            dbs = dbs + jnp.sum(df, axis=0, keepdims=True)
        db_ref[...] = dbs

    return _call(body, name=name, out_shape=[_sds((S, W), F32), _sds((1, W), F32)])(fr, bf, dcum)


def _flip(v, bit):
    return 1 - v if bit else v


HBM_SPEC = pl.BlockSpec(memory_space=pltpu.HBM)
SEM_SPEC = pl.BlockSpec(memory_space=pltpu.SEMAPHORE)


def _xchg_copies(src, land, sems, scatter):
    n = len(src)
    send, recv, loc = sems[:7 * n], sems[7 * n:14 * n], sems[14 * n:15 * n]
    x, y, c = lax.axis_index("x"), lax.axis_index("y"), lax.axis_index("c")
    me = 4 * x + 2 * y + c

    def peer(m):
        return _flip(x, m & 4), _flip(y, m & 2), _flip(c, m & 1)

    def copy(i, m):
        px, py, pc = peer(m)
        return pltpu.make_async_remote_copy(
            src_ref=src[i].at[4 * px + 2 * py + pc] if scatter[i] else src[i], dst_ref=land[i].at[me],
            send_sem=send[7 * i + m - 1], recv_sem=recv[7 * i + m - 1], device_id=(px, py, pc), device_id_type=MESH)

    def arrival(i, m):
        px, py, pc = peer(m)
        slot = land[i].at[4 * px + 2 * py + pc]
        return pltpu.make_async_remote_copy(src_ref=slot, dst_ref=slot, send_sem=send[7 * i + m - 1],
                                            recv_sem=recv[7 * i + m - 1], device_id=(px, py, pc), device_id_type=MESH)

    def own(i):
        return pltpu.make_async_copy(src[i].at[me] if scatter[i] else src[i], land[i].at[me], loc[i])

    return copy, arrival, own


def _xchg_start(srcs, scatter, *, name):
    n = len(srcs)
    lands = [_sds((N_DEV,) + s.shape[-2:], s.dtype) for s in srcs]

    ns = 15 * n

    def body(*refs):
        src, land, sems, token = refs[:n], refs[n:2 * n], refs[2 * n:2 * n + ns], refs[-1]
        copy, _, own = _xchg_copies(src, land, sems, scatter)
        for i in range(n):
            own(i).start()
            for m in range(1, N_DEV):
                copy(i, m).start()
        token[...] = jnp.zeros(token.shape, F32)

    thru = [pltpu.HBM(s.shape, s.dtype) for s in srcs] + [pltpu.HBM(s.shape, s.dtype) for s in lands]
    out = pl.pallas_call(
        body, name=name,
        out_shape=(*[pltpu.SemaphoreType.DMA(())] * ns, *thru, _sds((8, 128), F32)),
        in_specs=[HBM_SPEC] * (2 * n),
        out_specs=(*[SEM_SPEC] * ns, *[HBM_SPEC] * (2 * n), pl.BlockSpec(memory_space=pltpu.VMEM)),
        input_output_aliases={i: ns + i for i in range(2 * n)},
        compiler_params=pltpu.CompilerParams(has_side_effects=pltpu.SideEffectType.DATAFLOW_SIDE_EFFECTING),
    )(*[pltpu.with_memory_space_constraint(s, pltpu.HBM) for s in srcs],
      *[pltpu.with_memory_space_constraint(lax.empty(s.shape, s.dtype), pltpu.HBM) for s in lands])
    bufs = list(out[ns:ns + 2 * n])
    return (list(out[:ns]), bufs[:n], bufs[n:]), out[-1]


def _xchg_wait(handle, after, scatter, *, name):
    sems, srcs, lands = handle
    n = len(srcs)
    ns = 15 * n

    def body(*refs):
        src, land = refs[:n], refs[n:2 * n]
        copy, arrival, own = _xchg_copies(src, land, refs[2 * n:2 * n + ns], scatter)
        for i in range(n):
            own(i).wait()
            for m in range(1, N_DEV):
                copy(i, m).wait_send()
                arrival(i, m).wait_recv()

    out = pl.pallas_call(
        body, name=name,
        out_shape=tuple(pltpu.HBM(s.shape, s.dtype) for s in srcs + lands),
        in_specs=[HBM_SPEC] * (2 * n) + [SEM_SPEC] * ns + [ANY],
        out_specs=tuple([HBM_SPEC] * (2 * n)),
        input_output_aliases={i: i for i in range(2 * n)},
        compiler_params=pltpu.CompilerParams(has_side_effects=pltpu.SideEffectType.DATAFLOW_SIDE_EFFECTING),
    )(*srcs, *lands, *sems, after)
    return list(out[n:])


def _reduce_adam_body(r_ref, w_ref, m_ref, v_ref, g_ref, d_ref, m2_ref, v2_ref):
    g = r_ref[0].astype(F32)
    for s in range(1, N_DEV):
        g = g + r_ref[s].astype(F32)
    mm = ADAM_B1 * m_ref[...] + (1.0 - ADAM_B1) * g
    vv = ADAM_B2 * v_ref[...] + (1.0 - ADAM_B2) * (g * g)
    m_hat = mm / (1.0 - ADAM_B1 ** ADAM_STEP)
    v_hat = vv / (1.0 - ADAM_B2 ** ADAM_STEP)
    g_ref[...] = g
    d_ref[...] = -ADAM_LR * (m_hat / (jnp.sqrt(v_hat) + ADAM_EPS) + ADAM_WD * w_ref[...])
    m2_ref[...] = mm
    v2_ref[...] = vv


def _reduce_adam(recv, w, m, v, *, name):
    R, L = w.shape
    tr = _pick(R, (256, 128, 64, 32, 16, 8))

    def body(*refs):
        _reduce_adam_body(*refs)

    blk = _rowblk(tr, L)
    return _call(body, name=name, grid=(R // tr,),
                 in_specs=[pl.BlockSpec((N_DEV, tr, L), lambda i: (0, i, 0)), blk, blk, blk],
                 out_specs=[blk, blk, blk, blk], out_shape=[_sds((R, L), F32)] * 4)(recv, w, m, v)


def _reduce_adam_layer(recv, w, m, v, idx, prev, *, name, after=None):
    r, c = w.shape[-2:]
    tr = _pick(r, (256, 176, 128, 64, 32, 16))
    if prev is None:
        prev = [lax.empty(w.shape, F32) for _ in range(4)]
    behind = [] if after is None else [after]

    def body(r_ref, w_ref, m_ref, v_ref, *rest):
        _reduce_adam_body(r_ref, w_ref, m_ref, v_ref, *rest[-4:])

    blk = pl.BlockSpec((None, tr, c), lambda i: (idx, i, 0))
    return pl.pallas_call(
        body, name=name, grid=(r // tr,),
        in_specs=[pl.BlockSpec((N_DEV, tr, c), lambda i: (0, i, 0)), blk, blk, blk] + [ANY] * (4 + len(behind)),
        out_specs=[blk] * 4, out_shape=[_sds(w.shape, F32)] * 4, input_output_aliases={4 + j: j for j in range(4)},
        compiler_params=pltpu.CompilerParams(vmem_limit_bytes=VMEM_LIMIT_V7X, dimension_semantics=("arbitrary",)),
    )(recv, w, m, v, *prev, *behind)


class _Pack:
    def __init__(self, shapes, row_mult):
        self.shapes, self.offs, rows = dict(shapes), {}, 0
        for name, shp in shapes:
            size = 1
            for d in shp:
                size *= d
            nr = -(-size // (16 * PACK_LANES)) * 16
            self.offs[name] = (rows, size, nr)
            rows += nr
        self.used = rows
        self.rows = -(-rows // row_mult) * row_mult

    def pack(self, arrays, dtype, lead=()):
        parts = []
        for name, (r0, size, nr) in self.offs.items():
            flat = arrays[name].astype(dtype).reshape(lead + (size,))
            flat = jnp.pad(flat, [(0, 0)] * len(lead) + [(0, nr * PACK_LANES - size)])
            parts.append(flat.reshape(lead + (nr, PACK_LANES)))
        if self.rows > self.used:
            parts.append(jnp.zeros(lead + (self.rows - self.used, PACK_LANES), dtype))
        return jnp.concatenate(parts, axis=len(lead))

    def unpack(self, buf, lead=()):
        out = {}
        for name, (r0, size, nr) in self.offs.items():
            flat = buf[..., r0:r0 + nr, :].reshape(lead + (nr * PACK_LANES,))
            out[name] = flat[..., :size].reshape(lead + tuple(self.shapes[name]))
        return out


def _to_full(g8, ax):
    t = jnp.moveaxis(g8, 0, ax)
    return t.reshape(t.shape[:ax] + (t.shape[ax] * t.shape[ax + 1],) + t.shape[ax + 2:])


def _to_shards(full, ax):
    shp = full.shape
    return jnp.moveaxis(full.reshape(shp[:ax] + (N_DEV, shp[ax] // N_DEV) + shp[ax + 1:]), ax, 0)


def _to_heads(a, H):
    S = a.shape[0]
    return a.reshape(S, H, HEAD_DIM).transpose(1, 0, 2)


def _from_heads(a):
    H, S, dh = a.shape
    return a.transpose(1, 0, 2).reshape(S, H * dh)


def _to_heads_t(a, H, tq):
    S = a.shape[0]
    return a.reshape(S // tq, tq, H, HEAD_DIM).transpose(2, 0, 3, 1)


def _from_heads_t(a):
    H, dh, S = a.shape
    return a.transpose(2, 0, 1).reshape(S, H * dh)


def kernel(x, mem, g_mix, w_in_a, b_glu, w_dw_a, b_dw_a, ln_g, ln_b, g_kv, w_kvf, b_f, w_in_b, g_mem, w_mem_kv, w_out, g_ffn, w_up, w_dw_f, b_dw_f, w_down, g_final, loss_target, m_g_mix, m_w_in_a, m_b_glu, m_w_dw_a, m_b_dw_a, m_ln_g, m_ln_b, m_g_kv, m_w_kvf, m_b_f, m_w_in_b, m_g_mem, m_w_mem_kv, m_w_out, m_g_ffn, m_w_up, m_w_dw_f, m_b_dw_f, m_w_down, m_g_final, v_g_mix, v_w_in_a, v_b_glu, v_w_dw_a, v_b_dw_a, v_ln_g, v_ln_b, v_g_kv, v_w_kvf, v_b_f, v_w_in_b, v_g_mem, v_w_mem_kv, v_w_out, v_g_ffn, v_w_up, v_w_dw_f, v_b_dw_f, v_w_down, v_g_final):
    given = dict(locals())
    W = {n: given[n] for n in WEIGHTS}
    x0, mem0, tgt = x[0], mem[0], loss_target[0]
    S, D = x0.shape
    depth, n_a = g_mix.shape[0], w_in_a.shape[0]
    C = w_dw_a.shape[2] * N_DEV
    Mw = D - C
    Fw = w_down.shape[1] * N_DEV
    H = b_f.shape[0]
    assert C == H * HEAD_DIM and (2 * C) % Mw == 0 and C % Mw == 0 and H <= GATE_LANES
    tq = _pick(S, (256, 128))
    nkv = 2 * C + GATE_LANES

    def mix_keys(l):
        keys = [("w_in_a", l) if l < n_a else ("w_in_b", l - n_a), ("w_mem_kv", l), ("w_out", l)]
        return keys + ([("w_kvf", 0)] if l == n_a else [])

    def ffn_keys(l):
        return [("w_up", l), ("w_down", l)]

    def key_ax(key):
        return big_ax[key[0]] - 1

    def stacked(d, n, pre=""):
        a = d[pre + n]
        return a[None] if n == "w_kvf" else (jnp.swapaxes(a, 1, 2) if n == "w_up" else a)

    W3 = {n: stacked(W, n) for n, _ in BIG}
    big_ax = {n: (1 if n in ("w_kvf", "w_up") else ax) for n, ax in BIG}
    pk_small = _Pack([(n, W[n].shape) for n, _ in SMALL], 8)
    pk_rep = _Pack([(n, W[n].shape) for n in REP] + [("loss", (1,))], 8)

    ws32 = pk_small.pack(W, F32)
    gathers, toks = {}, []
    for l in range(depth):
        for tag, keys in (("mix", mix_keys(l)), ("ffn", ffn_keys(l))):
            srcs = [W3[n][i].astype(COMM_DT) for n, i in keys] + ([ws32] if (l, tag) == (0, "mix") else [])
            gathers[l, tag], t = _xchg_start(srcs, [False] * len(srcs), name=f"w_gather_start_{tag}{l}")
            toks.append(t)
    tok = sum(t[0, 0] for t in toks)
    g_mix, g_mem = g_mix + tok, g_mem + tok
    bf_pad = jnp.pad(b_f, (0, GATE_LANES - H)).reshape(1, GATE_LANES)

    def gathered(l, tag, keys, after):
        n = len(gathers[l, tag][1])
        lands = _xchg_wait(gathers[l, tag], after, [False] * n, name=f"w_gather_wait_{tag}{l}")
        return {k: _to_full(a, key_ax(k)).astype(MXU_DT) for k, a in zip(keys, lands)}, lands[len(keys):]

    mem_n = _rms_fwd(mem0, g_mem, name="mem_norm")
    sv = []
    xs = x0
    for l in range(depth):
        wl, extra = gathered(l, "mix", mix_keys(l), xs)
        if extra:
            gs = pk_small.unpack(extra[0], (N_DEV,))
            small = {n: _to_full(gs[n], ax) for n, ax in SMALL}
        t = dict(x_in=xs, w=wl)
        t["mkv"] = _mm(mem_n, wl["w_mem_kv", l], name=f"mem_kv{l}", out_dtype=MXU_DT)
        t["h"] = _rms_fwd(xs, g_mix[l], name=f"mix_norm{l}")
        if l < n_a:
            t["p"] = _mm(t["h"], wl["w_in_a", l], name=f"in_proj{l}", out_dtype=F32)
            t["v2"] = _glu_conv_fwd(t["p"], small["b_glu"][l], small["w_dw_a"][l], small["b_dw_a"][l],
                                    name=f"glu_conv{l}")
            memo = _mem_attn_fwd(t["p"], 2 * C // Mw, t["mkv"], 0, Mw=Mw, name=f"mem_attn{l}")
            t["cat"] = _ln_silu_cat_fwd(t["v2"], small["ln_g"][l], small["ln_b"][l], memo, name=f"ln_silu{l}")
        else:
            if l == n_a:
                wkvf = jnp.pad(wl["w_kvf", 0], ((0, 0), (0, nkv - w_kvf.shape[1])))
                hk = _rms_fwd(xs, g_kv, name="kv_norm")
                kvf = _mm(hk, wkvf, name="kv_proj", out_dtype=F32)
                kb, vb = kvf[:, :C].astype(MXU_DT), kvf[:, C:2 * C].astype(MXU_DT)
                k_h, v_h = _to_heads(kb, H), _to_heads(vb, H)
                kt_h, vt_h = _to_heads_t(kb, H, tq), _to_heads_t(vb, H, tq)
                fr = kvf[:, 2 * C:]
                cum = _fgate_fwd(fr, bf_pad, name="fgate")
                cum_t = cum[:, :H].T
                cq_row, ck_col = cum_t.reshape(H, S // tq, 1, tq), cum_t.reshape(H, S, 1)
            t["p"] = _mm(t["h"], wl["w_in_b", l - n_a], name=f"in_proj{l}", out_dtype=F32)
            t["q_h"] = _to_heads(t["p"][:, :C], H).astype(MXU_DT)
            t["ot"], t["lse"] = _foxt_fwd(t["q_h"], k_h, vt_h, cq_row, ck_col, tq=tq, name=f"fox{l}")
            memo = _mem_attn_fwd(t["p"], C // Mw, t["mkv"], 0, Mw=Mw, name=f"mem_attn{l}")
            t["cat"] = jnp.concatenate([_from_heads_t(t["ot"]), memo], axis=1).astype(MXU_DT)
        t["x_mid"] = _mm(t["cat"], wl["w_out", l], name=f"out_proj{l}", out_dtype=F32, add=xs)
        wl.update(gathered(l, "ffn", ffn_keys(l), t["x_mid"])[0])
        t["h2"] = _rms_fwd(t["x_mid"], g_ffn[l], name=f"ffn_norm{l}")
        t["ug"] = _mm(t["h2"], wl["w_up", l], name=f"up_gate{l}", out_dtype=F32, nt=True, cols=(0, Fw))
        t["uv"] = _mm(t["h2"], wl["w_up", l], name=f"up_val{l}", out_dtype=F32, nt=True, cols=(Fw, Fw))
        t["act"] = _ffn_act_fwd(t["ug"], t["uv"], small["w_dw_f"][l], b_dw_f[l], name=f"ffn_act{l}")
        xs = _mm(t["act"], wl["w_down", l], name=f"down_proj{l}", out_dtype=F32, add=t["x_mid"])
        sv.append(t)
    loss_dev, dx, dg_final = _loss_bwd(xs, g_final, tgt, name="loss_head")

    M1 = {n: given["m_" + n] for n in WEIGHTS}
    V1 = {n: given["v_" + n] for n in WEIGHTS}
    M3 = {n: stacked(given, n, "m_") for n, _ in BIG}
    V3 = {n: stacked(given, n, "v_") for n, _ in BIG}
    res, chain, pending, waited = {}, {}, [], []

    def start_grads(tag, l, keys, gl, extra=(), extra_scatter=()):
        srcs = [_to_shards(gl[k], key_ax(k)) for k in keys] + list(extra)
        scatter = [True] * len(keys) + list(extra_scatter)
        handle, tk = _xchg_start(srcs, scatter, name=f"g_xchg_start_{tag}{l}")
        pending.append((f"{tag}{l}", keys, handle, scatter))
        return tk

    def finish_grads(after):
        tag, keys, handle, scatter = pending.pop(0)
        lands = _xchg_wait(handle, after, scatter, name=f"g_xchg_wait_{tag}")
        waited.extend(zip(keys, lands))
        return lands[len(keys):]

    def update_waited(after=None):
        for (n, i), recv in waited:
            chain[n] = _reduce_adam_layer(recv, W3[n], M3[n], V3[n], i, chain.get(n), name=f"adam_{n}{i}",
                                          after=after)
        waited.clear()

    G = {n: [None] * W[n].shape[0] for n in ("g_mix", "b_glu", "w_dw_a", "b_dw_a", "ln_g", "ln_b", "g_ffn",
                                              "w_dw_f", "b_dw_f")}
    dk_sum = dv_sum = dck_sum = dmem_n = None
    started = None
    for l in reversed(range(depth)):
        t = sv[l]
        wl, gl = t["w"], {}
        dact = _mm(dx, wl["w_down", l], name=f"d_act{l}", out_dtype=F32, nt=True, after=started)
        gl["w_down", l] = _mm_tn(t["act"], dx, name=f"dw_down{l}", out_dtype=COMM_DT)
        dug, duv, G["w_dw_f"][l], db = _ffn_act_bwd(t["ug"], t["uv"], dact, small["w_dw_f"][l], b_dw_f[l],
                                                    name=f"d_ffn_act{l}")
        G["b_dw_f"][l] = db[0]
        dh2 = _mm_nn2(dug, duv, wl["w_up", l], name=f"d_up{l}")
        gl["w_up", l] = jnp.concatenate([_mm_tn(dug, t["h2"], name=f"dw_up_gate{l}", out_dtype=COMM_DT),
                                         _mm_tn(duv, t["h2"], name=f"dw_up_val{l}", out_dtype=COMM_DT)], axis=0)
        dx, dg = _rms_bwd(t["x_mid"], g_ffn[l], dh2, dx, name=f"d_ffn_norm{l}")
        G["g_ffn"][l] = dg[0]
        started = start_grads("ffn", l, ffn_keys(l), gl)
        if len(pending) > 2:
            finish_grads(dx)

        dcat = _mm(dx, wl["w_out", l], name=f"d_cat{l}", out_dtype=F32, nt=True, after=started)
        gl["w_out", l] = _mm_tn(t["cat"], dx, name=f"dw_out{l}", out_dtype=COMM_DT)
        if l >= n_a:
            dot_h = dcat[:, :C].T.reshape(H, HEAD_DIM, S)
            dqt_h, dk_h, dv_h, dcq, dck = _foxt_bwd(t["q_h"], k_h, kt_h, v_h, cq_row, ck_col, t["ot"], t["lse"],
                                                    dot_h, tq=tq, name=f"d_fox{l}")
            dck = dck.reshape(H, S) + dcq.reshape(H, S)
            dk_sum = dk_h if dk_sum is None else dk_sum + dk_h
            dv_sum = dv_h if dv_sum is None else dv_sum + dv_h
            dck_sum = dck if dck_sum is None else dck_sum + dck
            dqm, dmk, dmv = _mem_attn_bwd(t["p"], C // Mw, t["mkv"], 0, dcat, C // Mw, Mw=Mw,
                                          name=f"d_mem_attn{l}")
            dp = jnp.concatenate([_from_heads_t(dqt_h), dqm], axis=1).astype(MXU_DT)
            key = ("w_in_b", l - n_a)
        else:
            dv2, dlg, dlb = _ln_silu_bwd(t["v2"], small["ln_g"][l], small["ln_b"][l], dcat, name=f"d_ln_silu{l}")
            G["ln_g"][l], G["ln_b"][l] = dlg[0], dlb[0]
            da, dgt, G["w_dw_a"][l], dbd, dba, dbg = _glu_conv_bwd(t["p"], small["b_glu"][l], small["w_dw_a"][l],
                                                                   dv2, name=f"d_glu_conv{l}")
            G["b_dw_a"][l] = dbd[0]
            G["b_glu"][l] = jnp.concatenate([dba[0], dbg[0]])
            dqm, dmk, dmv = _mem_attn_bwd(t["p"], 2 * C // Mw, t["mkv"], 0, dcat, C // Mw, Mw=Mw,
                                          name=f"d_mem_attn{l}")
            dp = jnp.concatenate([da, dgt, dqm], axis=1).astype(MXU_DT)
            key = ("w_in_a", l)
        dmkv = jnp.concatenate([dmk, dmv], axis=1).astype(MXU_DT)
        gl["w_mem_kv", l] = _mm_tn(mem_n, dmkv, name=f"dw_mem_kv{l}", out_dtype=COMM_DT)
        dmem_n = _mm(dmkv, wl["w_mem_kv", l], name=f"d_mem_kv{l}", out_dtype=F32, nt=True, add=dmem_n)
        dh = _mm(dp, wl[key], name=f"d_in_proj{l}", out_dtype=F32, nt=True)
        gl[key] = _mm_tn(t["h"], dp, name=f"dw_in_proj{l}", out_dtype=COMM_DT)
        dx, dg = _rms_bwd(t["x_in"], g_mix[l], dh, dx, name=f"d_mix_norm{l}")
        G["g_mix"][l] = dg[0]
        if l == n_a:
            dcum = jnp.pad(dck_sum.T, ((0, 0), (0, GATE_LANES - H)))
            df, dbf = _fgate_bwd(fr, bf_pad, dcum, name="d_fgate")
            dkvf = jnp.concatenate([_from_heads(dk_sum), _from_heads(dv_sum), df], axis=1).astype(MXU_DT)
            dhk = _mm(dkvf, wkvf, name="d_kv_proj", out_dtype=F32, nt=True)
            gl["w_kvf", 0] = _mm_tn(hk, dkvf, name="dw_kv_proj", out_dtype=COMM_DT)[:, :w_kvf.shape[1]]
            dx, dg_kv = _rms_bwd(t["x_in"], g_kv, dhk, dx, name="d_kv_norm")
        if l > 0:
            started = start_grads("mix", l, mix_keys(l), gl)
            if len(pending) > 2:
                finish_grads(dx)

    _, dg_mem = _rms_bwd(mem0, g_mem, dmem_n, None, name="d_mem_norm")
    grads = {n: jnp.stack(v) for n, v in G.items()}
    grads.update(g_kv=dg_kv[0], b_f=dbf[0, :H], g_mem=dg_mem[0], g_final=dg_final[0], loss=loss_dev.reshape(1))
    gs8 = pk_small.pack({n: _to_shards(grads[n], ax) for n, ax in SMALL}, F32, (N_DEV,))
    last = start_grads("mix", 0, mix_keys(0), gl, [gs8, pk_rep.pack(grads, F32)], [True, False])
    while len(pending) > 1:
        finish_grads(dx)
    update_waited(after=last)
    extra = finish_grads(chain["w_down"][0])
    update_waited()
    no_state = dict(loss=jnp.zeros((1,), F32))
    for pk, recv, tag in ((pk_small, extra[0], "small"), (pk_rep, extra[1], "rep")):
        w32 = ws32 if tag == "small" else pk.pack({**W, **no_state}, F32)
        outs = _reduce_adam(recv, w32, pk.pack({**M1, **no_state}, F32), pk.pack({**V1, **no_state}, F32),
                            name=f"adam_{tag}")
        for kind, buf in zip(("grad", "delta", "new_m", "new_v"), outs):
            for n, a in pk.unpack(buf).items():
                res[kind, n] = a
    for n, outs in chain.items():
        for kind, a in zip(("grad", "delta", "new_m", "new_v"), outs):
            res[kind, n] = jnp.swapaxes(a, 1, 2) if n == "w_up" else a.reshape(W[n].shape)

    loss = res["grad", "loss"][0]
    return (loss, dx[None], *[res[kind, n] for kind in ("grad", "delta", "new_m", "new_v") for n in WEIGHTS])
```

```python
import jax
import jax.numpy as jnp
from jax import lax
from jax.experimental import pallas as pl
from jax.experimental.pallas import tpu as pltpu

F32 = jnp.float32
MXU_DT = jnp.bfloat16
COMM_DT = jnp.bfloat16

N_DEV = 8
HEAD_DIM = 64
RMS_EPS = 1e-6
LN_EPS = 1e-5
ADAM_LR = 0.001
ADAM_B1 = 0.9
ADAM_B2 = 0.999
ADAM_EPS = 1e-08
ADAM_WD = 0.01
ADAM_STEP = 10

PACK_LANES = 1024
GATE_LANES = 128
VMEM_LIMIT_V7X = 56 << 20
NEG = -1e30
MESH = pl.DeviceIdType.MESH
ANY = pl.BlockSpec(memory_space=pl.ANY)
NT = (((1,), (1,)), ((), ()))
NN = (((1,), (0,)), ((), ()))

BIG = (("w_in_a", 2), ("w_kvf", 0), ("w_in_b", 1), ("w_mem_kv", 1), ("w_out", 1), ("w_up", 2), ("w_down", 1))
SMALL = (("b_glu", 1), ("w_dw_a", 2), ("b_dw_a", 1), ("ln_g", 1), ("ln_b", 1), ("w_dw_f", 2))
REP = ("g_mix", "g_kv", "b_f", "g_mem", "g_ffn", "b_dw_f", "g_final")
WEIGHTS = ("g_mix", "w_in_a", "b_glu", "w_dw_a", "b_dw_a", "ln_g", "ln_b", "g_kv", "w_kvf", "b_f", "w_in_b",
           "g_mem", "w_mem_kv", "w_out", "g_ffn", "w_up", "w_dw_f", "b_dw_f", "w_down", "g_final")


def _sds(shape, dtype):
    return jax.ShapeDtypeStruct(tuple(shape), dtype)


def _call(body, *, name, out_shape, grid=(), in_specs=None, out_specs=None, scratch_shapes=(), aliases=None):
    params = dict(vmem_limit_bytes=VMEM_LIMIT_V7X)
    if grid:
        params["dimension_semantics"] = ("arbitrary",) * len(grid)
    kw = {} if aliases is None else dict(input_output_aliases=aliases)
    if in_specs is not None:
        kw["in_specs"] = in_specs
    if out_specs is not None:
        kw["out_specs"] = out_specs
    return pl.pallas_call(body, name=name, grid=grid, out_shape=out_shape, scratch_shapes=list(scratch_shapes),
                          compiler_params=pltpu.CompilerParams(**params), **kw)


def _res(shape):
    nd = len(shape)
    return pl.BlockSpec(tuple(shape), lambda *_: (0,) * nd)


def _colblk(rows, tc, off=0):
    return pl.BlockSpec((rows, tc), lambda j: (0, j + off))


def _rowblk(tm, cols, off=0):
    return pl.BlockSpec((tm, cols), lambda i: (i, off))


def _pick(n, opts=(512, 256, 128)):
    for t in opts:
        if n % t == 0:
            return t
    return n


def _sig(z):
    return 1.0 / (1.0 + jnp.exp(-z))


def _dot(a, b, dims=NN):
    return lax.dot_general(a, b, dims, preferred_element_type=F32)


def _dot_tn(a, b):
    return _dot(a.T.astype(b.dtype), b)


def _mm_tn(a, b, *, name, out_dtype, rows=None, prev=None):
    S, K = a.shape
    N = b.shape[1]
    tk, rc = _pick(K, (256, 128)), _pick(S)
    cast = b.dtype != MXU_DT
    first, total = (0, K) if rows is None else rows
    assert first % tk == 0
    chained = prev is not None

    def body(a_ref, b_ref, *rest):
        o_ref, acc_ref, bb = rest[chained], rest[chained + 1], rest[chained + 2:]
        if cast:
            @pl.when(pl.program_id(0) == 0)
            def _():
                for r0 in range(0, S, rc):
                    bb[0][r0:r0 + rc, :] = b_ref[r0:r0 + rc, :].astype(MXU_DT)
            b_ref = bb[0]
        for n, r0 in enumerate(range(0, S, rc)):
            part = _dot(a_ref[r0:r0 + rc, :].astype(F32).T.astype(MXU_DT), b_ref[r0:r0 + rc, :])
            if n == 0:
                acc_ref[...] = part
            else:
                acc_ref[...] += part
        o_ref[...] = acc_ref[...].astype(out_dtype)

    return _call(body, name=name, grid=(K // tk,), in_specs=[_colblk(S, tk), _res((S, N))] + [ANY] * chained,
                 out_specs=pl.BlockSpec((tk, N), lambda j: (j + first // tk, 0)),
                 out_shape=_sds((total, N), out_dtype), aliases={2: 0} if chained else None,
                 scratch_shapes=[pltpu.VMEM((tk, N), F32)] + ([pltpu.VMEM((S, N), MXU_DT)] if cast else []))(
                     a, b, *([prev] if chained else []))


def _mm_nn2(a1, a2, w, *, name):
    M, Fw = a1.shape
    N = w.shape[1]
    tm, nc = _pick(M), _pick(N)

    def body(a1_ref, a2_ref, w_ref, o_ref):
        v1, v2 = a1_ref[...], a2_ref[...]
        for n0 in range(0, N, nc):
            o_ref[:, n0:n0 + nc] = _dot(v1, w_ref[0:Fw, n0:n0 + nc]) + _dot(v2, w_ref[Fw:2 * Fw, n0:n0 + nc])

    return _call(body, name=name, grid=(M // tm,), in_specs=[_rowblk(tm, Fw), _rowblk(tm, Fw), _res(w.shape)],
                 out_specs=_rowblk(tm, N), out_shape=_sds((M, N), F32))(a1, a2, w)


def _mm(a, w, *, name, out_dtype, nt=False, add=None, cols=None, after=None):
    M, K = a.shape
    N = w.shape[0] if nt else w.shape[1]
    assert (w.shape[1] if nt else w.shape[0]) == K
    dims = NT if nt else NN
    has_add, has_after = add is not None, after is not None
    if cols is not None:
        assert not has_add and not has_after
        c0, N = cols
        tn, rc = _pick(N), _pick(M)
        assert c0 % tn == 0

        def body(a_ref, w_ref, o_ref):
            wv = w_ref[...]
            for r0 in range(0, M, rc):
                o_ref[r0:r0 + rc, :] = _dot(a_ref[r0:r0 + rc, :], wv, dims).astype(out_dtype)

        w_spec = (pl.BlockSpec((tn, K), lambda j: (j + c0 // tn, 0)) if nt else _colblk(K, tn, c0 // tn))
        return _call(body, name=name, grid=(N // tn,), in_specs=[_res((M, K)), w_spec],
                     out_specs=_colblk(M, tn), out_shape=_sds((M, N), out_dtype))(a, w)
    tn, rc = _pick(N), _pick(M)
    cast = a.dtype != MXU_DT

    def body(*refs):
        a_ref, w_ref, o_ref = refs[0], refs[1], refs[2 + has_add + has_after]
        if cast:
            @pl.when(pl.program_id(0) == 0)
            def _():
                for r0 in range(0, M, rc):
                    refs[-1][r0:r0 + rc, :] = a_ref[r0:r0 + rc, :].astype(MXU_DT)
            a_ref = refs[-1]
        wv = w_ref[...]
        for r0 in range(0, M, rc):
            acc = _dot(a_ref[r0:r0 + rc, :], wv, dims)
            if has_add:
                acc = acc + refs[2][r0:r0 + rc, :]
            o_ref[r0:r0 + rc, :] = acc.astype(out_dtype)

    w_spec = pl.BlockSpec((tn, K), lambda j: (j, 0)) if nt else pl.BlockSpec((K, tn), lambda j: (0, j))
    in_specs = ([_res((M, K)), w_spec] + ([_colblk(M, tn)] if has_add else [])
                + ([_res(after.shape)] if has_after else []))
    args = (a, w) + ((add,) if has_add else ()) + ((after,) if has_after else ())
    return _call(body, name=name, grid=(N // tn,), in_specs=in_specs, out_specs=_colblk(M, tn),
                 out_shape=_sds((M, N), out_dtype),
                 scratch_shapes=[pltpu.VMEM((M, K), MXU_DT)] if cast else [])(*args)


def _rms_fwd(x, g, *, name):
    M, D = x.shape
    tm = _pick(M, (256, 128))

    def body(x_ref, g_ref, h_ref):
        xf = x_ref[...]
        r = lax.rsqrt(jnp.mean(xf * xf, axis=-1, keepdims=True) + RMS_EPS)
        h_ref[...] = ((xf * r) * g_ref[...]).astype(MXU_DT)

    return _call(body, name=name, grid=(M // tm,), in_specs=[_rowblk(tm, D), _res((1, D))],
                 out_specs=_rowblk(tm, D), out_shape=_sds((M, D), MXU_DT))(x, g.reshape(1, D))


def _rms_bwd(x, g, dh, dx_in, *, name):
    M, D = x.shape
    tm = _pick(M, (256, 128))
    with_dx = dx_in is not None

    def body(*refs):
        if with_dx:
            x_ref, g_ref, dh_ref, dxin_ref, dx_ref, dg_ref = refs
        else:
            x_ref, g_ref, dh_ref, dg_ref = refs
        i = pl.program_id(0)
        xf = x_ref[...]
        r = lax.rsqrt(jnp.mean(xf * xf, axis=-1, keepdims=True) + RMS_EPS)
        y = xf * r
        dh_v = dh_ref[...]
        if with_dx:
            dy = dh_v * g_ref[...]
            dx_ref[...] = dxin_ref[...] + r * (dy - y * jnp.mean(dy * y, axis=-1, keepdims=True))
        part = jnp.sum(dh_v * y, axis=0, keepdims=True)

        @pl.when(i == 0)
        def _():
            dg_ref[...] = part

        @pl.when(i > 0)
        def _():
            dg_ref[...] += part

    ins = [x, g.reshape(1, D), dh] + ([dx_in] if with_dx else [])
    in_specs = [_rowblk(tm, D), _res((1, D)), _rowblk(tm, D)] + ([_rowblk(tm, D)] if with_dx else [])
    if with_dx:
        out_specs, out_shape = [_rowblk(tm, D), _res((1, D))], [_sds((M, D), F32), _sds((1, D), F32)]
    else:
        out_specs, out_shape = [_res((1, D))], [_sds((1, D), F32)]
    out = _call(body, name=name, grid=(M // tm,), in_specs=in_specs, out_specs=out_specs, out_shape=out_shape)(*ins)
    return out if with_dx else (None, out[0])


def _loss_bwd(x, g, t, *, name):
    M, D = x.shape
    tm = _pick(M, (256, 128))

    def body(x_ref, g_ref, t_ref, dx_ref, dg_ref, ls_ref):
        i = pl.program_id(0)
        xf = x_ref[...]
        r = lax.rsqrt(jnp.mean(xf * xf, axis=-1, keepdims=True) + RMS_EPS)
        xr = xf * r
        e = xr * g_ref[...] - t_ref[...]
        dout = e * (1.0 / D)
        dy = dout * g_ref[...]
        dx_ref[...] = r * (dy - xr * jnp.mean(dy * xr, axis=-1, keepdims=True))
        part = jnp.sum(dout * xr, axis=0, keepdims=True)
        lpart = jnp.zeros(ls_ref.shape, F32) + (0.5 / D) * jnp.sum(e * e, keepdims=True)

        @pl.when(i == 0)
        def _():
            dg_ref[...] = part
            ls_ref[...] = lpart

        @pl.when(i > 0)
        def _():
            dg_ref[...] += part
            ls_ref[...] += lpart

    dx, dg, ls = _call(body, name=name, grid=(M // tm,),
                       in_specs=[_rowblk(tm, D), _res((1, D)), _rowblk(tm, D)],
                       out_specs=[_rowblk(tm, D), _res((1, D)), _res((8, 128))],
                       out_shape=[_sds((M, D), F32), _sds((1, D), F32), _sds((8, 128), F32)])(x, g.reshape(1, D), t)
    return ls[0, 0], dx, dg


def _shift_rows(ext, off, rows):
    if off % 8 == 0:
        return ext[off:off + rows, :]
    return pltpu.roll(ext, ext.shape[0] - off, 0)[0:rows, :]


def _ext(pad_ref, c, rows, halo):
    return pad_ref[pl.ds(pl.multiple_of(c * rows, rows), rows + halo), :]


def _conv_chunk(pad_ref, c, rows, halo, w_ref, taps):
    ext = _ext(pad_ref, c, rows, halo)
    acc = None
    for k in range(taps):
        term = w_ref[k:k + 1, :] * _shift_rows(ext, halo - (taps - 1) + k, rows)
        acc = term if acc is None else acc + term
    return acc


def _conv_t_chunk(pad_ref, c, rows, halo, w_ref, taps):
    ext = _ext(pad_ref, c, rows, halo)
    acc = None
    for k in range(taps):
        term = w_ref[k:k + 1, :] * _shift_rows(ext, taps - 1 - k, rows)
        acc = term if acc is None else acc + term
    return acc


def _conv_wgrad_chunk(pad_ref, c, rows, halo, dy, dw_ref, taps):
    ext = _ext(pad_ref, c, rows, halo)
    for k in range(taps):
        dw_ref[k:k + 1, :] += jnp.sum(dy * _shift_rows(ext, halo - (taps - 1) + k, rows), axis=0, keepdims=True)


A_HALO, F_HALO = 32, 8


def _glu_conv_fwd(p, b_glu, w_dw, b_dw, *, name):
    S = p.shape[0]
    taps, C = w_dw.shape
    tc, rows = 128, _pick(S, (256, 128))
    nb, nch = C // tc, S // rows

    def body(a_ref, g_ref, ba_ref, bg_ref, w_ref, bd_ref, o_ref, pad_ref):
        pad_ref[0:A_HALO, :] = jnp.zeros((A_HALO, tc), F32)

        def fill(c, _):
            r = pl.ds(pl.multiple_of(c * rows, rows), rows)
            v1 = (a_ref[r, :] + ba_ref[...]) * _sig(g_ref[r, :] + bg_ref[...])
            pad_ref[pl.ds(pl.multiple_of(A_HALO + c * rows, 8), rows), :] = v1
            return 0

        lax.fori_loop(0, nch, fill, 0)

        def conv(c, _):
            o_ref[pl.ds(pl.multiple_of(c * rows, rows), rows), :] = (
                _conv_chunk(pad_ref, c, rows, A_HALO, w_ref, taps) + bd_ref[...])
            return 0

        lax.fori_loop(0, nch, conv, 0)

    b2 = b_glu.reshape(1, 2 * C)
    return _call(body, name=name, grid=(nb,),
                 in_specs=[_colblk(S, tc), _colblk(S, tc, nb), _colblk(1, tc), _colblk(1, tc, nb),
                           _colblk(taps, tc), _colblk(1, tc)],
                 out_specs=_colblk(S, tc), out_shape=_sds((S, C), F32),
                 scratch_shapes=[pltpu.VMEM((S + A_HALO, tc), F32)])(p, p, b2, b2, w_dw, b_dw.reshape(1, C))


def _glu_conv_bwd(p, b_glu, w_dw, dv2, *, name):
    S = p.shape[0]
    taps, C = w_dw.shape
    tc, rows = 128, _pick(S, (256, 128))
    nb, nch = C // tc, S // rows

    def body(a_ref, g_ref, ba_ref, bg_ref, w_ref, dy_ref, da_ref, dgt_ref, dw_ref, dbd_ref, dba_ref, dbg_ref,
             padx_ref, pady_ref):
        padx_ref[0:A_HALO, :] = jnp.zeros((A_HALO, tc), F32)
        pady_ref[S:S + A_HALO, :] = jnp.zeros((A_HALO, tc), F32)
        dw_ref[...] = jnp.zeros((taps, tc), F32)

        def fill(c, _):
            r = pl.ds(pl.multiple_of(c * rows, rows), rows)
            v1 = (a_ref[r, :] + ba_ref[...]) * _sig(g_ref[r, :] + bg_ref[...])
            padx_ref[pl.ds(pl.multiple_of(A_HALO + c * rows, 8), rows), :] = v1
            pady_ref[r, :] = dy_ref[r, :]
            return 0

        lax.fori_loop(0, nch, fill, 0)

        def back(c, carry):
            sd, sa, sg = carry
            r = pl.ds(pl.multiple_of(c * rows, rows), rows)
            dy = dy_ref[r, :]
            _conv_wgrad_chunk(padx_ref, c, rows, A_HALO, dy, dw_ref, taps)
            dv1 = _conv_t_chunk(pady_ref, c, rows, A_HALO, w_ref, taps)
            a = a_ref[r, :] + ba_ref[...]
            s = _sig(g_ref[r, :] + bg_ref[...])
            da = dv1 * s
            dgt = dv1 * a * s * (1.0 - s)
            da_ref[r, :] = da
            dgt_ref[r, :] = dgt
            return (sd + jnp.sum(dy, axis=0, keepdims=True), sa + jnp.sum(da, axis=0, keepdims=True),
                    sg + jnp.sum(dgt, axis=0, keepdims=True))

        z = jnp.zeros((1, tc), F32)
        sd, sa, sg = lax.fori_loop(0, nch, back, (z, z, z))
        dbd_ref[...] = sd
        dba_ref[...] = sa
        dbg_ref[...] = sg

    b2 = b_glu.reshape(1, 2 * C)
    return _call(body, name=name, grid=(nb,),
                 in_specs=[_colblk(S, tc), _colblk(S, tc, nb), _colblk(1, tc), _colblk(1, tc, nb),
                           _colblk(taps, tc), _colblk(S, tc)],
                 out_specs=[_colblk(S, tc), _colblk(S, tc), _colblk(taps, tc), _colblk(1, tc), _colblk(1, tc),
                            _colblk(1, tc)],
                 out_shape=[_sds((S, C), F32), _sds((S, C), F32), _sds((taps, C), F32), _sds((1, C), F32),
                            _sds((1, C), F32), _sds((1, C), F32)],
                 scratch_shapes=[pltpu.VMEM((S + A_HALO, tc), F32), pltpu.VMEM((S + A_HALO, tc), F32)])(
                     p, p, b2, b2, w_dw, dv2)


def _ln_silu_cat_fwd(v2, ln_g, ln_b, memo, *, name):
    S, C = v2.shape
    Mw = memo.shape[1]
    tm = _pick(S, (256, 128))

    def body(v_ref, g_ref, b_ref, m_ref, o_ref):
        v = v_ref[...]
        mu = jnp.mean(v, axis=-1, keepdims=True)
        d = v - mu
        y = d * lax.rsqrt(jnp.mean(d * d, axis=-1, keepdims=True) + LN_EPS) * g_ref[...] + b_ref[...]
        o_ref[:, 0:C] = (y * _sig(y)).astype(MXU_DT)
        o_ref[:, C:C + Mw] = m_ref[...].astype(MXU_DT)

    return _call(body, name=name, grid=(S // tm,),
                 in_specs=[_rowblk(tm, C), _res((1, C)), _res((1, C)), _rowblk(tm, Mw)],
                 out_specs=_rowblk(tm, C + Mw), out_shape=_sds((S, C + Mw), MXU_DT))(
                     v2, ln_g.reshape(1, C), ln_b.reshape(1, C), memo)


def _ln_silu_bwd(v2, ln_g, ln_b, dcat, *, name):
    S, C = v2.shape
    tm = _pick(S, (256, 128))

    def body(v_ref, g_ref, b_ref, dm_ref, dv_ref, dg_ref, db_ref):
        i = pl.program_id(0)
        v = v_ref[...]
        mu = jnp.mean(v, axis=-1, keepdims=True)
        d = v - mu
        rstd = lax.rsqrt(jnp.mean(d * d, axis=-1, keepdims=True) + LN_EPS)
        xh = d * rstd
        y = xh * g_ref[...] + b_ref[...]
        s = _sig(y)
        dyv = dm_ref[...] * (s * (1.0 + y * (1.0 - s)))
        dxh = dyv * g_ref[...]
        dv_ref[...] = rstd * (dxh - jnp.mean(dxh, axis=-1, keepdims=True)
                              - xh * jnp.mean(dxh * xh, axis=-1, keepdims=True))
        pg = jnp.sum(dyv * xh, axis=0, keepdims=True)
        pb = jnp.sum(dyv, axis=0, keepdims=True)

        @pl.when(i == 0)
        def _():
            dg_ref[...] = pg
            db_ref[...] = pb

        @pl.when(i > 0)
        def _():
            dg_ref[...] += pg
            db_ref[...] += pb

    return _call(body, name=name, grid=(S // tm,),
                 in_specs=[_rowblk(tm, C), _res((1, C)), _res((1, C)), _rowblk(tm, C)],
                 out_specs=[_rowblk(tm, C), _res((1, C)), _res((1, C))],
                 out_shape=[_sds((S, C), F32), _sds((1, C), F32), _sds((1, C), F32)])(
                     v2, ln_g.reshape(1, C), ln_b.reshape(1, C), dcat)


def _ffn_act_fwd(ug, uv, w_dw, b_dw, *, name):
    S, Fw = ug.shape
    taps = w_dw.shape[0]
    tc, rows = _pick(Fw, (256, 128)), _pick(S, (256, 128))
    nb, nch = Fw // tc, S // rows

    def body(ug_ref, uv_ref, wg_ref, wv_ref, bg_ref, bv_ref, o_ref, pg_ref, pv_ref):
        pg_ref[0:F_HALO, :] = jnp.zeros((F_HALO, tc), F32)
        pv_ref[0:F_HALO, :] = jnp.zeros((F_HALO, tc), F32)
        pg_ref[F_HALO:F_HALO + S, :] = ug_ref[...]
        pv_ref[F_HALO:F_HALO + S, :] = uv_ref[...]

        def act(c, _):
            gc = _conv_chunk(pg_ref, c, rows, F_HALO, wg_ref, taps) + bg_ref[...]
            vc = _conv_chunk(pv_ref, c, rows, F_HALO, wv_ref, taps) + bv_ref[...]
            o_ref[pl.ds(pl.multiple_of(c * rows, rows), rows), :] = (gc * _sig(gc) * vc).astype(MXU_DT)
            return 0

        lax.fori_loop(0, nch, act, 0)

    b2 = b_dw.reshape(1, 2 * Fw)
    return _call(body, name=name, grid=(nb,),
                 in_specs=[_colblk(S, tc), _colblk(S, tc), _colblk(taps, tc), _colblk(taps, tc, nb),
                           _colblk(1, tc), _colblk(1, tc, nb)],
                 out_specs=_colblk(S, tc), out_shape=_sds((S, Fw), MXU_DT),
                 scratch_shapes=[pltpu.VMEM((S + F_HALO, tc), F32), pltpu.VMEM((S + F_HALO, tc), F32)])(
                     ug, uv, w_dw, w_dw, b2, b2)


def _ffn_act_bwd(ug, uv, dact, w_dw, b_dw, *, name):
    S, Fw = ug.shape
    taps = w_dw.shape[0]
    tc, rows = _pick(Fw, (256, 128)), _pick(S, (256, 128))
    nb, nch = Fw // tc, S // rows

    def body(ug_ref, uv_ref, da_ref, wg_ref, wv_ref, bg_ref, bv_ref, dug_ref, duv_ref, dwg_ref, dwv_ref,
             dbg_ref, dbv_ref, pg_ref, pv_ref, qg_ref, qv_ref):
        pg_ref[0:F_HALO, :] = jnp.zeros((F_HALO, tc), F32)
        pv_ref[0:F_HALO, :] = jnp.zeros((F_HALO, tc), F32)
        qg_ref[S:S + F_HALO, :] = jnp.zeros((F_HALO, tc), F32)
        qv_ref[S:S + F_HALO, :] = jnp.zeros((F_HALO, tc), F32)
        pg_ref[F_HALO:F_HALO + S, :] = ug_ref[...]
        pv_ref[F_HALO:F_HALO + S, :] = uv_ref[...]
        dwg_ref[...] = jnp.zeros((taps, tc), F32)
        dwv_ref[...] = jnp.zeros((taps, tc), F32)

        def grads(c, carry):
            sg, sv = carry
            r = pl.ds(pl.multiple_of(c * rows, rows), rows)
            gc = _conv_chunk(pg_ref, c, rows, F_HALO, wg_ref, taps) + bg_ref[...]
            vc = _conv_chunk(pv_ref, c, rows, F_HALO, wv_ref, taps) + bv_ref[...]
            s = _sig(gc)
            da = da_ref[r, :]
            dgc = da * vc * (s * (1.0 + gc * (1.0 - s)))
            dvc = da * (gc * s)
            qg_ref[r, :] = dgc
            qv_ref[r, :] = dvc
            _conv_wgrad_chunk(pg_ref, c, rows, F_HALO, dgc, dwg_ref, taps)
            _conv_wgrad_chunk(pv_ref, c, rows, F_HALO, dvc, dwv_ref, taps)
            return sg + jnp.sum(dgc, axis=0, keepdims=True), sv + jnp.sum(dvc, axis=0, keepdims=True)

        z = jnp.zeros((1, tc), F32)
        sg, sv = lax.fori_loop(0, nch, grads, (z, z))
        dbg_ref[...] = sg
        dbv_ref[...] = sv

        def back(c, _):
            r = pl.ds(pl.multiple_of(c * rows, rows), rows)
            dug_ref[r, :] = _conv_t_chunk(qg_ref, c, rows, F_HALO, wg_ref, taps).astype(MXU_DT)
            duv_ref[r, :] = _conv_t_chunk(qv_ref, c, rows, F_HALO, wv_ref, taps).astype(MXU_DT)
            return 0

        lax.fori_loop(0, nch, back, 0)

    b2 = b_dw.reshape(1, 2 * Fw)
    pad = pltpu.VMEM((S + F_HALO, tc), F32)
    dug, duv, dwg, dwv, dbg, dbv = _call(
        body, name=name, grid=(nb,),
        in_specs=[_colblk(S, tc), _colblk(S, tc), _colblk(S, tc), _colblk(taps, tc), _colblk(taps, tc, nb),
                  _colblk(1, tc), _colblk(1, tc, nb)],
        out_specs=[_colblk(S, tc), _colblk(S, tc), _colblk(taps, tc), _colblk(taps, tc), _colblk(1, tc),
                   _colblk(1, tc)],
        out_shape=[_sds((S, Fw), MXU_DT), _sds((S, Fw), MXU_DT), _sds((taps, Fw), F32), _sds((taps, Fw), F32),
                   _sds((1, Fw), F32), _sds((1, Fw), F32)],
        scratch_shapes=[pad, pad, pad, pad])(ug, uv, dact, w_dw, w_dw, b2, b2)
    return dug, duv, jnp.concatenate([dwg, dwv], axis=1), jnp.concatenate([dbg, dbv], axis=1)


def _head_mask(h, width):
    lane = lax.broadcasted_iota(jnp.int32, (1, width), 1)
    return (lane >= h * HEAD_DIM) & (lane < (h + 1) * HEAD_DIM)


def _mem_attn_fwd(p, qblk, mkv, l, *, Mw, name):
    S, ML = p.shape[0], mkv.shape[0]
    tm, nh, scale = _pick(S, (256, 128)), Mw // HEAD_DIM, HEAD_DIM ** -0.5

    def body(q_ref, k_ref, v_ref, o_ref):
        q, kv, vv = q_ref[...], k_ref[...], v_ref[...]
        out = jnp.zeros((tm, Mw), F32)
        for h in range(nh):
            mk = _head_mask(h, Mw)
            s = _dot(jnp.where(mk, q, 0.0).astype(MXU_DT), kv, NT) * scale
            e = jnp.exp(s - jnp.max(s, axis=-1, keepdims=True))
            pr = e / jnp.sum(e, axis=-1, keepdims=True)
            out = out + _dot(pr.astype(MXU_DT), jnp.where(mk, vv, jnp.zeros_like(vv)))
        o_ref[...] = out

    return _call(body, name=name, grid=(S // tm,),
                 in_specs=[_rowblk(tm, Mw, qblk), pl.BlockSpec((ML, Mw), lambda i: (0, 2 * l)),
                           pl.BlockSpec((ML, Mw), lambda i: (0, 2 * l + 1))],
                 out_specs=_rowblk(tm, Mw), out_shape=_sds((S, Mw), F32))(p, mkv, mkv)


def _mem_attn_bwd(p, qblk, mkv, l, dcat, doblk, *, Mw, name):
    S, ML = p.shape[0], mkv.shape[0]
    tm, nh, scale = _pick(S, (256, 128)), Mw // HEAD_DIM, HEAD_DIM ** -0.5

    def body(q_ref, k_ref, v_ref, do_ref, dq_ref, dk_ref, dv_ref):
        i = pl.program_id(0)

        @pl.when(i == 0)
        def _():
            dk_ref[...] = jnp.zeros((ML, Mw), F32)
            dv_ref[...] = jnp.zeros((ML, Mw), F32)

        q, kv, vv, do = q_ref[...], k_ref[...], v_ref[...], do_ref[...]
        dq = jnp.zeros((tm, Mw), F32)
        for h in range(nh):
            mk = _head_mask(h, Mw)
            qh = jnp.where(mk, q, 0.0).astype(MXU_DT)
            s = _dot(qh, kv, NT) * scale
            e = jnp.exp(s - jnp.max(s, axis=-1, keepdims=True))
            pr = e / jnp.sum(e, axis=-1, keepdims=True)
            doh = jnp.where(mk, do, 0.0).astype(MXU_DT)
            dv_ref[...] += _dot_tn(pr, doh)
            dp = _dot(doh, vv, NT)
            ds = pr * (dp - jnp.sum(dp * pr, axis=-1, keepdims=True))
            dq = dq + _dot(ds.astype(MXU_DT), jnp.where(mk, kv, jnp.zeros_like(kv))) * scale
            dk_ref[...] += _dot_tn(ds, qh) * scale
        dq_ref[...] = dq

    return _call(body, name=name, grid=(S // tm,),
                 in_specs=[_rowblk(tm, Mw, qblk), pl.BlockSpec((ML, Mw), lambda i: (0, 2 * l)),
                           pl.BlockSpec((ML, Mw), lambda i: (0, 2 * l + 1)), _rowblk(tm, Mw, doblk)],
                 out_specs=[_rowblk(tm, Mw), _res((ML, Mw)), _res((ML, Mw))],
                 out_shape=[_sds((S, Mw), F32), _sds((ML, Mw), F32), _sds((ML, Mw), F32)])(p, mkv, mkv, dcat)


FOX_GROUP = 3


def _foxt_specs(S, dh, tq):
    nb, G = S // tq, FOX_GROUP
    rows = pl.BlockSpec((G, tq, dh), lambda h, i: (h, i, 0))
    seq = pl.BlockSpec((G, S, dh), lambda h, i: (h, 0, 0))
    seq_t = pl.BlockSpec((G, nb, dh, tq), lambda h, i: (h, 0, 0, 0))
    blk_t = pl.BlockSpec((G, dh, tq), lambda h, i: (h, 0, i))
    col = pl.BlockSpec((G, S, 1), lambda h, i: (h, 0, 0))
    row_all = pl.BlockSpec((G, nb, 1, tq), lambda h, i: (h, 0, 0, 0))
    row = pl.BlockSpec((G, 1, 1, tq), lambda h, i: (h, i, 0, 0))
    return rows, seq, seq_t, blk_t, col, row_all, row


def _foxt_logits(kv, qv, cq_row, ck_col, scale, diag):
    s = _dot(kv, qv, NT) * scale + cq_row - ck_col
    if not diag:
        return s
    keys = lax.broadcasted_iota(jnp.int32, s.shape, 0)
    queries = lax.broadcasted_iota(jnp.int32, s.shape, 1)
    return jnp.where(keys <= queries, s, NEG)


def _foxt_fwd(q, k, vt, cq_row, ck_col, *, tq, name):
    H, S, dh = q.shape
    nb, scale, G = S // tq, dh ** -0.5, FOX_GROUP
    assert H % G == 0
    rows, seq, seq_t, blk_t, col, row_all, row = _foxt_specs(S, dh, tq)

    def body(q_ref, k_ref, vt_ref, cq_ref, ck_ref, o_ref, lse_ref):
        i = pl.program_id(1)
        qv, cqv = [q_ref[e] for e in range(G)], [cq_ref[e, i] for e in range(G)]

        def kblock(j, carry, diag):
            r = pl.ds(pl.multiple_of(j * tq, tq), tq)
            out = []
            for e in range(G):
                m, l, acc = carry[e]
                s = _foxt_logits(k_ref[e, r, :], qv[e], cqv[e], ck_ref[e, r, :], scale, diag)
                m2 = jnp.maximum(m, jnp.max(s, axis=0, keepdims=True))
                pr = jnp.exp(s - m2)
                al = jnp.exp(m - m2)
                out.append((m2, al * l + jnp.sum(pr, axis=0, keepdims=True),
                            al * acc + _dot(vt_ref[e, j], pr.astype(MXU_DT))))
            return tuple(out)

        init = tuple((jnp.full((1, tq), NEG, F32), jnp.zeros((1, tq), F32), jnp.zeros((dh, tq), F32))
                     for _ in range(G))
        carry = lax.fori_loop(0, i, lambda j, c: kblock(j, c, False), init)
        for e, (m, l, acc) in enumerate(kblock(i, carry, True)):
            o_ref[e] = acc / l
            lse_ref[e, 0] = m + jnp.log(l)

    return _call(body, name=name, grid=(H // G, nb), in_specs=[rows, seq, seq_t, row_all, col],
                 out_specs=[blk_t, row],
                 out_shape=[_sds((H, dh, S), F32), _sds((H, nb, 1, tq), F32)])(q, k, vt, cq_row, ck_col)


def _foxt_bwd(q, k, kt, v, cq_row, ck_col, ot, lse, dot_, *, tq, name):
    H, S, dh = q.shape
    nb, scale, G = S // tq, dh ** -0.5, FOX_GROUP
    rows, seq, seq_t, blk_t, col, row_all, row = _foxt_specs(S, dh, tq)

    def body(q_ref, k_ref, kt_ref, v_ref, cq_ref, ck_ref, ot_ref, lse_ref, dot_ref, dq_ref, dk_ref, dv_ref,
             dcq_ref, dck_ref):
        i = pl.program_id(1)

        @pl.when(i == 0)
        def _():
            dk_ref[...] = jnp.zeros((G, S, dh), F32)
            dv_ref[...] = jnp.zeros((G, S, dh), F32)
            dck_ref[...] = jnp.zeros((G, S, 1), F32)

        qv, cqv, lsev = [q_ref[e] for e in range(G)], [cq_ref[e, i] for e in range(G)], [lse_ref[e, 0] for e in range(G)]
        dob = [dot_ref[e].astype(MXU_DT) for e in range(G)]
        delta = [jnp.sum(dob[e].astype(F32) * ot_ref[e], axis=0, keepdims=True) for e in range(G)]

        def kblock(j, carry, diag):
            r = pl.ds(pl.multiple_of(j * tq, tq), tq)
            out = []
            for e in range(G):
                dq, rs = carry[e]
                pr = jnp.exp(_foxt_logits(k_ref[e, r, :], qv[e], cqv[e], ck_ref[e, r, :], scale, diag) - lsev[e])
                ds = pr * (_dot(v_ref[e, r, :], dob[e]) - delta[e])
                dsb = ds.astype(MXU_DT)
                dk_ref[e, r, :] += _dot(dsb, qv[e]) * scale
                dv_ref[e, r, :] += _dot(pr.astype(MXU_DT), dob[e], NT)
                dck_ref[e, r, :] += -jnp.sum(ds, axis=1, keepdims=True)
                out.append((dq + _dot(kt_ref[e, j], dsb), rs + jnp.sum(ds, axis=0, keepdims=True)))
            return tuple(out)

        init = tuple((jnp.zeros((dh, tq), F32), jnp.zeros((1, tq), F32)) for _ in range(G))
        carry = lax.fori_loop(0, i, lambda j, c: kblock(j, c, False), init)
        for e, (dq, rs) in enumerate(kblock(i, carry, True)):
            dq_ref[e] = dq * scale
            dcq_ref[e, 0] = rs

    return _call(body, name=name, grid=(H // G, nb),
                 in_specs=[rows, seq, seq_t, seq, row_all, col, blk_t, row, blk_t],
                 out_specs=[blk_t, seq, seq, row, col],
                 out_shape=[_sds((H, dh, S), F32), _sds((H, S, dh), F32), _sds((H, S, dh), F32),
                            _sds((H, nb, 1, tq), F32), _sds((H, S, 1), F32)])(
                                q, k, kt, v, cq_row, ck_col, ot, lse, dot_)


def _tri(n, lower):
    r = lax.broadcasted_iota(jnp.int32, (n, n), 0)
    c = lax.broadcasted_iota(jnp.int32, (n, n), 1)
    return ((c <= r) if lower else (c >= r)).astype(F32)


def _fgate_fwd(fr, bf, *, name):
    S, W = fr.shape
    B = _pick(S, (256, 128))

    def body(f_ref, b_ref, cum_ref):
        L = _tri(B, True)
        carry = jnp.zeros((1, W), F32)
        for blk in range(S // B):
            z = f_ref[blk * B:(blk + 1) * B, :] + b_ref[...]
            ls = jnp.minimum(z, 0.0) - jnp.log(1.0 + jnp.exp(-jnp.abs(z)))
            cum_ref[blk * B:(blk + 1) * B, :] = jnp.dot(L, ls, precision=lax.Precision.HIGHEST,
                                                        preferred_element_type=F32) + carry
            carry = carry + jnp.sum(ls, axis=0, keepdims=True)

    return _call(body, name=name, out_shape=_sds((S, W), F32))(fr, bf)


def _fgate_bwd(fr, bf, dcum, *, name):
    S, W = fr.shape
    B = _pick(S, (256, 128))

    def body(f_ref, b_ref, dc_ref, df_ref, db_ref):
        U = _tri(B, False)
        carry = jnp.zeros((1, W), F32)
        dbs = jnp.zeros((1, W), F32)
        for blk in reversed(range(S // B)):
            dc = dc_ref[blk * B:(blk + 1) * B, :]
            dls = jnp.dot(U, dc, precision=lax.Precision.HIGHEST, preferred_element_type=F32) + carry
            carry = carry + jnp.sum(dc, axis=0, keepdims=True)
            z = f_ref[blk * B:(blk + 1) * B, :] + b_ref[...]
            df = dls * (1.0 / (1.0 + jnp.exp(z)))
            df_ref[blk * B:(blk + 1) * B, :] = df
            dbs = dbs + jnp.sum(df, axis=0, keepdims=True)
        db_ref[...] = dbs

    return _call(body, name=name, out_shape=[_sds((S, W), F32), _sds((1, W), F32)])(fr, bf, dcum)


def _flip(v, bit):
    return 1 - v if bit else v


HBM_SPEC = pl.BlockSpec(memory_space=pltpu.HBM)
SEM_SPEC = pl.BlockSpec(memory_space=pltpu.SEMAPHORE)


def _xchg_copies(src, land, sems, scatter):
    n = len(src)
    send, recv, loc = sems[:7 * n], sems[7 * n:14 * n], sems[14 * n:15 * n]
    x, y, c = lax.axis_index("x"), lax.axis_index("y"), lax.axis_index("c")
    me = 4 * x + 2 * y + c

    def peer(m):
        return _flip(x, m & 4), _flip(y, m & 2), _flip(c, m & 1)

    def copy(i, m):
        px, py, pc = peer(m)
        return pltpu.make_async_remote_copy(
            src_ref=src[i].at[4 * px + 2 * py + pc] if scatter[i] else src[i], dst_ref=land[i].at[me],
            send_sem=send[7 * i + m - 1], recv_sem=recv[7 * i + m - 1], device_id=(px, py, pc), device_id_type=MESH)

    def arrival(i, m):
        px, py, pc = peer(m)
        slot = land[i].at[4 * px + 2 * py + pc]
        return pltpu.make_async_remote_copy(src_ref=slot, dst_ref=slot, send_sem=send[7 * i + m - 1],
                                            recv_sem=recv[7 * i + m - 1], device_id=(px, py, pc), device_id_type=MESH)

    def own(i):
        return pltpu.make_async_copy(src[i].at[me] if scatter[i] else src[i], land[i].at[me], loc[i])

    return copy, arrival, own


def _xchg_start(srcs, scatter, *, name):
    n = len(srcs)
    lands = [_sds((N_DEV,) + s.shape[-2:], s.dtype) for s in srcs]

    ns = 15 * n

    def body(*refs):
        src, land, sems, token = refs[:n], refs[n:2 * n], refs[2 * n:2 * n + ns], refs[-1]
        copy, _, own = _xchg_copies(src, land, sems, scatter)
        for i in range(n):
            own(i).start()
            for m in range(1, N_DEV):
                copy(i, m).start()
        token[...] = jnp.zeros(token.shape, F32)

    thru = [pltpu.HBM(s.shape, s.dtype) for s in srcs] + [pltpu.HBM(s.shape, s.dtype) for s in lands]
    out = pl.pallas_call(
        body, name=name,
        out_shape=(*[pltpu.SemaphoreType.DMA(())] * ns, *thru, _sds((8, 128), F32)),
        in_specs=[HBM_SPEC] * (2 * n),
        out_specs=(*[SEM_SPEC] * ns, *[HBM_SPEC] * (2 * n), pl.BlockSpec(memory_space=pltpu.VMEM)),
        input_output_aliases={i: ns + i for i in range(2 * n)},
        compiler_params=pltpu.CompilerParams(has_side_effects=pltpu.SideEffectType.DATAFLOW_SIDE_EFFECTING),
    )(*[pltpu.with_memory_space_constraint(s, pltpu.HBM) for s in srcs],
      *[pltpu.with_memory_space_constraint(lax.empty(s.shape, s.dtype), pltpu.HBM) for s in lands])
    bufs = list(out[ns:ns + 2 * n])
    return (list(out[:ns]), bufs[:n], bufs[n:]), out[-1]


def _xchg_wait(handle, after, scatter, *, name):
    sems, srcs, lands = handle
    n = len(srcs)
    ns = 15 * n

    def body(*refs):
        src, land = refs[:n], refs[n:2 * n]
        copy, arrival, own = _xchg_copies(src, land, refs[2 * n:2 * n + ns], scatter)
        for i in range(n):
            own(i).wait()
            for m in range(1, N_DEV):
                copy(i, m).wait_send()
                arrival(i, m).wait_recv()

    out = pl.pallas_call(
        body, name=name,
        out_shape=tuple(pltpu.HBM(s.shape, s.dtype) for s in srcs + lands),
        in_specs=[HBM_SPEC] * (2 * n) + [SEM_SPEC] * ns + [ANY],
        out_specs=tuple([HBM_SPEC] * (2 * n)),
        input_output_aliases={i: i for i in range(2 * n)},
        compiler_params=pltpu.CompilerParams(has_side_effects=pltpu.SideEffectType.DATAFLOW_SIDE_EFFECTING),
    )(*srcs, *lands, *sems, after)
    return list(out[n:])


def _reduce_adam_body(r_ref, w_ref, m_ref, v_ref, g_ref, d_ref, m2_ref, v2_ref):
    g = r_ref[0].astype(F32)
    for s in range(1, N_DEV):
        g = g + r_ref[s].astype(F32)
    mm = ADAM_B1 * m_ref[...] + (1.0 - ADAM_B1) * g
    vv = ADAM_B2 * v_ref[...] + (1.0 - ADAM_B2) * (g * g)
    m_hat = mm / (1.0 - ADAM_B1 ** ADAM_STEP)
    v_hat = vv / (1.0 - ADAM_B2 ** ADAM_STEP)
    g_ref[...] = g
    d_ref[...] = -ADAM_LR * (m_hat / (jnp.sqrt(v_hat) + ADAM_EPS) + ADAM_WD * w_ref[...])
    m2_ref[...] = mm
    v2_ref[...] = vv


def _reduce_adam(recv, w, m, v, *, name):
    R, L = w.shape
    tr = _pick(R, (256, 128, 64, 32, 16, 8))

    def body(*refs):
        _reduce_adam_body(*refs)

    blk = _rowblk(tr, L)
    return _call(body, name=name, grid=(R // tr,),
                 in_specs=[pl.BlockSpec((N_DEV, tr, L), lambda i: (0, i, 0)), blk, blk, blk],
                 out_specs=[blk, blk, blk, blk], out_shape=[_sds((R, L), F32)] * 4)(recv, w, m, v)


def _reduce_adam_layer(recv, w, m, v, idx, prev, *, name, after=None):
    r, c = w.shape[-2:]
    tr = _pick(r, (256, 176, 128, 112, 64, 32, 16))
    if prev is None:
        prev = [lax.empty(w.shape, F32) for _ in range(4)]
    behind = [] if after is None else [after]

    def body(r_ref, w_ref, m_ref, v_ref, *rest):
        _reduce_adam_body(r_ref, w_ref, m_ref, v_ref, *rest[-4:])

    blk = pl.BlockSpec((None, tr, c), lambda i: (idx, i, 0))
    return pl.pallas_call(
        body, name=name, grid=(r // tr,),
        in_specs=[pl.BlockSpec((N_DEV, tr, c), lambda i: (0, i, 0)), blk, blk, blk] + [ANY] * (4 + len(behind)),
        out_specs=[blk] * 4, out_shape=[_sds(w.shape, F32)] * 4, input_output_aliases={4 + j: j for j in range(4)},
        compiler_params=pltpu.CompilerParams(vmem_limit_bytes=VMEM_LIMIT_V7X, dimension_semantics=("arbitrary",)),
    )(recv, w, m, v, *prev, *behind)


class _Pack:
    def __init__(self, shapes, row_mult):
        self.shapes, self.offs, rows = dict(shapes), {}, 0
        for name, shp in shapes:
            size = 1
            for d in shp:
                size *= d
            nr = -(-size // (16 * PACK_LANES)) * 16
            self.offs[name] = (rows, size, nr)
            rows += nr
        self.used = rows
        self.rows = -(-rows // row_mult) * row_mult

    def pack(self, arrays, dtype, lead=()):
        parts = []
        for name, (r0, size, nr) in self.offs.items():
            flat = arrays[name].astype(dtype).reshape(lead + (size,))
            flat = jnp.pad(flat, [(0, 0)] * len(lead) + [(0, nr * PACK_LANES - size)])
            parts.append(flat.reshape(lead + (nr, PACK_LANES)))
        if self.rows > self.used:
            parts.append(jnp.zeros(lead + (self.rows - self.used, PACK_LANES), dtype))
        return jnp.concatenate(parts, axis=len(lead))

    def unpack(self, buf, lead=()):
        out = {}
        for name, (r0, size, nr) in self.offs.items():
            flat = buf[..., r0:r0 + nr, :].reshape(lead + (nr * PACK_LANES,))
            out[name] = flat[..., :size].reshape(lead + tuple(self.shapes[name]))
        return out


def _to_full(g8, ax):
    t = jnp.moveaxis(g8, 0, ax)
    return t.reshape(t.shape[:ax] + (t.shape[ax] * t.shape[ax + 1],) + t.shape[ax + 2:])


def _to_shards(full, ax):
    shp = full.shape
    return jnp.moveaxis(full.reshape(shp[:ax] + (N_DEV, shp[ax] // N_DEV) + shp[ax + 1:]), ax, 0)


def _to_heads(a, H):
    S = a.shape[0]
    return a.reshape(S, H, HEAD_DIM).transpose(1, 0, 2)


def _from_heads(a):
    H, S, dh = a.shape
    return a.transpose(1, 0, 2).reshape(S, H * dh)


def _to_heads_t(a, H, tq):
    S = a.shape[0]
    return a.reshape(S // tq, tq, H, HEAD_DIM).transpose(2, 0, 3, 1)


def _from_heads_t(a):
    H, dh, S = a.shape
    return a.transpose(2, 0, 1).reshape(S, H * dh)


def kernel(x, mem, g_mix, w_in_a, b_glu, w_dw_a, b_dw_a, ln_g, ln_b, g_kv, w_kvf, b_f, w_in_b, g_mem, w_mem_kv, w_out, g_ffn, w_up, w_dw_f, b_dw_f, w_down, g_final, loss_target, m_g_mix, m_w_in_a, m_b_glu, m_w_dw_a, m_b_dw_a, m_ln_g, m_ln_b, m_g_kv, m_w_kvf, m_b_f, m_w_in_b, m_g_mem, m_w_mem_kv, m_w_out, m_g_ffn, m_w_up, m_w_dw_f, m_b_dw_f, m_w_down, m_g_final, v_g_mix, v_w_in_a, v_b_glu, v_w_dw_a, v_b_dw_a, v_ln_g, v_ln_b, v_g_kv, v_w_kvf, v_b_f, v_w_in_b, v_g_mem, v_w_mem_kv, v_w_out, v_g_ffn, v_w_up, v_w_dw_f, v_b_dw_f, v_w_down, v_g_final):
    given = dict(locals())
    W = {n: given[n] for n in WEIGHTS}
    x0, mem0, tgt = x[0], mem[0], loss_target[0]
    S, D = x0.shape
    depth, n_a = g_mix.shape[0], w_in_a.shape[0]
    C = w_dw_a.shape[2] * N_DEV
    Mw = D - C
    Fw = w_down.shape[1] * N_DEV
    H = b_f.shape[0]
    assert C == H * HEAD_DIM and (2 * C) % Mw == 0 and C % Mw == 0 and H <= GATE_LANES
    tq = _pick(S, (256, 128))
    nkv = 2 * C + GATE_LANES

    def mix_keys(l):
        keys = [("w_in_a", l) if l < n_a else ("w_in_b", l - n_a), ("w_mem_kv", l), ("w_out", l)]
        return keys + ([("w_kvf", 0)] if l == n_a else [])

    def ffn_keys(l):
        return [("w_up", l), ("w_down", l)]

    def key_ax(key):
        return big_ax[key[0]] - 1

    transposed = ("w_up", "w_in_a")

    def stacked(d, n, pre=""):
        a = d[pre + n]
        return a[None] if n == "w_kvf" else (jnp.swapaxes(a, 1, 2) if n in transposed else a)

    W3 = {n: stacked(W, n) for n, _ in BIG}
    big_ax = {n: (1 if n == "w_kvf" or n in transposed else ax) for n, ax in BIG}
    pk_small = _Pack([(n, W[n].shape) for n, _ in SMALL], 8)
    pk_rep = _Pack([(n, W[n].shape) for n in REP] + [("loss", (1,))], 8)

    ws32 = pk_small.pack(W, F32)
    gathers, toks = {}, []
    for l in range(depth):
        for tag, keys in (("mix", mix_keys(l)), ("ffn", ffn_keys(l))):
            srcs = [W3[n][i].astype(COMM_DT) for n, i in keys] + ([ws32] if (l, tag) == (0, "mix") else [])
            gathers[l, tag], t = _xchg_start(srcs, [False] * len(srcs), name=f"w_gather_start_{tag}{l}")
            toks.append(t)
    tok = sum(t[0, 0] for t in toks)
    g_mix, g_mem = g_mix + tok, g_mem + tok
    bf_pad = jnp.pad(b_f, (0, GATE_LANES - H)).reshape(1, GATE_LANES)

    def gathered(l, tag, keys, after):
        n = len(gathers[l, tag][1])
        lands = _xchg_wait(gathers[l, tag], after, [False] * n, name=f"w_gather_wait_{tag}{l}")
        return {k: _to_full(a, key_ax(k)).astype(MXU_DT) for k, a in zip(keys, lands)}, lands[len(keys):]

    mem_n = _rms_fwd(mem0, g_mem, name="mem_norm")
    sv = []
    xs = x0
    for l in range(depth):
        wl, extra = gathered(l, "mix", mix_keys(l), xs)
        if extra:
            gs = pk_small.unpack(extra[0], (N_DEV,))
            small = {n: _to_full(gs[n], ax) for n, ax in SMALL}
        t = dict(x_in=xs, w=wl)
        t["mkv"] = _mm(mem_n, wl["w_mem_kv", l], name=f"mem_kv{l}", out_dtype=MXU_DT)
        t["h"] = _rms_fwd(xs, g_mix[l], name=f"mix_norm{l}")
        if l < n_a:
            t["p"] = _mm(t["h"], wl["w_in_a", l], name=f"in_proj{l}", out_dtype=F32, nt=True)
            t["v2"] = _glu_conv_fwd(t["p"], small["b_glu"][l], small["w_dw_a"][l], small["b_dw_a"][l],
                                    name=f"glu_conv{l}")
            memo = _mem_attn_fwd(t["p"], 2 * C // Mw, t["mkv"], 0, Mw=Mw, name=f"mem_attn{l}")
            t["cat"] = _ln_silu_cat_fwd(t["v2"], small["ln_g"][l], small["ln_b"][l], memo, name=f"ln_silu{l}")
        else:
            if l == n_a:
                wkvf = jnp.pad(wl["w_kvf", 0], ((0, 0), (0, nkv - w_kvf.shape[1])))
                hk = _rms_fwd(xs, g_kv, name="kv_norm")
                kvf = _mm(hk, wkvf, name="kv_proj", out_dtype=F32)
                kb, vb = kvf[:, :C].astype(MXU_DT), kvf[:, C:2 * C].astype(MXU_DT)
                k_h, v_h = _to_heads(kb, H), _to_heads(vb, H)
                kt_h, vt_h = _to_heads_t(kb, H, tq), _to_heads_t(vb, H, tq)
                fr = kvf[:, 2 * C:]
                cum = _fgate_fwd(fr, bf_pad, name="fgate")
                cum_t = cum[:, :H].T
                cq_row, ck_col = cum_t.reshape(H, S // tq, 1, tq), cum_t.reshape(H, S, 1)
            t["p"] = _mm(t["h"], wl["w_in_b", l - n_a], name=f"in_proj{l}", out_dtype=F32)
            t["q_h"] = _to_heads(t["p"][:, :C], H).astype(MXU_DT)
            t["ot"], t["lse"] = _foxt_fwd(t["q_h"], k_h, vt_h, cq_row, ck_col, tq=tq, name=f"fox{l}")
            memo = _mem_attn_fwd(t["p"], C // Mw, t["mkv"], 0, Mw=Mw, name=f"mem_attn{l}")
            t["cat"] = jnp.concatenate([_from_heads_t(t["ot"]), memo], axis=1).astype(MXU_DT)
        t["x_mid"] = _mm(t["cat"], wl["w_out", l], name=f"out_proj{l}", out_dtype=F32, add=xs)
        wl.update(gathered(l, "ffn", ffn_keys(l), t["x_mid"])[0])
        t["h2"] = _rms_fwd(t["x_mid"], g_ffn[l], name=f"ffn_norm{l}")
        t["ug"] = _mm(t["h2"], wl["w_up", l], name=f"up_gate{l}", out_dtype=F32, nt=True, cols=(0, Fw))
        t["uv"] = _mm(t["h2"], wl["w_up", l], name=f"up_val{l}", out_dtype=F32, nt=True, cols=(Fw, Fw))
        t["act"] = _ffn_act_fwd(t["ug"], t["uv"], small["w_dw_f"][l], b_dw_f[l], name=f"ffn_act{l}")
        xs = _mm(t["act"], wl["w_down", l], name=f"down_proj{l}", out_dtype=F32, add=t["x_mid"])
        sv.append(t)
    loss_dev, dx, dg_final = _loss_bwd(xs, g_final, tgt, name="loss_head")

    M1 = {n: given["m_" + n] for n in WEIGHTS}
    V1 = {n: given["v_" + n] for n in WEIGHTS}
    M3 = {n: stacked(given, n, "m_") for n, _ in BIG}
    V3 = {n: stacked(given, n, "v_") for n, _ in BIG}
    res, chain, pending, waited = {}, {}, [], []

    def start_grads(tag, l, keys, gl, extra=(), extra_scatter=()):
        srcs = [_to_shards(gl[k], key_ax(k)) for k in keys] + list(extra)
        scatter = [True] * len(keys) + list(extra_scatter)
        handle, tk = _xchg_start(srcs, scatter, name=f"g_xchg_start_{tag}{l}")
        pending.append((f"{tag}{l}", keys, handle, scatter))
        return tk

    def finish_grads(after):
        tag, keys, handle, scatter = pending.pop(0)
        lands = _xchg_wait(handle, after, scatter, name=f"g_xchg_wait_{tag}")
        waited.extend(zip(keys, lands))
        return lands[len(keys):]

    def update_waited(after=None):
        for (n, i), recv in waited:
            chain[n] = _reduce_adam_layer(recv, W3[n], M3[n], V3[n], i, chain.get(n), name=f"adam_{n}{i}",
                                          after=after)
        waited.clear()

    G = {n: [None] * W[n].shape[0] for n in ("g_mix", "b_glu", "w_dw_a", "b_dw_a", "ln_g", "ln_b", "g_ffn",
                                              "w_dw_f", "b_dw_f")}
    dk_sum = dv_sum = dck_sum = dmem_n = None
    started = None
    for l in reversed(range(depth)):
        t = sv[l]
        wl, gl = t["w"], {}
        dact = _mm(dx, wl["w_down", l], name=f"d_act{l}", out_dtype=F32, nt=True, after=started)
        gl["w_down", l] = _mm_tn(t["act"], dx, name=f"dw_down{l}", out_dtype=COMM_DT)
        dug, duv, G["w_dw_f"][l], db = _ffn_act_bwd(t["ug"], t["uv"], dact, small["w_dw_f"][l], b_dw_f[l],
                                                    name=f"d_ffn_act{l}")
        G["b_dw_f"][l] = db[0]
        dh2 = _mm_nn2(dug, duv, wl["w_up", l], name=f"d_up{l}")
        half = _mm_tn(dug, t["h2"], name=f"dw_up_gate{l}", out_dtype=COMM_DT, rows=(0, 2 * Fw))
        gl["w_up", l] = _mm_tn(duv, t["h2"], name=f"dw_up_val{l}", out_dtype=COMM_DT, rows=(Fw, 2 * Fw), prev=half)
        dx, dg = _rms_bwd(t["x_mid"], g_ffn[l], dh2, dx, name=f"d_ffn_norm{l}")
        G["g_ffn"][l] = dg[0]
        started = start_grads("ffn", l, ffn_keys(l), gl)
        if len(pending) > 2:
            finish_grads(dx)

        dcat = _mm(dx, wl["w_out", l], name=f"d_cat{l}", out_dtype=F32, nt=True, after=started)
        gl["w_out", l] = _mm_tn(t["cat"], dx, name=f"dw_out{l}", out_dtype=COMM_DT)
        if l >= n_a:
            dot_h = dcat[:, :C].T.reshape(H, HEAD_DIM, S)
            dqt_h, dk_h, dv_h, dcq, dck = _foxt_bwd(t["q_h"], k_h, kt_h, v_h, cq_row, ck_col, t["ot"], t["lse"],
                                                    dot_h, tq=tq, name=f"d_fox{l}")
            dck = dck.reshape(H, S) + dcq.reshape(H, S)
            dk_sum = dk_h if dk_sum is None else dk_sum + dk_h
            dv_sum = dv_h if dv_sum is None else dv_sum + dv_h
            dck_sum = dck if dck_sum is None else dck_sum + dck
            dqm, dmk, dmv = _mem_attn_bwd(t["p"], C // Mw, t["mkv"], 0, dcat, C // Mw, Mw=Mw,
                                          name=f"d_mem_attn{l}")
            dp = jnp.concatenate([_from_heads_t(dqt_h), dqm], axis=1).astype(MXU_DT)
            key = ("w_in_b", l - n_a)
        else:
            dv2, dlg, dlb = _ln_silu_bwd(t["v2"], small["ln_g"][l], small["ln_b"][l], dcat, name=f"d_ln_silu{l}")
            G["ln_g"][l], G["ln_b"][l] = dlg[0], dlb[0]
            da, dgt, G["w_dw_a"][l], dbd, dba, dbg = _glu_conv_bwd(t["p"], small["b_glu"][l], small["w_dw_a"][l],
                                                                   dv2, name=f"d_glu_conv{l}")
            G["b_dw_a"][l] = dbd[0]
            G["b_glu"][l] = jnp.concatenate([dba[0], dbg[0]])
            dqm, dmk, dmv = _mem_attn_bwd(t["p"], 2 * C // Mw, t["mkv"], 0, dcat, C // Mw, Mw=Mw,
                                          name=f"d_mem_attn{l}")
            dp = jnp.concatenate([da, dgt, dqm], axis=1).astype(MXU_DT)
            key = ("w_in_a", l)
        dmkv = jnp.concatenate([dmk, dmv], axis=1).astype(MXU_DT)
        gl["w_mem_kv", l] = _mm_tn(mem_n, dmkv, name=f"dw_mem_kv{l}", out_dtype=COMM_DT)
        dmem_n = _mm(dmkv, wl["w_mem_kv", l], name=f"d_mem_kv{l}", out_dtype=F32, nt=True, add=dmem_n)
        if key[0] in transposed:
            dh = _mm(dp, wl[key], name=f"d_in_proj{l}", out_dtype=F32)
            gl[key] = _mm_tn(dp, t["h"], name=f"dw_in_proj{l}", out_dtype=COMM_DT)
        else:
            dh = _mm(dp, wl[key], name=f"d_in_proj{l}", out_dtype=F32, nt=True)
            gl[key] = _mm_tn(t["h"], dp, name=f"dw_in_proj{l}", out_dtype=COMM_DT)
        dx, dg = _rms_bwd(t["x_in"], g_mix[l], dh, dx, name=f"d_mix_norm{l}")
        G["g_mix"][l] = dg[0]
        if l == n_a:
            dcum = jnp.pad(dck_sum.T, ((0, 0), (0, GATE_LANES - H)))
            df, dbf = _fgate_bwd(fr, bf_pad, dcum, name="d_fgate")
            dkvf = jnp.concatenate([_from_heads(dk_sum), _from_heads(dv_sum), df], axis=1).astype(MXU_DT)
            dhk = _mm(dkvf, wkvf, name="d_kv_proj", out_dtype=F32, nt=True)
            gl["w_kvf", 0] = _mm_tn(hk, dkvf, name="dw_kv_proj", out_dtype=COMM_DT)[:, :w_kvf.shape[1]]
            dx, dg_kv = _rms_bwd(t["x_in"], g_kv, dhk, dx, name="d_kv_norm")
        if l > 0:
            started = start_grads("mix", l, mix_keys(l), gl)
            if len(pending) > 2:
                finish_grads(dx)

    _, dg_mem = _rms_bwd(mem0, g_mem, dmem_n, None, name="d_mem_norm")
    grads = {n: jnp.stack(v) for n, v in G.items()}
    grads.update(g_kv=dg_kv[0], b_f=dbf[0, :H], g_mem=dg_mem[0], g_final=dg_final[0], loss=loss_dev.reshape(1))
    gs8 = pk_small.pack({n: _to_shards(grads[n], ax) for n, ax in SMALL}, F32, (N_DEV,))
    last = start_grads("mix", 0, mix_keys(0), gl, [gs8, pk_rep.pack(grads, F32)], [True, False])
    while len(pending) > 1:
        finish_grads(dx)
    update_waited(after=last)
    extra = finish_grads(chain["w_down"][0])
    update_waited()
    no_state = dict(loss=jnp.zeros((1,), F32))
    for pk, recv, tag in ((pk_small, extra[0], "small"), (pk_rep, extra[1], "rep")):
        w32 = ws32 if tag == "small" else pk.pack({**W, **no_state}, F32)
        outs = _reduce_adam(recv, w32, pk.pack({**M1, **no_state}, F32), pk.pack({**V1, **no_state}, F32),
                            name=f"adam_{tag}")
        for kind, buf in zip(("grad", "delta", "new_m", "new_v"), outs):
            for n, a in pk.unpack(buf).items():
                res[kind, n] = a
    for n, outs in chain.items():
        for kind, a in zip(("grad", "delta", "new_m", "new_v"), outs):
            res[kind, n] = jnp.swapaxes(a, 1, 2) if n in transposed else a.reshape(W[n].shape)

    loss = res["grad", "loss"][0]
    return (loss, dx[None], *[res[kind, n] for kind in ("grad", "delta", "new_m", "new_v") for n in WEIGHTS])
```

```python
import jax
import jax.numpy as jnp
from jax import lax
from jax.experimental import pallas as pl
from jax.experimental.pallas import tpu as pltpu

F32 = jnp.float32
MXU_DT = jnp.bfloat16
COMM_DT = jnp.bfloat16

N_DEV = 8
HEAD_DIM = 64
RMS_EPS = 1e-6
LN_EPS = 1e-5
ADAM_LR = 0.001
ADAM_B1 = 0.9
ADAM_B2 = 0.999
ADAM_EPS = 1e-08
ADAM_WD = 0.01
ADAM_STEP = 10

PACK_LANES = 1024
GATE_LANES = 128
VMEM_LIMIT_V7X = 56 << 20
NEG = -1e30
MESH = pl.DeviceIdType.MESH
ANY = pl.BlockSpec(memory_space=pl.ANY)
NT = (((1,), (1,)), ((), ()))
NN = (((1,), (0,)), ((), ()))

BIG = (("w_in_a", 2), ("w_kvf", 0), ("w_in_b", 1), ("w_mem_kv", 1), ("w_out", 1), ("w_up", 2), ("w_down", 1))
SMALL = (("b_glu", 1), ("w_dw_a", 2), ("b_dw_a", 1), ("ln_g", 1), ("ln_b", 1), ("w_dw_f", 2))
REP = ("g_mix", "g_kv", "b_f", "g_mem", "g_ffn", "b_dw_f", "g_final")
WEIGHTS = ("g_mix", "w_in_a", "b_glu", "w_dw_a", "b_dw_a", "ln_g", "ln_b", "g_kv", "w_kvf", "b_f", "w_in_b",
           "g_mem", "w_mem_kv", "w_out", "g_ffn", "w_up", "w_dw_f", "b_dw_f", "w_down", "g_final")


def _sds(shape, dtype):
    return jax.ShapeDtypeStruct(tuple(shape), dtype)


def _call(body, *, name, out_shape, grid=(), in_specs=None, out_specs=None, scratch_shapes=(), aliases=None):
    params = dict(vmem_limit_bytes=VMEM_LIMIT_V7X)
    if grid:
        params["dimension_semantics"] = ("arbitrary",) * len(grid)
    kw = {} if aliases is None else dict(input_output_aliases=aliases)
    if in_specs is not None:
        kw["in_specs"] = in_specs
    if out_specs is not None:
        kw["out_specs"] = out_specs
    return pl.pallas_call(body, name=name, grid=grid, out_shape=out_shape, scratch_shapes=list(scratch_shapes),
                          compiler_params=pltpu.CompilerParams(**params), **kw)


def _res(shape):
    nd = len(shape)
    return pl.BlockSpec(tuple(shape), lambda *_: (0,) * nd)


def _colblk(rows, tc, off=0):
    return pl.BlockSpec((rows, tc), lambda j: (0, j + off))


def _rowblk(tm, cols, off=0):
    return pl.BlockSpec((tm, cols), lambda i: (i, off))


def _pick(n, opts=(512, 256, 128)):
    for t in opts:
        if n % t == 0:
            return t
    return n


def _sig(z):
    return 1.0 / (1.0 + jnp.exp(-z))


def _dot(a, b, dims=NN):
    return lax.dot_general(a, b, dims, preferred_element_type=F32)


def _dot_tn(a, b):
    return _dot(a.T.astype(b.dtype), b)


def _mm_tn(a, b, *, name, out_dtype, rows=None, prev=None):
    S, K = a.shape
    N = b.shape[1]
    tk, rc = _pick(K, (256, 128)), _pick(S)
    cast = b.dtype != MXU_DT
    first, total = (0, K) if rows is None else rows
    assert first % tk == 0
    chained = prev is not None

    def body(a_ref, b_ref, *rest):
        o_ref, acc_ref, bb = rest[chained], rest[chained + 1], rest[chained + 2:]
        if cast:
            @pl.when(pl.program_id(0) == 0)
            def _():
                for r0 in range(0, S, rc):
                    bb[0][r0:r0 + rc, :] = b_ref[r0:r0 + rc, :].astype(MXU_DT)
            b_ref = bb[0]
        for n, r0 in enumerate(range(0, S, rc)):
            part = _dot(a_ref[r0:r0 + rc, :].astype(F32).T.astype(MXU_DT), b_ref[r0:r0 + rc, :])
            if n == 0:
                acc_ref[...] = part
            else:
                acc_ref[...] += part
        o_ref[...] = acc_ref[...].astype(out_dtype)

    return _call(body, name=name, grid=(K // tk,), in_specs=[_colblk(S, tk), _res((S, N))] + [ANY] * chained,
                 out_specs=pl.BlockSpec((tk, N), lambda j: (j + first // tk, 0)),
                 out_shape=_sds((total, N), out_dtype), aliases={2: 0} if chained else None,
                 scratch_shapes=[pltpu.VMEM((tk, N), F32)] + ([pltpu.VMEM((S, N), MXU_DT)] if cast else []))(
                     a, b, *([prev] if chained else []))


def _mm_nn2(a1, a2, w, *, name):
    M, Fw = a1.shape
    N = w.shape[1]
    tm, nc = _pick(M), _pick(N)

    def body(a1_ref, a2_ref, w_ref, o_ref):
        v1, v2 = a1_ref[...], a2_ref[...]
        for n0 in range(0, N, nc):
            o_ref[:, n0:n0 + nc] = _dot(v1, w_ref[0:Fw, n0:n0 + nc]) + _dot(v2, w_ref[Fw:2 * Fw, n0:n0 + nc])

    return _call(body, name=name, grid=(M // tm,), in_specs=[_rowblk(tm, Fw), _rowblk(tm, Fw), _res(w.shape)],
                 out_specs=_rowblk(tm, N), out_shape=_sds((M, N), F32))(a1, a2, w)


def _mm(a, w, *, name, out_dtype, nt=False, add=None, cols=None, after=None):
    M, K = a.shape
    N = w.shape[0] if nt else w.shape[1]
    assert (w.shape[1] if nt else w.shape[0]) == K
    dims = NT if nt else NN
    has_add, has_after = add is not None, after is not None
    if cols is not None:
        assert not has_add and not has_after
        c0, N = cols
        tn, rc = _pick(N), _pick(M)
        assert c0 % tn == 0

        def body(a_ref, w_ref, o_ref):
            wv = w_ref[...]
            for r0 in range(0, M, rc):
                o_ref[r0:r0 + rc, :] = _dot(a_ref[r0:r0 + rc, :], wv, dims).astype(out_dtype)

        w_spec = (pl.BlockSpec((tn, K), lambda j: (j + c0 // tn, 0)) if nt else _colblk(K, tn, c0 // tn))
        return _call(body, name=name, grid=(N // tn,), in_specs=[_res((M, K)), w_spec],
                     out_specs=_colblk(M, tn), out_shape=_sds((M, N), out_dtype))(a, w)
    tn, rc = _pick(N), _pick(M)
    cast = a.dtype != MXU_DT

    def body(*refs):
        a_ref, w_ref, o_ref = refs[0], refs[1], refs[2 + has_add + has_after]
        if cast:
            @pl.when(pl.program_id(0) == 0)
            def _():
                for r0 in range(0, M, rc):
                    refs[-1][r0:r0 + rc, :] = a_ref[r0:r0 + rc, :].astype(MXU_DT)
            a_ref = refs[-1]
        wv = w_ref[...]
        for r0 in range(0, M, rc):
            acc = _dot(a_ref[r0:r0 + rc, :], wv, dims)
            if has_add:
                acc = acc + refs[2][r0:r0 + rc, :]
            o_ref[r0:r0 + rc, :] = acc.astype(out_dtype)

    w_spec = pl.BlockSpec((tn, K), lambda j: (j, 0)) if nt else pl.BlockSpec((K, tn), lambda j: (0, j))
    in_specs = ([_res((M, K)), w_spec] + ([_colblk(M, tn)] if has_add else [])
                + ([_res(after.shape)] if has_after else []))
    args = (a, w) + ((add,) if has_add else ()) + ((after,) if has_after else ())
    return _call(body, name=name, grid=(N // tn,), in_specs=in_specs, out_specs=_colblk(M, tn),
                 out_shape=_sds((M, N), out_dtype),
                 scratch_shapes=[pltpu.VMEM((M, K), MXU_DT)] if cast else [])(*args)


def _rms_fwd(x, g, *, name):
    M, D = x.shape
    tm = _pick(M, (256, 128))

    def body(x_ref, g_ref, h_ref):
        xf = x_ref[...]
        r = lax.rsqrt(jnp.mean(xf * xf, axis=-1, keepdims=True) + RMS_EPS)
        h_ref[...] = ((xf * r) * g_ref[...]).astype(MXU_DT)

    return _call(body, name=name, grid=(M // tm,), in_specs=[_rowblk(tm, D), _res((1, D))],
                 out_specs=_rowblk(tm, D), out_shape=_sds((M, D), MXU_DT))(x, g.reshape(1, D))


def _rms_bwd(x, g, dh, dx_in, *, name):
    M, D = x.shape
    tm = _pick(M, (256, 128))
    with_dx = dx_in is not None

    def body(*refs):
        if with_dx:
            x_ref, g_ref, dh_ref, dxin_ref, dx_ref, dg_ref = refs
        else:
            x_ref, g_ref, dh_ref, dg_ref = refs
        i = pl.program_id(0)
        xf = x_ref[...]
        r = lax.rsqrt(jnp.mean(xf * xf, axis=-1, keepdims=True) + RMS_EPS)
        y = xf * r
        dh_v = dh_ref[...]
        if with_dx:
            dy = dh_v * g_ref[...]
            dx_ref[...] = dxin_ref[...] + r * (dy - y * jnp.mean(dy * y, axis=-1, keepdims=True))
        part = jnp.sum(dh_v * y, axis=0, keepdims=True)

        @pl.when(i == 0)
        def _():
            dg_ref[...] = part

        @pl.when(i > 0)
        def _():
            dg_ref[...] += part

    ins = [x, g.reshape(1, D), dh] + ([dx_in] if with_dx else [])
    in_specs = [_rowblk(tm, D), _res((1, D)), _rowblk(tm, D)] + ([_rowblk(tm, D)] if with_dx else [])
    if with_dx:
        out_specs, out_shape = [_rowblk(tm, D), _res((1, D))], [_sds((M, D), F32), _sds((1, D), F32)]
    else:
        out_specs, out_shape = [_res((1, D))], [_sds((1, D), F32)]
    out = _call(body, name=name, grid=(M // tm,), in_specs=in_specs, out_specs=out_specs, out_shape=out_shape)(*ins)
    return out if with_dx else (None, out[0])


def _loss_bwd(x, g, t, *, name):
    M, D = x.shape
    tm = _pick(M, (256, 128))

    def body(x_ref, g_ref, t_ref, dx_ref, dg_ref, ls_ref):
        i = pl.program_id(0)
        xf = x_ref[...]
        r = lax.rsqrt(jnp.mean(xf * xf, axis=-1, keepdims=True) + RMS_EPS)
        xr = xf * r
        e = xr * g_ref[...] - t_ref[...]
        dout = e * (1.0 / D)
        dy = dout * g_ref[...]
        dx_ref[...] = r * (dy - xr * jnp.mean(dy * xr, axis=-1, keepdims=True))
        part = jnp.sum(dout * xr, axis=0, keepdims=True)
        lpart = jnp.zeros(ls_ref.shape, F32) + (0.5 / D) * jnp.sum(e * e, keepdims=True)

        @pl.when(i == 0)
        def _():
            dg_ref[...] = part
            ls_ref[...] = lpart

        @pl.when(i > 0)
        def _():
            dg_ref[...] += part
            ls_ref[...] += lpart

    dx, dg, ls = _call(body, name=name, grid=(M // tm,),
                       in_specs=[_rowblk(tm, D), _res((1, D)), _rowblk(tm, D)],
                       out_specs=[_rowblk(tm, D), _res((1, D)), _res((8, 128))],
                       out_shape=[_sds((M, D), F32), _sds((1, D), F32), _sds((8, 128), F32)])(x, g.reshape(1, D), t)
    return ls[0, 0], dx, dg


def _shift_rows(ext, off, rows):
    if off % 8 == 0:
        return ext[off:off + rows, :]
    return pltpu.roll(ext, ext.shape[0] - off, 0)[0:rows, :]


def _ext(pad_ref, c, rows, halo):
    return pad_ref[pl.ds(pl.multiple_of(c * rows, rows), rows + halo), :]


def _conv_chunk(pad_ref, c, rows, halo, w_ref, taps):
    ext = _ext(pad_ref, c, rows, halo)
    acc = None
    for k in range(taps):
        term = w_ref[k:k + 1, :] * _shift_rows(ext, halo - (taps - 1) + k, rows)
        acc = term if acc is None else acc + term
    return acc


def _conv_t_chunk(pad_ref, c, rows, halo, w_ref, taps):
    ext = _ext(pad_ref, c, rows, halo)
    acc = None
    for k in range(taps):
        term = w_ref[k:k + 1, :] * _shift_rows(ext, taps - 1 - k, rows)
        acc = term if acc is None else acc + term
    return acc


def _conv_wgrad_chunk(pad_ref, c, rows, halo, dy, dw_ref, taps):
    ext = _ext(pad_ref, c, rows, halo)
    for k in range(taps):
        dw_ref[k:k + 1, :] += jnp.sum(dy * _shift_rows(ext, halo - (taps - 1) + k, rows), axis=0, keepdims=True)


A_HALO, F_HALO = 32, 8


def _glu_conv_fwd(p, b_glu, w_dw, b_dw, *, name):
    S = p.shape[0]
    taps, C = w_dw.shape
    tc, rows = 128, _pick(S, (256, 128))
    nb, nch = C // tc, S // rows

    def body(a_ref, g_ref, ba_ref, bg_ref, w_ref, bd_ref, o_ref, pad_ref):
        pad_ref[0:A_HALO, :] = jnp.zeros((A_HALO, tc), F32)

        def fill(c, _):
            r = pl.ds(pl.multiple_of(c * rows, rows), rows)
            v1 = (a_ref[r, :] + ba_ref[...]) * _sig(g_ref[r, :] + bg_ref[...])
            pad_ref[pl.ds(pl.multiple_of(A_HALO + c * rows, 8), rows), :] = v1
            return 0

        lax.fori_loop(0, nch, fill, 0)

        def conv(c, _):
            o_ref[pl.ds(pl.multiple_of(c * rows, rows), rows), :] = (
                _conv_chunk(pad_ref, c, rows, A_HALO, w_ref, taps) + bd_ref[...])
            return 0

        lax.fori_loop(0, nch, conv, 0)

    b2 = b_glu.reshape(1, 2 * C)
    return _call(body, name=name, grid=(nb,),
                 in_specs=[_colblk(S, tc), _colblk(S, tc, nb), _colblk(1, tc), _colblk(1, tc, nb),
                           _colblk(taps, tc), _colblk(1, tc)],
                 out_specs=_colblk(S, tc), out_shape=_sds((S, C), F32),
                 scratch_shapes=[pltpu.VMEM((S + A_HALO, tc), F32)])(p, p, b2, b2, w_dw, b_dw.reshape(1, C))


def _glu_conv_bwd(p, b_glu, w_dw, dv2, *, name):
    S = p.shape[0]
    taps, C = w_dw.shape
    tc, rows = 128, _pick(S, (256, 128))
    nb, nch = C // tc, S // rows

    def body(a_ref, g_ref, ba_ref, bg_ref, w_ref, dy_ref, da_ref, dgt_ref, dw_ref, dbd_ref, dba_ref, dbg_ref,
             padx_ref, pady_ref):
        padx_ref[0:A_HALO, :] = jnp.zeros((A_HALO, tc), F32)
        pady_ref[S:S + A_HALO, :] = jnp.zeros((A_HALO, tc), F32)
        dw_ref[...] = jnp.zeros((taps, tc), F32)

        def fill(c, _):
            r = pl.ds(pl.multiple_of(c * rows, rows), rows)
            v1 = (a_ref[r, :] + ba_ref[...]) * _sig(g_ref[r, :] + bg_ref[...])
            padx_ref[pl.ds(pl.multiple_of(A_HALO + c * rows, 8), rows), :] = v1
            pady_ref[r, :] = dy_ref[r, :]
            return 0

        lax.fori_loop(0, nch, fill, 0)

        def back(c, carry):
            sd, sa, sg = carry
            r = pl.ds(pl.multiple_of(c * rows, rows), rows)
            dy = dy_ref[r, :]
            _conv_wgrad_chunk(padx_ref, c, rows, A_HALO, dy, dw_ref, taps)
            dv1 = _conv_t_chunk(pady_ref, c, rows, A_HALO, w_ref, taps)
            a = a_ref[r, :] + ba_ref[...]
            s = _sig(g_ref[r, :] + bg_ref[...])
            da = dv1 * s
            dgt = dv1 * a * s * (1.0 - s)
            da_ref[r, :] = da
            dgt_ref[r, :] = dgt
            return (sd + jnp.sum(dy, axis=0, keepdims=True), sa + jnp.sum(da, axis=0, keepdims=True),
                    sg + jnp.sum(dgt, axis=0, keepdims=True))

        z = jnp.zeros((1, tc), F32)
        sd, sa, sg = lax.fori_loop(0, nch, back, (z, z, z))
        dbd_ref[...] = sd
        dba_ref[...] = sa
        dbg_ref[...] = sg

    b2 = b_glu.reshape(1, 2 * C)
    return _call(body, name=name, grid=(nb,),
                 in_specs=[_colblk(S, tc), _colblk(S, tc, nb), _colblk(1, tc), _colblk(1, tc, nb),
                           _colblk(taps, tc), _colblk(S, tc)],
                 out_specs=[_colblk(S, tc), _colblk(S, tc), _colblk(taps, tc), _colblk(1, tc), _colblk(1, tc),
                            _colblk(1, tc)],
                 out_shape=[_sds((S, C), F32), _sds((S, C), F32), _sds((taps, C), F32), _sds((1, C), F32),
                            _sds((1, C), F32), _sds((1, C), F32)],
                 scratch_shapes=[pltpu.VMEM((S + A_HALO, tc), F32), pltpu.VMEM((S + A_HALO, tc), F32)])(
                     p, p, b2, b2, w_dw, dv2)


def _ln_silu_cat_fwd(v2, ln_g, ln_b, memo, *, name):
    S, C = v2.shape
    Mw = memo.shape[1]
    tm = _pick(S, (256, 128))

    def body(v_ref, g_ref, b_ref, m_ref, o_ref):
        v = v_ref[...]
        mu = jnp.mean(v, axis=-1, keepdims=True)
        d = v - mu
        y = d * lax.rsqrt(jnp.mean(d * d, axis=-1, keepdims=True) + LN_EPS) * g_ref[...] + b_ref[...]
        o_ref[:, 0:C] = (y * _sig(y)).astype(MXU_DT)
        o_ref[:, C:C + Mw] = m_ref[...].astype(MXU_DT)

    return _call(body, name=name, grid=(S // tm,),
                 in_specs=[_rowblk(tm, C), _res((1, C)), _res((1, C)), _rowblk(tm, Mw)],
                 out_specs=_rowblk(tm, C + Mw), out_shape=_sds((S, C + Mw), MXU_DT))(
                     v2, ln_g.reshape(1, C), ln_b.reshape(1, C), memo)


def _ln_silu_bwd(v2, ln_g, ln_b, dcat, *, name):
    S, C = v2.shape
    tm = _pick(S, (256, 128))

    def body(v_ref, g_ref, b_ref, dm_ref, dv_ref, dg_ref, db_ref):
        i = pl.program_id(0)
        v = v_ref[...]
        mu = jnp.mean(v, axis=-1, keepdims=True)
        d = v - mu
        rstd = lax.rsqrt(jnp.mean(d * d, axis=-1, keepdims=True) + LN_EPS)
        xh = d * rstd
        y = xh * g_ref[...] + b_ref[...]
        s = _sig(y)
        dyv = dm_ref[...] * (s * (1.0 + y * (1.0 - s)))
        dxh = dyv * g_ref[...]
        dv_ref[...] = rstd * (dxh - jnp.mean(dxh, axis=-1, keepdims=True)
                              - xh * jnp.mean(dxh * xh, axis=-1, keepdims=True))
        pg = jnp.sum(dyv * xh, axis=0, keepdims=True)
        pb = jnp.sum(dyv, axis=0, keepdims=True)

        @pl.when(i == 0)
        def _():
            dg_ref[...] = pg
            db_ref[...] = pb

        @pl.when(i > 0)
        def _():
            dg_ref[...] += pg
            db_ref[...] += pb

    return _call(body, name=name, grid=(S // tm,),
                 in_specs=[_rowblk(tm, C), _res((1, C)), _res((1, C)), _rowblk(tm, C)],
                 out_specs=[_rowblk(tm, C), _res((1, C)), _res((1, C))],
                 out_shape=[_sds((S, C), F32), _sds((1, C), F32), _sds((1, C), F32)])(
                     v2, ln_g.reshape(1, C), ln_b.reshape(1, C), dcat)


def _ffn_act_fwd(ug, uv, w_dw, b_dw, *, name):
    S, Fw = ug.shape
    taps = w_dw.shape[0]
    tc, rows = _pick(Fw, (256, 128)), _pick(S, (256, 128))
    nb, nch = Fw // tc, S // rows

    def body(ug_ref, uv_ref, wg_ref, wv_ref, bg_ref, bv_ref, o_ref, pg_ref, pv_ref):
        pg_ref[0:F_HALO, :] = jnp.zeros((F_HALO, tc), F32)
        pv_ref[0:F_HALO, :] = jnp.zeros((F_HALO, tc), F32)
        pg_ref[F_HALO:F_HALO + S, :] = ug_ref[...]
        pv_ref[F_HALO:F_HALO + S, :] = uv_ref[...]

        def act(c, _):
            gc = _conv_chunk(pg_ref, c, rows, F_HALO, wg_ref, taps) + bg_ref[...]
            vc = _conv_chunk(pv_ref, c, rows, F_HALO, wv_ref, taps) + bv_ref[...]
            o_ref[pl.ds(pl.multiple_of(c * rows, rows), rows), :] = (gc * _sig(gc) * vc).astype(MXU_DT)
            return 0

        lax.fori_loop(0, nch, act, 0)

    b2 = b_dw.reshape(1, 2 * Fw)
    return _call(body, name=name, grid=(nb,),
                 in_specs=[_colblk(S, tc), _colblk(S, tc), _colblk(taps, tc), _colblk(taps, tc, nb),
                           _colblk(1, tc), _colblk(1, tc, nb)],
                 out_specs=_colblk(S, tc), out_shape=_sds((S, Fw), MXU_DT),
                 scratch_shapes=[pltpu.VMEM((S + F_HALO, tc), F32), pltpu.VMEM((S + F_HALO, tc), F32)])(
                     ug, uv, w_dw, w_dw, b2, b2)


def _ffn_act_bwd(ug, uv, dact, w_dw, b_dw, *, name):
    S, Fw = ug.shape
    taps = w_dw.shape[0]
    tc, rows = _pick(Fw, (256, 128)), _pick(S, (256, 128))
    nb, nch = Fw // tc, S // rows

    def body(ug_ref, uv_ref, da_ref, wg_ref, wv_ref, bg_ref, bv_ref, dug_ref, duv_ref, dwg_ref, dwv_ref,
             dbg_ref, dbv_ref, pg_ref, pv_ref, qg_ref, qv_ref):
        pg_ref[0:F_HALO, :] = jnp.zeros((F_HALO, tc), F32)
        pv_ref[0:F_HALO, :] = jnp.zeros((F_HALO, tc), F32)
        qg_ref[S:S + F_HALO, :] = jnp.zeros((F_HALO, tc), F32)
        qv_ref[S:S + F_HALO, :] = jnp.zeros((F_HALO, tc), F32)
        pg_ref[F_HALO:F_HALO + S, :] = ug_ref[...]
        pv_ref[F_HALO:F_HALO + S, :] = uv_ref[...]
        dwg_ref[...] = jnp.zeros((taps, tc), F32)
        dwv_ref[...] = jnp.zeros((taps, tc), F32)

        def grads(c, carry):
            sg, sv = carry
            r = pl.ds(pl.multiple_of(c * rows, rows), rows)
            gc = _conv_chunk(pg_ref, c, rows, F_HALO, wg_ref, taps) + bg_ref[...]
            vc = _conv_chunk(pv_ref, c, rows, F_HALO, wv_ref, taps) + bv_ref[...]
            s = _sig(gc)
            da = da_ref[r, :]
            dgc = da * vc * (s * (1.0 + gc * (1.0 - s)))
            dvc = da * (gc * s)
            qg_ref[r, :] = dgc
            qv_ref[r, :] = dvc
            _conv_wgrad_chunk(pg_ref, c, rows, F_HALO, dgc, dwg_ref, taps)
            _conv_wgrad_chunk(pv_ref, c, rows, F_HALO, dvc, dwv_ref, taps)
            return sg + jnp.sum(dgc, axis=0, keepdims=True), sv + jnp.sum(dvc, axis=0, keepdims=True)

        z = jnp.zeros((1, tc), F32)
        sg, sv = lax.fori_loop(0, nch, grads, (z, z))
        dbg_ref[...] = sg
        dbv_ref[...] = sv

        def back(c, _):
            r = pl.ds(pl.multiple_of(c * rows, rows), rows)
            dug_ref[r, :] = _conv_t_chunk(qg_ref, c, rows, F_HALO, wg_ref, taps).astype(MXU_DT)
            duv_ref[r, :] = _conv_t_chunk(qv_ref, c, rows, F_HALO, wv_ref, taps).astype(MXU_DT)
            return 0

        lax.fori_loop(0, nch, back, 0)

    b2 = b_dw.reshape(1, 2 * Fw)
    pad = pltpu.VMEM((S + F_HALO, tc), F32)
    dug, duv, dwg, dwv, dbg, dbv = _call(
        body, name=name, grid=(nb,),
        in_specs=[_colblk(S, tc), _colblk(S, tc), _colblk(S, tc), _colblk(taps, tc), _colblk(taps, tc, nb),
                  _colblk(1, tc), _colblk(1, tc, nb)],
        out_specs=[_colblk(S, tc), _colblk(S, tc), _colblk(taps, tc), _colblk(taps, tc), _colblk(1, tc),
                   _colblk(1, tc)],
        out_shape=[_sds((S, Fw), MXU_DT), _sds((S, Fw), MXU_DT), _sds((taps, Fw), F32), _sds((taps, Fw), F32),
                   _sds((1, Fw), F32), _sds((1, Fw), F32)],
        scratch_shapes=[pad, pad, pad, pad])(ug, uv, dact, w_dw, w_dw, b2, b2)
    return dug, duv, jnp.concatenate([dwg, dwv], axis=1), jnp.concatenate([dbg, dbv], axis=1)


def _head_mask(h, width):
    lane = lax.broadcasted_iota(jnp.int32, (1, width), 1)
    return (lane >= h * HEAD_DIM) & (lane < (h + 1) * HEAD_DIM)


def _mem_attn_fwd(p, qblk, mkv, l, *, Mw, name):
    S, ML = p.shape[0], mkv.shape[0]
    tm, nh, scale = _pick(S, (256, 128)), Mw // HEAD_DIM, HEAD_DIM ** -0.5

    def body(q_ref, k_ref, v_ref, o_ref):
        q, kv, vv = q_ref[...], k_ref[...], v_ref[...]
        out = jnp.zeros((tm, Mw), F32)
        for h in range(nh):
            mk = _head_mask(h, Mw)
            s = _dot(jnp.where(mk, q, 0.0).astype(MXU_DT), kv, NT) * scale
            e = jnp.exp(s - jnp.max(s, axis=-1, keepdims=True))
            pr = e / jnp.sum(e, axis=-1, keepdims=True)
            out = out + _dot(pr.astype(MXU_DT), jnp.where(mk, vv, jnp.zeros_like(vv)))
        o_ref[...] = out

    return _call(body, name=name, grid=(S // tm,),
                 in_specs=[_rowblk(tm, Mw, qblk), pl.BlockSpec((ML, Mw), lambda i: (0, 2 * l)),
                           pl.BlockSpec((ML, Mw), lambda i: (0, 2 * l + 1))],
                 out_specs=_rowblk(tm, Mw), out_shape=_sds((S, Mw), F32))(p, mkv, mkv)


def _mem_attn_bwd(p, qblk, mkv, l, dcat, doblk, *, Mw, name):
    S, ML = p.shape[0], mkv.shape[0]
    tm, nh, scale = _pick(S, (256, 128)), Mw // HEAD_DIM, HEAD_DIM ** -0.5

    def body(q_ref, k_ref, v_ref, do_ref, dq_ref, dk_ref, dv_ref):
        i = pl.program_id(0)

        @pl.when(i == 0)
        def _():
            dk_ref[...] = jnp.zeros((ML, Mw), F32)
            dv_ref[...] = jnp.zeros((ML, Mw), F32)

        q, kv, vv, do = q_ref[...], k_ref[...], v_ref[...], do_ref[...]
        dq = jnp.zeros((tm, Mw), F32)
        for h in range(nh):
            mk = _head_mask(h, Mw)
            qh = jnp.where(mk, q, 0.0).astype(MXU_DT)
            s = _dot(qh, kv, NT) * scale
            e = jnp.exp(s - jnp.max(s, axis=-1, keepdims=True))
            pr = e / jnp.sum(e, axis=-1, keepdims=True)
            doh = jnp.where(mk, do, 0.0).astype(MXU_DT)
            dv_ref[...] += _dot_tn(pr, doh)
            dp = _dot(doh, vv, NT)
            ds = pr * (dp - jnp.sum(dp * pr, axis=-1, keepdims=True))
            dq = dq + _dot(ds.astype(MXU_DT), jnp.where(mk, kv, jnp.zeros_like(kv))) * scale
            dk_ref[...] += _dot_tn(ds, qh) * scale
        dq_ref[...] = dq

    return _call(body, name=name, grid=(S // tm,),
                 in_specs=[_rowblk(tm, Mw, qblk), pl.BlockSpec((ML, Mw), lambda i: (0, 2 * l)),
                           pl.BlockSpec((ML, Mw), lambda i: (0, 2 * l + 1)), _rowblk(tm, Mw, doblk)],
                 out_specs=[_rowblk(tm, Mw), _res((ML, Mw)), _res((ML, Mw))],
                 out_shape=[_sds((S, Mw), F32), _sds((ML, Mw), F32), _sds((ML, Mw), F32)])(p, mkv, mkv, dcat)


FOX_GROUP = 3


def _foxt_specs(S, dh, tq):
    nb, G = S // tq, FOX_GROUP
    rows = pl.BlockSpec((G, tq, dh), lambda h, i: (h, i, 0))
    seq = pl.BlockSpec((G, S, dh), lambda h, i: (h, 0, 0))
    seq_t = pl.BlockSpec((G, nb, dh, tq), lambda h, i: (h, 0, 0, 0))
    blk_t = pl.BlockSpec((G, dh, tq), lambda h, i: (h, 0, i))
    col = pl.BlockSpec((G, S, 1), lambda h, i: (h, 0, 0))
    row_all = pl.BlockSpec((G, nb, 1, tq), lambda h, i: (h, 0, 0, 0))
    row = pl.BlockSpec((G, 1, 1, tq), lambda h, i: (h, i, 0, 0))
    return rows, seq, seq_t, blk_t, col, row_all, row


def _foxt_logits(kv, qv, cq_row, ck_col, scale, diag):
    s = _dot(kv, qv, NT) * scale + cq_row - ck_col
    if not diag:
        return s
    keys = lax.broadcasted_iota(jnp.int32, s.shape, 0)
    queries = lax.broadcasted_iota(jnp.int32, s.shape, 1)
    return jnp.where(keys <= queries, s, NEG)


def _foxt_fwd(q, k, vt, cq_row, ck_col, *, tq, name):
    H, S, dh = q.shape
    nb, scale, G = S // tq, dh ** -0.5, FOX_GROUP
    assert H % G == 0
    rows, seq, seq_t, blk_t, col, row_all, row = _foxt_specs(S, dh, tq)

    def body(q_ref, k_ref, vt_ref, cq_ref, ck_ref, o_ref, lse_ref):
        i = pl.program_id(1)
        qv, cqv = [q_ref[e] for e in range(G)], [cq_ref[e, i] for e in range(G)]

        def kblock(j, carry, diag):
            r = pl.ds(pl.multiple_of(j * tq, tq), tq)
            out = []
            for e in range(G):
                m, l, acc = carry[e]
                s = _foxt_logits(k_ref[e, r, :], qv[e], cqv[e], ck_ref[e, r, :], scale, diag)
                m2 = jnp.maximum(m, jnp.max(s, axis=0, keepdims=True))
                pr = jnp.exp(s - m2)
                al = jnp.exp(m - m2)
                out.append((m2, al * l + jnp.sum(pr, axis=0, keepdims=True),
                            al * acc + _dot(vt_ref[e, j], pr.astype(MXU_DT))))
            return tuple(out)

        init = tuple((jnp.full((1, tq), NEG, F32), jnp.zeros((1, tq), F32), jnp.zeros((dh, tq), F32))
                     for _ in range(G))
        carry = lax.fori_loop(0, i, lambda j, c: kblock(j, c, False), init)
        for e, (m, l, acc) in enumerate(kblock(i, carry, True)):
            o_ref[e] = acc / l
            lse_ref[e, 0] = m + jnp.log(l)

    return _call(body, name=name, grid=(H // G, nb), in_specs=[rows, seq, seq_t, row_all, col],
                 out_specs=[blk_t, row],
                 out_shape=[_sds((H, dh, S), F32), _sds((H, nb, 1, tq), F32)])(q, k, vt, cq_row, ck_col)


def _foxt_bwd(q, k, kt, v, cq_row, ck_col, ot, lse, dot_, *, tq, name):
    H, S, dh = q.shape
    nb, scale, G = S // tq, dh ** -0.5, FOX_GROUP
    rows, seq, seq_t, blk_t, col, row_all, row = _foxt_specs(S, dh, tq)

    def body(q_ref, k_ref, kt_ref, v_ref, cq_ref, ck_ref, ot_ref, lse_ref, dot_ref, dq_ref, dk_ref, dv_ref,
             dcq_ref, dck_ref):
        i = pl.program_id(1)

        @pl.when(i == 0)
        def _():
            dk_ref[...] = jnp.zeros((G, S, dh), F32)
            dv_ref[...] = jnp.zeros((G, S, dh), F32)
            dck_ref[...] = jnp.zeros((G, S, 1), F32)

        qv, cqv, lsev = [q_ref[e] for e in range(G)], [cq_ref[e, i] for e in range(G)], [lse_ref[e, 0] for e in range(G)]
        dob = [dot_ref[e].astype(MXU_DT) for e in range(G)]
        delta = [jnp.sum(dob[e].astype(F32) * ot_ref[e], axis=0, keepdims=True) for e in range(G)]

        def kblock(j, carry, diag):
            r = pl.ds(pl.multiple_of(j * tq, tq), tq)
            out = []
            for e in range(G):
                dq, rs = carry[e]
                pr = jnp.exp(_foxt_logits(k_ref[e, r, :], qv[e], cqv[e], ck_ref[e, r, :], scale, diag) - lsev[e])
                ds = pr * (_dot(v_ref[e, r, :], dob[e]) - delta[e])
                dsb = ds.astype(MXU_DT)
                dk_ref[e, r, :] += _dot(dsb, qv[e]) * scale
                dv_ref[e, r, :] += _dot(pr.astype(MXU_DT), dob[e], NT)
                dck_ref[e, r, :] += -jnp.sum(ds, axis=1, keepdims=True)
                out.append((dq + _dot(kt_ref[e, j], dsb), rs + jnp.sum(ds, axis=0, keepdims=True)))
            return tuple(out)

        init = tuple((jnp.zeros((dh, tq), F32), jnp.zeros((1, tq), F32)) for _ in range(G))
        carry = lax.fori_loop(0, i, lambda j, c: kblock(j, c, False), init)
        for e, (dq, rs) in enumerate(kblock(i, carry, True)):
            dq_ref[e] = dq * scale
            dcq_ref[e, 0] = rs

    return _call(body, name=name, grid=(H // G, nb),
                 in_specs=[rows, seq, seq_t, seq, row_all, col, blk_t, row, blk_t],
                 out_specs=[blk_t, seq, seq, row, col],
                 out_shape=[_sds((H, dh, S), F32), _sds((H, S, dh), F32), _sds((H, S, dh), F32),
                            _sds((H, nb, 1, tq), F32), _sds((H, S, 1), F32)])(
                                q, k, kt, v, cq_row, ck_col, ot, lse, dot_)


def _tri(n, lower):
    r = lax.broadcasted_iota(jnp.int32, (n, n), 0)
    c = lax.broadcasted_iota(jnp.int32, (n, n), 1)
    return ((c <= r) if lower else (c >= r)).astype(F32)


def _fgate_fwd(fr, bf, *, name):
    S, W = fr.shape
    B = _pick(S, (256, 128))

    def body(f_ref, b_ref, cum_ref):
        L = _tri(B, True)
        carry = jnp.zeros((1, W), F32)
        for blk in range(S // B):
            z = f_ref[blk * B:(blk + 1) * B, :] + b_ref[...]
            ls = jnp.minimum(z, 0.0) - jnp.log(1.0 + jnp.exp(-jnp.abs(z)))
            cum_ref[blk * B:(blk + 1) * B, :] = jnp.dot(L, ls, precision=lax.Precision.HIGHEST,
                                                        preferred_element_type=F32) + carry
            carry = carry + jnp.sum(ls, axis=0, keepdims=True)

    return _call(body, name=name, out_shape=_sds((S, W), F32))(fr, bf)


def _fgate_bwd(fr, bf, dcum, *, name):
    S, W = fr.shape
    B = _pick(S, (256, 128))

    def body(f_ref, b_ref, dc_ref, df_ref, db_ref):
        U = _tri(B, False)
        carry = jnp.zeros((1, W), F32)
        dbs = jnp.zeros((1, W), F32)
        for blk in reversed(range(S // B)):
            dc = dc_ref[blk * B:(blk + 1) * B, :]
            dls = jnp.dot(U, dc, precision=lax.Precision.HIGHEST, preferred_element_type=F32) + carry
            carry = carry + jnp.sum(dc, axis=0, keepdims=True)
            z = f_ref[blk * B:(blk + 1) * B, :] + b_ref[...]
            df = dls * (1.0 / (1.0 + jnp.exp(z)))
            df_ref[blk * B:(blk + 1) * B, :] = df
            dbs = dbs + jnp.sum(df, axis=0, keepdims=True)
        db_ref[...] = dbs

    return _call(body, name=name, out_shape=[_sds((S, W), F32), _sds((1, W), F32)])(fr, bf, dcum)


def _flip(v, bit):
    return 1 - v if bit else v


HBM_SPEC = pl.BlockSpec(memory_space=pltpu.HBM)
SEM_SPEC = pl.BlockSpec(memory_space=pltpu.SEMAPHORE)


def _xchg_copies(src, land, sems, scatter):
    n = len(src)
    send, recv, loc = sems[:7 * n], sems[7 * n:14 * n], sems[14 * n:15 * n]
    x, y, c = lax.axis_index("x"), lax.axis_index("y"), lax.axis_index("c")
    me = 4 * x + 2 * y + c

    def peer(m):
        return _flip(x, m & 4), _flip(y, m & 2), _flip(c, m & 1)

    def copy(i, m):
        px, py, pc = peer(m)
        return pltpu.make_async_remote_copy(
            src_ref=src[i].at[4 * px + 2 * py + pc] if scatter[i] else src[i], dst_ref=land[i].at[me],
            send_sem=send[7 * i + m - 1], recv_sem=recv[7 * i + m - 1], device_id=(px, py, pc), device_id_type=MESH)

    def arrival(i, m):
        px, py, pc = peer(m)
        slot = land[i].at[4 * px + 2 * py + pc]
        return pltpu.make_async_remote_copy(src_ref=slot, dst_ref=slot, send_sem=send[7 * i + m - 1],
                                            recv_sem=recv[7 * i + m - 1], device_id=(px, py, pc), device_id_type=MESH)

    def own(i):
        return pltpu.make_async_copy(src[i].at[me] if scatter[i] else src[i], land[i].at[me], loc[i])

    return copy, arrival, own


def _xchg_start(srcs, scatter, *, name):
    n = len(srcs)
    lands = [_sds((N_DEV,) + s.shape[-2:], s.dtype) for s in srcs]

    ns = 15 * n

    def body(*refs):
        src, land, sems, token = refs[:n], refs[n:2 * n], refs[2 * n:2 * n + ns], refs[-1]
        copy, _, own = _xchg_copies(src, land, sems, scatter)
        for i in range(n):
            own(i).start()
            for m in range(1, N_DEV):
                copy(i, m).start()
        token[...] = jnp.zeros(token.shape, F32)

    thru = [pltpu.HBM(s.shape, s.dtype) for s in srcs] + [pltpu.HBM(s.shape, s.dtype) for s in lands]
    out = pl.pallas_call(
        body, name=name,
        out_shape=(*[pltpu.SemaphoreType.DMA(())] * ns, *thru, _sds((8, 128), F32)),
        in_specs=[HBM_SPEC] * (2 * n),
        out_specs=(*[SEM_SPEC] * ns, *[HBM_SPEC] * (2 * n), pl.BlockSpec(memory_space=pltpu.VMEM)),
        input_output_aliases={i: ns + i for i in range(2 * n)},
        compiler_params=pltpu.CompilerParams(has_side_effects=pltpu.SideEffectType.DATAFLOW_SIDE_EFFECTING),
    )(*[pltpu.with_memory_space_constraint(s, pltpu.HBM) for s in srcs],
      *[pltpu.with_memory_space_constraint(lax.empty(s.shape, s.dtype), pltpu.HBM) for s in lands])
    bufs = list(out[ns:ns + 2 * n])
    return (list(out[:ns]), bufs[:n], bufs[n:]), out[-1]


def _xchg_wait(handle, after, scatter, *, name):
    sems, srcs, lands = handle
    n = len(srcs)
    ns = 15 * n

    def body(*refs):
        src, land = refs[:n], refs[n:2 * n]
        copy, arrival, own = _xchg_copies(src, land, refs[2 * n:2 * n + ns], scatter)
        for i in range(n):
            own(i).wait()
            for m in range(1, N_DEV):
                copy(i, m).wait_send()
                arrival(i, m).wait_recv()

    out = pl.pallas_call(
        body, name=name,
        out_shape=tuple(pltpu.HBM(s.shape, s.dtype) for s in srcs + lands),
        in_specs=[HBM_SPEC] * (2 * n) + [SEM_SPEC] * ns + [ANY],
        out_specs=tuple([HBM_SPEC] * (2 * n)),
        input_output_aliases={i: i for i in range(2 * n)},
        compiler_params=pltpu.CompilerParams(has_side_effects=pltpu.SideEffectType.DATAFLOW_SIDE_EFFECTING),
    )(*srcs, *lands, *sems, after)
    return list(out[n:])


def _reduce_adam_body(r_ref, w_ref, m_ref, v_ref, g_ref, d_ref, m2_ref, v2_ref):
    g = r_ref[0].astype(F32)
    for s in range(1, N_DEV):
        g = g + r_ref[s].astype(F32)
    mm = ADAM_B1 * m_ref[...] + (1.0 - ADAM_B1) * g
    vv = ADAM_B2 * v_ref[...] + (1.0 - ADAM_B2) * (g * g)
    m_hat = mm / (1.0 - ADAM_B1 ** ADAM_STEP)
    v_hat = vv / (1.0 - ADAM_B2 ** ADAM_STEP)
    g_ref[...] = g
    d_ref[...] = -ADAM_LR * (m_hat / (jnp.sqrt(v_hat) + ADAM_EPS) + ADAM_WD * w_ref[...])
    m2_ref[...] = mm
    v2_ref[...] = vv


def _reduce_adam(recv, w, m, v, *, name):
    R, L = w.shape
    tr = _pick(R, (256, 128, 64, 32, 16, 8))

    def body(*refs):
        _reduce_adam_body(*refs)

    blk = _rowblk(tr, L)
    return _call(body, name=name, grid=(R // tr,),
                 in_specs=[pl.BlockSpec((N_DEV, tr, L), lambda i: (0, i, 0)), blk, blk, blk],
                 out_specs=[blk, blk, blk, blk], out_shape=[_sds((R, L), F32)] * 4)(recv, w, m, v)


def _reduce_adam_layer(recv, w, m, v, idx, prev, *, name, after=None):
    r, c = w.shape[-2:]
    tr = _pick(r, (256, 176, 128, 112, 64, 32, 16))
    if prev is None:
        prev = [lax.empty(w.shape, F32) for _ in range(4)]
    behind = [] if after is None else [after]

    def body(r_ref, w_ref, m_ref, v_ref, *rest):
        _reduce_adam_body(r_ref, w_ref, m_ref, v_ref, *rest[-4:])

    blk = pl.BlockSpec((None, tr, c), lambda i: (idx, i, 0))
    return pl.pallas_call(
        body, name=name, grid=(r // tr,),
        in_specs=[pl.BlockSpec((N_DEV, tr, c), lambda i: (0, i, 0)), blk, blk, blk] + [ANY] * (4 + len(behind)),
        out_specs=[blk] * 4, out_shape=[_sds(w.shape, F32)] * 4, input_output_aliases={4 + j: j for j in range(4)},
        compiler_params=pltpu.CompilerParams(vmem_limit_bytes=VMEM_LIMIT_V7X, dimension_semantics=("arbitrary",)),
    )(recv, w, m, v, *prev, *behind)


class _Pack:
    def __init__(self, shapes, row_mult):
        self.shapes, self.offs, rows = dict(shapes), {}, 0
        for name, shp in shapes:
            size = 1
            for d in shp:
                size *= d
            nr = -(-size // (16 * PACK_LANES)) * 16
            self.offs[name] = (rows, size, nr)
            rows += nr
        self.used = rows
        self.rows = -(-rows // row_mult) * row_mult

    def pack(self, arrays, dtype, lead=()):
        parts = []
        for name, (r0, size, nr) in self.offs.items():
            flat = arrays[name].astype(dtype).reshape(lead + (size,))
            flat = jnp.pad(flat, [(0, 0)] * len(lead) + [(0, nr * PACK_LANES - size)])
            parts.append(flat.reshape(lead + (nr, PACK_LANES)))
        if self.rows > self.used:
            parts.append(jnp.zeros(lead + (self.rows - self.used, PACK_LANES), dtype))
        return jnp.concatenate(parts, axis=len(lead))

    def unpack(self, buf, lead=()):
        out = {}
        for name, (r0, size, nr) in self.offs.items():
            flat = buf[..., r0:r0 + nr, :].reshape(lead + (nr * PACK_LANES,))
            out[name] = flat[..., :size].reshape(lead + tuple(self.shapes[name]))
        return out


def _to_full(g8, ax):
    t = jnp.moveaxis(g8, 0, ax)
    return t.reshape(t.shape[:ax] + (t.shape[ax] * t.shape[ax + 1],) + t.shape[ax + 2:])


def _to_shards(full, ax):
    shp = full.shape
    return jnp.moveaxis(full.reshape(shp[:ax] + (N_DEV, shp[ax] // N_DEV) + shp[ax + 1:]), ax, 0)


def _to_heads(a, H):
    S = a.shape[0]
    return a.reshape(S, H, HEAD_DIM).transpose(1, 0, 2)


def _from_heads(a):
    H, S, dh = a.shape
    return a.transpose(1, 0, 2).reshape(S, H * dh)


def _to_heads_t(a, H, tq):
    S = a.shape[0]
    return a.reshape(S // tq, tq, H, HEAD_DIM).transpose(2, 0, 3, 1)


def _from_heads_t(a):
    H, dh, S = a.shape
    return a.transpose(2, 0, 1).reshape(S, H * dh)


def kernel(x, mem, g_mix, w_in_a, b_glu, w_dw_a, b_dw_a, ln_g, ln_b, g_kv, w_kvf, b_f, w_in_b, g_mem, w_mem_kv, w_out, g_ffn, w_up, w_dw_f, b_dw_f, w_down, g_final, loss_target, m_g_mix, m_w_in_a, m_b_glu, m_w_dw_a, m_b_dw_a, m_ln_g, m_ln_b, m_g_kv, m_w_kvf, m_b_f, m_w_in_b, m_g_mem, m_w_mem_kv, m_w_out, m_g_ffn, m_w_up, m_w_dw_f, m_b_dw_f, m_w_down, m_g_final, v_g_mix, v_w_in_a, v_b_glu, v_w_dw_a, v_b_dw_a, v_ln_g, v_ln_b, v_g_kv, v_w_kvf, v_b_f, v_w_in_b, v_g_mem, v_w_mem_kv, v_w_out, v_g_ffn, v_w_up, v_w_dw_f, v_b_dw_f, v_w_down, v_g_final):
    given = dict(locals())
    W = {n: given[n] for n in WEIGHTS}
    x0, mem0, tgt = x[0], mem[0], loss_target[0]
    S, D = x0.shape
    depth, n_a = g_mix.shape[0], w_in_a.shape[0]
    C = w_dw_a.shape[2] * N_DEV
    Mw = D - C
    Fw = w_down.shape[1] * N_DEV
    H = b_f.shape[0]
    assert C == H * HEAD_DIM and (2 * C) % Mw == 0 and C % Mw == 0 and H <= GATE_LANES
    tq = _pick(S, (256, 128))
    nkv = 2 * C + GATE_LANES

    def mix_keys(l):
        keys = [("w_in_a", l) if l < n_a else ("w_in_b", l - n_a), ("w_mem_kv", l), ("w_out", l)]
        return keys + ([("w_kvf", 0)] if l == n_a else [])

    def ffn_keys(l):
        return [("w_up", l), ("w_down", l)]

    def key_ax(key):
        return big_ax[key[0]] - 1

    transposed = ("w_up", "w_in_a")

    def stacked(d, n, pre=""):
        a = d[pre + n]
        return a[None] if n == "w_kvf" else (jnp.swapaxes(a, 1, 2) if n in transposed else a)

    W3 = {n: stacked(W, n) for n, _ in BIG}
    big_ax = {n: (1 if n == "w_kvf" or n in transposed else ax) for n, ax in BIG}
    pk_small = _Pack([(n, W[n].shape) for n, _ in SMALL], 8)
    pk_rep = _Pack([(n, W[n].shape) for n in REP] + [("loss", (1,))], 8)

    ws32 = pk_small.pack(W, F32)
    gathers, toks = {}, []
    for l in range(depth):
        for tag, keys in (("mix", mix_keys(l)), ("up", [("w_up", l)]), ("down", [("w_down", l)])):
            srcs = [W3[n][i].astype(COMM_DT) for n, i in keys] + ([ws32] if (l, tag) == (0, "mix") else [])
            gathers[l, tag], t = _xchg_start(srcs, [False] * len(srcs), name=f"w_gather_start_{tag}{l}")
            toks.append(t)
    tok = sum(t[0, 0] for t in toks)
    g_mix, g_mem = g_mix + tok, g_mem + tok
    bf_pad = jnp.pad(b_f, (0, GATE_LANES - H)).reshape(1, GATE_LANES)

    def gathered(l, tag, keys, after):
        n = len(gathers[l, tag][1])
        lands = _xchg_wait(gathers[l, tag], after, [False] * n, name=f"w_gather_wait_{tag}{l}")
        return {k: _to_full(a, key_ax(k)).astype(MXU_DT) for k, a in zip(keys, lands)}, lands[len(keys):]

    mem_n = _rms_fwd(mem0, g_mem, name="mem_norm")
    sv = []
    xs = x0
    for l in range(depth):
        wl, extra = gathered(l, "mix", mix_keys(l), xs)
        if extra:
            gs = pk_small.unpack(extra[0], (N_DEV,))
            small = {n: _to_full(gs[n], ax) for n, ax in SMALL}
        t = dict(x_in=xs, w=wl)
        t["mkv"] = _mm(mem_n, wl["w_mem_kv", l], name=f"mem_kv{l}", out_dtype=MXU_DT)
        t["h"] = _rms_fwd(xs, g_mix[l], name=f"mix_norm{l}")
        if l < n_a:
            t["p"] = _mm(t["h"], wl["w_in_a", l], name=f"in_proj{l}", out_dtype=F32, nt=True)
            t["v2"] = _glu_conv_fwd(t["p"], small["b_glu"][l], small["w_dw_a"][l], small["b_dw_a"][l],
                                    name=f"glu_conv{l}")
            memo = _mem_attn_fwd(t["p"], 2 * C // Mw, t["mkv"], 0, Mw=Mw, name=f"mem_attn{l}")
            t["cat"] = _ln_silu_cat_fwd(t["v2"], small["ln_g"][l], small["ln_b"][l], memo, name=f"ln_silu{l}")
        else:
            if l == n_a:
                wkvf = jnp.pad(wl["w_kvf", 0], ((0, 0), (0, nkv - w_kvf.shape[1])))
                hk = _rms_fwd(xs, g_kv, name="kv_norm")
                kvf = _mm(hk, wkvf, name="kv_proj", out_dtype=F32)
                kb, vb = kvf[:, :C].astype(MXU_DT), kvf[:, C:2 * C].astype(MXU_DT)
                k_h, v_h = _to_heads(kb, H), _to_heads(vb, H)
                kt_h, vt_h = _to_heads_t(kb, H, tq), _to_heads_t(vb, H, tq)
                fr = kvf[:, 2 * C:]
                cum = _fgate_fwd(fr, bf_pad, name="fgate")
                cum_t = cum[:, :H].T
                cq_row, ck_col = cum_t.reshape(H, S // tq, 1, tq), cum_t.reshape(H, S, 1)
            t["p"] = _mm(t["h"], wl["w_in_b", l - n_a], name=f"in_proj{l}", out_dtype=F32)
            t["q_h"] = _to_heads(t["p"][:, :C], H).astype(MXU_DT)
            t["ot"], t["lse"] = _foxt_fwd(t["q_h"], k_h, vt_h, cq_row, ck_col, tq=tq, name=f"fox{l}")
            memo = _mem_attn_fwd(t["p"], C // Mw, t["mkv"], 0, Mw=Mw, name=f"mem_attn{l}")
            t["cat"] = jnp.concatenate([_from_heads_t(t["ot"]), memo], axis=1).astype(MXU_DT)
        t["x_mid"] = _mm(t["cat"], wl["w_out", l], name=f"out_proj{l}", out_dtype=F32, add=xs)
        wl.update(gathered(l, "up", [("w_up", l)], t["x_mid"])[0])
        t["h2"] = _rms_fwd(t["x_mid"], g_ffn[l], name=f"ffn_norm{l}")
        t["ug"] = _mm(t["h2"], wl["w_up", l], name=f"up_gate{l}", out_dtype=F32, nt=True, cols=(0, Fw))
        t["uv"] = _mm(t["h2"], wl["w_up", l], name=f"up_val{l}", out_dtype=F32, nt=True, cols=(Fw, Fw))
        t["act"] = _ffn_act_fwd(t["ug"], t["uv"], small["w_dw_f"][l], b_dw_f[l], name=f"ffn_act{l}")
        wl.update(gathered(l, "down", [("w_down", l)], t["act"])[0])
        xs = _mm(t["act"], wl["w_down", l], name=f"down_proj{l}", out_dtype=F32, add=t["x_mid"])
        sv.append(t)
    loss_dev, dx, dg_final = _loss_bwd(xs, g_final, tgt, name="loss_head")

    M1 = {n: given["m_" + n] for n in WEIGHTS}
    V1 = {n: given["v_" + n] for n in WEIGHTS}
    M3 = {n: stacked(given, n, "m_") for n, _ in BIG}
    V3 = {n: stacked(given, n, "v_") for n, _ in BIG}
    res, chain, pending, waited = {}, {}, [], []

    def start_grads(tag, l, keys, gl, extra=(), extra_scatter=()):
        srcs = [_to_shards(gl[k], key_ax(k)) for k in keys] + list(extra)
        scatter = [True] * len(keys) + list(extra_scatter)
        handle, tk = _xchg_start(srcs, scatter, name=f"g_xchg_start_{tag}{l}")
        pending.append((f"{tag}{l}", keys, handle, scatter))
        return tk

    def finish_grads(after):
        tag, keys, handle, scatter = pending.pop(0)
        lands = _xchg_wait(handle, after, scatter, name=f"g_xchg_wait_{tag}")
        waited.extend(zip(keys, lands))
        return lands[len(keys):]

    def update_waited(after=None):
        for (n, i), recv in waited:
            chain[n] = _reduce_adam_layer(recv, W3[n], M3[n], V3[n], i, chain.get(n), name=f"adam_{n}{i}",
                                          after=after)
        waited.clear()

    G = {n: [None] * W[n].shape[0] for n in ("g_mix", "b_glu", "w_dw_a", "b_dw_a", "ln_g", "ln_b", "g_ffn",
                                              "w_dw_f", "b_dw_f")}
    dk_sum = dv_sum = dck_sum = dmem_n = None
    started = None
    for l in reversed(range(depth)):
        t = sv[l]
        wl, gl = t["w"], {}
        dact = _mm(dx, wl["w_down", l], name=f"d_act{l}", out_dtype=F32, nt=True, after=started)
        gl["w_down", l] = _mm_tn(t["act"], dx, name=f"dw_down{l}", out_dtype=COMM_DT)
        dug, duv, G["w_dw_f"][l], db = _ffn_act_bwd(t["ug"], t["uv"], dact, small["w_dw_f"][l], b_dw_f[l],
                                                    name=f"d_ffn_act{l}")
        G["b_dw_f"][l] = db[0]
        dh2 = _mm_nn2(dug, duv, wl["w_up", l], name=f"d_up{l}")
        half = _mm_tn(dug, t["h2"], name=f"dw_up_gate{l}", out_dtype=COMM_DT, rows=(0, 2 * Fw))
        gl["w_up", l] = _mm_tn(duv, t["h2"], name=f"dw_up_val{l}", out_dtype=COMM_DT, rows=(Fw, 2 * Fw), prev=half)
        dx, dg = _rms_bwd(t["x_mid"], g_ffn[l], dh2, dx, name=f"d_ffn_norm{l}")
        G["g_ffn"][l] = dg[0]
        started = start_grads("ffn", l, ffn_keys(l), gl)
        if len(pending) > 2:
            finish_grads(dx)

        dcat = _mm(dx, wl["w_out", l], name=f"d_cat{l}", out_dtype=F32, nt=True, after=started)
        gl["w_out", l] = _mm_tn(t["cat"], dx, name=f"dw_out{l}", out_dtype=COMM_DT)
        if l >= n_a:
            dot_h = dcat[:, :C].T.reshape(H, HEAD_DIM, S)
            dqt_h, dk_h, dv_h, dcq, dck = _foxt_bwd(t["q_h"], k_h, kt_h, v_h, cq_row, ck_col, t["ot"], t["lse"],
                                                    dot_h, tq=tq, name=f"d_fox{l}")
            dck = dck.reshape(H, S) + dcq.reshape(H, S)
            dk_sum = dk_h if dk_sum is None else dk_sum + dk_h
            dv_sum = dv_h if dv_sum is None else dv_sum + dv_h
            dck_sum = dck if dck_sum is None else dck_sum + dck
            dqm, dmk, dmv = _mem_attn_bwd(t["p"], C // Mw, t["mkv"], 0, dcat, C // Mw, Mw=Mw,
                                          name=f"d_mem_attn{l}")
            dp = jnp.concatenate([_from_heads_t(dqt_h), dqm], axis=1).astype(MXU_DT)
            key = ("w_in_b", l - n_a)
        else:
            dv2, dlg, dlb = _ln_silu_bwd(t["v2"], small["ln_g"][l], small["ln_b"][l], dcat, name=f"d_ln_silu{l}")
            G["ln_g"][l], G["ln_b"][l] = dlg[0], dlb[0]
            da, dgt, G["w_dw_a"][l], dbd, dba, dbg = _glu_conv_bwd(t["p"], small["b_glu"][l], small["w_dw_a"][l],
                                                                   dv2, name=f"d_glu_conv{l}")
            G["b_dw_a"][l] = dbd[0]
            G["b_glu"][l] = jnp.concatenate([dba[0], dbg[0]])
            dqm, dmk, dmv = _mem_attn_bwd(t["p"], 2 * C // Mw, t["mkv"], 0, dcat, C // Mw, Mw=Mw,
                                          name=f"d_mem_attn{l}")
            dp = jnp.concatenate([da, dgt, dqm], axis=1).astype(MXU_DT)
            key = ("w_in_a", l)
        dmkv = jnp.concatenate([dmk, dmv], axis=1).astype(MXU_DT)
        gl["w_mem_kv", l] = _mm_tn(mem_n, dmkv, name=f"dw_mem_kv{l}", out_dtype=COMM_DT)
        dmem_n = _mm(dmkv, wl["w_mem_kv", l], name=f"d_mem_kv{l}", out_dtype=F32, nt=True, add=dmem_n)
        if key[0] in transposed:
            dh = _mm(dp, wl[key], name=f"d_in_proj{l}", out_dtype=F32)
            gl[key] = _mm_tn(dp, t["h"], name=f"dw_in_proj{l}", out_dtype=COMM_DT)
        else:
            dh = _mm(dp, wl[key], name=f"d_in_proj{l}", out_dtype=F32, nt=True)
            gl[key] = _mm_tn(t["h"], dp, name=f"dw_in_proj{l}", out_dtype=COMM_DT)
        dx, dg = _rms_bwd(t["x_in"], g_mix[l], dh, dx, name=f"d_mix_norm{l}")
        G["g_mix"][l] = dg[0]
        if l == n_a:
            dcum = jnp.pad(dck_sum.T, ((0, 0), (0, GATE_LANES - H)))
            df, dbf = _fgate_bwd(fr, bf_pad, dcum, name="d_fgate")
            dkvf = jnp.concatenate([_from_heads(dk_sum), _from_heads(dv_sum), df], axis=1).astype(MXU_DT)
            dhk = _mm(dkvf, wkvf, name="d_kv_proj", out_dtype=F32, nt=True)
            gl["w_kvf", 0] = _mm_tn(hk, dkvf, name="dw_kv_proj", out_dtype=COMM_DT)[:, :w_kvf.shape[1]]
            dx, dg_kv = _rms_bwd(t["x_in"], g_kv, dhk, dx, name="d_kv_norm")
        if l > 0:
            started = start_grads("mix", l, mix_keys(l), gl)
            if len(pending) > 2:
                finish_grads(dx)

    _, dg_mem = _rms_bwd(mem0, g_mem, dmem_n, None, name="d_mem_norm")
    grads = {n: jnp.stack(v) for n, v in G.items()}
    grads.update(g_kv=dg_kv[0], b_f=dbf[0, :H], g_mem=dg_mem[0], g_final=dg_final[0], loss=loss_dev.reshape(1))
    gs8 = pk_small.pack({n: _to_shards(grads[n], ax) for n, ax in SMALL}, F32, (N_DEV,))
    last = start_grads("mix", 0, mix_keys(0), gl, [gs8, pk_rep.pack(grads, F32)], [True, False])
    while len(pending) > 1:
        finish_grads(dx)
    update_waited(after=last)
    extra = finish_grads(chain["w_down"][0])
    update_waited()
    no_state = dict(loss=jnp.zeros((1,), F32))
    for pk, recv, tag in ((pk_small, extra[0], "small"), (pk_rep, extra[1], "rep")):
        w32 = ws32 if tag == "small" else pk.pack({**W, **no_state}, F32)
        outs = _reduce_adam(recv, w32, pk.pack({**M1, **no_state}, F32), pk.pack({**V1, **no_state}, F32),
                            name=f"adam_{tag}")
        for kind, buf in zip(("grad", "delta", "new_m", "new_v"), outs):
            for n, a in pk.unpack(buf).items():
                res[kind, n] = a
    for n, outs in chain.items():
        for kind, a in zip(("grad", "delta", "new_m", "new_v"), outs):
            res[kind, n] = jnp.swapaxes(a, 1, 2) if n in transposed else a.reshape(W[n].shape)

    loss = res["grad", "loss"][0]
    return (loss, dx[None], *[res[kind, n] for kind in ("grad", "delta", "new_m", "new_v") for n in WEIGHTS])
```

```python
import jax
import jax.numpy as jnp
from jax import lax
from jax.experimental import pallas as pl
from jax.experimental.pallas import tpu as pltpu

F32 = jnp.float32
MXU_DT = jnp.bfloat16
COMM_DT = jnp.bfloat16

N_DEV = 8
HEAD_DIM = 64
RMS_EPS = 1e-6
LN_EPS = 1e-5
ADAM_LR = 0.001
ADAM_B1 = 0.9
ADAM_B2 = 0.999
ADAM_EPS = 1e-08
ADAM_WD = 0.01
ADAM_STEP = 10

PACK_LANES = 1024
GATE_LANES = 128
VMEM_LIMIT_V7X = 56 << 20
NEG = -1e30
MESH = pl.DeviceIdType.MESH
ANY = pl.BlockSpec(memory_space=pl.ANY)
NT = (((1,), (1,)), ((), ()))
NN = (((1,), (0,)), ((), ()))

BIG = (("w_in_a", 2), ("w_kvf", 0), ("w_in_b", 1), ("w_mem_kv", 1), ("w_out", 1), ("w_up", 2), ("w_down", 1))
SMALL = (("b_glu", 1), ("w_dw_a", 2), ("b_dw_a", 1), ("ln_g", 1), ("ln_b", 1), ("w_dw_f", 2))
REP = ("g_mix", "g_kv", "b_f", "g_mem", "g_ffn", "b_dw_f", "g_final")
WEIGHTS = ("g_mix", "w_in_a", "b_glu", "w_dw_a", "b_dw_a", "ln_g", "ln_b", "g_kv", "w_kvf", "b_f", "w_in_b",
           "g_mem", "w_mem_kv", "w_out", "g_ffn", "w_up", "w_dw_f", "b_dw_f", "w_down", "g_final")


def _sds(shape, dtype):
    return jax.ShapeDtypeStruct(tuple(shape), dtype)


def _call(body, *, name, out_shape, grid=(), in_specs=None, out_specs=None, scratch_shapes=(), aliases=None):
    params = dict(vmem_limit_bytes=VMEM_LIMIT_V7X)
    if grid:
        params["dimension_semantics"] = ("arbitrary",) * len(grid)
    kw = {} if aliases is None else dict(input_output_aliases=aliases)
    if in_specs is not None:
        kw["in_specs"] = in_specs
    if out_specs is not None:
        kw["out_specs"] = out_specs
    return pl.pallas_call(body, name=name, grid=grid, out_shape=out_shape, scratch_shapes=list(scratch_shapes),
                          compiler_params=pltpu.CompilerParams(**params), **kw)


def _res(shape):
    nd = len(shape)
    return pl.BlockSpec(tuple(shape), lambda *_: (0,) * nd)


def _colblk(rows, tc, off=0):
    return pl.BlockSpec((rows, tc), lambda j: (0, j + off))


def _rowblk(tm, cols, off=0):
    return pl.BlockSpec((tm, cols), lambda i: (i, off))


def _pick(n, opts=(512, 256, 128)):
    for t in opts:
        if n % t == 0:
            return t
    return n


def _sig(z):
    return 1.0 / (1.0 + jnp.exp(-z))


def _dot(a, b, dims=NN):
    return lax.dot_general(a, b, dims, preferred_element_type=F32)


def _dot_tn(a, b):
    return _dot(a.T.astype(b.dtype), b)


def _mm_tn(a, b, *, name, out_dtype, rows=None, prev=None):
    S, K = a.shape
    N = b.shape[1]
    tk, rc = _pick(K, (256, 128)), _pick(S)
    cast = b.dtype != MXU_DT
    first, total = (0, K) if rows is None else rows
    assert first % tk == 0
    chained = prev is not None

    def body(a_ref, b_ref, *rest):
        o_ref, acc_ref, bb = rest[chained], rest[chained + 1], rest[chained + 2:]
        if cast:
            @pl.when(pl.program_id(0) == 0)
            def _():
                for r0 in range(0, S, rc):
                    bb[0][r0:r0 + rc, :] = b_ref[r0:r0 + rc, :].astype(MXU_DT)
            b_ref = bb[0]
        for n, r0 in enumerate(range(0, S, rc)):
            part = _dot(a_ref[r0:r0 + rc, :].astype(F32).T.astype(MXU_DT), b_ref[r0:r0 + rc, :])
            if n == 0:
                acc_ref[...] = part
            else:
                acc_ref[...] += part
        o_ref[...] = acc_ref[...].astype(out_dtype)

    return _call(body, name=name, grid=(K // tk,), in_specs=[_colblk(S, tk), _res((S, N))] + [ANY] * chained,
                 out_specs=pl.BlockSpec((tk, N), lambda j: (j + first // tk, 0)),
                 out_shape=_sds((total, N), out_dtype), aliases={2: 0} if chained else None,
                 scratch_shapes=[pltpu.VMEM((tk, N), F32)] + ([pltpu.VMEM((S, N), MXU_DT)] if cast else []))(
                     a, b, *([prev] if chained else []))


def _mm_nn2(a1, a2, w, *, name):
    M, Fw = a1.shape
    N = w.shape[1]
    tm, nc = _pick(M), _pick(N)

    def body(a1_ref, a2_ref, w_ref, o_ref):
        v1, v2 = a1_ref[...], a2_ref[...]
        for n0 in range(0, N, nc):
            o_ref[:, n0:n0 + nc] = _dot(v1, w_ref[0:Fw, n0:n0 + nc]) + _dot(v2, w_ref[Fw:2 * Fw, n0:n0 + nc])

    return _call(body, name=name, grid=(M // tm,), in_specs=[_rowblk(tm, Fw), _rowblk(tm, Fw), _res(w.shape)],
                 out_specs=_rowblk(tm, N), out_shape=_sds((M, N), F32))(a1, a2, w)


def _mm(a, w, *, name, out_dtype, nt=False, add=None, cols=None, after=None):
    M, K = a.shape
    N = w.shape[0] if nt else w.shape[1]
    assert (w.shape[1] if nt else w.shape[0]) == K
    dims = NT if nt else NN
    has_add, has_after = add is not None, after is not None
    if cols is not None:
        assert not has_add and not has_after
        c0, N = cols
        tn, rc = _pick(N), _pick(M)
        assert c0 % tn == 0

        def body(a_ref, w_ref, o_ref):
            wv = w_ref[...]
            for r0 in range(0, M, rc):
                o_ref[r0:r0 + rc, :] = _dot(a_ref[r0:r0 + rc, :], wv, dims).astype(out_dtype)

        w_spec = (pl.BlockSpec((tn, K), lambda j: (j + c0 // tn, 0)) if nt else _colblk(K, tn, c0 // tn))
        return _call(body, name=name, grid=(N // tn,), in_specs=[_res((M, K)), w_spec],
                     out_specs=_colblk(M, tn), out_shape=_sds((M, N), out_dtype))(a, w)
    tn, rc = _pick(N), _pick(M)
    cast = a.dtype != MXU_DT

    def body(*refs):
        a_ref, w_ref, o_ref = refs[0], refs[1], refs[2 + has_add + has_after]
        if cast:
            @pl.when(pl.program_id(0) == 0)
            def _():
                for r0 in range(0, M, rc):
                    refs[-1][r0:r0 + rc, :] = a_ref[r0:r0 + rc, :].astype(MXU_DT)
            a_ref = refs[-1]
        wv = w_ref[...]
        for r0 in range(0, M, rc):
            acc = _dot(a_ref[r0:r0 + rc, :], wv, dims)
            if has_add:
                acc = acc + refs[2][r0:r0 + rc, :]
            o_ref[r0:r0 + rc, :] = acc.astype(out_dtype)

    w_spec = pl.BlockSpec((tn, K), lambda j: (j, 0)) if nt else pl.BlockSpec((K, tn), lambda j: (0, j))
    in_specs = ([_res((M, K)), w_spec] + ([_colblk(M, tn)] if has_add else [])
                + ([_res(after.shape)] if has_after else []))
    args = (a, w) + ((add,) if has_add else ()) + ((after,) if has_after else ())
    return _call(body, name=name, grid=(N // tn,), in_specs=in_specs, out_specs=_colblk(M, tn),
                 out_shape=_sds((M, N), out_dtype),
                 scratch_shapes=[pltpu.VMEM((M, K), MXU_DT)] if cast else [])(*args)


def _rms_fwd(x, g, *, name):
    M, D = x.shape
    tm = _pick(M, (512, 256, 128))

    def body(x_ref, g_ref, h_ref):
        xf = x_ref[...]
        r = lax.rsqrt(jnp.mean(xf * xf, axis=-1, keepdims=True) + RMS_EPS)
        h_ref[...] = ((xf * r) * g_ref[...]).astype(MXU_DT)

    return _call(body, name=name, grid=(M // tm,), in_specs=[_rowblk(tm, D), _res((1, D))],
                 out_specs=_rowblk(tm, D), out_shape=_sds((M, D), MXU_DT))(x, g.reshape(1, D))


def _rms_bwd(x, g, dh, dx_in, *, name):
    M, D = x.shape
    tm = _pick(M, (512, 256, 128))
    with_dx = dx_in is not None

    def body(*refs):
        if with_dx:
            x_ref, g_ref, dh_ref, dxin_ref, dx_ref, dg_ref = refs
        else:
            x_ref, g_ref, dh_ref, dg_ref = refs
        i = pl.program_id(0)
        xf = x_ref[...]
        r = lax.rsqrt(jnp.mean(xf * xf, axis=-1, keepdims=True) + RMS_EPS)
        y = xf * r
        dh_v = dh_ref[...]
        if with_dx:
            dy = dh_v * g_ref[...]
            dx_ref[...] = dxin_ref[...] + r * (dy - y * jnp.mean(dy * y, axis=-1, keepdims=True))
        part = jnp.sum(dh_v * y, axis=0, keepdims=True)

        @pl.when(i == 0)
        def _():
            dg_ref[...] = part

        @pl.when(i > 0)
        def _():
            dg_ref[...] += part

    ins = [x, g.reshape(1, D), dh] + ([dx_in] if with_dx else [])
    in_specs = [_rowblk(tm, D), _res((1, D)), _rowblk(tm, D)] + ([_rowblk(tm, D)] if with_dx else [])
    if with_dx:
        out_specs, out_shape = [_rowblk(tm, D), _res((1, D))], [_sds((M, D), F32), _sds((1, D), F32)]
    else:
        out_specs, out_shape = [_res((1, D))], [_sds((1, D), F32)]
    out = _call(body, name=name, grid=(M // tm,), in_specs=in_specs, out_specs=out_specs, out_shape=out_shape)(*ins)
    return out if with_dx else (None, out[0])


def _loss_bwd(x, g, t, *, name):
    M, D = x.shape
    tm = _pick(M, (512, 256, 128))

    def body(x_ref, g_ref, t_ref, dx_ref, dg_ref, ls_ref):
        i = pl.program_id(0)
        xf = x_ref[...]
        r = lax.rsqrt(jnp.mean(xf * xf, axis=-1, keepdims=True) + RMS_EPS)
        xr = xf * r
        e = xr * g_ref[...] - t_ref[...]
        dout = e * (1.0 / D)
        dy = dout * g_ref[...]
        dx_ref[...] = r * (dy - xr * jnp.mean(dy * xr, axis=-1, keepdims=True))
        part = jnp.sum(dout * xr, axis=0, keepdims=True)
        lpart = jnp.zeros(ls_ref.shape, F32) + (0.5 / D) * jnp.sum(e * e, keepdims=True)

        @pl.when(i == 0)
        def _():
            dg_ref[...] = part
            ls_ref[...] = lpart

        @pl.when(i > 0)
        def _():
            dg_ref[...] += part
            ls_ref[...] += lpart

    dx, dg, ls = _call(body, name=name, grid=(M // tm,),
                       in_specs=[_rowblk(tm, D), _res((1, D)), _rowblk(tm, D)],
                       out_specs=[_rowblk(tm, D), _res((1, D)), _res((8, 128))],
                       out_shape=[_sds((M, D), F32), _sds((1, D), F32), _sds((8, 128), F32)])(x, g.reshape(1, D), t)
    return ls[0, 0], dx, dg


def _shift_rows(ext, off, rows):
    if off % 8 == 0:
        return ext[off:off + rows, :]
    return pltpu.roll(ext, ext.shape[0] - off, 0)[0:rows, :]


def _ext(pad_ref, c, rows, halo):
    return pad_ref[pl.ds(pl.multiple_of(c * rows, rows), rows + halo), :]


def _conv_chunk(pad_ref, c, rows, halo, w_ref, taps):
    ext = _ext(pad_ref, c, rows, halo)
    acc = None
    for k in range(taps):
        term = w_ref[k:k + 1, :] * _shift_rows(ext, halo - (taps - 1) + k, rows)
        acc = term if acc is None else acc + term
    return acc


def _conv_t_chunk(pad_ref, c, rows, halo, w_ref, taps):
    ext = _ext(pad_ref, c, rows, halo)
    acc = None
    for k in range(taps):
        term = w_ref[k:k + 1, :] * _shift_rows(ext, taps - 1 - k, rows)
        acc = term if acc is None else acc + term
    return acc


def _conv_wgrad_chunk(pad_ref, c, rows, halo, dy, dw_ref, taps):
    ext = _ext(pad_ref, c, rows, halo)
    for k in range(taps):
        dw_ref[k:k + 1, :] += jnp.sum(dy * _shift_rows(ext, halo - (taps - 1) + k, rows), axis=0, keepdims=True)


A_HALO, F_HALO = 32, 8


def _glu_conv_fwd(p, b_glu, w_dw, b_dw, *, name):
    S = p.shape[0]
    taps, C = w_dw.shape
    tc, rows = 128, _pick(S, (256, 128))
    nb, nch = C // tc, S // rows

    def body(a_ref, g_ref, ba_ref, bg_ref, w_ref, bd_ref, o_ref, pad_ref):
        pad_ref[0:A_HALO, :] = jnp.zeros((A_HALO, tc), F32)

        def fill(c, _):
            r = pl.ds(pl.multiple_of(c * rows, rows), rows)
            v1 = (a_ref[r, :] + ba_ref[...]) * _sig(g_ref[r, :] + bg_ref[...])
            pad_ref[pl.ds(pl.multiple_of(A_HALO + c * rows, 8), rows), :] = v1
            return 0

        lax.fori_loop(0, nch, fill, 0)

        def conv(c, _):
            o_ref[pl.ds(pl.multiple_of(c * rows, rows), rows), :] = (
                _conv_chunk(pad_ref, c, rows, A_HALO, w_ref, taps) + bd_ref[...])
            return 0

        lax.fori_loop(0, nch, conv, 0)

    b2 = b_glu.reshape(1, 2 * C)
    return _call(body, name=name, grid=(nb,),
                 in_specs=[_colblk(S, tc), _colblk(S, tc, nb), _colblk(1, tc), _colblk(1, tc, nb),
                           _colblk(taps, tc), _colblk(1, tc)],
                 out_specs=_colblk(S, tc), out_shape=_sds((S, C), F32),
                 scratch_shapes=[pltpu.VMEM((S + A_HALO, tc), F32)])(p, p, b2, b2, w_dw, b_dw.reshape(1, C))


def _glu_conv_bwd(p, b_glu, w_dw, dv2, *, name):
    S = p.shape[0]
    taps, C = w_dw.shape
    tc, rows = 128, _pick(S, (256, 128))
    nb, nch = C // tc, S // rows

    def body(a_ref, g_ref, ba_ref, bg_ref, w_ref, dy_ref, da_ref, dgt_ref, dw_ref, dbd_ref, dba_ref, dbg_ref,
             padx_ref, pady_ref):
        padx_ref[0:A_HALO, :] = jnp.zeros((A_HALO, tc), F32)
        pady_ref[S:S + A_HALO, :] = jnp.zeros((A_HALO, tc), F32)
        dw_ref[...] = jnp.zeros((taps, tc), F32)

        def fill(c, _):
            r = pl.ds(pl.multiple_of(c * rows, rows), rows)
            v1 = (a_ref[r, :] + ba_ref[...]) * _sig(g_ref[r, :] + bg_ref[...])
            padx_ref[pl.ds(pl.multiple_of(A_HALO + c * rows, 8), rows), :] = v1
            pady_ref[r, :] = dy_ref[r, :]
            return 0

        lax.fori_loop(0, nch, fill, 0)

        def back(c, carry):
            sd, sa, sg = carry
            r = pl.ds(pl.multiple_of(c * rows, rows), rows)
            dy = dy_ref[r, :]
            _conv_wgrad_chunk(padx_ref, c, rows, A_HALO, dy, dw_ref, taps)
            dv1 = _conv_t_chunk(pady_ref, c, rows, A_HALO, w_ref, taps)
            a = a_ref[r, :] + ba_ref[...]
            s = _sig(g_ref[r, :] + bg_ref[...])
            da = dv1 * s
            dgt = dv1 * a * s * (1.0 - s)
            da_ref[r, :] = da
            dgt_ref[r, :] = dgt
            return (sd + jnp.sum(dy, axis=0, keepdims=True), sa + jnp.sum(da, axis=0, keepdims=True),
                    sg + jnp.sum(dgt, axis=0, keepdims=True))

        z = jnp.zeros((1, tc), F32)
        sd, sa, sg = lax.fori_loop(0, nch, back, (z, z, z))
        dbd_ref[...] = sd
        dba_ref[...] = sa
        dbg_ref[...] = sg

    b2 = b_glu.reshape(1, 2 * C)
    return _call(body, name=name, grid=(nb,),
                 in_specs=[_colblk(S, tc), _colblk(S, tc, nb), _colblk(1, tc), _colblk(1, tc, nb),
                           _colblk(taps, tc), _colblk(S, tc)],
                 out_specs=[_colblk(S, tc), _colblk(S, tc), _colblk(taps, tc), _colblk(1, tc), _colblk(1, tc),
                            _colblk(1, tc)],
                 out_shape=[_sds((S, C), F32), _sds((S, C), F32), _sds((taps, C), F32), _sds((1, C), F32),
                            _sds((1, C), F32), _sds((1, C), F32)],
                 scratch_shapes=[pltpu.VMEM((S + A_HALO, tc), F32), pltpu.VMEM((S + A_HALO, tc), F32)])(
                     p, p, b2, b2, w_dw, dv2)


def _ln_silu_cat_fwd(v2, ln_g, ln_b, memo, *, name):
    S, C = v2.shape
    Mw = memo.shape[1]
    tm = _pick(S, (256, 128))

    def body(v_ref, g_ref, b_ref, m_ref, o_ref):
        v = v_ref[...]
        mu = jnp.mean(v, axis=-1, keepdims=True)
        d = v - mu
        y = d * lax.rsqrt(jnp.mean(d * d, axis=-1, keepdims=True) + LN_EPS) * g_ref[...] + b_ref[...]
        o_ref[:, 0:C] = (y * _sig(y)).astype(MXU_DT)
        o_ref[:, C:C + Mw] = m_ref[...].astype(MXU_DT)

    return _call(body, name=name, grid=(S // tm,),
                 in_specs=[_rowblk(tm, C), _res((1, C)), _res((1, C)), _rowblk(tm, Mw)],
                 out_specs=_rowblk(tm, C + Mw), out_shape=_sds((S, C + Mw), MXU_DT))(
                     v2, ln_g.reshape(1, C), ln_b.reshape(1, C), memo)


def _ln_silu_bwd(v2, ln_g, ln_b, dcat, *, name):
    S, C = v2.shape
    tm = _pick(S, (256, 128))

    def body(v_ref, g_ref, b_ref, dm_ref, dv_ref, dg_ref, db_ref):
        i = pl.program_id(0)
        v = v_ref[...]
        mu = jnp.mean(v, axis=-1, keepdims=True)
        d = v - mu
        rstd = lax.rsqrt(jnp.mean(d * d, axis=-1, keepdims=True) + LN_EPS)
        xh = d * rstd
        y = xh * g_ref[...] + b_ref[...]
        s = _sig(y)
        dyv = dm_ref[...] * (s * (1.0 + y * (1.0 - s)))
        dxh = dyv * g_ref[...]
        dv_ref[...] = rstd * (dxh - jnp.mean(dxh, axis=-1, keepdims=True)
                              - xh * jnp.mean(dxh * xh, axis=-1, keepdims=True))
        pg = jnp.sum(dyv * xh, axis=0, keepdims=True)
        pb = jnp.sum(dyv, axis=0, keepdims=True)

        @pl.when(i == 0)
        def _():
            dg_ref[...] = pg
            db_ref[...] = pb

        @pl.when(i > 0)
        def _():
            dg_ref[...] += pg
            db_ref[...] += pb

    return _call(body, name=name, grid=(S // tm,),
                 in_specs=[_rowblk(tm, C), _res((1, C)), _res((1, C)), _rowblk(tm, C)],
                 out_specs=[_rowblk(tm, C), _res((1, C)), _res((1, C))],
                 out_shape=[_sds((S, C), F32), _sds((1, C), F32), _sds((1, C), F32)])(
                     v2, ln_g.reshape(1, C), ln_b.reshape(1, C), dcat)


def _ffn_act_fwd(ug, uv, w_dw, b_dw, *, name):
    S, Fw = ug.shape
    taps = w_dw.shape[0]
    tc, rows = _pick(Fw, (256, 128)), _pick(S, (256, 128))
    nb, nch = Fw // tc, S // rows

    def body(ug_ref, uv_ref, wg_ref, wv_ref, bg_ref, bv_ref, o_ref, pg_ref, pv_ref):
        pg_ref[0:F_HALO, :] = jnp.zeros((F_HALO, tc), F32)
        pv_ref[0:F_HALO, :] = jnp.zeros((F_HALO, tc), F32)
        pg_ref[F_HALO:F_HALO + S, :] = ug_ref[...]
        pv_ref[F_HALO:F_HALO + S, :] = uv_ref[...]

        def act(c, _):
            gc = _conv_chunk(pg_ref, c, rows, F_HALO, wg_ref, taps) + bg_ref[...]
            vc = _conv_chunk(pv_ref, c, rows, F_HALO, wv_ref, taps) + bv_ref[...]
            o_ref[pl.ds(pl.multiple_of(c * rows, rows), rows), :] = (gc * _sig(gc) * vc).astype(MXU_DT)
            return 0

        lax.fori_loop(0, nch, act, 0)

    b2 = b_dw.reshape(1, 2 * Fw)
    return _call(body, name=name, grid=(nb,),
                 in_specs=[_colblk(S, tc), _colblk(S, tc), _colblk(taps, tc), _colblk(taps, tc, nb),
                           _colblk(1, tc), _colblk(1, tc, nb)],
                 out_specs=_colblk(S, tc), out_shape=_sds((S, Fw), MXU_DT),
                 scratch_shapes=[pltpu.VMEM((S + F_HALO, tc), F32), pltpu.VMEM((S + F_HALO, tc), F32)])(
                     ug, uv, w_dw, w_dw, b2, b2)


def _ffn_act_bwd(ug, uv, dact, w_dw, b_dw, *, name):
    S, Fw = ug.shape
    taps = w_dw.shape[0]
    tc, rows = _pick(Fw, (256, 128)), _pick(S, (256, 128))
    nb, nch = Fw // tc, S // rows

    def body(ug_ref, uv_ref, da_ref, wg_ref, wv_ref, bg_ref, bv_ref, dug_ref, duv_ref, dwg_ref, dwv_ref,
             dbg_ref, dbv_ref, pg_ref, pv_ref, qg_ref, qv_ref):
        pg_ref[0:F_HALO, :] = jnp.zeros((F_HALO, tc), F32)
        pv_ref[0:F_HALO, :] = jnp.zeros((F_HALO, tc), F32)
        qg_ref[S:S + F_HALO, :] = jnp.zeros((F_HALO, tc), F32)
        qv_ref[S:S + F_HALO, :] = jnp.zeros((F_HALO, tc), F32)
        pg_ref[F_HALO:F_HALO + S, :] = ug_ref[...]
        pv_ref[F_HALO:F_HALO + S, :] = uv_ref[...]
        dwg_ref[...] = jnp.zeros((taps, tc), F32)
        dwv_ref[...] = jnp.zeros((taps, tc), F32)

        def grads(c, carry):
            sg, sv = carry
            r = pl.ds(pl.multiple_of(c * rows, rows), rows)
            gc = _conv_chunk(pg_ref, c, rows, F_HALO, wg_ref, taps) + bg_ref[...]
            vc = _conv_chunk(pv_ref, c, rows, F_HALO, wv_ref, taps) + bv_ref[...]
            s = _sig(gc)
            da = da_ref[r, :]
            dgc = da * vc * (s * (1.0 + gc * (1.0 - s)))
            dvc = da * (gc * s)
            qg_ref[r, :] = dgc
            qv_ref[r, :] = dvc
            _conv_wgrad_chunk(pg_ref, c, rows, F_HALO, dgc, dwg_ref, taps)
            _conv_wgrad_chunk(pv_ref, c, rows, F_HALO, dvc, dwv_ref, taps)
            return sg + jnp.sum(dgc, axis=0, keepdims=True), sv + jnp.sum(dvc, axis=0, keepdims=True)

        z = jnp.zeros((1, tc), F32)
        sg, sv = lax.fori_loop(0, nch, grads, (z, z))
        dbg_ref[...] = sg
        dbv_ref[...] = sv

        def back(c, _):
            r = pl.ds(pl.multiple_of(c * rows, rows), rows)
            dug_ref[r, :] = _conv_t_chunk(qg_ref, c, rows, F_HALO, wg_ref, taps).astype(MXU_DT)
            duv_ref[r, :] = _conv_t_chunk(qv_ref, c, rows, F_HALO, wv_ref, taps).astype(MXU_DT)
            return 0

        lax.fori_loop(0, nch, back, 0)

    b2 = b_dw.reshape(1, 2 * Fw)
    pad = pltpu.VMEM((S + F_HALO, tc), F32)
    dug, duv, dwg, dwv, dbg, dbv = _call(
        body, name=name, grid=(nb,),
        in_specs=[_colblk(S, tc), _colblk(S, tc), _colblk(S, tc), _colblk(taps, tc), _colblk(taps, tc, nb),
                  _colblk(1, tc), _colblk(1, tc, nb)],
        out_specs=[_colblk(S, tc), _colblk(S, tc), _colblk(taps, tc), _colblk(taps, tc), _colblk(1, tc),
                   _colblk(1, tc)],
        out_shape=[_sds((S, Fw), MXU_DT), _sds((S, Fw), MXU_DT), _sds((taps, Fw), F32), _sds((taps, Fw), F32),
                   _sds((1, Fw), F32), _sds((1, Fw), F32)],
        scratch_shapes=[pad, pad, pad, pad])(ug, uv, dact, w_dw, w_dw, b2, b2)
    return dug, duv, jnp.concatenate([dwg, dwv], axis=1), jnp.concatenate([dbg, dbv], axis=1)


def _head_mask(h, width):
    lane = lax.broadcasted_iota(jnp.int32, (1, width), 1)
    return (lane >= h * HEAD_DIM) & (lane < (h + 1) * HEAD_DIM)


def _mem_attn_fwd(p, qblk, mkv, l, *, Mw, name):
    S, ML = p.shape[0], mkv.shape[0]
    tm, nh, scale = _pick(S, (256, 128)), Mw // HEAD_DIM, HEAD_DIM ** -0.5

    def body(q_ref, k_ref, v_ref, o_ref):
        q, kv, vv = q_ref[...], k_ref[...], v_ref[...]
        out = jnp.zeros((tm, Mw), F32)
        for h in range(nh):
            mk = _head_mask(h, Mw)
            s = _dot(jnp.where(mk, q, 0.0).astype(MXU_DT), kv, NT) * scale
            e = jnp.exp(s - jnp.max(s, axis=-1, keepdims=True))
            pr = e / jnp.sum(e, axis=-1, keepdims=True)
            out = out + _dot(pr.astype(MXU_DT), jnp.where(mk, vv, jnp.zeros_like(vv)))
        o_ref[...] = out

    return _call(body, name=name, grid=(S // tm,),
                 in_specs=[_rowblk(tm, Mw, qblk), pl.BlockSpec((ML, Mw), lambda i: (0, 2 * l)),
                           pl.BlockSpec((ML, Mw), lambda i: (0, 2 * l + 1))],
                 out_specs=_rowblk(tm, Mw), out_shape=_sds((S, Mw), F32))(p, mkv, mkv)


def _mem_attn_bwd(p, qblk, mkv, l, dcat, doblk, *, Mw, name):
    S, ML = p.shape[0], mkv.shape[0]
    tm, nh, scale = _pick(S, (256, 128)), Mw // HEAD_DIM, HEAD_DIM ** -0.5

    def body(q_ref, k_ref, v_ref, do_ref, dq_ref, dk_ref, dv_ref):
        i = pl.program_id(0)

        @pl.when(i == 0)
        def _():
            dk_ref[...] = jnp.zeros((ML, Mw), F32)
            dv_ref[...] = jnp.zeros((ML, Mw), F32)

        q, kv, vv, do = q_ref[...], k_ref[...], v_ref[...], do_ref[...]
        dq = jnp.zeros((tm, Mw), F32)
        for h in range(nh):
            mk = _head_mask(h, Mw)
            qh = jnp.where(mk, q, 0.0).astype(MXU_DT)
            s = _dot(qh, kv, NT) * scale
            e = jnp.exp(s - jnp.max(s, axis=-1, keepdims=True))
            pr = e / jnp.sum(e, axis=-1, keepdims=True)
            doh = jnp.where(mk, do, 0.0).astype(MXU_DT)
            dv_ref[...] += _dot_tn(pr, doh)
            dp = _dot(doh, vv, NT)
            ds = pr * (dp - jnp.sum(dp * pr, axis=-1, keepdims=True))
            dq = dq + _dot(ds.astype(MXU_DT), jnp.where(mk, kv, jnp.zeros_like(kv))) * scale
            dk_ref[...] += _dot_tn(ds, qh) * scale
        dq_ref[...] = dq

    return _call(body, name=name, grid=(S // tm,),
                 in_specs=[_rowblk(tm, Mw, qblk), pl.BlockSpec((ML, Mw), lambda i: (0, 2 * l)),
                           pl.BlockSpec((ML, Mw), lambda i: (0, 2 * l + 1)), _rowblk(tm, Mw, doblk)],
                 out_specs=[_rowblk(tm, Mw), _res((ML, Mw)), _res((ML, Mw))],
                 out_shape=[_sds((S, Mw), F32), _sds((ML, Mw), F32), _sds((ML, Mw), F32)])(p, mkv, mkv, dcat)


FOX_GROUP = 3


def _foxt_specs(S, dh, tq):
    nb, G = S // tq, FOX_GROUP
    rows = pl.BlockSpec((G, tq, dh), lambda h, i: (h, i, 0))
    seq = pl.BlockSpec((G, S, dh), lambda h, i: (h, 0, 0))
    seq_t = pl.BlockSpec((G, nb, dh, tq), lambda h, i: (h, 0, 0, 0))
    blk_t = pl.BlockSpec((G, dh, tq), lambda h, i: (h, 0, i))
    col = pl.BlockSpec((G, S, 1), lambda h, i: (h, 0, 0))
    row_all = pl.BlockSpec((G, nb, 1, tq), lambda h, i: (h, 0, 0, 0))
    row = pl.BlockSpec((G, 1, 1, tq), lambda h, i: (h, i, 0, 0))
    return rows, seq, seq_t, blk_t, col, row_all, row


def _foxt_logits(kv, qv, cq_row, ck_col, scale, diag):
    s = _dot(kv, qv, NT) * scale + cq_row - ck_col
    if not diag:
        return s
    keys = lax.broadcasted_iota(jnp.int32, s.shape, 0)
    queries = lax.broadcasted_iota(jnp.int32, s.shape, 1)
    return jnp.where(keys <= queries, s, NEG)


def _foxt_fwd(q, k, vt, cq_row, ck_col, *, tq, name):
    H, S, dh = q.shape
    nb, scale, G = S // tq, dh ** -0.5, FOX_GROUP
    assert H % G == 0
    rows, seq, seq_t, blk_t, col, row_all, row = _foxt_specs(S, dh, tq)

    def body(q_ref, k_ref, vt_ref, cq_ref, ck_ref, o_ref, lse_ref):
        i = pl.program_id(1)
        qv, cqv = [q_ref[e] for e in range(G)], [cq_ref[e, i] for e in range(G)]

        def kblock(j, carry, diag):
            r = pl.ds(pl.multiple_of(j * tq, tq), tq)
            out = []
            for e in range(G):
                m, l, acc = carry[e]
                s = _foxt_logits(k_ref[e, r, :], qv[e], cqv[e], ck_ref[e, r, :], scale, diag)
                m2 = jnp.maximum(m, jnp.max(s, axis=0, keepdims=True))
                pr = jnp.exp(s - m2)
                al = jnp.exp(m - m2)
                out.append((m2, al * l + jnp.sum(pr, axis=0, keepdims=True),
                            al * acc + _dot(vt_ref[e, j], pr.astype(MXU_DT))))
            return tuple(out)

        init = tuple((jnp.full((1, tq), NEG, F32), jnp.zeros((1, tq), F32), jnp.zeros((dh, tq), F32))
                     for _ in range(G))
        carry = lax.fori_loop(0, i, lambda j, c: kblock(j, c, False), init)
        for e, (m, l, acc) in enumerate(kblock(i, carry, True)):
            o_ref[e] = acc / l
            lse_ref[e, 0] = m + jnp.log(l)

    return _call(body, name=name, grid=(H // G, nb), in_specs=[rows, seq, seq_t, row_all, col],
                 out_specs=[blk_t, row],
                 out_shape=[_sds((H, dh, S), F32), _sds((H, nb, 1, tq), F32)])(q, k, vt, cq_row, ck_col)


def _foxt_bwd(q, k, kt, v, cq_row, ck_col, ot, lse, dot_, *, tq, name):
    H, S, dh = q.shape
    nb, scale, G = S // tq, dh ** -0.5, FOX_GROUP
    rows, seq, seq_t, blk_t, col, row_all, row = _foxt_specs(S, dh, tq)

    def body(q_ref, k_ref, kt_ref, v_ref, cq_ref, ck_ref, ot_ref, lse_ref, dot_ref, dq_ref, dk_ref, dv_ref,
             dcq_ref, dck_ref):
        i = pl.program_id(1)

        @pl.when(i == 0)
        def _():
            dk_ref[...] = jnp.zeros((G, S, dh), F32)
            dv_ref[...] = jnp.zeros((G, S, dh), F32)
            dck_ref[...] = jnp.zeros((G, S, 1), F32)

        qv, cqv, lsev = [q_ref[e] for e in range(G)], [cq_ref[e, i] for e in range(G)], [lse_ref[e, 0] for e in range(G)]
        dob = [dot_ref[e].astype(MXU_DT) for e in range(G)]
        delta = [jnp.sum(dob[e].astype(F32) * ot_ref[e], axis=0, keepdims=True) for e in range(G)]

        def kblock(j, carry, diag):
            r = pl.ds(pl.multiple_of(j * tq, tq), tq)
            out = []
            for e in range(G):
                dq, rs = carry[e]
                pr = jnp.exp(_foxt_logits(k_ref[e, r, :], qv[e], cqv[e], ck_ref[e, r, :], scale, diag) - lsev[e])
                ds = pr * (_dot(v_ref[e, r, :], dob[e]) - delta[e])
                dsb = ds.astype(MXU_DT)
                dk_ref[e, r, :] += _dot(dsb, qv[e]) * scale
                dv_ref[e, r, :] += _dot(pr.astype(MXU_DT), dob[e], NT)
                dck_ref[e, r, :] += -jnp.sum(ds, axis=1, keepdims=True)
                out.append((dq + _dot(kt_ref[e, j], dsb), rs + jnp.sum(ds, axis=0, keepdims=True)))
            return tuple(out)

        init = tuple((jnp.zeros((dh, tq), F32), jnp.zeros((1, tq), F32)) for _ in range(G))
        carry = lax.fori_loop(0, i, lambda j, c: kblock(j, c, False), init)
        for e, (dq, rs) in enumerate(kblock(i, carry, True)):
            dq_ref[e] = dq * scale
            dcq_ref[e, 0] = rs

    return _call(body, name=name, grid=(H // G, nb),
                 in_specs=[rows, seq, seq_t, seq, row_all, col, blk_t, row, blk_t],
                 out_specs=[blk_t, seq, seq, row, col],
                 out_shape=[_sds((H, dh, S), F32), _sds((H, S, dh), F32), _sds((H, S, dh), F32),
                            _sds((H, nb, 1, tq), F32), _sds((H, S, 1), F32)])(
                                q, k, kt, v, cq_row, ck_col, ot, lse, dot_)


def _tri(n, lower):
    r = lax.broadcasted_iota(jnp.int32, (n, n), 0)
    c = lax.broadcasted_iota(jnp.int32, (n, n), 1)
    return ((c <= r) if lower else (c >= r)).astype(F32)


def _fgate_fwd(fr, bf, *, name):
    S, W = fr.shape
    B = _pick(S, (256, 128))

    def body(f_ref, b_ref, cum_ref):
        L = _tri(B, True)
        carry = jnp.zeros((1, W), F32)
        for blk in range(S // B):
            z = f_ref[blk * B:(blk + 1) * B, :] + b_ref[...]
            ls = jnp.minimum(z, 0.0) - jnp.log(1.0 + jnp.exp(-jnp.abs(z)))
            cum_ref[blk * B:(blk + 1) * B, :] = jnp.dot(L, ls, precision=lax.Precision.HIGHEST,
                                                        preferred_element_type=F32) + carry
            carry = carry + jnp.sum(ls, axis=0, keepdims=True)

    return _call(body, name=name, out_shape=_sds((S, W), F32))(fr, bf)


def _fgate_bwd(fr, bf, dcum, *, name):
    S, W = fr.shape
    B = _pick(S, (256, 128))

    def body(f_ref, b_ref, dc_ref, df_ref, db_ref):
        U = _tri(B, False)
        carry = jnp.zeros((1, W), F32)
        dbs = jnp.zeros((1, W), F32)
        for blk in reversed(range(S // B)):
            dc = dc_ref[blk * B:(blk + 1) * B, :]
            dls = jnp.dot(U, dc, precision=lax.Precision.HIGHEST, preferred_element_type=F32) + carry
            carry = carry + jnp.sum(dc, axis=0, keepdims=True)
            z = f_ref[blk * B:(blk + 1) * B, :] + b_ref[...]
            df = dls * (1.0 / (1.0 + jnp.exp(z)))
            df_ref[blk * B:(blk + 1) * B, :] = df
            dbs = dbs + jnp.sum(df, axis=0, keepdims=True)
        db_ref[...] = dbs

    return _call(body, name=name, out_shape=[_sds((S, W), F32), _sds((1, W), F32)])(fr, bf, dcum)


def _flip(v, bit):
    return 1 - v if bit else v


HBM_SPEC = pl.BlockSpec(memory_space=pltpu.HBM)
SEM_SPEC = pl.BlockSpec(memory_space=pltpu.SEMAPHORE)


def _xchg_copies(src, land, sems, scatter):
    n = len(src)
    send, recv, loc = sems[:7 * n], sems[7 * n:14 * n], sems[14 * n:15 * n]
    x, y, c = lax.axis_index("x"), lax.axis_index("y"), lax.axis_index("c")
    me = 4 * x + 2 * y + c

    def peer(m):
        return _flip(x, m & 4), _flip(y, m & 2), _flip(c, m & 1)

    def copy(i, m):
        px, py, pc = peer(m)
        return pltpu.make_async_remote_copy(
            src_ref=src[i].at[4 * px + 2 * py + pc] if scatter[i] else src[i], dst_ref=land[i].at[me],
            send_sem=send[7 * i + m - 1], recv_sem=recv[7 * i + m - 1], device_id=(px, py, pc), device_id_type=MESH)

    def arrival(i, m):
        px, py, pc = peer(m)
        slot = land[i].at[4 * px + 2 * py + pc]
        return pltpu.make_async_remote_copy(src_ref=slot, dst_ref=slot, send_sem=send[7 * i + m - 1],
                                            recv_sem=recv[7 * i + m - 1], device_id=(px, py, pc), device_id_type=MESH)

    def own(i):
        return pltpu.make_async_copy(src[i].at[me] if scatter[i] else src[i], land[i].at[me], loc[i])

    return copy, arrival, own


def _xchg_start(srcs, scatter, *, name):
    n = len(srcs)
    lands = [_sds((N_DEV,) + s.shape[-2:], s.dtype) for s in srcs]

    ns = 15 * n

    def body(*refs):
        src, land, sems, token = refs[:n], refs[n:2 * n], refs[2 * n:2 * n + ns], refs[-1]
        copy, _, own = _xchg_copies(src, land, sems, scatter)
        for i in range(n):
            own(i).start()
            for m in range(1, N_DEV):
                copy(i, m).start()
        token[...] = jnp.zeros(token.shape, F32)

    thru = [pltpu.HBM(s.shape, s.dtype) for s in srcs] + [pltpu.HBM(s.shape, s.dtype) for s in lands]
    out = pl.pallas_call(
        body, name=name,
        out_shape=(*[pltpu.SemaphoreType.DMA(())] * ns, *thru, _sds((8, 128), F32)),
        in_specs=[HBM_SPEC] * (2 * n),
        out_specs=(*[SEM_SPEC] * ns, *[HBM_SPEC] * (2 * n), pl.BlockSpec(memory_space=pltpu.VMEM)),
        input_output_aliases={i: ns + i for i in range(2 * n)},
        compiler_params=pltpu.CompilerParams(has_side_effects=pltpu.SideEffectType.DATAFLOW_SIDE_EFFECTING),
    )(*[pltpu.with_memory_space_constraint(s, pltpu.HBM) for s in srcs],
      *[pltpu.with_memory_space_constraint(lax.empty(s.shape, s.dtype), pltpu.HBM) for s in lands])
    bufs = list(out[ns:ns + 2 * n])
    return (list(out[:ns]), bufs[:n], bufs[n:]), out[-1]


def _xchg_wait(handle, after, scatter, *, name):
    sems, srcs, lands = handle
    n = len(srcs)
    ns = 15 * n

    def body(*refs):
        src, land = refs[:n], refs[n:2 * n]
        copy, arrival, own = _xchg_copies(src, land, refs[2 * n:2 * n + ns], scatter)
        for i in range(n):
            own(i).wait()
            for m in range(1, N_DEV):
                copy(i, m).wait_send()
                arrival(i, m).wait_recv()

    out = pl.pallas_call(
        body, name=name,
        out_shape=tuple(pltpu.HBM(s.shape, s.dtype) for s in srcs + lands),
        in_specs=[HBM_SPEC] * (2 * n) + [SEM_SPEC] * ns + [ANY],
        out_specs=tuple([HBM_SPEC] * (2 * n)),
        input_output_aliases={i: i for i in range(2 * n)},
        compiler_params=pltpu.CompilerParams(has_side_effects=pltpu.SideEffectType.DATAFLOW_SIDE_EFFECTING),
    )(*srcs, *lands, *sems, after)
    return list(out[n:])


def _reduce_adam_body(r_ref, w_ref, m_ref, v_ref, g_ref, d_ref, m2_ref, v2_ref):
    g = r_ref[0].astype(F32)
    for s in range(1, N_DEV):
        g = g + r_ref[s].astype(F32)
    mm = ADAM_B1 * m_ref[...] + (1.0 - ADAM_B1) * g
    vv = ADAM_B2 * v_ref[...] + (1.0 - ADAM_B2) * (g * g)
    m_hat = mm / (1.0 - ADAM_B1 ** ADAM_STEP)
    v_hat = vv / (1.0 - ADAM_B2 ** ADAM_STEP)
    g_ref[...] = g
    d_ref[...] = -ADAM_LR * (m_hat / (jnp.sqrt(v_hat) + ADAM_EPS) + ADAM_WD * w_ref[...])
    m2_ref[...] = mm
    v2_ref[...] = vv


def _reduce_adam(recv, w, m, v, *, name):
    R, L = w.shape
    tr = _pick(R, (256, 128, 64, 32, 16, 8))

    def body(*refs):
        _reduce_adam_body(*refs)

    blk = _rowblk(tr, L)
    return _call(body, name=name, grid=(R // tr,),
                 in_specs=[pl.BlockSpec((N_DEV, tr, L), lambda i: (0, i, 0)), blk, blk, blk],
                 out_specs=[blk, blk, blk, blk], out_shape=[_sds((R, L), F32)] * 4)(recv, w, m, v)


def _reduce_adam_layer(recv, w, m, v, idx, prev, *, name, after=None):
    r, c = w.shape[-2:]
    tr = _pick(r, (256, 176, 128, 112, 64, 32, 16))
    if prev is None:
        prev = [lax.empty(w.shape, F32) for _ in range(4)]
    behind = [] if after is None else [after]

    def body(r_ref, w_ref, m_ref, v_ref, *rest):
        _reduce_adam_body(r_ref, w_ref, m_ref, v_ref, *rest[-4:])

    blk = pl.BlockSpec((None, tr, c), lambda i: (idx, i, 0))
    return pl.pallas_call(
        body, name=name, grid=(r // tr,),
        in_specs=[pl.BlockSpec((N_DEV, tr, c), lambda i: (0, i, 0)), blk, blk, blk] + [ANY] * (4 + len(behind)),
        out_specs=[blk] * 4, out_shape=[_sds(w.shape, F32)] * 4, input_output_aliases={4 + j: j for j in range(4)},
        compiler_params=pltpu.CompilerParams(vmem_limit_bytes=VMEM_LIMIT_V7X, dimension_semantics=("arbitrary",)),
    )(recv, w, m, v, *prev, *behind)


class _Pack:
    def __init__(self, shapes, row_mult):
        self.shapes, self.offs, rows = dict(shapes), {}, 0
        for name, shp in shapes:
            size = 1
            for d in shp:
                size *= d
            nr = -(-size // (16 * PACK_LANES)) * 16
            self.offs[name] = (rows, size, nr)
            rows += nr
        self.used = rows
        self.rows = -(-rows // row_mult) * row_mult

    def pack(self, arrays, dtype, lead=()):
        parts = []
        for name, (r0, size, nr) in self.offs.items():
            flat = arrays[name].astype(dtype).reshape(lead + (size,))
            flat = jnp.pad(flat, [(0, 0)] * len(lead) + [(0, nr * PACK_LANES - size)])
            parts.append(flat.reshape(lead + (nr, PACK_LANES)))
        if self.rows > self.used:
            parts.append(jnp.zeros(lead + (self.rows - self.used, PACK_LANES), dtype))
        return jnp.concatenate(parts, axis=len(lead))

    def unpack(self, buf, lead=()):
        out = {}
        for name, (r0, size, nr) in self.offs.items():
            flat = buf[..., r0:r0 + nr, :].reshape(lead + (nr * PACK_LANES,))
            out[name] = flat[..., :size].reshape(lead + tuple(self.shapes[name]))
        return out


def _to_full(g8, ax):
    t = jnp.moveaxis(g8, 0, ax)
    return t.reshape(t.shape[:ax] + (t.shape[ax] * t.shape[ax + 1],) + t.shape[ax + 2:])


def _to_shards(full, ax):
    shp = full.shape
    return jnp.moveaxis(full.reshape(shp[:ax] + (N_DEV, shp[ax] // N_DEV) + shp[ax + 1:]), ax, 0)


def _to_heads(a, H):
    S = a.shape[0]
    return a.reshape(S, H, HEAD_DIM).transpose(1, 0, 2)


def _from_heads(a):
    H, S, dh = a.shape
    return a.transpose(1, 0, 2).reshape(S, H * dh)


def _to_heads_t(a, H, tq):
    S = a.shape[0]
    return a.reshape(S // tq, tq, H, HEAD_DIM).transpose(2, 0, 3, 1)


def _from_heads_t(a):
    H, dh, S = a.shape
    return a.transpose(2, 0, 1).reshape(S, H * dh)


def kernel(x, mem, g_mix, w_in_a, b_glu, w_dw_a, b_dw_a, ln_g, ln_b, g_kv, w_kvf, b_f, w_in_b, g_mem, w_mem_kv, w_out, g_ffn, w_up, w_dw_f, b_dw_f, w_down, g_final, loss_target, m_g_mix, m_w_in_a, m_b_glu, m_w_dw_a, m_b_dw_a, m_ln_g, m_ln_b, m_g_kv, m_w_kvf, m_b_f, m_w_in_b, m_g_mem, m_w_mem_kv, m_w_out, m_g_ffn, m_w_up, m_w_dw_f, m_b_dw_f, m_w_down, m_g_final, v_g_mix, v_w_in_a, v_b_glu, v_w_dw_a, v_b_dw_a, v_ln_g, v_ln_b, v_g_kv, v_w_kvf, v_b_f, v_w_in_b, v_g_mem, v_w_mem_kv, v_w_out, v_g_ffn, v_w_up, v_w_dw_f, v_b_dw_f, v_w_down, v_g_final):
    given = dict(locals())
    W = {n: given[n] for n in WEIGHTS}
    x0, mem0, tgt = x[0], mem[0], loss_target[0]
    S, D = x0.shape
    depth, n_a = g_mix.shape[0], w_in_a.shape[0]
    C = w_dw_a.shape[2] * N_DEV
    Mw = D - C
    Fw = w_down.shape[1] * N_DEV
    H = b_f.shape[0]
    assert C == H * HEAD_DIM and (2 * C) % Mw == 0 and C % Mw == 0 and H <= GATE_LANES
    tq = _pick(S, (256, 128))
    nkv = 2 * C + GATE_LANES

    def mix_keys(l):
        keys = [("w_in_a", l) if l < n_a else ("w_in_b", l - n_a), ("w_mem_kv", l), ("w_out", l)]
        return keys + ([("w_kvf", 0)] if l == n_a else [])

    def ffn_keys(l):
        return [("w_up", l), ("w_down", l)]

    def key_ax(key):
        return big_ax[key[0]] - 1

    transposed = ("w_up", "w_in_a")

    def stacked(d, n, pre=""):
        a = d[pre + n]
        return a[None] if n == "w_kvf" else (jnp.swapaxes(a, 1, 2) if n in transposed else a)

    W3 = {n: stacked(W, n) for n, _ in BIG}
    big_ax = {n: (1 if n == "w_kvf" or n in transposed else ax) for n, ax in BIG}
    pk_small = _Pack([(n, W[n].shape) for n, _ in SMALL], 8)
    pk_rep = _Pack([(n, W[n].shape) for n in REP] + [("loss", (1,))], 8)

    ws32 = pk_small.pack(W, F32)
    gathers, toks = {}, []
    for l in range(depth):
        for tag, keys in (("mix", mix_keys(l)), ("up", [("w_up", l)]), ("down", [("w_down", l)])):
            srcs = [W3[n][i].astype(COMM_DT) for n, i in keys] + ([ws32] if (l, tag) == (0, "mix") else [])
            gathers[l, tag], t = _xchg_start(srcs, [False] * len(srcs), name=f"w_gather_start_{tag}{l}")
            toks.append(t)
    tok = sum(t[0, 0] for t in toks)
    g_mix, g_mem = g_mix + tok, g_mem + tok
    bf_pad = jnp.pad(b_f, (0, GATE_LANES - H)).reshape(1, GATE_LANES)

    def gathered(l, tag, keys, after):
        n = len(gathers[l, tag][1])
        lands = _xchg_wait(gathers[l, tag], after, [False] * n, name=f"w_gather_wait_{tag}{l}")
        return {k: _to_full(a, key_ax(k)).astype(MXU_DT) for k, a in zip(keys, lands)}, lands[len(keys):]

    mem_n = _rms_fwd(mem0, g_mem, name="mem_norm")
    sv = []
    xs = x0
    for l in range(depth):
        wl, extra = gathered(l, "mix", mix_keys(l), xs)
        if extra:
            gs = pk_small.unpack(extra[0], (N_DEV,))
            small = {n: _to_full(gs[n], ax) for n, ax in SMALL}
        t = dict(x_in=xs, w=wl)
        t["mkv"] = _mm(mem_n, wl["w_mem_kv", l], name=f"mem_kv{l}", out_dtype=MXU_DT)
        t["h"] = _rms_fwd(xs, g_mix[l], name=f"mix_norm{l}")
        if l < n_a:
            t["p"] = _mm(t["h"], wl["w_in_a", l], name=f"in_proj{l}", out_dtype=F32, nt=True)
            t["v2"] = _glu_conv_fwd(t["p"], small["b_glu"][l], small["w_dw_a"][l], small["b_dw_a"][l],
                                    name=f"glu_conv{l}")
            memo = _mem_attn_fwd(t["p"], 2 * C // Mw, t["mkv"], 0, Mw=Mw, name=f"mem_attn{l}")
            t["cat"] = _ln_silu_cat_fwd(t["v2"], small["ln_g"][l], small["ln_b"][l], memo, name=f"ln_silu{l}")
        else:
            if l == n_a:
                wkvf = jnp.pad(wl["w_kvf", 0], ((0, 0), (0, nkv - w_kvf.shape[1])))
                hk = _rms_fwd(xs, g_kv, name="kv_norm")
                kvf = _mm(hk, wkvf, name="kv_proj", out_dtype=F32)
                kb, vb = kvf[:, :C].astype(MXU_DT), kvf[:, C:2 * C].astype(MXU_DT)
                k_h, v_h = _to_heads(kb, H), _to_heads(vb, H)
                kt_h, vt_h = _to_heads_t(kb, H, tq), _to_heads_t(vb, H, tq)
                fr = kvf[:, 2 * C:]
                cum = _fgate_fwd(fr, bf_pad, name="fgate")
                cum_t = cum[:, :H].T
                cq_row, ck_col = cum_t.reshape(H, S // tq, 1, tq), cum_t.reshape(H, S, 1)
            t["p"] = _mm(t["h"], wl["w_in_b", l - n_a], name=f"in_proj{l}", out_dtype=F32)
            t["q_h"] = _to_heads(t["p"][:, :C], H).astype(MXU_DT)
            t["ot"], t["lse"] = _foxt_fwd(t["q_h"], k_h, vt_h, cq_row, ck_col, tq=tq, name=f"fox{l}")
            memo = _mem_attn_fwd(t["p"], C // Mw, t["mkv"], 0, Mw=Mw, name=f"mem_attn{l}")
            t["cat"] = jnp.concatenate([_from_heads_t(t["ot"]), memo], axis=1).astype(MXU_DT)
        t["x_mid"] = _mm(t["cat"], wl["w_out", l], name=f"out_proj{l}", out_dtype=F32, add=xs)
        wl.update(gathered(l, "up", [("w_up", l)], t["x_mid"])[0])
        t["h2"] = _rms_fwd(t["x_mid"], g_ffn[l], name=f"ffn_norm{l}")
        t["ug"] = _mm(t["h2"], wl["w_up", l], name=f"up_gate{l}", out_dtype=F32, nt=True, cols=(0, Fw))
        t["uv"] = _mm(t["h2"], wl["w_up", l], name=f"up_val{l}", out_dtype=F32, nt=True, cols=(Fw, Fw))
        t["act"] = _ffn_act_fwd(t["ug"], t["uv"], small["w_dw_f"][l], b_dw_f[l], name=f"ffn_act{l}")
        wl.update(gathered(l, "down", [("w_down", l)], t["act"])[0])
        xs = _mm(t["act"], wl["w_down", l], name=f"down_proj{l}", out_dtype=F32, add=t["x_mid"])
        sv.append(t)
    loss_dev, dx, dg_final = _loss_bwd(xs, g_final, tgt, name="loss_head")

    M1 = {n: given["m_" + n] for n in WEIGHTS}
    V1 = {n: given["v_" + n] for n in WEIGHTS}
    M3 = {n: stacked(given, n, "m_") for n, _ in BIG}
    V3 = {n: stacked(given, n, "v_") for n, _ in BIG}
    res, chain, pending, waited = {}, {}, [], []

    def start_grads(tag, l, keys, gl, extra=(), extra_scatter=()):
        srcs = [_to_shards(gl[k], key_ax(k)) for k in keys] + list(extra)
        scatter = [True] * len(keys) + list(extra_scatter)
        handle, tk = _xchg_start(srcs, scatter, name=f"g_xchg_start_{tag}{l}")
        pending.append((f"{tag}{l}", keys, handle, scatter))
        return tk

    def finish_grads(after):
        tag, keys, handle, scatter = pending.pop(0)
        lands = _xchg_wait(handle, after, scatter, name=f"g_xchg_wait_{tag}")
        waited.extend(zip(keys, lands))
        return lands[len(keys):]

    def update_waited(after=None):
        for (n, i), recv in waited:
            chain[n] = _reduce_adam_layer(recv, W3[n], M3[n], V3[n], i, chain.get(n), name=f"adam_{n}{i}",
                                          after=after)
        waited.clear()

    G = {n: [None] * W[n].shape[0] for n in ("g_mix", "b_glu", "w_dw_a", "b_dw_a", "ln_g", "ln_b", "g_ffn",
                                              "w_dw_f", "b_dw_f")}
    dk_sum = dv_sum = dck_sum = dmem_n = None
    started = None
    for l in reversed(range(depth)):
        t = sv[l]
        wl, gl = t["w"], {}
        dact = _mm(dx, wl["w_down", l], name=f"d_act{l}", out_dtype=F32, nt=True, after=started)
        gl["w_down", l] = _mm_tn(t["act"], dx, name=f"dw_down{l}", out_dtype=COMM_DT)
        dug, duv, G["w_dw_f"][l], db = _ffn_act_bwd(t["ug"], t["uv"], dact, small["w_dw_f"][l], b_dw_f[l],
                                                    name=f"d_ffn_act{l}")
        G["b_dw_f"][l] = db[0]
        dh2 = _mm_nn2(dug, duv, wl["w_up", l], name=f"d_up{l}")
        half = _mm_tn(dug, t["h2"], name=f"dw_up_gate{l}", out_dtype=COMM_DT, rows=(0, 2 * Fw))
        gl["w_up", l] = _mm_tn(duv, t["h2"], name=f"dw_up_val{l}", out_dtype=COMM_DT, rows=(Fw, 2 * Fw), prev=half)
        dx, dg = _rms_bwd(t["x_mid"], g_ffn[l], dh2, dx, name=f"d_ffn_norm{l}")
        G["g_ffn"][l] = dg[0]
        started = start_grads("ffn", l, ffn_keys(l), gl)
        if len(pending) > 2:
            finish_grads(dx)

        dcat = _mm(dx, wl["w_out", l], name=f"d_cat{l}", out_dtype=F32, nt=True, after=started)
        gl["w_out", l] = _mm_tn(t["cat"], dx, name=f"dw_out{l}", out_dtype=COMM_DT)
        if l >= n_a:
            dot_h = dcat[:, :C].T.reshape(H, HEAD_DIM, S)
            dqt_h, dk_h, dv_h, dcq, dck = _foxt_bwd(t["q_h"], k_h, kt_h, v_h, cq_row, ck_col, t["ot"], t["lse"],
                                                    dot_h, tq=tq, name=f"d_fox{l}")
            dck = dck.reshape(H, S) + dcq.reshape(H, S)
            dk_sum = dk_h if dk_sum is None else dk_sum + dk_h
            dv_sum = dv_h if dv_sum is None else dv_sum + dv_h
            dck_sum = dck if dck_sum is None else dck_sum + dck
            dqm, dmk, dmv = _mem_attn_bwd(t["p"], C // Mw, t["mkv"], 0, dcat, C // Mw, Mw=Mw,
                                          name=f"d_mem_attn{l}")
            dp = jnp.concatenate([_from_heads_t(dqt_h), dqm], axis=1).astype(MXU_DT)
            key = ("w_in_b", l - n_a)
        else:
            dv2, dlg, dlb = _ln_silu_bwd(t["v2"], small["ln_g"][l], small["ln_b"][l], dcat, name=f"d_ln_silu{l}")
            G["ln_g"][l], G["ln_b"][l] = dlg[0], dlb[0]
            da, dgt, G["w_dw_a"][l], dbd, dba, dbg = _glu_conv_bwd(t["p"], small["b_glu"][l], small["w_dw_a"][l],
                                                                   dv2, name=f"d_glu_conv{l}")
            G["b_dw_a"][l] = dbd[0]
            G["b_glu"][l] = jnp.concatenate([dba[0], dbg[0]])
            dqm, dmk, dmv = _mem_attn_bwd(t["p"], 2 * C // Mw, t["mkv"], 0, dcat, C // Mw, Mw=Mw,
                                          name=f"d_mem_attn{l}")
            dp = jnp.concatenate([da, dgt, dqm], axis=1).astype(MXU_DT)
            key = ("w_in_a", l)
        dmkv = jnp.concatenate([dmk, dmv], axis=1).astype(MXU_DT)
        gl["w_mem_kv", l] = _mm_tn(mem_n, dmkv, name=f"dw_mem_kv{l}", out_dtype=COMM_DT)
        dmem_n = _mm(dmkv, wl["w_mem_kv", l], name=f"d_mem_kv{l}", out_dtype=F32, nt=True, add=dmem_n)
        if key[0] in transposed:
            dh = _mm(dp, wl[key], name=f"d_in_proj{l}", out_dtype=F32)
            gl[key] = _mm_tn(dp, t["h"], name=f"dw_in_proj{l}", out_dtype=COMM_DT)
        else:
            dh = _mm(dp, wl[key], name=f"d_in_proj{l}", out_dtype=F32, nt=True)
            gl[key] = _mm_tn(t["h"], dp, name=f"dw_in_proj{l}", out_dtype=COMM_DT)
        dx, dg = _rms_bwd(t["x_in"], g_mix[l], dh, dx, name=f"d_mix_norm{l}")
        G["g_mix"][l] = dg[0]
        if l == n_a:
            dcum = jnp.pad(dck_sum.T, ((0, 0), (0, GATE_LANES - H)))
            df, dbf = _fgate_bwd(fr, bf_pad, dcum, name="d_fgate")
            dkvf = jnp.concatenate([_from_heads(dk_sum), _from_heads(dv_sum), df], axis=1).astype(MXU_DT)
            dhk = _mm(dkvf, wkvf, name="d_kv_proj", out_dtype=F32, nt=True)
            gl["w_kvf", 0] = _mm_tn(hk, dkvf, name="dw_kv_proj", out_dtype=COMM_DT)[:, :w_kvf.shape[1]]
            dx, dg_kv = _rms_bwd(t["x_in"], g_kv, dhk, dx, name="d_kv_norm")
        if l > 0:
            started = start_grads("mix", l, mix_keys(l), gl)
            if len(pending) > 2:
                finish_grads(dx)

    _, dg_mem = _rms_bwd(mem0, g_mem, dmem_n, None, name="d_mem_norm")
    grads = {n: jnp.stack(v) for n, v in G.items()}
    grads.update(g_kv=dg_kv[0], b_f=dbf[0, :H], g_mem=dg_mem[0], g_final=dg_final[0], loss=loss_dev.reshape(1))
    gs8 = pk_small.pack({n: _to_shards(grads[n], ax) for n, ax in SMALL}, F32, (N_DEV,))
    last = start_grads("mix", 0, mix_keys(0), gl, [gs8, pk_rep.pack(grads, F32)], [True, False])
    while len(pending) > 1:
        finish_grads(dx)
    update_waited(after=last)
    extra = finish_grads(chain["w_down"][0])
    update_waited()
    no_state = dict(loss=jnp.zeros((1,), F32))
    for pk, recv, tag in ((pk_small, extra[0], "small"), (pk_rep, extra[1], "rep")):
        w32 = ws32 if tag == "small" else pk.pack({**W, **no_state}, F32)
        outs = _reduce_adam(recv, w32, pk.pack({**M1, **no_state}, F32), pk.pack({**V1, **no_state}, F32),
                            name=f"adam_{tag}")
        for kind, buf in zip(("grad", "delta", "new_m", "new_v"), outs):
            for n, a in pk.unpack(buf).items():
                res[kind, n] = a
    for n, outs in chain.items():
        for kind, a in zip(("grad", "delta", "new_m", "new_v"), outs):
            res[kind, n] = jnp.swapaxes(a, 1, 2) if n in transposed else a.reshape(W[n].shape)

    loss = res["grad", "loss"][0]
    return (loss, dx[None], *[res[kind, n] for kind in ("grad", "delta", "new_m", "new_v") for n in WEIGHTS])
```
